```python
import jax, jax.numpy as jnp
from jax import lax
import numpy as np

D_MODEL = 1024
BATCH = 8
SEQ = 8192
DEPTH = 2

POOL_WIDTH = D_MODEL // 2
POOL_WINDOWS = (2, 4, 8, 16)
N_POOL_GROUPS = len(POOL_WINDOWS)
POOL_GROUP = POOL_WIDTH // N_POOL_GROUPS
N_HEADS = 8
HEAD_DIM = 64
ATTN_WIDTH = N_HEADS * HEAD_DIM
Q_BLOCK = 128
IN_WIDTH = 2 * POOL_WIDTH + 4 * ATTN_WIDTH + 2 * D_MODEL
RMS_EPS = 1e-6

kernel_name = "hybrid_pool_stickbreak_gated"


def rms_norm(x, g):
    xf = x.astype(jnp.float32)
    y = xf * lax.rsqrt(jnp.mean(xf * xf, axis=-1, keepdims=True) + RMS_EPS)
    return (y * g.astype(jnp.float32)).astype(x.dtype)


def multiscale_pool(u, w_group, scale):
    B, S, _ = u.shape
    grp = u.astype(jnp.float32).reshape(B, S, N_POOL_GROUPS, POOL_GROUP)
    cs = jnp.cumsum(grp, axis=1)
    pos = jnp.arange(S)
    means = []
    for g, w in enumerate(POOL_WINDOWS):
        c = cs[:, :, g]
        prev = jnp.pad(c, ((0, 0), (w, 0), (0, 0)))[:, :S]
        cnt = jnp.minimum(pos + 1, w).astype(jnp.float32)[None, :, None]
        means.append((c - prev) / cnt)
    pooled = jnp.stack(means, axis=2) - grp
    mixed = jnp.einsum('bsgc,gcd->bsgd', pooled, w_group.astype(jnp.float32))
    return (mixed.reshape(B, S, POOL_WIDTH) * scale.astype(jnp.float32)).astype(u.dtype)


def stick_breaking_attention(q, k, v):
    B, S, H, Dh = q.shape
    n_blocks = S // Q_BLOCK
    qb = q.reshape(B, n_blocks, Q_BLOCK, H, Dh).transpose(1, 0, 2, 3, 4)
    kf = k.astype(jnp.float32)
    vf = v.astype(jnp.float32)
    key_pos = jnp.arange(S)
    inv_sqrt_d = 1.0 / float(np.sqrt(Dh))

    def one_block(args):
        q_blk, blk = args
        logits = jnp.einsum('bqhd,bkhd->bhqk', q_blk.astype(jnp.float32), kf) * inv_sqrt_d
        q_pos = blk * Q_BLOCK + jnp.arange(Q_BLOCK)
        mask = (key_pos[None, :] < q_pos[:, None])[None, None]
        log_beta = jax.nn.log_sigmoid(logits)
        log_1m_beta = jnp.where(mask, jax.nn.log_sigmoid(-logits), 0.0)
        later = lax.cumsum(log_1m_beta, axis=3, reverse=True) - log_1m_beta
        wts = jnp.where(mask, jnp.exp(log_beta + later), 0.0)
        return jnp.einsum('bhqk,bkhd->bqhd', wts, vf)

    out = lax.map(one_block, (qb, jnp.arange(n_blocks)))
    return out.transpose(1, 0, 2, 3, 4).reshape(B, S, H, Dh).astype(q.dtype)


def _fwd_setup_inputs(seed: int = 0) -> dict:
    key = jax.random.key(seed)
    ks = jax.random.split(key, 11)
    f32 = jnp.float32
    x = jax.random.normal(ks[0], (BATCH, SEQ, D_MODEL), f32)
    norm_g = 1.0 + 0.05 * jax.random.normal(ks[1], (DEPTH, D_MODEL), f32)
    w_in = jax.random.normal(ks[2], (DEPTH, D_MODEL, IN_WIDTH), f32) * D_MODEL ** -0.5
    b_gate = 0.01 * jax.random.normal(ks[3], (DEPTH, 2 * D_MODEL), f32)
    pool_w = jax.random.normal(ks[4], (DEPTH, N_POOL_GROUPS, POOL_GROUP, POOL_GROUP), f32) * POOL_GROUP ** -0.5
    pool_scale = 1.0 + 0.1 * jax.random.normal(ks[5], (DEPTH, POOL_WIDTH), f32)
    w_pool_up = jax.random.normal(ks[6], (DEPTH, POOL_WIDTH, D_MODEL), f32) * POOL_WIDTH ** -0.5
    w_attn_up = jax.random.normal(ks[7], (DEPTH, ATTN_WIDTH, D_MODEL), f32) * ATTN_WIDTH ** -0.5
    w_out = jax.random.normal(ks[8], (DEPTH, D_MODEL, D_MODEL), f32) * D_MODEL ** -0.5
    final_g = 1.0 + 0.05 * jax.random.normal(ks[9], (D_MODEL,), f32)
    return {"x": x, "norm_g": norm_g, "w_in": w_in, "b_gate": b_gate, "pool_w": pool_w,
            "pool_scale": pool_scale, "w_pool_up": w_pool_up, "w_attn_up": w_attn_up,
            "w_out": w_out, "final_g": final_g}


def _fwd_reference(x, norm_g, w_in, b_gate, pool_w, pool_scale, w_pool_up, w_attn_up, w_out, final_g):
    B, S, D = x.shape
    splits = np.cumsum([POOL_WIDTH, POOL_WIDTH, ATTN_WIDTH, ATTN_WIDTH, ATTN_WIDTH, ATTN_WIDTH]).tolist()
    for l in range(DEPTH):
        h = rms_norm(x, norm_g[l])
        proj = jnp.einsum('bsd,de->bse', h, w_in[l])
        u_pool, z_pool, q, k, v, z_attn, gate_logits = jnp.split(proj, splits, axis=-1)
        y_pool = multiscale_pool(u_pool, pool_w[l], pool_scale[l]) * jax.nn.silu(z_pool)
        attn = stick_breaking_attention(q.reshape(B, S, N_HEADS, HEAD_DIM),
                                        k.reshape(B, S, N_HEADS, HEAD_DIM),
                                        v.reshape(B, S, N_HEADS, HEAD_DIM))
        y_attn = attn.reshape(B, S, ATTN_WIDTH) * jax.nn.silu(z_attn)
        gates = jax.nn.sigmoid(gate_logits + b_gate[l]).reshape(B, S, 2, D)
        merged = (gates[:, :, 0] * jnp.einsum('bsp,pd->bsd', y_pool, w_pool_up[l])
                  + gates[:, :, 1] * jnp.einsum('bsa,ad->bsd', y_attn, w_attn_up[l]))
        x = x + jnp.einsum('bsd,de->bse', merged, w_out[l])
    return rms_norm(x, final_g)


import jax as _jax
import jax.numpy as _jnp

TWIN_FORMAT = 'train_step'
FWD_PARAMS = ['x', 'norm_g', 'w_in', 'b_gate', 'pool_w', 'pool_scale', 'w_pool_up', 'w_attn_up', 'w_out', 'final_g']
TWIN_WEIGHTS = ['norm_g', 'w_in', 'b_gate', 'pool_w', 'pool_scale', 'w_pool_up', 'w_attn_up', 'w_out', 'final_g']
TWIN_DIFF_INPUT = 'x'
TWIN_INPUTS = ['x', 'norm_g', 'w_in', 'b_gate', 'pool_w', 'pool_scale', 'w_pool_up', 'w_attn_up', 'w_out', 'final_g', 'loss_target', 'm_norm_g', 'm_w_in', 'm_b_gate', 'm_pool_w', 'm_pool_scale', 'm_w_pool_up', 'm_w_attn_up', 'm_w_out', 'm_final_g', 'v_norm_g', 'v_w_in', 'v_b_gate', 'v_pool_w', 'v_pool_scale', 'v_w_pool_up', 'v_w_attn_up', 'v_w_out', 'v_final_g']
TWIN_OUTPUTS = ['loss', 'grad_x', 'grad_norm_g', 'grad_w_in', 'grad_b_gate', 'grad_pool_w', 'grad_pool_scale', 'grad_w_pool_up', 'grad_w_attn_up', 'grad_w_out', 'grad_final_g', 'delta_norm_g', 'delta_w_in', 'delta_b_gate', 'delta_pool_w', 'delta_pool_scale', 'delta_w_pool_up', 'delta_w_attn_up', 'delta_w_out', 'delta_final_g', 'new_m_norm_g', 'new_m_w_in', 'new_m_b_gate', 'new_m_pool_w', 'new_m_pool_scale', 'new_m_w_pool_up', 'new_m_w_attn_up', 'new_m_w_out', 'new_m_final_g', 'new_v_norm_g', 'new_v_w_in', 'new_v_b_gate', 'new_v_pool_w', 'new_v_pool_scale', 'new_v_w_pool_up', 'new_v_w_attn_up', 'new_v_w_out', 'new_v_final_g']
TWIN_LEAF_KINDS = {'loss': 'loss', 'grad_x': 'grad_x', 'grad_norm_g': 'grad_w', 'grad_w_in': 'grad_w', 'grad_b_gate': 'grad_w', 'grad_pool_w': 'grad_w', 'grad_pool_scale': 'grad_w', 'grad_w_pool_up': 'grad_w', 'grad_w_attn_up': 'grad_w', 'grad_w_out': 'grad_w', 'grad_final_g': 'grad_w', 'delta_norm_g': 'delta_w', 'delta_w_in': 'delta_w', 'delta_b_gate': 'delta_w', 'delta_pool_w': 'delta_w', 'delta_pool_scale': 'delta_w', 'delta_w_pool_up': 'delta_w', 'delta_w_attn_up': 'delta_w', 'delta_w_out': 'delta_w', 'delta_final_g': 'delta_w', 'new_m_norm_g': 'new_m', 'new_m_w_in': 'new_m', 'new_m_b_gate': 'new_m', 'new_m_pool_w': 'new_m', 'new_m_pool_scale': 'new_m', 'new_m_w_pool_up': 'new_m', 'new_m_w_attn_up': 'new_m', 'new_m_w_out': 'new_m', 'new_m_final_g': 'new_m', 'new_v_norm_g': 'new_v', 'new_v_w_in': 'new_v', 'new_v_b_gate': 'new_v', 'new_v_pool_w': 'new_v', 'new_v_pool_scale': 'new_v', 'new_v_w_pool_up': 'new_v', 'new_v_w_attn_up': 'new_v', 'new_v_w_out': 'new_v', 'new_v_final_g': 'new_v'}


def _forward(args):
    return _fwd_reference(*[args[k] for k in FWD_PARAMS])


def _output_shape():
    def fwd():
        inp = _fwd_setup_inputs(0)
        return _fwd_reference(*[inp[k] for k in FWD_PARAMS])
    out = _jax.eval_shape(fwd)
    return out.shape, out.dtype

N_MICROBATCH = 1
ADAM_LR = 0.001
ADAM_B1 = 0.9
ADAM_B2 = 0.999
ADAM_EPS = 1e-08
ADAM_WD = 0.01
ADAM_STEP = 10
PER_EXAMPLE_BATCH_AXIS = {'x': 0, 'loss_target': 0}
SHARED_INPUTS = []
_WEIGHT_DTYPES = {'norm_g': _jnp.float32, 'w_in': _jnp.float32, 'b_gate': _jnp.float32, 'pool_w': _jnp.float32, 'pool_scale': _jnp.float32, 'w_pool_up': _jnp.float32, 'w_attn_up': _jnp.float32, 'w_out': _jnp.float32, 'final_g': _jnp.float32}
MOMENT_SCALE = {'norm_g': 1.346567e-01, 'w_in': 5.860207e-02, 'b_gate': 2.346504e-02, 'pool_w': 9.803097e-02, 'pool_scale': 9.988338e-02, 'w_pool_up': 6.939144e-02, 'w_attn_up': 5.035109e-02, 'w_out': 8.657807e-02, 'final_g': 6.407300e+01}


def _to_microbatches(a, axis):
    t = _jnp.moveaxis(a, axis, 0)
    t = t.reshape((N_MICROBATCH, t.shape[0] // N_MICROBATCH) + t.shape[1:])
    return _jnp.moveaxis(t, 1, axis + 1)


def setup_inputs(seed: int = 0) -> dict:
    inp = _fwd_setup_inputs(seed)
    key = _jax.random.fold_in(_jax.random.key(seed), 7919)
    shape, _ = _output_shape()
    out = dict(inp)
    out["loss_target"] = _jax.random.normal(_jax.random.fold_in(key, 0), shape, _jnp.float32)
    for i, name in enumerate(TWIN_WEIGHTS):
        w = inp[name].astype(_jnp.float32)
        if MOMENT_SCALE is None:
            s = _jnp.sqrt(_jnp.mean(_jnp.square(w)) + 1e-30)
        else:
            s = MOMENT_SCALE[name]
        km, kv = _jax.random.split(_jax.random.fold_in(key, i + 1))
        out[name] = w
        out["m_" + name] = s * _jax.random.normal(km, w.shape, _jnp.float32)
        out["v_" + name] = (s * s) * _jax.random.uniform(kv, w.shape, _jnp.float32, 0.5, 1.5)
    if N_MICROBATCH > 1:
        for name, axis in PER_EXAMPLE_BATCH_AXIS.items():
            out[name] = _to_microbatches(out[name], axis)
    return {'x': out['x'], 'norm_g': out['norm_g'], 'w_in': out['w_in'], 'b_gate': out['b_gate'], 'pool_w': out['pool_w'], 'pool_scale': out['pool_scale'], 'w_pool_up': out['w_pool_up'], 'w_attn_up': out['w_attn_up'], 'w_out': out['w_out'], 'final_g': out['final_g'], 'loss_target': out['loss_target'], 'm_norm_g': out['m_norm_g'], 'm_w_in': out['m_w_in'], 'm_b_gate': out['m_b_gate'], 'm_pool_w': out['m_pool_w'], 'm_pool_scale': out['m_pool_scale'], 'm_w_pool_up': out['m_w_pool_up'], 'm_w_attn_up': out['m_w_attn_up'], 'm_w_out': out['m_w_out'], 'm_final_g': out['m_final_g'], 'v_norm_g': out['v_norm_g'], 'v_w_in': out['v_w_in'], 'v_b_gate': out['v_b_gate'], 'v_pool_w': out['v_pool_w'], 'v_pool_scale': out['v_pool_scale'], 'v_w_pool_up': out['v_w_pool_up'], 'v_w_attn_up': out['v_w_attn_up'], 'v_w_out': out['v_w_out'], 'v_final_g': out['v_final_g']}


def _loss(weights, diff, rest, loss_target):
    with _jax.named_scope("forward"):
        args = {**rest, TWIN_DIFF_INPUT: diff, **{k: w.astype(_WEIGHT_DTYPES[k]) for k, w in weights.items()}}
        y = _forward(args)
    with _jax.named_scope("loss_head"):
        err = _jnp.square(y.astype(_jnp.float32) - loss_target)
        return 0.5 * _jnp.sum(_jnp.mean(err, axis=-1)) if err.ndim else 0.5 * err


def _adamw(w, g, m, v):
    m = ADAM_B1 * m + (1.0 - ADAM_B1) * g
    v = ADAM_B2 * v + (1.0 - ADAM_B2) * _jnp.square(g)
    m_hat = m / (1.0 - ADAM_B1 ** ADAM_STEP)
    v_hat = v / (1.0 - ADAM_B2 ** ADAM_STEP)
    delta = -ADAM_LR * (m_hat / (_jnp.sqrt(v_hat) + ADAM_EPS) + ADAM_WD * w)
    return delta, m, v


def reference(x, norm_g, w_in, b_gate, pool_w, pool_scale, w_pool_up, w_attn_up, w_out, final_g, loss_target, m_norm_g, m_w_in, m_b_gate, m_pool_w, m_pool_scale, m_w_pool_up, m_w_attn_up, m_w_out, m_final_g, v_norm_g, v_w_in, v_b_gate, v_pool_w, v_pool_scale, v_w_pool_up, v_w_attn_up, v_w_out, v_final_g):
    given = dict(x=x, norm_g=norm_g, w_in=w_in, b_gate=b_gate, pool_w=pool_w, pool_scale=pool_scale, w_pool_up=w_pool_up, w_attn_up=w_attn_up, w_out=w_out, final_g=final_g, loss_target=loss_target, m_norm_g=m_norm_g, m_w_in=m_w_in, m_b_gate=m_b_gate, m_pool_w=m_pool_w, m_pool_scale=m_pool_scale, m_w_pool_up=m_w_pool_up, m_w_attn_up=m_w_attn_up, m_w_out=m_w_out, m_final_g=m_final_g, v_norm_g=v_norm_g, v_w_in=v_w_in, v_b_gate=v_b_gate, v_pool_w=v_pool_w, v_pool_scale=v_pool_scale, v_w_pool_up=v_w_pool_up, v_w_attn_up=v_w_attn_up, v_w_out=v_w_out, v_final_g=v_final_g)
    weights = {n: given[n] for n in TWIN_WEIGHTS}
    shared = {n: given[n] for n in SHARED_INPUTS}
    per_example = {n: given[n] for n in ['x']}
    grad_fn = _jax.value_and_grad(_loss, argnums=(0, 1))

    def one_microbatch(ex, loss_target):
        ex = dict(ex)
        diff = ex.pop(TWIN_DIFF_INPUT)
        return grad_fn(weights, diff, {**shared, **ex}, loss_target)

    if N_MICROBATCH == 1:
        loss, (grad_w, grad_x) = one_microbatch(per_example, given["loss_target"])
    else:
        def body(carry, xs):
            loss_sum, grad_sum = carry
            l_k, (gw_k, gx_k) = one_microbatch(xs[0], xs[1])
            with _jax.named_scope("update"):
                return (loss_sum + l_k, _jax.tree.map(_jnp.add, grad_sum, gw_k)), gx_k

        init = (_jnp.zeros((), _jnp.float32), _jax.tree.map(_jnp.zeros_like, weights))
        (loss, grad_w), grad_x = _jax.lax.scan(body, init, (per_example, given["loss_target"]))
    with _jax.named_scope("update"):
        delta_w, new_m, new_v = {}, {}, {}
        for n in TWIN_WEIGHTS:
            delta_w[n], new_m[n], new_v[n] = _adamw(weights[n], grad_w[n], given["m_" + n], given["v_" + n])
    return (loss, grad_x, *[grad_w[n] for n in TWIN_WEIGHTS], *[delta_w[n] for n in TWIN_WEIGHTS],
            *[new_m[n] for n in TWIN_WEIGHTS], *[new_v[n] for n in TWIN_WEIGHTS])
```

```python
import functools

import jax
import jax.numpy as jnp
from jax import lax
from jax.experimental import pallas as pl
from jax.experimental.pallas import tpu as pltpu

F32 = jnp.float32
BF16 = jnp.bfloat16

N_DEV = 8
HEAD_DIM = 64
ATTN_W = 512
POOL_W = 512
POOL_G = 128
POOL_WINDOWS = (2, 4, 8, 16)
HALO = 16
LANE = 128
RMS_EPS = 1e-6
ADAM_LR, ADAM_B1, ADAM_B2, ADAM_EPS, ADAM_WD, ADAM_STEP = 0.001, 0.9, 0.999, 1e-08, 0.01, 10
VMEM_LIMIT = 56 * 1024 * 1024

NT_DIMS = (((1,), (1,)), ((), ()))
TN_DIMS = (((0,), (0,)), ((), ()))


def _pc(body, **kw):
    return pl.pallas_call(body, **kw)


def _params(*sem):
    return pltpu.CompilerParams(dimension_semantics=sem, vmem_limit_bytes=VMEM_LIMIT)


def _sigmoid(z):
    return 1.0 / (1.0 + jnp.exp(-z))


def _silu_and_grad(z):
    s = _sigmoid(z)
    return z * s, s * (1.0 + z * (1.0 - s))


def _my_index():
    return 4 * lax.axis_index("x") + 2 * lax.axis_index("y") + lax.axis_index("c")


def _peer(k):
    x, y, c = lax.axis_index("x"), lax.axis_index("y"), lax.axis_index("c")
    px = lax.rem(x + ((k >> 2) & 1), 2)
    py = lax.rem(y + ((k >> 1) & 1), 2)
    pc = lax.rem(c + (k & 1), 2)
    return (px, py, pc), 4 * px + 2 * py + pc


def _exchange(arrays, scatter, name):
    n = len(arrays)

    def body(*refs):
        ins, outs = refs[:n], refs[n:2 * n]
        send_sems, recv_sems, local_sems = refs[2 * n:]
        me = _my_index()

        def src(a, idx):
            return ins[a].at[idx] if scatter[a] else ins[a]

        local = [pltpu.make_async_copy(src(a, me), outs[a].at[me], local_sems.at[a]) for a in range(n)]
        for cp in local:
            cp.start()
        sends = []
        order = (1, 2, 4, 3, 5, 6, 7)
        for k in order:
            dev, pidx = _peer(k)
            for a in range(n):
                cp = pltpu.make_async_remote_copy(
                    src_ref=src(a, pidx), dst_ref=outs[a].at[me],
                    send_sem=send_sems.at[a * N_DEV + k], recv_sem=recv_sems.at[a * N_DEV + k],
                    device_id=dev, device_id_type=pl.DeviceIdType.MESH)
                cp.start()
                sends.append(cp)
        for k in order:
            dev, pidx = _peer(k)
            for a in range(n):
                pltpu.make_async_remote_copy(
                    src_ref=src(a, pidx), dst_ref=outs[a].at[pidx],
                    send_sem=send_sems.at[a * N_DEV + k], recv_sem=recv_sems.at[a * N_DEV + k],
                    device_id=dev, device_id_type=pl.DeviceIdType.MESH).wait_recv()
        for cp in sends:
            cp.wait_send()
        for cp in local:
            cp.wait()

    out_shape = []
    for a, s in zip(arrays, scatter):
        piece = a.shape[1:] if s else a.shape
        out_shape.append(jax.ShapeDtypeStruct((N_DEV,) + tuple(piece), a.dtype))
    any_spec = pl.BlockSpec(memory_space=pl.ANY)
    return _pc(
        body, name=name, out_shape=tuple(out_shape),
        in_specs=[any_spec] * n, out_specs=tuple([any_spec] * n),
        scratch_shapes=[pltpu.SemaphoreType.DMA((n * N_DEV,)), pltpu.SemaphoreType.DMA((n * N_DEV,)),
                        pltpu.SemaphoreType.DMA((n,))],
    )(*arrays)


def _mm_nn_res(a, b, res, tm, name):
    T, K = a.shape
    N = b.shape[1]

    def body(a_ref, b_ref, r_ref, o_ref):
        o_ref[...] = r_ref[...] + jnp.dot(a_ref[...], b_ref[...], preferred_element_type=F32)

    return _pc(
        body, name=name, grid=(T // tm,), out_shape=jax.ShapeDtypeStruct((T, N), F32),
        in_specs=[pl.BlockSpec((tm, K), lambda i: (i, 0)), pl.BlockSpec((K, N), lambda i: (0, 0)),
                  pl.BlockSpec((tm, N), lambda i: (i, 0))],
        out_specs=pl.BlockSpec((tm, N), lambda i: (i, 0)),
        compiler_params=_params("parallel"),
    )(a, b, res)


def _mm_nt(a, b, tm, tk, name):
    T, K = a.shape
    N = b.shape[0]
    nk = K // tk

    def body(a_ref, b_ref, o_ref):
        part = lax.dot_general(a_ref[...].astype(BF16), b_ref[...], NT_DIMS, preferred_element_type=F32)
        if nk == 1:
            o_ref[...] = part
        else:
            k = pl.program_id(1)

            @pl.when(k == 0)
            def _():
                o_ref[...] = part

            @pl.when(k > 0)
            def _():
                o_ref[...] += part

    return _pc(
        body, name=name, grid=(T // tm, nk), out_shape=jax.ShapeDtypeStruct((T, N), F32),
        in_specs=[pl.BlockSpec((tm, tk), lambda i, k: (i, k)), pl.BlockSpec((N, tk), lambda i, k: (0, k))],
        out_specs=pl.BlockSpec((tm, N), lambda i, k: (i, 0)),
        compiler_params=_params("parallel", "arbitrary"),
    )(a, b)


def _mm_tn(a, b, n_col_shards, tk, name):
    T, M = a.shape
    N = b.shape[1]
    tn = N // n_col_shards

    def body(a_ref, b_ref, o_ref):
        k = pl.program_id(1)
        part = lax.dot_general(a_ref[...].astype(BF16), b_ref[...].astype(BF16), TN_DIMS,
                               preferred_element_type=F32)

        @pl.when(k == 0)
        def _():
            o_ref[...] = part

        @pl.when(k > 0)
        def _():
            o_ref[...] += part

    return _pc(
        body, name=name, grid=(n_col_shards, T // tk),
        out_shape=jax.ShapeDtypeStruct((n_col_shards, M, tn), F32),
        in_specs=[pl.BlockSpec((tk, M), lambda j, k: (k, 0)), pl.BlockSpec((tk, tn), lambda j, k: (k, j))],
        out_specs=pl.BlockSpec((None, M, tn), lambda j, k: (j, 0, 0)),
        compiler_params=_params("parallel", "arbitrary"),
    )(a, b)


def _norm_inproj(x, g, w, tm, tn):
    T, D = x.shape
    NW = w.shape[1]

    def body(x_ref, g_ref, w_ref, proj_ref, projb_ref, h_ref, h_sc):
        @pl.when(pl.program_id(1) == 0)
        def _():
            xv = x_ref[...]
            r = lax.rsqrt(jnp.mean(xv * xv, axis=-1, keepdims=True) + RMS_EPS)
            h = ((xv * r) * g_ref[...]).astype(BF16)
            h_sc[...] = h
            h_ref[...] = h

        acc = jnp.dot(h_sc[...], w_ref[...], preferred_element_type=F32)
        proj_ref[...] = acc
        projb_ref[...] = acc.astype(BF16)

    return _pc(
        body, name="norm_inproj", grid=(T // tm, NW // tn),
        out_shape=(jax.ShapeDtypeStruct((T, NW), F32), jax.ShapeDtypeStruct((T, NW), BF16),
                   jax.ShapeDtypeStruct((T, D), BF16)),
        in_specs=[pl.BlockSpec((tm, D), lambda i, j: (i, 0)), pl.BlockSpec((1, D), lambda i, j: (0, 0)),
                  pl.BlockSpec((D, tn), lambda i, j: (0, j))],
        out_specs=(pl.BlockSpec((tm, tn), lambda i, j: (i, j)), pl.BlockSpec((tm, tn), lambda i, j: (i, j)),
                   pl.BlockSpec((tm, D), lambda i, j: (i, 0))),
        scratch_shapes=[pltpu.VMEM((tm, D), BF16)],
        compiler_params=_params("parallel", "arbitrary"),
    )(x, g, w)


def _window_sums(xh, forward):
    n = xh.shape[0]
    sums, s, step = [], xh, 1
    for _ in POOL_WINDOWS:
        s = s + pltpu.roll(s, step if forward else n - step, 0)
        sums.append(s)
        step *= 2
    return sums


def _pooled(u, halo, row):
    sums = _window_sums(jnp.concatenate([halo, u], axis=0), True)
    out = []
    for g, w in enumerate(POOL_WINDOWS):
        cols = slice(g * POOL_G, (g + 1) * POOL_G)
        cnt = jnp.minimum(row + 1, w).astype(F32)
        out.append(sums[g][HALO:, cols] / cnt - u[:, cols])
    return out


def _pool_fwd(proj, pool_w, scale, R):
    T = proj.shape[0]

    def body(u_ref, z_ref, pw_ref, sc_ref, y_ref, halo_sc):
        i = pl.program_id(0)

        @pl.when(i == 0)
        def _():
            halo_sc[...] = jnp.zeros_like(halo_sc)

        u = u_ref[...]
        row = i * R + lax.broadcasted_iota(jnp.int32, (R, 1), 0)
        pooled = _pooled(u, halo_sc[...], row)
        mixed = jnp.concatenate(
            [jnp.dot(pooled[g].astype(BF16), pw_ref[g].astype(BF16), preferred_element_type=F32)
             for g in range(len(POOL_WINDOWS))], axis=1)
        z = z_ref[...]
        y_ref[...] = ((mixed * sc_ref[...]) * (z * _sigmoid(z))).astype(BF16)
        halo_sc[...] = u[R - HALO:, :]

    return _pc(
        body, name="pool_fwd", grid=(T // R,), out_shape=jax.ShapeDtypeStruct((T, POOL_W), BF16),
        in_specs=[pl.BlockSpec((R, POOL_W), lambda i: (i, 0)), pl.BlockSpec((R, POOL_W), lambda i: (i, 1)),
                  pl.BlockSpec((4, POOL_G, POOL_G), lambda i: (0, 0, 0)), pl.BlockSpec((1, POOL_W), lambda i: (0, 0))],
        out_specs=pl.BlockSpec((R, POOL_W), lambda i: (i, 0)),
        scratch_shapes=[pltpu.VMEM((HALO, POOL_W), F32)],
        compiler_params=_params("arbitrary"),
    )(proj, proj, pool_w, scale)


def _softplus(l):
    return jnp.maximum(l, 0.0) + jnp.log(1.0 + jnp.exp(-jnp.abs(l)))


def _split_dot(m, x):
    hi = x.astype(BF16)
    lo = (x - hi.astype(F32)).astype(BF16)
    return jnp.dot(m, hi, preferred_element_type=F32) + jnp.dot(m, lo, preferred_element_type=F32)


def _head_masks(q, scale):
    lane = lax.broadcasted_iota(jnp.int32, (1, LANE), 1)
    qf = q.astype(F32) * scale
    return [jnp.where(lane < HEAD_DIM, qf, 0.0).astype(BF16), jnp.where(lane >= HEAD_DIM, qf, 0.0).astype(BF16)]


def _fill_transposed(src_ref, dst_sc, nk, tk):
    rowid = lax.broadcasted_iota(jnp.int32, (LANE, 1), 0)

    def step(j, carry):
        blk = src_ref[pl.ds(pl.multiple_of(j * tk, tk), tk), :].astype(F32).T
        dst_sc[0, j] = jnp.where(rowid < HEAD_DIM, blk, 0.0).astype(BF16)
        dst_sc[1, j] = jnp.where(rowid >= HEAD_DIM, blk, 0.0).astype(BF16)
        return carry

    lax.fori_loop(0, nk, step, 0)


def _attn_fwd(projb, proj, tq):
    T = projb.shape[0]
    nq = T // tq
    tk, nk = tq, nq
    n_pairs = ATTN_W // LANE
    cb = 2 * POOL_W // LANE

    def body(q_ref, k_ref, v_ref, za_ref, o_ref, y_ref, c_ref, vT_sc, oT_sc):
        i = pl.program_id(1)

        @pl.when(i == 0)
        def _():
            _fill_transposed(v_ref, vT_sc, nk, tk)

        qs = _head_masks(q_ref[...], HEAD_DIM ** -0.5)
        valid = lax.broadcasted_iota(jnp.int32, (tk, tq), 0) < lax.broadcasted_iota(jnp.int32, (tk, tq), 1)
        upper = (lax.broadcasted_iota(jnp.int32, (tk, tk), 1) > lax.broadcasted_iota(jnp.int32, (tk, tk), 0)).astype(BF16)
        oT_sc[...] = jnp.zeros_like(oT_sc)

        def tile(j, cs, masked):
            kj = k_ref[pl.ds(pl.multiple_of(j * tk, tk), tk), :]
            new = []
            for a in range(2):
                l = lax.dot_general(kj, qs[a], NT_DIMS, preferred_element_type=F32)
                sp = _softplus(l)
                lm = jnp.where(valid, -sp, 0.0) if masked else -sp
                later = _split_dot(upper, lm)
                w = jnp.exp((l - sp) + later + cs[a])
                if masked:
                    w = jnp.where(valid, w, 0.0)
                oT_sc[...] += jnp.dot(vT_sc[a, j], w.astype(BF16), preferred_element_type=F32)
                c_ref[a, 0, pl.ds(j, 1), :] = cs[a]
                new.append(cs[a] + jnp.sum(lm, axis=0, keepdims=True))
            return tuple(new)

        zero = jnp.zeros((1, tq), F32)
        cs = tile(i, (zero, zero), True)
        lax.fori_loop(0, i, lambda jj, c: tile(i - 1 - jj, c, False), cs)
        o = oT_sc[...].T
        o_ref[...] = o
        za = za_ref[...]
        y_ref[...] = (o * (za * _sigmoid(za))).astype(BF16)

    return _pc(
        body, name="attn_fwd", grid=(n_pairs, nq),
        out_shape=(jax.ShapeDtypeStruct((T, ATTN_W), F32), jax.ShapeDtypeStruct((T, ATTN_W), BF16),
                   jax.ShapeDtypeStruct((2 * n_pairs, nq, nk, tq), F32)),
        in_specs=[pl.BlockSpec((tq, LANE), lambda p, i: (i, cb + p)),
                  pl.BlockSpec((T, LANE), lambda p, i: (0, cb + n_pairs + p)),
                  pl.BlockSpec((T, LANE), lambda p, i: (0, cb + 2 * n_pairs + p)),
                  pl.BlockSpec((tq, LANE), lambda p, i: (i, cb + 3 * n_pairs + p))],
        out_specs=(pl.BlockSpec((tq, LANE), lambda p, i: (i, p)), pl.BlockSpec((tq, LANE), lambda p, i: (i, p)),
                   pl.BlockSpec((2, 1, nk, tq), lambda p, i: (p, i, 0, 0))),
        scratch_shapes=[pltpu.VMEM((2, nk, LANE, tk), BF16), pltpu.VMEM((LANE, tq), F32)],
        compiler_params=_params("arbitrary", "arbitrary"),
    )(projb, projb, projb, proj)


def _merge_fwd(y_pool, y_attn, w_pu, w_au, proj, b_gate, tm):
    T = y_pool.shape[0]
    D = w_pu.shape[1]
    gb = (2 * POOL_W + 4 * ATTN_W) // D

    def body(yp_ref, ya_ref, wpu_ref, wau_ref, gl0_ref, gl1_ref, bg_ref, m_ref, p_ref, a_ref):
        p = jnp.dot(yp_ref[...], wpu_ref[...], preferred_element_type=F32)
        a = jnp.dot(ya_ref[...], wau_ref[...], preferred_element_type=F32)
        g0 = _sigmoid(gl0_ref[...] + bg_ref[:, :D])
        g1 = _sigmoid(gl1_ref[...] + bg_ref[:, D:])
        m_ref[...] = (g0 * p + g1 * a).astype(BF16)
        p_ref[...] = p
        a_ref[...] = a

    row = lambda i: (i, 0)
    fixed = lambda i: (0, 0)
    return _pc(
        body, name="merge_fwd", grid=(T // tm,),
        out_shape=(jax.ShapeDtypeStruct((T, D), BF16), jax.ShapeDtypeStruct((T, D), F32),
                   jax.ShapeDtypeStruct((T, D), F32)),
        in_specs=[pl.BlockSpec((tm, POOL_W), row), pl.BlockSpec((tm, ATTN_W), row),
                  pl.BlockSpec((POOL_W, D), fixed), pl.BlockSpec((ATTN_W, D), fixed),
                  pl.BlockSpec((tm, D), lambda i: (i, gb)), pl.BlockSpec((tm, D), lambda i: (i, gb + 1)),
                  pl.BlockSpec((1, 2 * D), fixed)],
        out_specs=(pl.BlockSpec((tm, D), row), pl.BlockSpec((tm, D), row), pl.BlockSpec((tm, D), row)),
        compiler_params=_params("parallel"),
    )(y_pool, y_attn, w_pu, w_au, proj, proj, b_gate)


def _final_loss(x, g, target, tm):
    T, D = x.shape

    def body(x_ref, g_ref, t_ref, dx_ref, dg_ref, loss_ref):
        @pl.when(pl.program_id(0) == 0)
        def _():
            dg_ref[...] = jnp.zeros_like(dg_ref)
            loss_ref[...] = jnp.zeros_like(loss_ref)

        xv, gv = x_ref[...], g_ref[...]
        r = lax.rsqrt(jnp.mean(xv * xv, axis=-1, keepdims=True) + RMS_EPS)
        xh = xv * r
        d = xh * gv - t_ref[...]
        loss_ref[...] += 0.5 * jnp.sum(jnp.mean(d * d, axis=-1, keepdims=True), axis=0, keepdims=True)
        dy = d * (1.0 / D)
        dg_ref[...] += jnp.sum(dy * xh, axis=0, keepdims=True)
        dh = dy * gv
        dx_ref[...] = r * (dh - xh * jnp.mean(dh * xh, axis=-1, keepdims=True))

    return _pc(
        body, name="final_loss", grid=(T // tm,),
        out_shape=(jax.ShapeDtypeStruct((T, D), F32), jax.ShapeDtypeStruct((1, D), F32),
                   jax.ShapeDtypeStruct((8, LANE), F32)),
        in_specs=[pl.BlockSpec((tm, D), lambda i: (i, 0)), pl.BlockSpec((1, D), lambda i: (0, 0)),
                  pl.BlockSpec((tm, D), lambda i: (i, 0))],
        out_specs=(pl.BlockSpec((tm, D), lambda i: (i, 0)), pl.BlockSpec((1, D), lambda i: (0, 0)),
                   pl.BlockSpec((8, LANE), lambda i: (0, 0))),
        compiler_params=_params("arbitrary"),
    )(x, g, target)


def _merge_bwd(dm, p, a, proj, b_gate, tm):
    T, D = dm.shape
    gb = (2 * POOL_W + 4 * ATTN_W) // D

    def body(dm_ref, p_ref, a_ref, gl0_ref, gl1_ref, bg_ref, dp_ref, da_ref, dgl_ref, dbg_ref):
        @pl.when(pl.program_id(0) == 0)
        def _():
            dbg_ref[...] = jnp.zeros_like(dbg_ref)

        dmv = dm_ref[...]
        g0 = _sigmoid(gl0_ref[...] + bg_ref[:, :D])
        g1 = _sigmoid(gl1_ref[...] + bg_ref[:, D:])
        dp_ref[...] = (dmv * g0).astype(BF16)
        da_ref[...] = (dmv * g1).astype(BF16)
        dgl0 = dmv * p_ref[...] * (g0 * (1.0 - g0))
        dgl1 = dmv * a_ref[...] * (g1 * (1.0 - g1))
        dgl_ref[:, :D] = dgl0.astype(BF16)
        dgl_ref[:, D:] = dgl1.astype(BF16)
        dbg_ref[:, :D] += jnp.sum(dgl0, axis=0, keepdims=True)
        dbg_ref[:, D:] += jnp.sum(dgl1, axis=0, keepdims=True)

    row = lambda i: (i, 0)
    fixed = lambda i: (0, 0)
    return _pc(
        body, name="merge_bwd", grid=(T // tm,),
        out_shape=(jax.ShapeDtypeStruct((T, D), BF16), jax.ShapeDtypeStruct((T, D), BF16),
                   jax.ShapeDtypeStruct((T, 2 * D), BF16), jax.ShapeDtypeStruct((1, 2 * D), F32)),
        in_specs=[pl.BlockSpec((tm, D), row), pl.BlockSpec((tm, D), row), pl.BlockSpec((tm, D), row),
                  pl.BlockSpec((tm, D), lambda i: (i, gb)), pl.BlockSpec((tm, D), lambda i: (i, gb + 1)),
                  pl.BlockSpec((1, 2 * D), fixed)],
        out_specs=(pl.BlockSpec((tm, D), row), pl.BlockSpec((tm, D), row), pl.BlockSpec((tm, 2 * D), row),
                   pl.BlockSpec((1, 2 * D), fixed)),
        compiler_params=_params("arbitrary"),
    )(dm, p, a, proj, proj, b_gate)


def _pool_bwd(proj, dy, pool_w, scale, R):
    T = proj.shape[0]
    nb = T // R
    hb = R // HALO

    def body(u_ref, up_ref, z_ref, dy_ref, pw_ref, sc_ref, du_ref, dz_ref, dpw_ref, dsc_ref, halo_sc):
        i = pl.program_id(0)
        rb = nb - 1 - i

        @pl.when(i == 0)
        def _():
            halo_sc[...] = jnp.zeros_like(halo_sc)
            dpw_ref[...] = jnp.zeros_like(dpw_ref)
            dsc_ref[...] = jnp.zeros_like(dsc_ref)

        u = u_ref[...]
        row = rb * R + lax.broadcasted_iota(jnp.int32, (R, 1), 0)
        before = jnp.where(rb > 0, up_ref[...], 0.0)
        pooled = _pooled(u, before, row)
        pw = [pw_ref[g].astype(BF16) for g in range(len(POOL_WINDOWS))]
        mixed = jnp.concatenate(
            [jnp.dot(pooled[g].astype(BF16), pw[g], preferred_element_type=F32) for g in range(len(POOL_WINDOWS))],
            axis=1)
        sc = sc_ref[...]
        silu, dsilu = _silu_and_grad(z_ref[...])
        dyv = dy_ref[...]
        dmp = dyv * silu
        dz_ref[...] = (dyv * (mixed * sc) * dsilu).astype(BF16)
        dsc_ref[...] += jnp.sum(dmp * mixed, axis=0, keepdims=True)
        dmixed = (dmp * sc).astype(BF16)
        dpn = []
        dpooled = []
        for g, w in enumerate(POOL_WINDOWS):
            cols = slice(g * POOL_G, (g + 1) * POOL_G)
            dpw_ref[g] += lax.dot_general(pooled[g].astype(BF16), dmixed[:, cols], TN_DIMS,
                                          preferred_element_type=F32)
            dpg = lax.dot_general(dmixed[:, cols], pw[g], NT_DIMS, preferred_element_type=F32)
            dpooled.append(dpg)
            dpn.append(dpg / jnp.minimum(row + 1, w).astype(F32))
        dpn = jnp.concatenate(dpn, axis=1)
        sums = _window_sums(jnp.concatenate([dpn, halo_sc[...]], axis=0), False)
        du_ref[...] = jnp.concatenate(
            [sums[g][:R, g * POOL_G:(g + 1) * POOL_G] - dpooled[g] for g in range(len(POOL_WINDOWS))],
            axis=1).astype(BF16)
        halo_sc[...] = dpn[:HALO, :]

    rev = lambda i: (nb - 1 - i, 0)
    return _pc(
        body, name="pool_bwd", grid=(nb,),
        out_shape=(jax.ShapeDtypeStruct((T, POOL_W), BF16), jax.ShapeDtypeStruct((T, POOL_W), BF16),
                   jax.ShapeDtypeStruct((4, POOL_G, POOL_G), F32), jax.ShapeDtypeStruct((1, POOL_W), F32)),
        in_specs=[pl.BlockSpec((R, POOL_W), rev),
                  pl.BlockSpec((HALO, POOL_W), lambda i: (jnp.maximum((nb - 1 - i) * hb - 1, 0), 0)),
                  pl.BlockSpec((R, POOL_W), lambda i: (nb - 1 - i, 1)),
                  pl.BlockSpec((R, POOL_W), rev),
                  pl.BlockSpec((4, POOL_G, POOL_G), lambda i: (0, 0, 0)), pl.BlockSpec((1, POOL_W), lambda i: (0, 0))],
        out_specs=(pl.BlockSpec((R, POOL_W), rev), pl.BlockSpec((R, POOL_W), rev),
                   pl.BlockSpec((4, POOL_G, POOL_G), lambda i: (0, 0, 0)), pl.BlockSpec((1, POOL_W), lambda i: (0, 0))),
        scratch_shapes=[pltpu.VMEM((HALO, POOL_W), F32)],
        compiler_params=_params("arbitrary"),
    )(proj, proj, proj, dy, pool_w, scale)


def _attn_gate_bwd(dya, o, proj, tm):
    T = dya.shape[0]
    zb = (2 * POOL_W + 3 * ATTN_W) // ATTN_W

    def body(dy_ref, o_ref, za_ref, do_ref, dza_ref):
        silu, dsilu = _silu_and_grad(za_ref[...])
        dyv = dy_ref[...]
        do_ref[...] = (dyv * silu).astype(BF16)
        dza_ref[...] = (dyv * o_ref[...] * dsilu).astype(BF16)

    row = lambda i: (i, 0)
    return _pc(
        body, name="attn_gate_bwd", grid=(T // tm,),
        out_shape=(jax.ShapeDtypeStruct((T, ATTN_W), BF16), jax.ShapeDtypeStruct((T, ATTN_W), BF16)),
        in_specs=[pl.BlockSpec((tm, ATTN_W), row), pl.BlockSpec((tm, ATTN_W), row),
                  pl.BlockSpec((tm, ATTN_W), lambda i: (i, zb))],
        out_specs=(pl.BlockSpec((tm, ATTN_W), row), pl.BlockSpec((tm, ATTN_W), row)),
        compiler_params=_params("parallel"),
    )(dya, o, proj)


def _attn_bwd(projb, do, carries, tq):
    T = projb.shape[0]
    nq = T // tq
    tk, nk = tq, nq
    n_pairs = ATTN_W // LANE
    cb = 2 * POOL_W // LANE
    scale = HEAD_DIM ** -0.5

    def body(q_ref, k_ref, v_ref, do_ref, c_ref, dq_ref, dk_ref, dv_ref, kT_sc, dqT_sc):
        i = pl.program_id(1)

        @pl.when(i == 0)
        def _():
            _fill_transposed(k_ref, kT_sc, nk, tk)
            dk_ref[...] = jnp.zeros_like(dk_ref)
            dv_ref[...] = jnp.zeros_like(dv_ref)

        qs = _head_masks(q_ref[...], scale)
        dos = _head_masks(do_ref[...], 1.0)
        valid = lax.broadcasted_iota(jnp.int32, (tk, tq), 0) < lax.broadcasted_iota(jnp.int32, (tk, tq), 1)
        kk0 = lax.broadcasted_iota(jnp.int32, (tk, tk), 0)
        kk1 = lax.broadcasted_iota(jnp.int32, (tk, tk), 1)
        upper = (kk1 > kk0).astype(BF16)
        lower = (kk1 < kk0).astype(BF16)
        dqT_sc[...] = jnp.zeros_like(dqT_sc)

        def tile(j, fs, masked):
            rows = pl.ds(pl.multiple_of(j * tk, tk), tk)
            kj = k_ref[rows, :]
            vj = v_ref[rows, :]
            new = []
            for a in range(2):
                l = lax.dot_general(kj, qs[a], NT_DIMS, preferred_element_type=F32)
                sp = _softplus(l)
                lm = jnp.where(valid, -sp, 0.0) if masked else -sp
                later = _split_dot(upper, lm)
                lb = l - sp
                w = jnp.exp(lb + later + c_ref[a, 0, pl.ds(j, 1), :])
                if masked:
                    w = jnp.where(valid, w, 0.0)
                beta = jnp.exp(lb)
                dw = lax.dot_general(vj, dos[a], NT_DIMS, preferred_element_type=F32)
                e = w * dw
                f = _split_dot(lower, e) + fs[a]
                dl = e - beta * (e + f)
                if masked:
                    dl = jnp.where(valid, dl, 0.0)
                dlb = dl.astype(BF16)
                dv_ref[rows, :] += jnp.dot(w.astype(BF16), dos[a], preferred_element_type=F32)
                dk_ref[rows, :] += jnp.dot(dlb, qs[a], preferred_element_type=F32)
                dqT_sc[...] += jnp.dot(kT_sc[a, j], dlb, preferred_element_type=F32)
                new.append(fs[a] + jnp.sum(e, axis=0, keepdims=True))
            return tuple(new)

        zero = jnp.zeros((1, tq), F32)
        fs = lax.fori_loop(0, i, lambda j, f: tile(j, f, False), (zero, zero))
        tile(i, fs, True)
        dq_ref[...] = dqT_sc[...].T * scale

    return _pc(
        body, name="attn_bwd", grid=(n_pairs, nq),
        out_shape=(jax.ShapeDtypeStruct((T, ATTN_W), F32), jax.ShapeDtypeStruct((T, ATTN_W), F32),
                   jax.ShapeDtypeStruct((T, ATTN_W), F32)),
        in_specs=[pl.BlockSpec((tq, LANE), lambda p, i: (i, cb + p)),
                  pl.BlockSpec((T, LANE), lambda p, i: (0, cb + n_pairs + p)),
                  pl.BlockSpec((T, LANE), lambda p, i: (0, cb + 2 * n_pairs + p)),
                  pl.BlockSpec((tq, LANE), lambda p, i: (i, p)),
                  pl.BlockSpec((2, 1, nk, tq), lambda p, i: (p, i, 0, 0))],
        out_specs=(pl.BlockSpec((tq, LANE), lambda p, i: (i, p)), pl.BlockSpec((T, LANE), lambda p, i: (0, p)),
                   pl.BlockSpec((T, LANE), lambda p, i: (0, p))),
        scratch_shapes=[pltpu.VMEM((2, nk, LANE, tk), BF16), pltpu.VMEM((LANE, tq), F32)],
        compiler_params=_params("arbitrary", "arbitrary"),
    )(projb, projb, projb, do, carries)


def _norm_bwd(dh, x, g, dxo, tm):
    T, D = x.shape

    def body(dh_ref, x_ref, g_ref, dxo_ref, dx_ref, dg_ref):
        @pl.when(pl.program_id(0) == 0)
        def _():
            dg_ref[...] = jnp.zeros_like(dg_ref)

        xv = x_ref[...]
        r = lax.rsqrt(jnp.mean(xv * xv, axis=-1, keepdims=True) + RMS_EPS)
        xh = xv * r
        dhv = dh_ref[...]
        dg_ref[...] += jnp.sum(dhv * xh, axis=0, keepdims=True)
        dhg = dhv * g_ref[...]
        dx_ref[...] = dxo_ref[...] + r * (dhg - xh * jnp.mean(dhg * xh, axis=-1, keepdims=True))

    row = lambda i: (i, 0)
    return _pc(
        body, name="norm_bwd", grid=(T // tm,),
        out_shape=(jax.ShapeDtypeStruct((T, D), F32), jax.ShapeDtypeStruct((1, D), F32)),
        in_specs=[pl.BlockSpec((tm, D), row), pl.BlockSpec((tm, D), row), pl.BlockSpec((1, D), lambda i: (0, 0)),
                  pl.BlockSpec((tm, D), row)],
        out_specs=(pl.BlockSpec((tm, D), row), pl.BlockSpec((1, D), lambda i: (0, 0))),
        compiler_params=_params("arbitrary"),
    )(dh, x, g, dxo)


def _adamw(pieces, w, m, v, name):
    rows, cols = w.shape
    br = rows
    while br * cols > 65536 and br % 16 == 0:
        br //= 2
    c1 = 1.0 / (1.0 - ADAM_B1 ** ADAM_STEP)
    c2 = 1.0 / (1.0 - ADAM_B2 ** ADAM_STEP)

    def body(p_ref, w_ref, m_ref, v_ref, g_ref, d_ref, nm_ref, nv_ref):
        g = p_ref[0]
        for s in range(1, N_DEV):
            g = g + p_ref[s]
        nm = ADAM_B1 * m_ref[...] + (1.0 - ADAM_B1) * g
        nv = ADAM_B2 * v_ref[...] + (1.0 - ADAM_B2) * (g * g)
        g_ref[...] = g
        nm_ref[...] = nm
        nv_ref[...] = nv
        d_ref[...] = -ADAM_LR * ((nm * c1) / (jnp.sqrt(nv * c2) + ADAM_EPS) + ADAM_WD * w_ref[...])

    blk = pl.BlockSpec((br, cols), lambda i: (i, 0))
    shape = jax.ShapeDtypeStruct((rows, cols), F32)
    return _pc(
        body, name=name, grid=(rows // br,), out_shape=(shape, shape, shape, shape),
        in_specs=[pl.BlockSpec((N_DEV, br, cols), lambda i: (0, i, 0)), blk, blk, blk],
        out_specs=(blk, blk, blk, blk),
        compiler_params=_params("parallel"),
    )(pieces, w, m, v)


def _rows128(a):
    flat = a.reshape(-1)
    n = flat.shape[0]
    padded = -(-n // (8 * LANE)) * (8 * LANE)
    if padded != n:
        flat = jnp.concatenate([flat, jnp.zeros((padded - n,), flat.dtype)])
    return flat.reshape(-1, LANE)


def _pack(parts):
    return jnp.concatenate([_rows128(p) for p in parts], axis=0)


def _unpack(packed, like):
    out, r = [], 0
    for a in like:
        n = a.size
        nr = -(-n // (8 * LANE)) * 8
        out.append(packed[r:r + nr].reshape(-1)[:n].reshape(a.shape))
        r += nr
    return out


def kernel(x, norm_g, w_in, b_gate, pool_w, pool_scale, w_pool_up, w_attn_up, w_out, final_g, loss_target, m_norm_g, m_w_in, m_b_gate, m_pool_w, m_pool_scale, m_w_pool_up, m_w_attn_up, m_w_out, m_final_g, v_norm_g, v_w_in, v_b_gate, v_pool_w, v_pool_scale, v_w_pool_up, v_w_attn_up, v_w_out, v_final_g):
    L = norm_g.shape[0]
    T, D = x.shape[1], x.shape[2]
    NW = w_in.shape[2] * N_DEV
    assert NW == 2 * POOL_W + 4 * ATTN_W + 2 * D and x.shape[0] == 1
    tm = min(512, T)
    tq = min(256, T // 2)
    x0 = x.reshape(T, D)
    target = loss_target.reshape(T, D)

    g_in, g_pu, g_au, g_out = _exchange(
        [w_in.astype(BF16), w_pool_up.astype(BF16), w_attn_up.astype(BF16), w_out.astype(BF16)],
        [False] * 4, "gather_weights")
    win_full = jnp.transpose(g_in, (1, 2, 0, 3)).reshape(L, D, NW)
    wpu_full = jnp.transpose(g_pu, (1, 2, 0, 3)).reshape(L, POOL_W, D)
    wau_full = jnp.transpose(g_au, (1, 2, 0, 3)).reshape(L, ATTN_W, D)
    wout_full = jnp.transpose(g_out, (1, 0, 2, 3)).reshape(L, D, D)

    saved = []
    xl = x0
    for l in range(L):
        proj, projb, h = _norm_inproj(xl, norm_g[l:l + 1], win_full[l], tm, 512)
        y_pool = _pool_fwd(proj, pool_w[l], pool_scale[l:l + 1], tm)
        o, y_attn, carries = _attn_fwd(projb, proj, tq)
        merged, p, a = _merge_fwd(y_pool, y_attn, wpu_full[l], wau_full[l], proj, b_gate[l:l + 1], min(256, T))
        x_next = _mm_nn_res(merged, wout_full[l], xl, tm, "out_proj")
        saved.append((xl, proj, projb, h, y_pool, o, y_attn, carries, merged, p, a))
        xl = x_next

    dx, d_final_g, loss_part = _final_loss(xl, final_g.reshape(1, D), target, tm)

    d_norm_g, d_b_gate, d_pool_w, d_pool_scale = [None] * L, [None] * L, [None] * L, [None] * L
    d_win, d_wpu, d_wau, d_wout = [None] * L, [None] * L, [None] * L, [None] * L
    for l in reversed(range(L)):
        xin, proj, projb, h, y_pool, o, y_attn, carries, merged, p, a = saved[l]
        dm = _mm_nt(dx, wout_full[l], tm, D, "d_merged")
        d_wout[l] = _mm_tn(merged, dx, 1, tm, "d_w_out").reshape(N_DEV, D // N_DEV, D)
        dp, da, dgl, d_b_gate[l] = _merge_bwd(dm, p, a, proj, b_gate[l:l + 1], min(256, T))
        d_wpu[l] = _mm_tn(y_pool, dp, N_DEV, tm, "d_w_pool_up")
        d_wau[l] = _mm_tn(y_attn, da, N_DEV, tm, "d_w_attn_up")
        dyp = _mm_nt(dp, wpu_full[l], tm, D, "d_y_pool")
        dya = _mm_nt(da, wau_full[l], tm, D, "d_y_attn")
        du, dzp, d_pool_w[l], d_pool_scale[l] = _pool_bwd(proj, dyp, pool_w[l], pool_scale[l:l + 1], tm)
        do, dza = _attn_gate_bwd(dya, o, proj, tm)
        dq, dk, dv = _attn_bwd(projb, do, carries, tq)
        dproj = jnp.concatenate([du, dzp, dq.astype(BF16), dk.astype(BF16), dv.astype(BF16), dza, dgl], axis=1)
        d_win[l] = _mm_tn(h, dproj, N_DEV, tm, "d_w_in")
        dh = _mm_nt(dproj, win_full[l], min(256, T), 1024 if NW % 1024 == 0 else NW, "d_h")
        dx, d_norm_g[l] = _norm_bwd(dh, xin, norm_g[l:l + 1], dx, tm)

    small_like = [norm_g, b_gate, pool_w, pool_scale, final_g, jnp.zeros((8, LANE), F32)]
    small = _pack([jnp.concatenate(d_norm_g, 0), jnp.concatenate(d_b_gate, 0), jnp.stack(d_pool_w, 0),
                   jnp.concatenate(d_pool_scale, 0), d_final_g, loss_part])
    sends = [small]
    flags = [False]
    for l in range(L):
        sends += [d_win[l], d_wpu[l], d_wau[l], d_wout[l]]
        flags += [True] * 4
    got = _exchange(sends, flags, "scatter_grads")
    r_small = got[0]
    r_in = jnp.stack([got[1 + 4 * l] for l in range(L)], axis=1)
    r_pu = jnp.stack([got[2 + 4 * l] for l in range(L)], axis=1)
    r_au = jnp.stack([got[3 + 4 * l] for l in range(L)], axis=1)
    r_out = jnp.stack([got[4 + 4 * l] for l in range(L)], axis=1)

    def update(pieces, w, m, v, name):
        cols = w.shape[-1]
        res = _adamw(pieces.reshape(N_DEV, -1, cols), w.reshape(-1, cols), m.reshape(-1, cols),
                     v.reshape(-1, cols), name)
        return [r.reshape(w.shape) for r in res]

    u_in = update(r_in, w_in, m_w_in, v_w_in, "adamw_w_in")
    u_pu = update(r_pu, w_pool_up, m_w_pool_up, v_w_pool_up, "adamw_w_pool_up")
    u_au = update(r_au, w_attn_up, m_w_attn_up, v_w_attn_up, "adamw_w_attn_up")
    u_out = update(r_out, w_out, m_w_out, v_w_out, "adamw_w_out")
    zeros = small_like[-1]
    smalls = _adamw(r_small,
                    _pack([norm_g, b_gate, pool_w, pool_scale, final_g, zeros]),
                    _pack([m_norm_g, m_b_gate, m_pool_w, m_pool_scale, m_final_g, zeros]),
                    _pack([v_norm_g, v_b_gate, v_pool_w, v_pool_scale, v_final_g, zeros]), "adamw_small")
    s_g, s_d, s_m, s_v = [_unpack(s, small_like) for s in smalls]
    loss = s_g[5][0, 0]

    def ordered(k):
        s = (s_g, s_d, s_m, s_v)[k]
        return [s[0], u_in[k], s[1], s[2], s[3], u_pu[k], u_au[k], u_out[k], s[4]]

    return (loss, dx.reshape(x.shape), *ordered(0), *ordered(1), *ordered(2), *ordered(3))
```

```python
import jax
import jax.numpy as jnp
from jax import lax
from jax.experimental import pallas as pl
from jax.experimental.pallas import tpu as pltpu

F32 = jnp.float32
BF16 = jnp.bfloat16

N_DEV = 8
HEAD_DIM = 64
ATTN_W = 512
POOL_W = 512
POOL_G = 128
POOL_WINDOWS = (2, 4, 8, 16)
HALO = 16
LANE = 128
RMS_EPS = 1e-6
ADAM_LR, ADAM_B1, ADAM_B2, ADAM_EPS, ADAM_WD, ADAM_STEP = 0.001, 0.9, 0.999, 1e-08, 0.01, 10
VMEM_LIMIT = 56 * 1024 * 1024

NT_DIMS = (((1,), (1,)), ((), ()))
TN_DIMS = (((0,), (0,)), ((), ()))


def _pc(body, **kw):
    return pl.pallas_call(body, **kw)


def _params(*sem):
    return pltpu.CompilerParams(dimension_semantics=sem, vmem_limit_bytes=VMEM_LIMIT)


def _sigmoid(z):
    return 1.0 / (1.0 + jnp.exp(-z))


def _silu_and_grad(z):
    s = _sigmoid(z)
    return z * s, s * (1.0 + z * (1.0 - s))


def _my_index():
    return 4 * lax.axis_index("x") + 2 * lax.axis_index("y") + lax.axis_index("c")


def _peer(k):
    x, y, c = lax.axis_index("x"), lax.axis_index("y"), lax.axis_index("c")
    px = lax.rem(x + ((k >> 2) & 1), 2)
    py = lax.rem(y + ((k >> 1) & 1), 2)
    pc = lax.rem(c + (k & 1), 2)
    return (px, py, pc), 4 * px + 2 * py + pc


def _exchange(arrays, scatter, name):
    n = len(arrays)

    def body(*refs):
        ins, outs = refs[:n], refs[n:2 * n]
        send_sems, recv_sems, local_sems = refs[2 * n:]
        me = _my_index()

        def src(a, idx):
            return ins[a].at[idx] if scatter[a] else ins[a]

        local = [pltpu.make_async_copy(src(a, me), outs[a].at[me], local_sems.at[a]) for a in range(n)]
        for cp in local:
            cp.start()
        sends = []
        order = (1, 2, 4, 3, 5, 6, 7)
        for k in order:
            dev, pidx = _peer(k)
            for a in range(n):
                cp = pltpu.make_async_remote_copy(
                    src_ref=src(a, pidx), dst_ref=outs[a].at[me],
                    send_sem=send_sems.at[a * N_DEV + k], recv_sem=recv_sems.at[a * N_DEV + k],
                    device_id=dev, device_id_type=pl.DeviceIdType.MESH)
                cp.start()
                sends.append(cp)
        for k in order:
            dev, pidx = _peer(k)
            for a in range(n):
                pltpu.make_async_remote_copy(
                    src_ref=src(a, pidx), dst_ref=outs[a].at[pidx],
                    send_sem=send_sems.at[a * N_DEV + k], recv_sem=recv_sems.at[a * N_DEV + k],
                    device_id=dev, device_id_type=pl.DeviceIdType.MESH).wait_recv()
        for cp in sends:
            cp.wait_send()
        for cp in local:
            cp.wait()

    out_shape = []
    for a, s in zip(arrays, scatter):
        piece = a.shape[1:] if s else a.shape
        out_shape.append(jax.ShapeDtypeStruct((N_DEV,) + tuple(piece), a.dtype))
    any_spec = pl.BlockSpec(memory_space=pl.ANY)
    return _pc(
        body, name=name, out_shape=tuple(out_shape),
        in_specs=[any_spec] * n, out_specs=tuple([any_spec] * n),
        scratch_shapes=[pltpu.SemaphoreType.DMA((n * N_DEV,)), pltpu.SemaphoreType.DMA((n * N_DEV,)),
                        pltpu.SemaphoreType.DMA((n,))],
    )(*arrays)


def _mm_nn_res(a, b, res, tm, name):
    T, K = a.shape
    N = b.shape[1]

    def body(a_ref, b_ref, r_ref, o_ref):
        o_ref[...] = r_ref[...] + jnp.dot(a_ref[...], b_ref[...], preferred_element_type=F32)

    return _pc(
        body, name=name, grid=(T // tm,), out_shape=jax.ShapeDtypeStruct((T, N), F32),
        in_specs=[pl.BlockSpec((tm, K), lambda i: (i, 0)), pl.BlockSpec((K, N), lambda i: (0, 0)),
                  pl.BlockSpec((tm, N), lambda i: (i, 0))],
        out_specs=pl.BlockSpec((tm, N), lambda i: (i, 0)),
        compiler_params=_params("parallel"),
    )(a, b, res)


def _mm_nt(a, b, tm, tk, name):
    T, K = a.shape
    N = b.shape[0]
    nk = K // tk

    def body(a_ref, b_ref, o_ref):
        part = lax.dot_general(a_ref[...].astype(BF16), b_ref[...], NT_DIMS, preferred_element_type=F32)
        if nk == 1:
            o_ref[...] = part
        else:
            k = pl.program_id(1)

            @pl.when(k == 0)
            def _():
                o_ref[...] = part

            @pl.when(k > 0)
            def _():
                o_ref[...] += part

    return _pc(
        body, name=name, grid=(T // tm, nk), out_shape=jax.ShapeDtypeStruct((T, N), F32),
        in_specs=[pl.BlockSpec((tm, tk), lambda i, k: (i, k)), pl.BlockSpec((N, tk), lambda i, k: (0, k))],
        out_specs=pl.BlockSpec((tm, N), lambda i, k: (i, 0)),
        compiler_params=_params("parallel", "arbitrary"),
    )(a, b)


def _mm_tn(a, b, n_col_shards, tk, name):
    T, M = a.shape
    N = b.shape[1]
    tn = N // n_col_shards

    def body(a_ref, b_ref, o_ref):
        k = pl.program_id(1)
        part = lax.dot_general(a_ref[...].astype(BF16), b_ref[...].astype(BF16), TN_DIMS,
                               preferred_element_type=F32)

        @pl.when(k == 0)
        def _():
            o_ref[...] = part

        @pl.when(k > 0)
        def _():
            o_ref[...] += part

    return _pc(
        body, name=name, grid=(n_col_shards, T // tk),
        out_shape=jax.ShapeDtypeStruct((n_col_shards, M, tn), F32),
        in_specs=[pl.BlockSpec((tk, M), lambda j, k: (k, 0)), pl.BlockSpec((tk, tn), lambda j, k: (k, j))],
        out_specs=pl.BlockSpec((None, M, tn), lambda j, k: (j, 0, 0)),
        compiler_params=_params("parallel", "arbitrary"),
    )(a, b)


def _norm_inproj(x, g, w, tm, tn):
    T, D = x.shape
    NW = w.shape[1]

    def body(x_ref, g_ref, w_ref, proj_ref, projb_ref, h_ref, h_sc):
        @pl.when(pl.program_id(1) == 0)
        def _():
            xv = x_ref[...]
            r = lax.rsqrt(jnp.mean(xv * xv, axis=-1, keepdims=True) + RMS_EPS)
            h = ((xv * r) * g_ref[...]).astype(BF16)
            h_sc[...] = h
            h_ref[...] = h

        acc = jnp.dot(h_sc[...], w_ref[...], preferred_element_type=F32)
        proj_ref[...] = acc
        projb_ref[...] = acc.astype(BF16)

    return _pc(
        body, name="norm_inproj", grid=(T // tm, NW // tn),
        out_shape=(jax.ShapeDtypeStruct((T, NW), F32), jax.ShapeDtypeStruct((T, NW), BF16),
                   jax.ShapeDtypeStruct((T, D), BF16)),
        in_specs=[pl.BlockSpec((tm, D), lambda i, j: (i, 0)), pl.BlockSpec((1, D), lambda i, j: (0, 0)),
                  pl.BlockSpec((D, tn), lambda i, j: (0, j))],
        out_specs=(pl.BlockSpec((tm, tn), lambda i, j: (i, j)), pl.BlockSpec((tm, tn), lambda i, j: (i, j)),
                   pl.BlockSpec((tm, D), lambda i, j: (i, 0))),
        scratch_shapes=[pltpu.VMEM((tm, D), BF16)],
        compiler_params=_params("parallel", "arbitrary"),
    )(x, g, w)


def _window_sums(xh, forward):
    n = xh.shape[0]
    sums, s, step = [], xh, 1
    for _ in POOL_WINDOWS:
        s = s + pltpu.roll(s, step if forward else n - step, 0)
        sums.append(s)
        step *= 2
    return sums


def _pooled(u, halo, row):
    sums = _window_sums(jnp.concatenate([halo, u], axis=0), True)
    out = []
    for g, w in enumerate(POOL_WINDOWS):
        cols = slice(g * POOL_G, (g + 1) * POOL_G)
        cnt = jnp.minimum(row + 1, w).astype(F32)
        out.append(sums[g][HALO:, cols] / cnt - u[:, cols])
    return out


def _pool_fwd(proj, pool_w, scale, R):
    T = proj.shape[0]

    def body(u_ref, z_ref, pw_ref, sc_ref, y_ref, halo_sc):
        i = pl.program_id(0)

        @pl.when(i == 0)
        def _():
            halo_sc[...] = jnp.zeros_like(halo_sc)

        u = u_ref[...]
        row = i * R + lax.broadcasted_iota(jnp.int32, (R, 1), 0)
        pooled = _pooled(u, halo_sc[...], row)
        mixed = jnp.concatenate(
            [jnp.dot(pooled[g].astype(BF16), pw_ref[g].astype(BF16), preferred_element_type=F32)
             for g in range(len(POOL_WINDOWS))], axis=1)
        z = z_ref[...]
        y_ref[...] = ((mixed * sc_ref[...]) * (z * _sigmoid(z))).astype(BF16)
        halo_sc[...] = u[R - HALO:, :]

    return _pc(
        body, name="pool_fwd", grid=(T // R,), out_shape=jax.ShapeDtypeStruct((T, POOL_W), BF16),
        in_specs=[pl.BlockSpec((R, POOL_W), lambda i: (i, 0)), pl.BlockSpec((R, POOL_W), lambda i: (i, 1)),
                  pl.BlockSpec((4, POOL_G, POOL_G), lambda i: (0, 0, 0)), pl.BlockSpec((1, POOL_W), lambda i: (0, 0))],
        out_specs=pl.BlockSpec((R, POOL_W), lambda i: (i, 0)),
        scratch_shapes=[pltpu.VMEM((HALO, POOL_W), F32)],
        compiler_params=_params("arbitrary"),
    )(proj, proj, pool_w, scale)


def _softplus(l):
    return jnp.maximum(l, 0.0) + jnp.log(1.0 + jnp.exp(-jnp.abs(l)))


def _head_lanes():
    lane = lax.broadcasted_iota(jnp.int32, (1, LANE), 1)
    return [lane < HEAD_DIM, lane >= HEAD_DIM]


def _head_masks(q, scale):
    qf = q.astype(F32) * scale
    return [jnp.where(m, qf, 0.0).astype(BF16) for m in _head_lanes()]


def _wide(c, width):
    return jnp.concatenate([c] * (width // LANE), axis=1)


def _fill_blocks(src_ref, nk, tk, transposed_sc=None, masked_sc=None):
    heads = _head_lanes()

    def step(j, carry):
        rows = pl.ds(pl.multiple_of(j * tk, tk), tk)
        blk = src_ref[rows, :]
        if transposed_sc is not None:
            transposed_sc[j] = blk.astype(F32).T.astype(BF16)
        if masked_sc is not None:
            for a in range(2):
                masked_sc[a, rows, :] = jnp.where(heads[a], blk, jnp.zeros_like(blk))
        return carry

    lax.fori_loop(0, nk, step, 0)


def _attn_fwd(projb, proj, tq):
    T = projb.shape[0]
    nq = T // tq
    tk, nk = tq, nq
    assert nk <= LANE
    n_pairs = ATTN_W // LANE
    cb = 2 * POOL_W // LANE

    def body(q_ref, k_ref, v_ref, za_ref, o_ref, y_ref, c_ref, kT_sc, vm_sc, *per_head):
        c_sc, cm_sc, o_sc, l_sc, w_sc = (per_head[2 * n:2 * n + 2] for n in range(5))
        i = pl.program_id(1)

        @pl.when(i == 0)
        def _():
            _fill_blocks(k_ref, nk, tk, transposed_sc=kT_sc)
            _fill_blocks(v_ref, nk, tk, masked_sc=vm_sc)

        qs = _head_masks(q_ref[...], HEAD_DIM ** -0.5)
        lane = lax.broadcasted_iota(jnp.int32, (1, LANE), 1)
        valid = lax.broadcasted_iota(jnp.int32, (tq, tk), 1) < lax.broadcasted_iota(jnp.int32, (tq, tk), 0)
        suffix = (lax.broadcasted_iota(jnp.int32, (tk, tk), 0) >= lax.broadcasted_iota(jnp.int32, (tk, tk), 1)).astype(BF16)
        for sc in per_head[:6]:
            sc[...] = jnp.zeros_like(sc)

        def logits(j, slot):
            kT = kT_sc[j]
            for a in range(2):
                l_sc[a][slot] = jnp.dot(qs[a], kT, preferred_element_type=F32)

        def values(j, slot):
            rows = pl.ds(pl.multiple_of(j * tk, tk), tk)
            for a in range(2):
                o_sc[a][...] += jnp.dot(w_sc[a][slot], vm_sc[a, rows, :], preferred_element_type=F32)

        def weights(j, slot, masked):
            sp = []
            for a in range(2):
                x = _softplus(l_sc[a][slot])
                sp.append((jnp.where(valid, x, 0.0) if masked else x).astype(BF16))
            inc = [jnp.dot(sp[a], suffix, preferred_element_type=F32) for a in range(2)]
            for a in range(2):
                c = c_sc[a][...]
                w = jnp.exp(l_sc[a][slot] - inc[a] - _wide(c, tk))
                if masked:
                    w = jnp.where(valid, w, 0.0)
                w_sc[a][slot] = w.astype(BF16)
                cm_sc[a][...] = jnp.where(lane == j, c, cm_sc[a][...])
                c_sc[a][...] = c + inc[a][:, 0:1]

        logits(i, 0)
        logits(jnp.maximum(i - 1, 0), 1)
        weights(i, 0, True)

        def step(t, slot):
            logits(jnp.maximum(i - t - 1, 0), 1 - slot)
            values(i - t + 1, 1 - slot)
            weights(i - t, slot, False)

        def two_steps(tt, carry):
            step(2 * tt + 1, 1)
            step(2 * tt + 2, 0)
            return carry

        lax.fori_loop(0, i // 2, two_steps, 0)

        @pl.when(i % 2 == 1)
        def _():
            step(i, 1)
            values(0, 1)

        @pl.when(i % 2 == 0)
        def _():
            values(0, 0)

        o = o_sc[0][...] + o_sc[1][...]
        o_ref[...] = o
        za = za_ref[...]
        y_ref[...] = (o * (za * _sigmoid(za))).astype(BF16)
        c_ref[0, 0] = cm_sc[0][...]
        c_ref[1, 0] = cm_sc[1][...]

    return _pc(
        body, name="attn_fwd", grid=(n_pairs, nq),
        out_shape=(jax.ShapeDtypeStruct((T, ATTN_W), F32), jax.ShapeDtypeStruct((T, ATTN_W), BF16),
                   jax.ShapeDtypeStruct((2 * n_pairs, nq, tq, LANE), F32)),
        in_specs=[pl.BlockSpec((tq, LANE), lambda p, i: (i, cb + p)),
                  pl.BlockSpec((T, LANE), lambda p, i: (0, cb + n_pairs + p)),
                  pl.BlockSpec((T, LANE), lambda p, i: (0, cb + 2 * n_pairs + p)),
                  pl.BlockSpec((tq, LANE), lambda p, i: (i, cb + 3 * n_pairs + p))],
        out_specs=(pl.BlockSpec((tq, LANE), lambda p, i: (i, p)), pl.BlockSpec((tq, LANE), lambda p, i: (i, p)),
                   pl.BlockSpec((2, 1, tq, LANE), lambda p, i: (p, i, 0, 0))),
        scratch_shapes=[pltpu.VMEM((nk, LANE, tk), BF16), pltpu.VMEM((2, T, LANE), BF16)]
        + [pltpu.VMEM((tq, LANE), F32)] * 6 + [pltpu.VMEM((2, tq, tk), F32)] * 2 + [pltpu.VMEM((2, tq, tk), BF16)] * 2,
        compiler_params=_params("arbitrary", "arbitrary"),
    )(projb, projb, projb, proj)


def _merge_fwd(y_pool, y_attn, w_pu, w_au, proj, b_gate, tm):
    T = y_pool.shape[0]
    D = w_pu.shape[1]
    gb = (2 * POOL_W + 4 * ATTN_W) // D

    def body(yp_ref, ya_ref, wpu_ref, wau_ref, gl0_ref, gl1_ref, bg_ref, m_ref, p_ref, a_ref):
        p = jnp.dot(yp_ref[...], wpu_ref[...], preferred_element_type=F32)
        a = jnp.dot(ya_ref[...], wau_ref[...], preferred_element_type=F32)
        g0 = _sigmoid(gl0_ref[...] + bg_ref[:, :D])
        g1 = _sigmoid(gl1_ref[...] + bg_ref[:, D:])
        m_ref[...] = (g0 * p + g1 * a).astype(BF16)
        p_ref[...] = p
        a_ref[...] = a

    row = lambda i: (i, 0)
    fixed = lambda i: (0, 0)
    return _pc(
        body, name="merge_fwd", grid=(T // tm,),
        out_shape=(jax.ShapeDtypeStruct((T, D), BF16), jax.ShapeDtypeStruct((T, D), F32),
                   jax.ShapeDtypeStruct((T, D), F32)),
        in_specs=[pl.BlockSpec((tm, POOL_W), row), pl.BlockSpec((tm, ATTN_W), row),
                  pl.BlockSpec((POOL_W, D), fixed), pl.BlockSpec((ATTN_W, D), fixed),
                  pl.BlockSpec((tm, D), lambda i: (i, gb)), pl.BlockSpec((tm, D), lambda i: (i, gb + 1)),
                  pl.BlockSpec((1, 2 * D), fixed)],
        out_specs=(pl.BlockSpec((tm, D), row), pl.BlockSpec((tm, D), row), pl.BlockSpec((tm, D), row)),
        compiler_params=_params("parallel"),
    )(y_pool, y_attn, w_pu, w_au, proj, proj, b_gate)


def _final_loss(x, g, target, tm):
    T, D = x.shape

    def body(x_ref, g_ref, t_ref, dx_ref, dg_ref, loss_ref):
        @pl.when(pl.program_id(0) == 0)
        def _():
            dg_ref[...] = jnp.zeros_like(dg_ref)
            loss_ref[...] = jnp.zeros_like(loss_ref)

        xv, gv = x_ref[...], g_ref[...]
        r = lax.rsqrt(jnp.mean(xv * xv, axis=-1, keepdims=True) + RMS_EPS)
        xh = xv * r
        d = xh * gv - t_ref[...]
        loss_ref[...] += 0.5 * jnp.sum(jnp.mean(d * d, axis=-1, keepdims=True), axis=0, keepdims=True)
        dy = d * (1.0 / D)
        dg_ref[...] += jnp.sum(dy * xh, axis=0, keepdims=True)
        dh = dy * gv
        dx_ref[...] = r * (dh - xh * jnp.mean(dh * xh, axis=-1, keepdims=True))

    return _pc(
        body, name="final_loss", grid=(T // tm,),
        out_shape=(jax.ShapeDtypeStruct((T, D), F32), jax.ShapeDtypeStruct((1, D), F32),
                   jax.ShapeDtypeStruct((8, LANE), F32)),
        in_specs=[pl.BlockSpec((tm, D), lambda i: (i, 0)), pl.BlockSpec((1, D), lambda i: (0, 0)),
                  pl.BlockSpec((tm, D), lambda i: (i, 0))],
        out_specs=(pl.BlockSpec((tm, D), lambda i: (i, 0)), pl.BlockSpec((1, D), lambda i: (0, 0)),
                   pl.BlockSpec((8, LANE), lambda i: (0, 0))),
        compiler_params=_params("arbitrary"),
    )(x, g, target)


def _merge_bwd(dm, p, a, proj, b_gate, tm):
    T, D = dm.shape
    gb = (2 * POOL_W + 4 * ATTN_W) // D

    def body(dm_ref, p_ref, a_ref, gl0_ref, gl1_ref, bg_ref, dp_ref, da_ref, dgl_ref, dbg_ref):
        @pl.when(pl.program_id(0) == 0)
        def _():
            dbg_ref[...] = jnp.zeros_like(dbg_ref)

        dmv = dm_ref[...]
        g0 = _sigmoid(gl0_ref[...] + bg_ref[:, :D])
        g1 = _sigmoid(gl1_ref[...] + bg_ref[:, D:])
        dp_ref[...] = (dmv * g0).astype(BF16)
        da_ref[...] = (dmv * g1).astype(BF16)
        dgl0 = dmv * p_ref[...] * (g0 * (1.0 - g0))
        dgl1 = dmv * a_ref[...] * (g1 * (1.0 - g1))
        dgl_ref[:, :D] = dgl0.astype(BF16)
        dgl_ref[:, D:] = dgl1.astype(BF16)
        dbg_ref[:, :D] += jnp.sum(dgl0, axis=0, keepdims=True)
        dbg_ref[:, D:] += jnp.sum(dgl1, axis=0, keepdims=True)

    row = lambda i: (i, 0)
    fixed = lambda i: (0, 0)
    return _pc(
        body, name="merge_bwd", grid=(T // tm,),
        out_shape=(jax.ShapeDtypeStruct((T, D), BF16), jax.ShapeDtypeStruct((T, D), BF16),
                   jax.ShapeDtypeStruct((T, 2 * D), BF16), jax.ShapeDtypeStruct((1, 2 * D), F32)),
        in_specs=[pl.BlockSpec((tm, D), row), pl.BlockSpec((tm, D), row), pl.BlockSpec((tm, D), row),
                  pl.BlockSpec((tm, D), lambda i: (i, gb)), pl.BlockSpec((tm, D), lambda i: (i, gb + 1)),
                  pl.BlockSpec((1, 2 * D), fixed)],
        out_specs=(pl.BlockSpec((tm, D), row), pl.BlockSpec((tm, D), row), pl.BlockSpec((tm, 2 * D), row),
                   pl.BlockSpec((1, 2 * D), fixed)),
        compiler_params=_params("arbitrary"),
    )(dm, p, a, proj, proj, b_gate)


def _pool_bwd(proj, dy, pool_w, scale, R):
    T = proj.shape[0]
    nb = T // R
    hb = R // HALO

    def body(u_ref, up_ref, z_ref, dy_ref, pw_ref, sc_ref, du_ref, dz_ref, dpw_ref, dsc_ref, halo_sc):
        i = pl.program_id(0)
        rb = nb - 1 - i

        @pl.when(i == 0)
        def _():
            halo_sc[...] = jnp.zeros_like(halo_sc)
            dpw_ref[...] = jnp.zeros_like(dpw_ref)
            dsc_ref[...] = jnp.zeros_like(dsc_ref)

        u = u_ref[...]
        row = rb * R + lax.broadcasted_iota(jnp.int32, (R, 1), 0)
        before = jnp.where(rb > 0, up_ref[...], 0.0)
        pooled = _pooled(u, before, row)
        pw = [pw_ref[g].astype(BF16) for g in range(len(POOL_WINDOWS))]
        mixed = jnp.concatenate(
            [jnp.dot(pooled[g].astype(BF16), pw[g], preferred_element_type=F32) for g in range(len(POOL_WINDOWS))],
            axis=1)
        sc = sc_ref[...]
        silu, dsilu = _silu_and_grad(z_ref[...])
        dyv = dy_ref[...]
        dmp = dyv * silu
        dz_ref[...] = (dyv * (mixed * sc) * dsilu).astype(BF16)
        dsc_ref[...] += jnp.sum(dmp * mixed, axis=0, keepdims=True)
        dmixed = (dmp * sc).astype(BF16)
        dpn = []
        dpooled = []
        for g, w in enumerate(POOL_WINDOWS):
            cols = slice(g * POOL_G, (g + 1) * POOL_G)
            dpw_ref[g] += lax.dot_general(pooled[g].astype(BF16), dmixed[:, cols], TN_DIMS,
                                          preferred_element_type=F32)
            dpg = lax.dot_general(dmixed[:, cols], pw[g], NT_DIMS, preferred_element_type=F32)
            dpooled.append(dpg)
            dpn.append(dpg / jnp.minimum(row + 1, w).astype(F32))
        dpn = jnp.concatenate(dpn, axis=1)
        sums = _window_sums(jnp.concatenate([dpn, halo_sc[...]], axis=0), False)
        du_ref[...] = jnp.concatenate(
            [sums[g][:R, g * POOL_G:(g + 1) * POOL_G] - dpooled[g] for g in range(len(POOL_WINDOWS))],
            axis=1).astype(BF16)
        halo_sc[...] = dpn[:HALO, :]

    rev = lambda i: (nb - 1 - i, 0)
    return _pc(
        body, name="pool_bwd", grid=(nb,),
        out_shape=(jax.ShapeDtypeStruct((T, POOL_W), BF16), jax.ShapeDtypeStruct((T, POOL_W), BF16),
                   jax.ShapeDtypeStruct((4, POOL_G, POOL_G), F32), jax.ShapeDtypeStruct((1, POOL_W), F32)),
        in_specs=[pl.BlockSpec((R, POOL_W), rev),
                  pl.BlockSpec((HALO, POOL_W), lambda i: (jnp.maximum((nb - 1 - i) * hb - 1, 0), 0)),
                  pl.BlockSpec((R, POOL_W), lambda i: (nb - 1 - i, 1)),
                  pl.BlockSpec((R, POOL_W), rev),
                  pl.BlockSpec((4, POOL_G, POOL_G), lambda i: (0, 0, 0)), pl.BlockSpec((1, POOL_W), lambda i: (0, 0))],
        out_specs=(pl.BlockSpec((R, POOL_W), rev), pl.BlockSpec((R, POOL_W), rev),
                   pl.BlockSpec((4, POOL_G, POOL_G), lambda i: (0, 0, 0)), pl.BlockSpec((1, POOL_W), lambda i: (0, 0))),
        scratch_shapes=[pltpu.VMEM((HALO, POOL_W), F32)],
        compiler_params=_params("arbitrary"),
    )(proj, proj, proj, dy, pool_w, scale)


def _attn_gate_bwd(dya, o, proj, tm):
    T = dya.shape[0]
    zb = (2 * POOL_W + 3 * ATTN_W) // ATTN_W

    def body(dy_ref, o_ref, za_ref, do_ref, dza_ref):
        silu, dsilu = _silu_and_grad(za_ref[...])
        dyv = dy_ref[...]
        do_ref[...] = (dyv * silu).astype(BF16)
        dza_ref[...] = (dyv * o_ref[...] * dsilu).astype(BF16)

    row = lambda i: (i, 0)
    return _pc(
        body, name="attn_gate_bwd", grid=(T // tm,),
        out_shape=(jax.ShapeDtypeStruct((T, ATTN_W), BF16), jax.ShapeDtypeStruct((T, ATTN_W), BF16)),
        in_specs=[pl.BlockSpec((tm, ATTN_W), row), pl.BlockSpec((tm, ATTN_W), row),
                  pl.BlockSpec((tm, ATTN_W), lambda i: (i, zb))],
        out_specs=(pl.BlockSpec((tm, ATTN_W), row), pl.BlockSpec((tm, ATTN_W), row)),
        compiler_params=_params("parallel"),
    )(dya, o, proj)


def _attn_bwd(projb, do, carries, tq):
    T = projb.shape[0]
    nq = T // tq
    tk, nk = tq, nq
    n_pairs = ATTN_W // LANE
    cb = 2 * POOL_W // LANE
    scale = HEAD_DIM ** -0.5

    def body(q_ref, k_ref, v_ref, do_ref, c_ref, dq_ref, dkT_ref, dvT_ref, kT_sc, vT_sc, km_sc, *per_head):
        f_sc, dq_sc, l_sc, dw_sc, dl_sc, w_sc = (per_head[2 * n:2 * n + 2] for n in range(6))
        i = pl.program_id(1)

        @pl.when(i == 0)
        def _():
            _fill_blocks(k_ref, nk, tk, transposed_sc=kT_sc, masked_sc=km_sc)
            _fill_blocks(v_ref, nk, tk, transposed_sc=vT_sc)
            dkT_ref[...] = jnp.zeros_like(dkT_ref)
            dvT_ref[...] = jnp.zeros_like(dvT_ref)

        qs = _head_masks(q_ref[...], scale)
        dos = _head_masks(do_ref[...], 1.0)
        qT = [x.astype(F32).T.astype(BF16) for x in qs]
        doT = [x.astype(F32).T.astype(BF16) for x in dos]
        lane = lax.broadcasted_iota(jnp.int32, (1, LANE), 1)
        valid = lax.broadcasted_iota(jnp.int32, (tq, tk), 1) < lax.broadcasted_iota(jnp.int32, (tq, tk), 0)
        kk0 = lax.broadcasted_iota(jnp.int32, (tk, tk), 0)
        kk1 = lax.broadcasted_iota(jnp.int32, (tk, tk), 1)
        suffix = (kk0 >= kk1).astype(BF16)
        prefix = (kk0 <= kk1).astype(BF16)
        for a in range(2):
            f_sc[a][...] = jnp.zeros_like(f_sc[a])
            dq_sc[a][...] = jnp.zeros_like(dq_sc[a])
            dl_sc[a][1] = jnp.zeros((tq, tk), BF16)
            w_sc[a][1] = jnp.zeros((tq, tk), BF16)

        def products(j, slot):
            kT = kT_sc[j]
            vT = vT_sc[j]
            for a in range(2):
                l_sc[a][slot] = jnp.dot(qs[a], kT, preferred_element_type=F32)
                dw_sc[a][slot] = jnp.dot(dos[a], vT, preferred_element_type=F32)

        def gradients(j, slot):
            rows = pl.ds(pl.multiple_of(j * tk, tk), tk)
            dkT = []
            dvT = []
            for a in range(2):
                dlb = dl_sc[a][slot]
                dq_sc[a][...] += jnp.dot(dlb, km_sc[a, rows, :], preferred_element_type=F32)
                dkT.append(jnp.dot(qT[a], dlb, preferred_element_type=F32))
                dvT.append(jnp.dot(doT[a], w_sc[a][slot], preferred_element_type=F32))
            dkT_ref[j] += dkT[0] + dkT[1]
            dvT_ref[j] += dvT[0] + dvT[1]

        def elementwise(j, slot, masked):
            sp, inc, e, beta, p = [None] * 2, [None] * 2, [None] * 2, [None] * 2, [None] * 2
            for a in range(2):
                x = _softplus(l_sc[a][slot])
                sp[a] = jnp.where(valid, x, 0.0) if masked else x
            for a in range(2):
                inc[a] = jnp.dot(sp[a].astype(BF16), suffix, preferred_element_type=F32)
            for a in range(2):
                l = l_sc[a][slot]
                c = jnp.sum(jnp.where(lane == j, c_ref[a, 0], 0.0), axis=1, keepdims=True)
                w = jnp.exp(l - inc[a] - c)
                if masked:
                    w = jnp.where(valid, w, 0.0)
                w_sc[a][slot] = w.astype(BF16)
                beta[a] = jnp.exp(l - sp[a])
                e[a] = w * dw_sc[a][slot]
            for a in range(2):
                p[a] = jnp.dot(e[a].astype(BF16), prefix, preferred_element_type=F32)
            for a in range(2):
                f = f_sc[a][...]
                dl = e[a] - beta[a] * (p[a] + _wide(f, tk))
                if masked:
                    dl = jnp.where(valid, dl, 0.0)
                dl_sc[a][slot] = dl.astype(BF16)
                f_sc[a][...] = f + p[a][:, tk - 1:tk]

        def step(t, slot):
            products(t + 1, 1 - slot)
            gradients(jnp.maximum(t - 1, 0), 1 - slot)
            elementwise(t, slot, False)

        def last(slot):
            gradients(jnp.maximum(i - 1, 0), 1 - slot)
            elementwise(i, slot, True)
            gradients(i, slot)

        products(0, 0)

        def two_steps(tt, carry):
            step(2 * tt, 0)
            step(2 * tt + 1, 1)
            return carry

        lax.fori_loop(0, i // 2, two_steps, 0)

        @pl.when(i % 2 == 1)
        def _():
            step(i - 1, 0)
            last(1)

        @pl.when(i % 2 == 0)
        def _():
            last(0)

        dq_ref[...] = ((dq_sc[0][...] + dq_sc[1][...]) * scale).astype(BF16)

    return _pc(
        body, name="attn_bwd", grid=(n_pairs, nq),
        out_shape=(jax.ShapeDtypeStruct((T, ATTN_W), BF16), jax.ShapeDtypeStruct((n_pairs, nk, LANE, tk), F32),
                   jax.ShapeDtypeStruct((n_pairs, nk, LANE, tk), F32)),
        in_specs=[pl.BlockSpec((tq, LANE), lambda p, i: (i, cb + p)),
                  pl.BlockSpec((T, LANE), lambda p, i: (0, cb + n_pairs + p)),
                  pl.BlockSpec((T, LANE), lambda p, i: (0, cb + 2 * n_pairs + p)),
                  pl.BlockSpec((tq, LANE), lambda p, i: (i, p)),
                  pl.BlockSpec((2, 1, tq, LANE), lambda p, i: (p, i, 0, 0))],
        out_specs=(pl.BlockSpec((tq, LANE), lambda p, i: (i, p)),
                   pl.BlockSpec((None, nk, LANE, tk), lambda p, i: (p, 0, 0, 0)),
                   pl.BlockSpec((None, nk, LANE, tk), lambda p, i: (p, 0, 0, 0))),
        scratch_shapes=[pltpu.VMEM((nk, LANE, tk), BF16), pltpu.VMEM((nk, LANE, tk), BF16),
                        pltpu.VMEM((2, T, LANE), BF16)]
        + [pltpu.VMEM((tq, LANE), F32)] * 4 + [pltpu.VMEM((2, tq, tk), F32)] * 4 + [pltpu.VMEM((2, tq, tk), BF16)] * 4,
        compiler_params=_params("arbitrary", "arbitrary"),
    )(projb, projb, projb, do, carries)


def _norm_bwd(dh, x, g, dxo, tm):
    T, D = x.shape

    def body(dh_ref, x_ref, g_ref, dxo_ref, dx_ref, dg_ref):
        @pl.when(pl.program_id(0) == 0)
        def _():
            dg_ref[...] = jnp.zeros_like(dg_ref)

        xv = x_ref[...]
        r = lax.rsqrt(jnp.mean(xv * xv, axis=-1, keepdims=True) + RMS_EPS)
        xh = xv * r
        dhv = dh_ref[...]
        dg_ref[...] += jnp.sum(dhv * xh, axis=0, keepdims=True)
        dhg = dhv * g_ref[...]
        dx_ref[...] = dxo_ref[...] + r * (dhg - xh * jnp.mean(dhg * xh, axis=-1, keepdims=True))

    row = lambda i: (i, 0)
    return _pc(
        body, name="norm_bwd", grid=(T // tm,),
        out_shape=(jax.ShapeDtypeStruct((T, D), F32), jax.ShapeDtypeStruct((1, D), F32)),
        in_specs=[pl.BlockSpec((tm, D), row), pl.BlockSpec((tm, D), row), pl.BlockSpec((1, D), lambda i: (0, 0)),
                  pl.BlockSpec((tm, D), row)],
        out_specs=(pl.BlockSpec((tm, D), row), pl.BlockSpec((1, D), lambda i: (0, 0))),
        compiler_params=_params("arbitrary"),
    )(dh, x, g, dxo)


def _adamw(pieces, w, m, v, name):
    rows, cols = w.shape
    br = rows
    while br * cols > 65536 and br % 16 == 0:
        br //= 2
    c1 = 1.0 / (1.0 - ADAM_B1 ** ADAM_STEP)
    c2 = 1.0 / (1.0 - ADAM_B2 ** ADAM_STEP)

    def body(p_ref, w_ref, m_ref, v_ref, g_ref, d_ref, nm_ref, nv_ref):
        g = p_ref[0]
        for s in range(1, N_DEV):
            g = g + p_ref[s]
        nm = ADAM_B1 * m_ref[...] + (1.0 - ADAM_B1) * g
        nv = ADAM_B2 * v_ref[...] + (1.0 - ADAM_B2) * (g * g)
        g_ref[...] = g
        nm_ref[...] = nm
        nv_ref[...] = nv
        d_ref[...] = -ADAM_LR * ((nm * c1) / (jnp.sqrt(nv * c2) + ADAM_EPS) + ADAM_WD * w_ref[...])

    blk = pl.BlockSpec((br, cols), lambda i: (i, 0))
    shape = jax.ShapeDtypeStruct((rows, cols), F32)
    return _pc(
        body, name=name, grid=(rows // br,), out_shape=(shape, shape, shape, shape),
        in_specs=[pl.BlockSpec((N_DEV, br, cols), lambda i: (0, i, 0)), blk, blk, blk],
        out_specs=(blk, blk, blk, blk),
        compiler_params=_params("parallel"),
    )(pieces, w, m, v)


def _rows128(a):
    flat = a.reshape(-1)
    n = flat.shape[0]
    padded = -(-n // (8 * LANE)) * (8 * LANE)
    if padded != n:
        flat = jnp.concatenate([flat, jnp.zeros((padded - n,), flat.dtype)])
    return flat.reshape(-1, LANE)


def _pack(parts):
    return jnp.concatenate([_rows128(p) for p in parts], axis=0)


def _unpack(packed, like):
    out, r = [], 0
    for a in like:
        n = a.size
        nr = -(-n // (8 * LANE)) * 8
        out.append(packed[r:r + nr].reshape(-1)[:n].reshape(a.shape))
        r += nr
    return out


def kernel(x, norm_g, w_in, b_gate, pool_w, pool_scale, w_pool_up, w_attn_up, w_out, final_g, loss_target, m_norm_g, m_w_in, m_b_gate, m_pool_w, m_pool_scale, m_w_pool_up, m_w_attn_up, m_w_out, m_final_g, v_norm_g, v_w_in, v_b_gate, v_pool_w, v_pool_scale, v_w_pool_up, v_w_attn_up, v_w_out, v_final_g):
    L = norm_g.shape[0]
    T, D = x.shape[1], x.shape[2]
    NW = w_in.shape[2] * N_DEV
    assert NW == 2 * POOL_W + 4 * ATTN_W + 2 * D and x.shape[0] == 1
    tm = min(512, T)
    tq = min(256, T // 2)
    x0 = x.reshape(T, D)
    target = loss_target.reshape(T, D)

    g_in, g_pu, g_au, g_out = _exchange(
        [w_in.astype(BF16), w_pool_up.astype(BF16), w_attn_up.astype(BF16), w_out.astype(BF16)],
        [False] * 4, "gather_weights")
    win_full = jnp.transpose(g_in, (1, 2, 0, 3)).reshape(L, D, NW)
    wpu_full = jnp.transpose(g_pu, (1, 2, 0, 3)).reshape(L, POOL_W, D)
    wau_full = jnp.transpose(g_au, (1, 2, 0, 3)).reshape(L, ATTN_W, D)
    wout_full = jnp.transpose(g_out, (1, 0, 2, 3)).reshape(L, D, D)

    saved = []
    xl = x0
    for l in range(L):
        proj, projb, h = _norm_inproj(xl, norm_g[l:l + 1], win_full[l], tm, 512)
        y_pool = _pool_fwd(proj, pool_w[l], pool_scale[l:l + 1], tm)
        o, y_attn, carries = _attn_fwd(projb, proj, tq)
        merged, p, a = _merge_fwd(y_pool, y_attn, wpu_full[l], wau_full[l], proj, b_gate[l:l + 1], min(256, T))
        x_next = _mm_nn_res(merged, wout_full[l], xl, tm, "out_proj")
        saved.append((xl, proj, projb, h, y_pool, o, y_attn, carries, merged, p, a))
        xl = x_next

    dx, d_final_g, loss_part = _final_loss(xl, final_g.reshape(1, D), target, tm)

    d_norm_g, d_b_gate, d_pool_w, d_pool_scale = [None] * L, [None] * L, [None] * L, [None] * L
    d_win, d_wpu, d_wau, d_wout = [None] * L, [None] * L, [None] * L, [None] * L
    for l in reversed(range(L)):
        xin, proj, projb, h, y_pool, o, y_attn, carries, merged, p, a = saved[l]
        dm = _mm_nt(dx, wout_full[l], tm, D, "d_merged")
        d_wout[l] = _mm_tn(merged, dx, 1, tm, "d_w_out").reshape(N_DEV, D // N_DEV, D)
        dp, da, dgl, d_b_gate[l] = _merge_bwd(dm, p, a, proj, b_gate[l:l + 1], min(256, T))
        d_wpu[l] = _mm_tn(y_pool, dp, N_DEV, tm, "d_w_pool_up")
        d_wau[l] = _mm_tn(y_attn, da, N_DEV, tm, "d_w_attn_up")
        dyp = _mm_nt(dp, wpu_full[l], tm, D, "d_y_pool")
        dya = _mm_nt(da, wau_full[l], tm, D, "d_y_attn")
        du, dzp, d_pool_w[l], d_pool_scale[l] = _pool_bwd(proj, dyp, pool_w[l], pool_scale[l:l + 1], tm)
        do, dza = _attn_gate_bwd(dya, o, proj, tm)
        dq, dkT, dvT = _attn_bwd(projb, do, carries, tq)
        dk, dv = [jnp.transpose(t, (1, 3, 0, 2)).reshape(T, ATTN_W).astype(BF16) for t in (dkT, dvT)]
        dproj = jnp.concatenate([du, dzp, dq, dk, dv, dza, dgl], axis=1)
        d_win[l] = _mm_tn(h, dproj, N_DEV, tm, "d_w_in")
        dh = _mm_nt(dproj, win_full[l], min(256, T), 1024 if NW % 1024 == 0 else NW, "d_h")
        dx, d_norm_g[l] = _norm_bwd(dh, xin, norm_g[l:l + 1], dx, tm)

    small_like = [norm_g, b_gate, pool_w, pool_scale, final_g, jnp.zeros((8, LANE), F32)]
    small = _pack([jnp.concatenate(d_norm_g, 0), jnp.concatenate(d_b_gate, 0), jnp.stack(d_pool_w, 0),
                   jnp.concatenate(d_pool_scale, 0), d_final_g, loss_part])
    sends = [small]
    flags = [False]
    for l in range(L):
        sends += [d_win[l], d_wpu[l], d_wau[l], d_wout[l]]
        flags += [True] * 4
    got = _exchange(sends, flags, "scatter_grads")
    r_small = got[0]
    r_in = jnp.stack([got[1 + 4 * l] for l in range(L)], axis=1)
    r_pu = jnp.stack([got[2 + 4 * l] for l in range(L)], axis=1)
    r_au = jnp.stack([got[3 + 4 * l] for l in range(L)], axis=1)
    r_out = jnp.stack([got[4 + 4 * l] for l in range(L)], axis=1)

    def update(pieces, w, m, v, name):
        cols = w.shape[-1]
        res = _adamw(pieces.reshape(N_DEV, -1, cols), w.reshape(-1, cols), m.reshape(-1, cols),
                     v.reshape(-1, cols), name)
        return [r.reshape(w.shape) for r in res]

    u_in = update(r_in, w_in, m_w_in, v_w_in, "adamw_w_in")
    u_pu = update(r_pu, w_pool_up, m_w_pool_up, v_w_pool_up, "adamw_w_pool_up")
    u_au = update(r_au, w_attn_up, m_w_attn_up, v_w_attn_up, "adamw_w_attn_up")
    u_out = update(r_out, w_out, m_w_out, v_w_out, "adamw_w_out")
    zeros = small_like[-1]
    smalls = _adamw(r_small,
                    _pack([norm_g, b_gate, pool_w, pool_scale, final_g, zeros]),
                    _pack([m_norm_g, m_b_gate, m_pool_w, m_pool_scale, m_final_g, zeros]),
                    _pack([v_norm_g, v_b_gate, v_pool_w, v_pool_scale, v_final_g, zeros]), "adamw_small")
    s_g, s_d, s_m, s_v = [_unpack(s, small_like) for s in smalls]
    loss = s_g[5][0, 0]

    def ordered(k):
        s = (s_g, s_d, s_m, s_v)[k]
        return [s[0], u_in[k], s[1], s[2], s[3], u_pu[k], u_au[k], u_out[k], s[4]]

    return (loss, dx.reshape(x.shape), *ordered(0), *ordered(1), *ordered(2), *ordered(3))
```

```python
import jax
import jax.numpy as jnp
from jax import lax
from jax.experimental import pallas as pl
from jax.experimental.pallas import tpu as pltpu

F32 = jnp.float32
BF16 = jnp.bfloat16

N_DEV = 8
HEAD_DIM = 64
ATTN_W = 512
POOL_W = 512
POOL_G = 128
POOL_WINDOWS = (2, 4, 8, 16)
HALO = 16
LANE = 128
RMS_EPS = 1e-6
ADAM_LR, ADAM_B1, ADAM_B2, ADAM_EPS, ADAM_WD, ADAM_STEP = 0.001, 0.9, 0.999, 1e-08, 0.01, 10
VMEM_LIMIT = 56 * 1024 * 1024

NT_DIMS = (((1,), (1,)), ((), ()))
TN_DIMS = (((0,), (0,)), ((), ()))


def _pc(body, **kw):
    return pl.pallas_call(body, **kw)


def _params(*sem):
    return pltpu.CompilerParams(dimension_semantics=sem, vmem_limit_bytes=VMEM_LIMIT)


def _sigmoid(z):
    return 1.0 / (1.0 + jnp.exp(-z))


def _silu_and_grad(z):
    s = _sigmoid(z)
    return z * s, s * (1.0 + z * (1.0 - s))


def _my_index():
    return 4 * lax.axis_index("x") + 2 * lax.axis_index("y") + lax.axis_index("c")


def _peer(k):
    x, y, c = lax.axis_index("x"), lax.axis_index("y"), lax.axis_index("c")
    px = lax.rem(x + ((k >> 2) & 1), 2)
    py = lax.rem(y + ((k >> 1) & 1), 2)
    pc = lax.rem(c + (k & 1), 2)
    return (px, py, pc), 4 * px + 2 * py + pc


def _exchange(arrays, scatter, name):
    n = len(arrays)

    def body(*refs):
        ins, outs = refs[:n], refs[n:2 * n]
        send_sems, recv_sems, local_sems = refs[2 * n:]
        me = _my_index()

        def src(a, idx):
            return ins[a].at[idx] if scatter[a] else ins[a]

        local = [pltpu.make_async_copy(src(a, me), outs[a].at[me], local_sems.at[a]) for a in range(n)]
        for cp in local:
            cp.start()
        sends = []
        order = (1, 2, 4, 3, 5, 6, 7)
        for k in order:
            dev, pidx = _peer(k)
            for a in range(n):
                cp = pltpu.make_async_remote_copy(
                    src_ref=src(a, pidx), dst_ref=outs[a].at[me],
                    send_sem=send_sems.at[a * N_DEV + k], recv_sem=recv_sems.at[a * N_DEV + k],
                    device_id=dev, device_id_type=pl.DeviceIdType.MESH)
                cp.start()
                sends.append(cp)
        for k in order:
            dev, pidx = _peer(k)
            for a in range(n):
                pltpu.make_async_remote_copy(
                    src_ref=src(a, pidx), dst_ref=outs[a].at[pidx],
                    send_sem=send_sems.at[a * N_DEV + k], recv_sem=recv_sems.at[a * N_DEV + k],
                    device_id=dev, device_id_type=pl.DeviceIdType.MESH).wait_recv()
        for cp in sends:
            cp.wait_send()
        for cp in local:
            cp.wait()

    out_shape = []
    for a, s in zip(arrays, scatter):
        piece = a.shape[1:] if s else a.shape
        out_shape.append(jax.ShapeDtypeStruct((N_DEV,) + tuple(piece), a.dtype))
    any_spec = pl.BlockSpec(memory_space=pl.ANY)
    return _pc(
        body, name=name, out_shape=tuple(out_shape),
        in_specs=[any_spec] * n, out_specs=tuple([any_spec] * n),
        scratch_shapes=[pltpu.SemaphoreType.DMA((n * N_DEV,)), pltpu.SemaphoreType.DMA((n * N_DEV,)),
                        pltpu.SemaphoreType.DMA((n,))],
    )(*arrays)


def _mm_nn_res(a, b, res, tm, name):
    T, K = a.shape
    N = b.shape[1]

    def body(a_ref, b_ref, r_ref, o_ref):
        o_ref[...] = r_ref[...] + jnp.dot(a_ref[...], b_ref[...], preferred_element_type=F32)

    return _pc(
        body, name=name, grid=(T // tm,), out_shape=jax.ShapeDtypeStruct((T, N), F32),
        in_specs=[pl.BlockSpec((tm, K), lambda i: (i, 0)), pl.BlockSpec((K, N), lambda i: (0, 0)),
                  pl.BlockSpec((tm, N), lambda i: (i, 0))],
        out_specs=pl.BlockSpec((tm, N), lambda i: (i, 0)),
        compiler_params=_params("parallel"),
    )(a, b, res)


def _mm_nt(a, b, tm, tk, name):
    T, K = a.shape
    N = b.shape[0]
    nk = K // tk

    def body(a_ref, b_ref, o_ref):
        part = lax.dot_general(a_ref[...].astype(BF16), b_ref[...], NT_DIMS, preferred_element_type=F32)
        if nk == 1:
            o_ref[...] = part
        else:
            k = pl.program_id(1)

            @pl.when(k == 0)
            def _():
                o_ref[...] = part

            @pl.when(k > 0)
            def _():
                o_ref[...] += part

    return _pc(
        body, name=name, grid=(T // tm, nk), out_shape=jax.ShapeDtypeStruct((T, N), F32),
        in_specs=[pl.BlockSpec((tm, tk), lambda i, k: (i, k)), pl.BlockSpec((N, tk), lambda i, k: (0, k))],
        out_specs=pl.BlockSpec((tm, N), lambda i, k: (i, 0)),
        compiler_params=_params("parallel", "arbitrary"),
    )(a, b)


def _mm_tn(a, b, n_col_shards, tn, tk, name):
    T, M = a.shape
    N = b.shape[1]
    sw = N // n_col_shards
    per_step = tn // sw
    nk = T // tk

    def body(a_ref, b_ref, o_ref, acc_sc):
        k = pl.program_id(1)
        part = lax.dot_general(a_ref[...].astype(BF16), b_ref[...].astype(BF16), TN_DIMS,
                               preferred_element_type=F32)

        @pl.when(k == 0)
        def _():
            acc_sc[...] = part

        @pl.when(k > 0)
        def _():
            acc_sc[...] += part

        @pl.when(k == nk - 1)
        def _():
            for s in range(per_step):
                o_ref[s] = acc_sc[:, s * sw:(s + 1) * sw].astype(BF16)

    return _pc(
        body, name=name, grid=(N // tn, nk),
        out_shape=jax.ShapeDtypeStruct((n_col_shards, M, sw), BF16),
        in_specs=[pl.BlockSpec((tk, M), lambda j, k: (k, 0)), pl.BlockSpec((tk, tn), lambda j, k: (k, j))],
        out_specs=pl.BlockSpec((per_step, M, sw), lambda j, k: (j, 0, 0)),
        scratch_shapes=[pltpu.VMEM((M, tn), F32)],
        compiler_params=_params("parallel", "arbitrary"),
    )(a, b)


def _proj_layout(D):
    return {"u": 0, "z_pool": POOL_W, "gates": 2 * POOL_W, "z_attn": 2 * POOL_W + 2 * D, "width": 2 * POOL_W + 2 * D + ATTN_W}


def _norm_inproj(x, g, w, tm):
    T, D = x.shape
    NW = w.shape[1]
    lay = _proj_layout(D)
    qkv0, za0, gl0 = 2 * POOL_W, 2 * POOL_W + 3 * ATTN_W, 2 * POOL_W + 4 * ATTN_W

    def body(x_ref, g_ref, w_ref, proj_ref, qkv_ref, h_ref):
        xv = x_ref[...]
        r = lax.rsqrt(jnp.mean(xv * xv, axis=-1, keepdims=True) + RMS_EPS)
        h = ((xv * r) * g_ref[...]).astype(BF16)
        h_ref[...] = h

        def cols(lo, hi):
            return jnp.dot(h, w_ref[:, lo:hi], preferred_element_type=F32)

        proj_ref[:, :lay["gates"]] = cols(0, qkv0)
        qkv_ref[...] = cols(qkv0, za0).astype(BF16)
        proj_ref[:, lay["gates"]:lay["z_attn"]] = cols(gl0, NW)
        proj_ref[:, lay["z_attn"]:] = cols(za0, gl0)

    return _pc(
        body, name="norm_inproj", grid=(T // tm,),
        out_shape=(jax.ShapeDtypeStruct((T, lay["width"]), F32), jax.ShapeDtypeStruct((T, 3 * ATTN_W), BF16),
                   jax.ShapeDtypeStruct((T, D), BF16)),
        in_specs=[pl.BlockSpec((tm, D), lambda i: (i, 0)), pl.BlockSpec((1, D), lambda i: (0, 0)),
                  pl.BlockSpec((D, NW), lambda i: (0, 0))],
        out_specs=(pl.BlockSpec((tm, lay["width"]), lambda i: (i, 0)), pl.BlockSpec((tm, 3 * ATTN_W), lambda i: (i, 0)),
                   pl.BlockSpec((tm, D), lambda i: (i, 0))),
        compiler_params=_params("parallel"),
    )(x, g, w)


def _window_sums(xh, forward):
    n = xh.shape[0]
    sums, s, step = [], xh, 1
    for _ in POOL_WINDOWS:
        s = s + pltpu.roll(s, step if forward else n - step, 0)
        sums.append(s)
        step *= 2
    return sums


def _pooled(u, halo, row):
    sums = _window_sums(jnp.concatenate([halo, u], axis=0), True)
    out = []
    for g, w in enumerate(POOL_WINDOWS):
        cols = slice(g * POOL_G, (g + 1) * POOL_G)
        cnt = jnp.minimum(row + 1, w).astype(F32)
        out.append(sums[g][HALO:, cols] / cnt - u[:, cols])
    return out


def _pool_fwd(proj, pool_w, scale, R):
    T = proj.shape[0]

    def body(u_ref, z_ref, pw_ref, sc_ref, y_ref, halo_sc):
        i = pl.program_id(0)

        @pl.when(i == 0)
        def _():
            halo_sc[...] = jnp.zeros_like(halo_sc)

        u = u_ref[...]
        row = i * R + lax.broadcasted_iota(jnp.int32, (R, 1), 0)
        pooled = _pooled(u, halo_sc[...], row)
        mixed = jnp.concatenate(
            [jnp.dot(pooled[g].astype(BF16), pw_ref[g].astype(BF16), preferred_element_type=F32)
             for g in range(len(POOL_WINDOWS))], axis=1)
        z = z_ref[...]
        y_ref[...] = ((mixed * sc_ref[...]) * (z * _sigmoid(z))).astype(BF16)
        halo_sc[...] = u[R - HALO:, :]

    return _pc(
        body, name="pool_fwd", grid=(T // R,), out_shape=jax.ShapeDtypeStruct((T, POOL_W), BF16),
        in_specs=[pl.BlockSpec((R, POOL_W), lambda i: (i, 0)), pl.BlockSpec((R, POOL_W), lambda i: (i, 1)),
                  pl.BlockSpec((4, POOL_G, POOL_G), lambda i: (0, 0, 0)), pl.BlockSpec((1, POOL_W), lambda i: (0, 0))],
        out_specs=pl.BlockSpec((R, POOL_W), lambda i: (i, 0)),
        scratch_shapes=[pltpu.VMEM((HALO, POOL_W), F32)],
        compiler_params=_params("arbitrary"),
    )(proj, proj, pool_w, scale)


def _softplus(l):
    return jnp.maximum(l, 0.0) + jnp.log(1.0 + jnp.exp(-jnp.abs(l)))


def _head_lanes():
    lane = lax.broadcasted_iota(jnp.int32, (1, LANE), 1)
    return [lane < HEAD_DIM, lane >= HEAD_DIM]


def _head_masks(q, scale):
    qf = q.astype(F32) * scale
    return [jnp.where(m, qf, 0.0).astype(BF16) for m in _head_lanes()]


def _wide(c, width):
    return jnp.concatenate([c] * (width // LANE), axis=1)


def _fill_blocks(src_ref, nk, tk, transposed_sc=None, masked_sc=None):
    heads = _head_lanes()

    def step(j, carry):
        rows = pl.ds(pl.multiple_of(j * tk, tk), tk)
        blk = src_ref[rows, :]
        if transposed_sc is not None:
            transposed_sc[j] = blk.astype(F32).T.astype(BF16)
        if masked_sc is not None:
            for a in range(2):
                masked_sc[a, rows, :] = jnp.where(heads[a], blk, jnp.zeros_like(blk))
        return carry

    lax.fori_loop(0, nk, step, 0)


def _attn_fwd(projb, proj, tq):
    T = projb.shape[0]
    nq = T // tq
    tk, nk = tq, nq
    assert nk <= LANE
    n_pairs = ATTN_W // LANE
    zb = (proj.shape[1] - ATTN_W) // LANE

    def body(q_ref, k_ref, v_ref, za_ref, o_ref, y_ref, c_ref, kT_sc, vm_sc, *per_head):
        c_sc, cm_sc, o_sc, l_sc, w_sc = (per_head[2 * n:2 * n + 2] for n in range(5))
        i = pl.program_id(1)

        @pl.when(i == 0)
        def _():
            _fill_blocks(k_ref, nk, tk, transposed_sc=kT_sc)
            _fill_blocks(v_ref, nk, tk, masked_sc=vm_sc)

        qs = _head_masks(q_ref[...], HEAD_DIM ** -0.5)
        lane = lax.broadcasted_iota(jnp.int32, (1, LANE), 1)
        valid = lax.broadcasted_iota(jnp.int32, (tq, tk), 1) < lax.broadcasted_iota(jnp.int32, (tq, tk), 0)
        suffix = (lax.broadcasted_iota(jnp.int32, (tk, tk), 0) >= lax.broadcasted_iota(jnp.int32, (tk, tk), 1)).astype(BF16)
        for sc in per_head[:6]:
            sc[...] = jnp.zeros_like(sc)

        def logits(j, slot):
            kT = kT_sc[j]
            for a in range(2):
                l_sc[a][slot] = jnp.dot(qs[a], kT, preferred_element_type=F32)

        def values(j, slot):
            rows = pl.ds(pl.multiple_of(j * tk, tk), tk)
            for a in range(2):
                o_sc[a][...] += jnp.dot(w_sc[a][slot], vm_sc[a, rows, :], preferred_element_type=F32)

        def weights(j, slot, masked):
            sp = []
            for a in range(2):
                x = _softplus(l_sc[a][slot])
                sp.append((jnp.where(valid, x, 0.0) if masked else x).astype(BF16))
            inc = [jnp.dot(sp[a], suffix, preferred_element_type=F32) for a in range(2)]
            for a in range(2):
                c = c_sc[a][...]
                w = jnp.exp(l_sc[a][slot] - inc[a] - _wide(c, tk))
                if masked:
                    w = jnp.where(valid, w, 0.0)
                w_sc[a][slot] = w.astype(BF16)
                cm_sc[a][...] = jnp.where(lane == j, c, cm_sc[a][...])
                c_sc[a][...] = c + inc[a][:, 0:1]

        logits(i, 0)
        logits(jnp.maximum(i - 1, 0), 1)
        weights(i, 0, True)

        def step(t, slot):
            logits(jnp.maximum(i - t - 1, 0), 1 - slot)
            values(i - t + 1, 1 - slot)
            weights(i - t, slot, False)

        def two_steps(tt, carry):
            step(2 * tt + 1, 1)
            step(2 * tt + 2, 0)
            return carry

        lax.fori_loop(0, i // 2, two_steps, 0)

        @pl.when(i % 2 == 1)
        def _():
            step(i, 1)
            values(0, 1)

        @pl.when(i % 2 == 0)
        def _():
            values(0, 0)

        o = o_sc[0][...] + o_sc[1][...]
        o_ref[...] = o
        za = za_ref[...]
        y_ref[...] = (o * (za * _sigmoid(za))).astype(BF16)
        c_ref[0, 0] = cm_sc[0][...]
        c_ref[1, 0] = cm_sc[1][...]

    return _pc(
        body, name="attn_fwd", grid=(n_pairs, nq),
        out_shape=(jax.ShapeDtypeStruct((T, ATTN_W), F32), jax.ShapeDtypeStruct((T, ATTN_W), BF16),
                   jax.ShapeDtypeStruct((2 * n_pairs, nq, tq, LANE), F32)),
        in_specs=[pl.BlockSpec((tq, LANE), lambda p, i: (i, p)),
                  pl.BlockSpec((T, LANE), lambda p, i: (0, n_pairs + p)),
                  pl.BlockSpec((T, LANE), lambda p, i: (0, 2 * n_pairs + p)),
                  pl.BlockSpec((tq, LANE), lambda p, i: (i, zb + p))],
        out_specs=(pl.BlockSpec((tq, LANE), lambda p, i: (i, p)), pl.BlockSpec((tq, LANE), lambda p, i: (i, p)),
                   pl.BlockSpec((2, 1, tq, LANE), lambda p, i: (p, i, 0, 0))),
        scratch_shapes=[pltpu.VMEM((nk, LANE, tk), BF16), pltpu.VMEM((2, T, LANE), BF16)]
        + [pltpu.VMEM((tq, LANE), F32)] * 6 + [pltpu.VMEM((2, tq, tk), F32)] * 2 + [pltpu.VMEM((2, tq, tk), BF16)] * 2,
        compiler_params=_params("arbitrary", "arbitrary"),
    )(projb, projb, projb, proj)


def _merge_fwd(y_pool, y_attn, w_pu, w_au, proj, b_gate, tm):
    T = y_pool.shape[0]
    D = w_pu.shape[1]
    gb = _proj_layout(D)["gates"] // D

    def body(yp_ref, ya_ref, wpu_ref, wau_ref, gl0_ref, gl1_ref, bg_ref, m_ref, p_ref, a_ref):
        p = jnp.dot(yp_ref[...], wpu_ref[...], preferred_element_type=F32)
        a = jnp.dot(ya_ref[...], wau_ref[...], preferred_element_type=F32)
        g0 = _sigmoid(gl0_ref[...] + bg_ref[:, :D])
        g1 = _sigmoid(gl1_ref[...] + bg_ref[:, D:])
        m_ref[...] = (g0 * p + g1 * a).astype(BF16)
        p_ref[...] = p
        a_ref[...] = a

    row = lambda i: (i, 0)
    fixed = lambda i: (0, 0)
    return _pc(
        body, name="merge_fwd", grid=(T // tm,),
        out_shape=(jax.ShapeDtypeStruct((T, D), BF16), jax.ShapeDtypeStruct((T, D), F32),
                   jax.ShapeDtypeStruct((T, D), F32)),
        in_specs=[pl.BlockSpec((tm, POOL_W), row), pl.BlockSpec((tm, ATTN_W), row),
                  pl.BlockSpec((POOL_W, D), fixed), pl.BlockSpec((ATTN_W, D), fixed),
                  pl.BlockSpec((tm, D), lambda i: (i, gb)), pl.BlockSpec((tm, D), lambda i: (i, gb + 1)),
                  pl.BlockSpec((1, 2 * D), fixed)],
        out_specs=(pl.BlockSpec((tm, D), row), pl.BlockSpec((tm, D), row), pl.BlockSpec((tm, D), row)),
        compiler_params=_params("parallel"),
    )(y_pool, y_attn, w_pu, w_au, proj, proj, b_gate)


def _final_loss(x, g, target, tm):
    T, D = x.shape

    def body(x_ref, g_ref, t_ref, dx_ref, dg_ref, loss_ref):
        @pl.when(pl.program_id(0) == 0)
        def _():
            dg_ref[...] = jnp.zeros_like(dg_ref)
            loss_ref[...] = jnp.zeros_like(loss_ref)

        xv, gv = x_ref[...], g_ref[...]
        r = lax.rsqrt(jnp.mean(xv * xv, axis=-1, keepdims=True) + RMS_EPS)
        xh = xv * r
        d = xh * gv - t_ref[...]
        loss_ref[...] += 0.5 * jnp.sum(jnp.mean(d * d, axis=-1, keepdims=True), axis=0, keepdims=True)
        dy = d * (1.0 / D)
        dg_ref[...] += jnp.sum(dy * xh, axis=0, keepdims=True)
        dh = dy * gv
        dx_ref[...] = r * (dh - xh * jnp.mean(dh * xh, axis=-1, keepdims=True))

    return _pc(
        body, name="final_loss", grid=(T // tm,),
        out_shape=(jax.ShapeDtypeStruct((T, D), F32), jax.ShapeDtypeStruct((1, D), F32),
                   jax.ShapeDtypeStruct((8, LANE), F32)),
        in_specs=[pl.BlockSpec((tm, D), lambda i: (i, 0)), pl.BlockSpec((1, D), lambda i: (0, 0)),
                  pl.BlockSpec((tm, D), lambda i: (i, 0))],
        out_specs=(pl.BlockSpec((tm, D), lambda i: (i, 0)), pl.BlockSpec((1, D), lambda i: (0, 0)),
                   pl.BlockSpec((8, LANE), lambda i: (0, 0))),
        compiler_params=_params("arbitrary"),
    )(x, g, target)


def _merge_bwd(dm, p, a, proj, b_gate, tm):
    T, D = dm.shape
    gb = _proj_layout(D)["gates"] // D

    def body(dm_ref, p_ref, a_ref, gl0_ref, gl1_ref, bg_ref, dp_ref, da_ref, dgl_ref, dbg_ref):
        @pl.when(pl.program_id(0) == 0)
        def _():
            dbg_ref[...] = jnp.zeros_like(dbg_ref)

        dmv = dm_ref[...]
        g0 = _sigmoid(gl0_ref[...] + bg_ref[:, :D])
        g1 = _sigmoid(gl1_ref[...] + bg_ref[:, D:])
        dp_ref[...] = (dmv * g0).astype(BF16)
        da_ref[...] = (dmv * g1).astype(BF16)
        dgl0 = dmv * p_ref[...] * (g0 * (1.0 - g0))
        dgl1 = dmv * a_ref[...] * (g1 * (1.0 - g1))
        dgl_ref[:, :D] = dgl0.astype(BF16)
        dgl_ref[:, D:] = dgl1.astype(BF16)
        dbg_ref[:, :D] += jnp.sum(dgl0, axis=0, keepdims=True)
        dbg_ref[:, D:] += jnp.sum(dgl1, axis=0, keepdims=True)

    row = lambda i: (i, 0)
    fixed = lambda i: (0, 0)
    return _pc(
        body, name="merge_bwd", grid=(T // tm,),
        out_shape=(jax.ShapeDtypeStruct((T, D), BF16), jax.ShapeDtypeStruct((T, D), BF16),
                   jax.ShapeDtypeStruct((T, 2 * D), BF16), jax.ShapeDtypeStruct((1, 2 * D), F32)),
        in_specs=[pl.BlockSpec((tm, D), row), pl.BlockSpec((tm, D), row), pl.BlockSpec((tm, D), row),
                  pl.BlockSpec((tm, D), lambda i: (i, gb)), pl.BlockSpec((tm, D), lambda i: (i, gb + 1)),
                  pl.BlockSpec((1, 2 * D), fixed)],
        out_specs=(pl.BlockSpec((tm, D), row), pl.BlockSpec((tm, D), row), pl.BlockSpec((tm, 2 * D), row),
                   pl.BlockSpec((1, 2 * D), fixed)),
        compiler_params=_params("arbitrary"),
    )(dm, p, a, proj, proj, b_gate)


def _pool_bwd(proj, dy, pool_w, scale, R):
    T = proj.shape[0]
    nb = T // R
    hb = R // HALO

    def body(u_ref, up_ref, z_ref, dy_ref, pw_ref, sc_ref, du_ref, dz_ref, dpw_ref, dsc_ref, halo_sc):
        i = pl.program_id(0)
        rb = nb - 1 - i

        @pl.when(i == 0)
        def _():
            halo_sc[...] = jnp.zeros_like(halo_sc)
            dpw_ref[...] = jnp.zeros_like(dpw_ref)
            dsc_ref[...] = jnp.zeros_like(dsc_ref)

        u = u_ref[...]
        row = rb * R + lax.broadcasted_iota(jnp.int32, (R, 1), 0)
        before = jnp.where(rb > 0, up_ref[...], 0.0)
        pooled = _pooled(u, before, row)
        pw = [pw_ref[g].astype(BF16) for g in range(len(POOL_WINDOWS))]
        mixed = jnp.concatenate(
            [jnp.dot(pooled[g].astype(BF16), pw[g], preferred_element_type=F32) for g in range(len(POOL_WINDOWS))],
            axis=1)
        sc = sc_ref[...]
        silu, dsilu = _silu_and_grad(z_ref[...])
        dyv = dy_ref[...]
        dmp = dyv * silu
        dz_ref[...] = (dyv * (mixed * sc) * dsilu).astype(BF16)
        dsc_ref[...] += jnp.sum(dmp * mixed, axis=0, keepdims=True)
        dmixed = (dmp * sc).astype(BF16)
        dpn = []
        dpooled = []
        for g, w in enumerate(POOL_WINDOWS):
            cols = slice(g * POOL_G, (g + 1) * POOL_G)
            dpw_ref[g] += lax.dot_general(pooled[g].astype(BF16), dmixed[:, cols], TN_DIMS,
                                          preferred_element_type=F32)
            dpg = lax.dot_general(dmixed[:, cols], pw[g], NT_DIMS, preferred_element_type=F32)
            dpooled.append(dpg)
            dpn.append(dpg / jnp.minimum(row + 1, w).astype(F32))
        dpn = jnp.concatenate(dpn, axis=1)
        sums = _window_sums(jnp.concatenate([dpn, halo_sc[...]], axis=0), False)
        du_ref[...] = jnp.concatenate(
            [sums[g][:R, g * POOL_G:(g + 1) * POOL_G] - dpooled[g] for g in range(len(POOL_WINDOWS))],
            axis=1).astype(BF16)
        halo_sc[...] = dpn[:HALO, :]

    rev = lambda i: (nb - 1 - i, 0)
    return _pc(
        body, name="pool_bwd", grid=(nb,),
        out_shape=(jax.ShapeDtypeStruct((T, POOL_W), BF16), jax.ShapeDtypeStruct((T, POOL_W), BF16),
                   jax.ShapeDtypeStruct((4, POOL_G, POOL_G), F32), jax.ShapeDtypeStruct((1, POOL_W), F32)),
        in_specs=[pl.BlockSpec((R, POOL_W), rev),
                  pl.BlockSpec((HALO, POOL_W), lambda i: (jnp.maximum((nb - 1 - i) * hb - 1, 0), 0)),
                  pl.BlockSpec((R, POOL_W), lambda i: (nb - 1 - i, 1)),
                  pl.BlockSpec((R, POOL_W), rev),
                  pl.BlockSpec((4, POOL_G, POOL_G), lambda i: (0, 0, 0)), pl.BlockSpec((1, POOL_W), lambda i: (0, 0))],
        out_specs=(pl.BlockSpec((R, POOL_W), rev), pl.BlockSpec((R, POOL_W), rev),
                   pl.BlockSpec((4, POOL_G, POOL_G), lambda i: (0, 0, 0)), pl.BlockSpec((1, POOL_W), lambda i: (0, 0))),
        scratch_shapes=[pltpu.VMEM((HALO, POOL_W), F32)],
        compiler_params=_params("arbitrary"),
    )(proj, proj, proj, dy, pool_w, scale)


def _attn_gate_bwd(dya, o, proj, tm):
    T = dya.shape[0]
    zb = (proj.shape[1] - ATTN_W) // ATTN_W

    def body(dy_ref, o_ref, za_ref, do_ref, dza_ref):
        silu, dsilu = _silu_and_grad(za_ref[...])
        dyv = dy_ref[...]
        do_ref[...] = (dyv * silu).astype(BF16)
        dza_ref[...] = (dyv * o_ref[...] * dsilu).astype(BF16)

    row = lambda i: (i, 0)
    return _pc(
        body, name="attn_gate_bwd", grid=(T // tm,),
        out_shape=(jax.ShapeDtypeStruct((T, ATTN_W), BF16), jax.ShapeDtypeStruct((T, ATTN_W), BF16)),
        in_specs=[pl.BlockSpec((tm, ATTN_W), row), pl.BlockSpec((tm, ATTN_W), row),
                  pl.BlockSpec((tm, ATTN_W), lambda i: (i, zb))],
        out_specs=(pl.BlockSpec((tm, ATTN_W), row), pl.BlockSpec((tm, ATTN_W), row)),
        compiler_params=_params("parallel"),
    )(dya, o, proj)


def _attn_bwd(projb, do, carries, tq):
    T = projb.shape[0]
    nq = T // tq
    tk, nk = tq, nq
    n_pairs = ATTN_W // LANE
    scale = HEAD_DIM ** -0.5

    def body(q_ref, k_ref, v_ref, do_ref, c_ref, dq_ref, dkT_ref, dvT_ref, kT_sc, vT_sc, km_sc, *per_head):
        f_sc, dq_sc, l_sc, dw_sc, dl_sc, w_sc = (per_head[2 * n:2 * n + 2] for n in range(6))
        i = pl.program_id(1)

        @pl.when(i == 0)
        def _():
            _fill_blocks(k_ref, nk, tk, transposed_sc=kT_sc, masked_sc=km_sc)
            _fill_blocks(v_ref, nk, tk, transposed_sc=vT_sc)
            dkT_ref[...] = jnp.zeros_like(dkT_ref)
            dvT_ref[...] = jnp.zeros_like(dvT_ref)

        qs = _head_masks(q_ref[...], scale)
        dos = _head_masks(do_ref[...], 1.0)
        qT = [x.astype(F32).T.astype(BF16) for x in qs]
        doT = [x.astype(F32).T.astype(BF16) for x in dos]
        lane = lax.broadcasted_iota(jnp.int32, (1, LANE), 1)
        valid = lax.broadcasted_iota(jnp.int32, (tq, tk), 1) < lax.broadcasted_iota(jnp.int32, (tq, tk), 0)
        kk0 = lax.broadcasted_iota(jnp.int32, (tk, tk), 0)
        kk1 = lax.broadcasted_iota(jnp.int32, (tk, tk), 1)
        suffix = (kk0 >= kk1).astype(BF16)
        prefix = (kk0 <= kk1).astype(BF16)
        for a in range(2):
            f_sc[a][...] = jnp.zeros_like(f_sc[a])
            dq_sc[a][...] = jnp.zeros_like(dq_sc[a])
            dl_sc[a][1] = jnp.zeros((tq, tk), BF16)
            w_sc[a][1] = jnp.zeros((tq, tk), BF16)

        def products(j, slot):
            kT = kT_sc[j]
            vT = vT_sc[j]
            for a in range(2):
                l_sc[a][slot] = jnp.dot(qs[a], kT, preferred_element_type=F32)
                dw_sc[a][slot] = jnp.dot(dos[a], vT, preferred_element_type=F32)

        def gradients(j, slot):
            rows = pl.ds(pl.multiple_of(j * tk, tk), tk)
            dkT = []
            dvT = []
            for a in range(2):
                dlb = dl_sc[a][slot]
                dq_sc[a][...] += jnp.dot(dlb, km_sc[a, rows, :], preferred_element_type=F32)
                dkT.append(jnp.dot(qT[a], dlb, preferred_element_type=F32))
                dvT.append(jnp.dot(doT[a], w_sc[a][slot], preferred_element_type=F32))
            dkT_ref[j] += dkT[0] + dkT[1]
            dvT_ref[j] += dvT[0] + dvT[1]

        def elementwise(j, slot, masked):
            sp, inc, e, beta, p = [None] * 2, [None] * 2, [None] * 2, [None] * 2, [None] * 2
            for a in range(2):
                x = _softplus(l_sc[a][slot])
                sp[a] = jnp.where(valid, x, 0.0) if masked else x
            for a in range(2):
                inc[a] = jnp.dot(sp[a].astype(BF16), suffix, preferred_element_type=F32)
            for a in range(2):
                l = l_sc[a][slot]
                c = jnp.sum(jnp.where(lane == j, c_ref[a, 0], 0.0), axis=1, keepdims=True)
                w = jnp.exp(l - inc[a] - c)
                if masked:
                    w = jnp.where(valid, w, 0.0)
                w_sc[a][slot] = w.astype(BF16)
                beta[a] = jnp.exp(l - sp[a])
                e[a] = w * dw_sc[a][slot]
            for a in range(2):
                p[a] = jnp.dot(e[a].astype(BF16), prefix, preferred_element_type=F32)
            for a in range(2):
                f = f_sc[a][...]
                dl = e[a] - beta[a] * (p[a] + _wide(f, tk))
                if masked:
                    dl = jnp.where(valid, dl, 0.0)
                dl_sc[a][slot] = dl.astype(BF16)
                f_sc[a][...] = f + p[a][:, tk - 1:tk]

        def step(t, slot):
            products(t + 1, 1 - slot)
            gradients(jnp.maximum(t - 1, 0), 1 - slot)
            elementwise(t, slot, False)

        def last(slot):
            gradients(jnp.maximum(i - 1, 0), 1 - slot)
            elementwise(i, slot, True)
            gradients(i, slot)

        products(0, 0)

        def two_steps(tt, carry):
            step(2 * tt, 0)
            step(2 * tt + 1, 1)
            return carry

        lax.fori_loop(0, i // 2, two_steps, 0)

        @pl.when(i % 2 == 1)
        def _():
            step(i - 1, 0)
            last(1)

        @pl.when(i % 2 == 0)
        def _():
            last(0)

        dq_ref[...] = ((dq_sc[0][...] + dq_sc[1][...]) * scale).astype(BF16)

    return _pc(
        body, name="attn_bwd", grid=(n_pairs, nq),
        out_shape=(jax.ShapeDtypeStruct((T, ATTN_W), BF16), jax.ShapeDtypeStruct((n_pairs, nk, LANE, tk), F32),
                   jax.ShapeDtypeStruct((n_pairs, nk, LANE, tk), F32)),
        in_specs=[pl.BlockSpec((tq, LANE), lambda p, i: (i, p)),
                  pl.BlockSpec((T, LANE), lambda p, i: (0, n_pairs + p)),
                  pl.BlockSpec((T, LANE), lambda p, i: (0, 2 * n_pairs + p)),
                  pl.BlockSpec((tq, LANE), lambda p, i: (i, p)),
                  pl.BlockSpec((2, 1, tq, LANE), lambda p, i: (p, i, 0, 0))],
        out_specs=(pl.BlockSpec((tq, LANE), lambda p, i: (i, p)),
                   pl.BlockSpec((None, nk, LANE, tk), lambda p, i: (p, 0, 0, 0)),
                   pl.BlockSpec((None, nk, LANE, tk), lambda p, i: (p, 0, 0, 0))),
        scratch_shapes=[pltpu.VMEM((nk, LANE, tk), BF16), pltpu.VMEM((nk, LANE, tk), BF16),
                        pltpu.VMEM((2, T, LANE), BF16)]
        + [pltpu.VMEM((tq, LANE), F32)] * 4 + [pltpu.VMEM((2, tq, tk), F32)] * 4 + [pltpu.VMEM((2, tq, tk), BF16)] * 4,
        compiler_params=_params("arbitrary", "arbitrary"),
    )(projb, projb, projb, do, carries)


def _norm_bwd(dh, x, g, dxo, tm):
    T, D = x.shape

    def body(dh_ref, x_ref, g_ref, dxo_ref, dx_ref, dg_ref):
        @pl.when(pl.program_id(0) == 0)
        def _():
            dg_ref[...] = jnp.zeros_like(dg_ref)

        xv = x_ref[...]
        r = lax.rsqrt(jnp.mean(xv * xv, axis=-1, keepdims=True) + RMS_EPS)
        xh = xv * r
        dhv = dh_ref[...]
        dg_ref[...] += jnp.sum(dhv * xh, axis=0, keepdims=True)
        dhg = dhv * g_ref[...]
        dx_ref[...] = dxo_ref[...] + r * (dhg - xh * jnp.mean(dhg * xh, axis=-1, keepdims=True))

    row = lambda i: (i, 0)
    return _pc(
        body, name="norm_bwd", grid=(T // tm,),
        out_shape=(jax.ShapeDtypeStruct((T, D), F32), jax.ShapeDtypeStruct((1, D), F32)),
        in_specs=[pl.BlockSpec((tm, D), row), pl.BlockSpec((tm, D), row), pl.BlockSpec((1, D), lambda i: (0, 0)),
                  pl.BlockSpec((tm, D), row)],
        out_specs=(pl.BlockSpec((tm, D), row), pl.BlockSpec((1, D), lambda i: (0, 0))),
        compiler_params=_params("arbitrary"),
    )(dh, x, g, dxo)


def _adamw(pieces, w, m, v, name):
    rows, cols = w.shape
    br = rows
    while br * cols > 65536 and br % 16 == 0:
        br //= 2
    c1 = 1.0 / (1.0 - ADAM_B1 ** ADAM_STEP)
    c2 = 1.0 / (1.0 - ADAM_B2 ** ADAM_STEP)

    def body(p_ref, w_ref, m_ref, v_ref, g_ref, d_ref, nm_ref, nv_ref):
        g = p_ref[0].astype(F32)
        for s in range(1, N_DEV):
            g = g + p_ref[s].astype(F32)
        nm = ADAM_B1 * m_ref[...] + (1.0 - ADAM_B1) * g
        nv = ADAM_B2 * v_ref[...] + (1.0 - ADAM_B2) * (g * g)
        g_ref[...] = g
        nm_ref[...] = nm
        nv_ref[...] = nv
        d_ref[...] = -ADAM_LR * ((nm * c1) / (jnp.sqrt(nv * c2) + ADAM_EPS) + ADAM_WD * w_ref[...])

    blk = pl.BlockSpec((br, cols), lambda i: (i, 0))
    shape = jax.ShapeDtypeStruct((rows, cols), F32)
    return _pc(
        body, name=name, grid=(rows // br,), out_shape=(shape, shape, shape, shape),
        in_specs=[pl.BlockSpec((N_DEV, br, cols), lambda i: (0, i, 0)), blk, blk, blk],
        out_specs=(blk, blk, blk, blk),
        compiler_params=_params("parallel"),
    )(pieces, w, m, v)


def _rows128(a):
    flat = a.reshape(-1)
    n = flat.shape[0]
    padded = -(-n // (8 * LANE)) * (8 * LANE)
    if padded != n:
        flat = jnp.concatenate([flat, jnp.zeros((padded - n,), flat.dtype)])
    return flat.reshape(-1, LANE)


def _pack(parts):
    return jnp.concatenate([_rows128(p) for p in parts], axis=0)


def _unpack(packed, like):
    out, r = [], 0
    for a in like:
        n = a.size
        nr = -(-n // (8 * LANE)) * 8
        out.append(packed[r:r + nr].reshape(-1)[:n].reshape(a.shape))
        r += nr
    return out


def kernel(x, norm_g, w_in, b_gate, pool_w, pool_scale, w_pool_up, w_attn_up, w_out, final_g, loss_target, m_norm_g, m_w_in, m_b_gate, m_pool_w, m_pool_scale, m_w_pool_up, m_w_attn_up, m_w_out, m_final_g, v_norm_g, v_w_in, v_b_gate, v_pool_w, v_pool_scale, v_w_pool_up, v_w_attn_up, v_w_out, v_final_g):
    L = norm_g.shape[0]
    T, D = x.shape[1], x.shape[2]
    NW = w_in.shape[2] * N_DEV
    assert NW == 2 * POOL_W + 4 * ATTN_W + 2 * D and x.shape[0] == 1
    tm = min(512, T)
    tq = min(256, T // 2)
    x0 = x.reshape(T, D)
    target = loss_target.reshape(T, D)

    g_in, g_pu, g_au, g_out = _exchange(
        [w_in.astype(BF16), w_pool_up.astype(BF16), w_attn_up.astype(BF16), w_out.astype(BF16)],
        [False] * 4, "gather_weights")
    win_full = jnp.transpose(g_in, (1, 2, 0, 3)).reshape(L, D, NW)
    wpu_full = jnp.transpose(g_pu, (1, 2, 0, 3)).reshape(L, POOL_W, D)
    wau_full = jnp.transpose(g_au, (1, 2, 0, 3)).reshape(L, ATTN_W, D)
    wout_full = jnp.transpose(g_out, (1, 0, 2, 3)).reshape(L, D, D)

    saved = []
    xl = x0
    for l in range(L):
        proj, projb, h = _norm_inproj(xl, norm_g[l:l + 1], win_full[l], min(256, T))
        y_pool = _pool_fwd(proj, pool_w[l], pool_scale[l:l + 1], tm)
        o, y_attn, carries = _attn_fwd(projb, proj, tq)
        merged, p, a = _merge_fwd(y_pool, y_attn, wpu_full[l], wau_full[l], proj, b_gate[l:l + 1], min(256, T))
        x_next = _mm_nn_res(merged, wout_full[l], xl, tm, "out_proj")
        saved.append((xl, proj, projb, h, y_pool, o, y_attn, carries, merged, p, a))
        xl = x_next

    dx, d_final_g, loss_part = _final_loss(xl, final_g.reshape(1, D), target, tm)

    d_norm_g, d_b_gate, d_pool_w, d_pool_scale = [None] * L, [None] * L, [None] * L, [None] * L
    d_win, d_wpu, d_wau, d_wout = [None] * L, [None] * L, [None] * L, [None] * L
    for l in reversed(range(L)):
        xin, proj, projb, h, y_pool, o, y_attn, carries, merged, p, a = saved[l]
        dm = _mm_nt(dx, wout_full[l], tm, D, "d_merged")
        d_wout[l] = _mm_tn(merged, dx, 1, D, min(1024, T), "d_w_out").reshape(N_DEV, D // N_DEV, D)
        dp, da, dgl, d_b_gate[l] = _merge_bwd(dm, p, a, proj, b_gate[l:l + 1], min(256, T))
        d_wpu[l] = _mm_tn(y_pool, dp, N_DEV, D, min(1024, T), "d_w_pool_up")
        d_wau[l] = _mm_tn(y_attn, da, N_DEV, D, min(1024, T), "d_w_attn_up")
        dyp = _mm_nt(dp, wpu_full[l], tm, D, "d_y_pool")
        dya = _mm_nt(da, wau_full[l], tm, D, "d_y_attn")
        du, dzp, d_pool_w[l], d_pool_scale[l] = _pool_bwd(proj, dyp, pool_w[l], pool_scale[l:l + 1], tm)
        do, dza = _attn_gate_bwd(dya, o, proj, tm)
        dq, dkT, dvT = _attn_bwd(projb, do, carries, tq)
        dk, dv = [jnp.transpose(t, (1, 3, 0, 2)).reshape(T, ATTN_W).astype(BF16) for t in (dkT, dvT)]
        dproj = jnp.concatenate([du, dzp, dq, dk, dv, dza, dgl], axis=1)
        d_win[l] = _mm_tn(h, dproj, N_DEV, NW // 2, min(1024, T), "d_w_in")
        dh = _mm_nt(dproj, win_full[l], min(256, T), NW, "d_h")
        dx, d_norm_g[l] = _norm_bwd(dh, xin, norm_g[l:l + 1], dx, tm)

    small_like = [norm_g, b_gate, pool_w, pool_scale, final_g, jnp.zeros((8, LANE), F32)]
    small = _pack([jnp.concatenate(d_norm_g, 0), jnp.concatenate(d_b_gate, 0), jnp.stack(d_pool_w, 0),
                   jnp.concatenate(d_pool_scale, 0), d_final_g, loss_part])
    sends = [small]
    flags = [False]
    for l in range(L):
        sends += [d_win[l], d_wpu[l], d_wau[l], d_wout[l]]
        flags += [True] * 4
    got = _exchange(sends, flags, "scatter_grads")
    r_small = got[0]
    r_in = jnp.stack([got[1 + 4 * l] for l in range(L)], axis=1)
    r_pu = jnp.stack([got[2 + 4 * l] for l in range(L)], axis=1)
    r_au = jnp.stack([got[3 + 4 * l] for l in range(L)], axis=1)
    r_out = jnp.stack([got[4 + 4 * l] for l in range(L)], axis=1)

    def update(pieces, w, m, v, name):
        cols = w.shape[-1]
        res = _adamw(pieces.reshape(N_DEV, -1, cols), w.reshape(-1, cols), m.reshape(-1, cols),
                     v.reshape(-1, cols), name)
        return [r.reshape(w.shape) for r in res]

    u_in = update(r_in, w_in, m_w_in, v_w_in, "adamw_w_in")
    u_pu = update(r_pu, w_pool_up, m_w_pool_up, v_w_pool_up, "adamw_w_pool_up")
    u_au = update(r_au, w_attn_up, m_w_attn_up, v_w_attn_up, "adamw_w_attn_up")
    u_out = update(r_out, w_out, m_w_out, v_w_out, "adamw_w_out")
    zeros = small_like[-1]
    smalls = _adamw(r_small,
                    _pack([norm_g, b_gate, pool_w, pool_scale, final_g, zeros]),
                    _pack([m_norm_g, m_b_gate, m_pool_w, m_pool_scale, m_final_g, zeros]),
                    _pack([v_norm_g, v_b_gate, v_pool_w, v_pool_scale, v_final_g, zeros]), "adamw_small")
    s_g, s_d, s_m, s_v = [_unpack(s, small_like) for s in smalls]
    loss = s_g[5][0, 0]

    def ordered(k):
        s = (s_g, s_d, s_m, s_v)[k]
        return [s[0], u_in[k], s[1], s[2], s[3], u_pu[k], u_au[k], u_out[k], s[4]]

    return (loss, dx.reshape(x.shape), *ordered(0), *ordered(1), *ordered(2), *ordered(3))
```

```python
import jax
import jax.numpy as jnp
from jax import lax
from jax.experimental import pallas as pl
from jax.experimental.pallas import tpu as pltpu

F32 = jnp.float32
BF16 = jnp.bfloat16

N_DEV = 8
HEAD_DIM = 64
ATTN_W = 512
POOL_W = 512
POOL_G = 128
POOL_WINDOWS = (2, 4, 8, 16)
HALO = 16
LANE = 128
RMS_EPS = 1e-6
ZERO_WEIGHT = 110.0
NO_CARRY = 3.0e38
ADAM_LR, ADAM_B1, ADAM_B2, ADAM_EPS, ADAM_WD, ADAM_STEP = 0.001, 0.9, 0.999, 1e-08, 0.01, 10
VMEM_LIMIT = 56 * 1024 * 1024

NT_DIMS = (((1,), (1,)), ((), ()))
TN_DIMS = (((0,), (0,)), ((), ()))


def _pc(body, **kw):
    return pl.pallas_call(body, **kw)


def _params(*sem):
    return pltpu.CompilerParams(dimension_semantics=sem, vmem_limit_bytes=VMEM_LIMIT)


def _sigmoid(z):
    return 1.0 / (1.0 + jnp.exp(-z))


def _silu_and_grad(z):
    s = _sigmoid(z)
    return z * s, s * (1.0 + z * (1.0 - s))


def _my_index():
    return 4 * lax.axis_index("x") + 2 * lax.axis_index("y") + lax.axis_index("c")


def _peer(k):
    x, y, c = lax.axis_index("x"), lax.axis_index("y"), lax.axis_index("c")
    px = lax.rem(x + ((k >> 2) & 1), 2)
    py = lax.rem(y + ((k >> 1) & 1), 2)
    pc = lax.rem(c + (k & 1), 2)
    return (px, py, pc), 4 * px + 2 * py + pc


def _exchange(arrays, scatter, name):
    n = len(arrays)

    def body(*refs):
        ins, outs = refs[:n], refs[n:2 * n]
        send_sems, recv_sems, local_sems = refs[2 * n:]
        me = _my_index()

        def src(a, idx):
            return ins[a].at[idx] if scatter[a] else ins[a]

        local = [pltpu.make_async_copy(src(a, me), outs[a].at[me], local_sems.at[a]) for a in range(n)]
        for cp in local:
            cp.start()
        sends = []
        order = (1, 2, 4, 3, 5, 6, 7)
        for k in order:
            dev, pidx = _peer(k)
            for a in range(n):
                cp = pltpu.make_async_remote_copy(
                    src_ref=src(a, pidx), dst_ref=outs[a].at[me],
                    send_sem=send_sems.at[a * N_DEV + k], recv_sem=recv_sems.at[a * N_DEV + k],
                    device_id=dev, device_id_type=pl.DeviceIdType.MESH)
                cp.start()
                sends.append(cp)
        for k in order:
            dev, pidx = _peer(k)
            for a in range(n):
                pltpu.make_async_remote_copy(
                    src_ref=src(a, pidx), dst_ref=outs[a].at[pidx],
                    send_sem=send_sems.at[a * N_DEV + k], recv_sem=recv_sems.at[a * N_DEV + k],
                    device_id=dev, device_id_type=pl.DeviceIdType.MESH).wait_recv()
        for cp in sends:
            cp.wait_send()
        for cp in local:
            cp.wait()

    out_shape = []
    for a, s in zip(arrays, scatter):
        piece = a.shape[1:] if s else a.shape
        out_shape.append(jax.ShapeDtypeStruct((N_DEV,) + tuple(piece), a.dtype))
    any_spec = pl.BlockSpec(memory_space=pl.ANY)
    return _pc(
        body, name=name, out_shape=tuple(out_shape),
        in_specs=[any_spec] * n, out_specs=tuple([any_spec] * n),
        scratch_shapes=[pltpu.SemaphoreType.DMA((n * N_DEV,)), pltpu.SemaphoreType.DMA((n * N_DEV,)),
                        pltpu.SemaphoreType.DMA((n,))],
    )(*arrays)


def _mm_nn_res(a, b, res, tm, name):
    T, K = a.shape
    N = b.shape[1]

    def body(a_ref, b_ref, r_ref, o_ref):
        o_ref[...] = r_ref[...] + jnp.dot(a_ref[...], b_ref[...], preferred_element_type=F32)

    return _pc(
        body, name=name, grid=(T // tm,), out_shape=jax.ShapeDtypeStruct((T, N), F32),
        in_specs=[pl.BlockSpec((tm, K), lambda i: (i, 0)), pl.BlockSpec((K, N), lambda i: (0, 0)),
                  pl.BlockSpec((tm, N), lambda i: (i, 0))],
        out_specs=pl.BlockSpec((tm, N), lambda i: (i, 0)),
        compiler_params=_params("parallel"),
    )(a, b, res)


def _mm_nt(a, b, tm, tk, name):
    T, K = a.shape
    N = b.shape[0]
    nk = K // tk

    def body(a_ref, b_ref, o_ref):
        part = lax.dot_general(a_ref[...].astype(BF16), b_ref[...], NT_DIMS, preferred_element_type=F32)
        if nk == 1:
            o_ref[...] = part
        else:
            k = pl.program_id(1)

            @pl.when(k == 0)
            def _():
                o_ref[...] = part

            @pl.when(k > 0)
            def _():
                o_ref[...] += part

    return _pc(
        body, name=name, grid=(T // tm, nk), out_shape=jax.ShapeDtypeStruct((T, N), F32),
        in_specs=[pl.BlockSpec((tm, tk), lambda i, k: (i, k)), pl.BlockSpec((N, tk), lambda i, k: (0, k))],
        out_specs=pl.BlockSpec((tm, N), lambda i, k: (i, 0)),
        compiler_params=_params("parallel", "arbitrary"),
    )(a, b)


def _mm_tn(a, b, n_col_shards, tn, tk, name):
    T, M = a.shape
    N = b.shape[1]
    sw = N // n_col_shards
    per_step = tn // sw
    nk = T // tk

    def body(a_ref, b_ref, o_ref, acc_sc):
        k = pl.program_id(1)
        part = lax.dot_general(a_ref[...].astype(BF16), b_ref[...].astype(BF16), TN_DIMS,
                               preferred_element_type=F32)

        @pl.when(k == 0)
        def _():
            acc_sc[...] = part

        @pl.when(k > 0)
        def _():
            acc_sc[...] += part

        @pl.when(k == nk - 1)
        def _():
            for s in range(per_step):
                o_ref[s] = acc_sc[:, s * sw:(s + 1) * sw].astype(BF16)

    return _pc(
        body, name=name, grid=(N // tn, nk),
        out_shape=jax.ShapeDtypeStruct((n_col_shards, M, sw), BF16),
        in_specs=[pl.BlockSpec((tk, M), lambda j, k: (k, 0)), pl.BlockSpec((tk, tn), lambda j, k: (k, j))],
        out_specs=pl.BlockSpec((per_step, M, sw), lambda j, k: (j, 0, 0)),
        scratch_shapes=[pltpu.VMEM((M, tn), F32)],
        compiler_params=_params("parallel", "arbitrary"),
    )(a, b)


def _proj_layout(D):
    return {"u": 0, "z_pool": POOL_W, "gates": 2 * POOL_W, "z_attn": 2 * POOL_W + 2 * D, "width": 2 * POOL_W + 2 * D + ATTN_W}


def _norm_inproj(x, g, w, tm):
    T, D = x.shape
    NW = w.shape[1]
    lay = _proj_layout(D)
    qkv0, za0, gl0 = 2 * POOL_W, 2 * POOL_W + 3 * ATTN_W, 2 * POOL_W + 4 * ATTN_W

    def body(x_ref, g_ref, w_ref, proj_ref, qkv_ref, h_ref):
        xv = x_ref[...]
        r = lax.rsqrt(jnp.mean(xv * xv, axis=-1, keepdims=True) + RMS_EPS)
        h = ((xv * r) * g_ref[...]).astype(BF16)
        h_ref[...] = h

        def cols(lo, hi):
            return jnp.dot(h, w_ref[:, lo:hi], preferred_element_type=F32)

        proj_ref[:, :lay["gates"]] = cols(0, qkv0)
        qkv_ref[...] = cols(qkv0, za0).astype(BF16)
        proj_ref[:, lay["gates"]:lay["z_attn"]] = cols(gl0, NW)
        proj_ref[:, lay["z_attn"]:] = cols(za0, gl0)

    return _pc(
        body, name="norm_inproj", grid=(T // tm,),
        out_shape=(jax.ShapeDtypeStruct((T, lay["width"]), F32), jax.ShapeDtypeStruct((T, 3 * ATTN_W), BF16),
                   jax.ShapeDtypeStruct((T, D), BF16)),
        in_specs=[pl.BlockSpec((tm, D), lambda i: (i, 0)), pl.BlockSpec((1, D), lambda i: (0, 0)),
                  pl.BlockSpec((D, NW), lambda i: (0, 0))],
        out_specs=(pl.BlockSpec((tm, lay["width"]), lambda i: (i, 0)), pl.BlockSpec((tm, 3 * ATTN_W), lambda i: (i, 0)),
                   pl.BlockSpec((tm, D), lambda i: (i, 0))),
        compiler_params=_params("parallel"),
    )(x, g, w)


def _window_sums(xh, forward):
    n = xh.shape[0]
    sums, s, step = [], xh, 1
    for _ in POOL_WINDOWS:
        s = s + pltpu.roll(s, step if forward else n - step, 0)
        sums.append(s)
        step *= 2
    return sums


def _pooled(u, halo, row):
    sums = _window_sums(jnp.concatenate([halo, u], axis=0), True)
    out = []
    for g, w in enumerate(POOL_WINDOWS):
        cols = slice(g * POOL_G, (g + 1) * POOL_G)
        cnt = jnp.minimum(row + 1, w).astype(F32)
        out.append(sums[g][HALO:, cols] / cnt - u[:, cols])
    return out


def _pool_fwd(proj, pool_w, scale, R):
    T = proj.shape[0]

    def body(u_ref, z_ref, pw_ref, sc_ref, y_ref, halo_sc):
        i = pl.program_id(0)

        @pl.when(i == 0)
        def _():
            halo_sc[...] = jnp.zeros_like(halo_sc)

        u = u_ref[...]
        row = i * R + lax.broadcasted_iota(jnp.int32, (R, 1), 0)
        pooled = _pooled(u, halo_sc[...], row)
        mixed = jnp.concatenate(
            [jnp.dot(pooled[g].astype(BF16), pw_ref[g].astype(BF16), preferred_element_type=F32)
             for g in range(len(POOL_WINDOWS))], axis=1)
        z = z_ref[...]
        y_ref[...] = ((mixed * sc_ref[...]) * (z * _sigmoid(z))).astype(BF16)
        halo_sc[...] = u[R - HALO:, :]

    return _pc(
        body, name="pool_fwd", grid=(T // R,), out_shape=jax.ShapeDtypeStruct((T, POOL_W), BF16),
        in_specs=[pl.BlockSpec((R, POOL_W), lambda i: (i, 0)), pl.BlockSpec((R, POOL_W), lambda i: (i, 1)),
                  pl.BlockSpec((4, POOL_G, POOL_G), lambda i: (0, 0, 0)), pl.BlockSpec((1, POOL_W), lambda i: (0, 0))],
        out_specs=pl.BlockSpec((R, POOL_W), lambda i: (i, 0)),
        scratch_shapes=[pltpu.VMEM((HALO, POOL_W), F32)],
        compiler_params=_params("arbitrary"),
    )(proj, proj, pool_w, scale)


def _softplus(l):
    return jnp.maximum(l, 0.0) + jnp.log(1.0 + jnp.exp(-jnp.abs(l)))


def _head_lanes():
    lane = lax.broadcasted_iota(jnp.int32, (1, LANE), 1)
    return [lane < HEAD_DIM, lane >= HEAD_DIM]


def _head_masks(q, scale):
    qf = q.astype(F32) * scale
    return [jnp.where(m, qf, 0.0).astype(BF16) for m in _head_lanes()]


def _wide(c, width):
    return jnp.concatenate([c] * (width // LANE), axis=1)


def _max_row_norm2(x, heads):
    sq = x.astype(F32) * x.astype(F32)
    return [jnp.max(jnp.sum(jnp.where(m, sq, 0.0), axis=1, keepdims=True), axis=0, keepdims=True) for m in heads]


def _fill_blocks(src_ref, nk, tk, transposed_sc=None, masked_sc=None, norm_sc=None):
    heads = _head_lanes()
    if norm_sc is not None:
        for a in range(2):
            norm_sc[a][...] = jnp.zeros_like(norm_sc[a])

    def step(j, carry):
        rows = pl.ds(pl.multiple_of(j * tk, tk), tk)
        blk = src_ref[rows, :]
        if norm_sc is not None:
            for a, n2 in enumerate(_max_row_norm2(blk, heads)):
                norm_sc[a][...] = jnp.maximum(norm_sc[a][...], n2)
        if transposed_sc is not None:
            transposed_sc[j] = blk.astype(F32).T.astype(BF16)
        if masked_sc is not None:
            for a in range(2):
                masked_sc[a, rows, :] = jnp.where(heads[a], blk, jnp.zeros_like(blk))
        return carry

    lax.fori_loop(0, nk, step, 0)


def _attn_fwd(projb, proj, tq):
    T = projb.shape[0]
    nq = T // tq
    tk, nk = tq, nq
    assert nk <= LANE
    n_pairs = ATTN_W // LANE
    zb = (proj.shape[1] - ATTN_W) // LANE

    def body(q_ref, k_ref, v_ref, za_ref, o_ref, y_ref, c_ref, kT_sc, vm_sc, *per_head):
        c_sc, cm_sc, o_sc, kn_sc, l_sc, w_sc = (per_head[2 * n:2 * n + 2] for n in range(6))
        i = pl.program_id(1)

        @pl.when(i == 0)
        def _():
            _fill_blocks(k_ref, nk, tk, transposed_sc=kT_sc, norm_sc=kn_sc)
            _fill_blocks(v_ref, nk, tk, masked_sc=vm_sc)

        qs = _head_masks(q_ref[...], HEAD_DIM ** -0.5)
        lane = lax.broadcasted_iota(jnp.int32, (1, LANE), 1)
        valid = lax.broadcasted_iota(jnp.int32, (tq, tk), 1) < lax.broadcasted_iota(jnp.int32, (tq, tk), 0)
        suffix = (lax.broadcasted_iota(jnp.int32, (tk, tk), 0) >= lax.broadcasted_iota(jnp.int32, (tk, tk), 1)).astype(BF16)
        for a in range(2):
            c_sc[a][...] = jnp.zeros_like(c_sc[a])
            o_sc[a][...] = jnp.zeros_like(o_sc[a])
            cm_sc[a][...] = jnp.full(cm_sc[a].shape, NO_CARRY, F32)
        l_max = [jnp.sqrt(qn * kn_sc[a][...]) for a, qn in enumerate(_max_row_norm2(qs[0] + qs[1], _head_lanes()))]

        def logits(j, slot):
            kT = kT_sc[j]
            for a in range(2):
                l_sc[a][slot] = jnp.dot(qs[a], kT, preferred_element_type=F32)

        def values(j, slot):
            rows = pl.ds(pl.multiple_of(j * tk, tk), tk)
            for a in range(2):
                o_sc[a][...] += jnp.dot(w_sc[a][slot], vm_sc[a, rows, :], preferred_element_type=F32)

        def weights(j, slot, masked):
            sp = []
            for a in range(2):
                x = _softplus(l_sc[a][slot])
                sp.append((jnp.where(valid, x, 0.0) if masked else x).astype(BF16))
            inc = [jnp.dot(sp[a], suffix, preferred_element_type=F32) for a in range(2)]
            for a in range(2):
                c = c_sc[a][...]
                w = jnp.exp(l_sc[a][slot] - inc[a] - _wide(c, tk))
                if masked:
                    w = jnp.where(valid, w, 0.0)
                w_sc[a][slot] = w.astype(BF16)
                cm_sc[a][...] = jnp.where(lane == j, c, cm_sc[a][...])
                c_sc[a][...] = c + inc[a][:, 0:1]

        def more():
            live = [jnp.min(c_sc[a][...], axis=0, keepdims=True) - l_max[a][0:1, :] <= ZERO_WEIGHT for a in range(2)]
            return jnp.max(jnp.where(jnp.logical_or(live[0], live[1]), 1, 0))

        logits(i, 0)
        logits(jnp.maximum(i - 1, 0), 1)
        weights(i, 0, True)

        def step(t, slot):
            logits(jnp.maximum(i - t - 1, 0), 1 - slot)
            values(i - t + 1, 1 - slot)
            weights(i - t, slot, False)

        def two_steps(carry):
            tt, _ = carry
            step(2 * tt + 1, 1)
            step(2 * tt + 2, 0)
            return tt + 1, more()

        pairs, go = lax.while_loop(lambda c: jnp.logical_and(c[0] < i // 2, c[1] > 0), two_steps, (0, more()))
        done = 2 * pairs
        one_more = jnp.logical_and(done == i - 1, go > 0)

        @pl.when(one_more)
        def _():
            step(i, 1)
            values(0, 1)

        @pl.when(jnp.logical_not(one_more))
        def _():
            values(i - done, 0)

        o = o_sc[0][...] + o_sc[1][...]
        o_ref[...] = o
        za = za_ref[...]
        y_ref[...] = (o * (za * _sigmoid(za))).astype(BF16)
        c_ref[0, 0] = cm_sc[0][...]
        c_ref[1, 0] = cm_sc[1][...]

    return _pc(
        body, name="attn_fwd", grid=(n_pairs, nq),
        out_shape=(jax.ShapeDtypeStruct((T, ATTN_W), F32), jax.ShapeDtypeStruct((T, ATTN_W), BF16),
                   jax.ShapeDtypeStruct((2 * n_pairs, nq, tq, LANE), F32)),
        in_specs=[pl.BlockSpec((tq, LANE), lambda p, i: (i, p)),
                  pl.BlockSpec((T, LANE), lambda p, i: (0, n_pairs + p)),
                  pl.BlockSpec((T, LANE), lambda p, i: (0, 2 * n_pairs + p)),
                  pl.BlockSpec((tq, LANE), lambda p, i: (i, zb + p))],
        out_specs=(pl.BlockSpec((tq, LANE), lambda p, i: (i, p)), pl.BlockSpec((tq, LANE), lambda p, i: (i, p)),
                   pl.BlockSpec((2, 1, tq, LANE), lambda p, i: (p, i, 0, 0))),
        scratch_shapes=[pltpu.VMEM((nk, LANE, tk), BF16), pltpu.VMEM((2, T, LANE), BF16)]
        + [pltpu.VMEM((tq, LANE), F32)] * 6 + [pltpu.VMEM((8, LANE), F32)] * 2
        + [pltpu.VMEM((2, tq, tk), F32)] * 2 + [pltpu.VMEM((2, tq, tk), BF16)] * 2,
        compiler_params=_params("arbitrary", "arbitrary"),
    )(projb, projb, projb, proj)


def _merge_fwd(y_pool, y_attn, w_pu, w_au, proj, b_gate, tm):
    T = y_pool.shape[0]
    D = w_pu.shape[1]
    gb = _proj_layout(D)["gates"] // D

    def body(yp_ref, ya_ref, wpu_ref, wau_ref, gl0_ref, gl1_ref, bg_ref, m_ref, p_ref, a_ref):
        p = jnp.dot(yp_ref[...], wpu_ref[...], preferred_element_type=F32)
        a = jnp.dot(ya_ref[...], wau_ref[...], preferred_element_type=F32)
        g0 = _sigmoid(gl0_ref[...] + bg_ref[:, :D])
        g1 = _sigmoid(gl1_ref[...] + bg_ref[:, D:])
        m_ref[...] = (g0 * p + g1 * a).astype(BF16)
        p_ref[...] = p
        a_ref[...] = a

    row = lambda i: (i, 0)
    fixed = lambda i: (0, 0)
    return _pc(
        body, name="merge_fwd", grid=(T // tm,),
        out_shape=(jax.ShapeDtypeStruct((T, D), BF16), jax.ShapeDtypeStruct((T, D), F32),
                   jax.ShapeDtypeStruct((T, D), F32)),
        in_specs=[pl.BlockSpec((tm, POOL_W), row), pl.BlockSpec((tm, ATTN_W), row),
                  pl.BlockSpec((POOL_W, D), fixed), pl.BlockSpec((ATTN_W, D), fixed),
                  pl.BlockSpec((tm, D), lambda i: (i, gb)), pl.BlockSpec((tm, D), lambda i: (i, gb + 1)),
                  pl.BlockSpec((1, 2 * D), fixed)],
        out_specs=(pl.BlockSpec((tm, D), row), pl.BlockSpec((tm, D), row), pl.BlockSpec((tm, D), row)),
        compiler_params=_params("parallel"),
    )(y_pool, y_attn, w_pu, w_au, proj, proj, b_gate)


def _final_loss(x, g, target, tm):
    T, D = x.shape

    def body(x_ref, g_ref, t_ref, dx_ref, dg_ref, loss_ref):
        @pl.when(pl.program_id(0) == 0)
        def _():
            dg_ref[...] = jnp.zeros_like(dg_ref)
            loss_ref[...] = jnp.zeros_like(loss_ref)

        xv, gv = x_ref[...], g_ref[...]
        r = lax.rsqrt(jnp.mean(xv * xv, axis=-1, keepdims=True) + RMS_EPS)
        xh = xv * r
        d = xh * gv - t_ref[...]
        loss_ref[...] += 0.5 * jnp.sum(jnp.mean(d * d, axis=-1, keepdims=True), axis=0, keepdims=True)
        dy = d * (1.0 / D)
        dg_ref[...] += jnp.sum(dy * xh, axis=0, keepdims=True)
        dh = dy * gv
        dx_ref[...] = r * (dh - xh * jnp.mean(dh * xh, axis=-1, keepdims=True))

    return _pc(
        body, name="final_loss", grid=(T // tm,),
        out_shape=(jax.ShapeDtypeStruct((T, D), F32), jax.ShapeDtypeStruct((1, D), F32),
                   jax.ShapeDtypeStruct((8, LANE), F32)),
        in_specs=[pl.BlockSpec((tm, D), lambda i: (i, 0)), pl.BlockSpec((1, D), lambda i: (0, 0)),
                  pl.BlockSpec((tm, D), lambda i: (i, 0))],
        out_specs=(pl.BlockSpec((tm, D), lambda i: (i, 0)), pl.BlockSpec((1, D), lambda i: (0, 0)),
                   pl.BlockSpec((8, LANE), lambda i: (0, 0))),
        compiler_params=_params("arbitrary"),
    )(x, g, target)


def _merge_bwd(dm, p, a, proj, b_gate, tm):
    T, D = dm.shape
    gb = _proj_layout(D)["gates"] // D

    def body(dm_ref, p_ref, a_ref, gl0_ref, gl1_ref, bg_ref, dp_ref, da_ref, dgl_ref, dbg_ref):
        @pl.when(pl.program_id(0) == 0)
        def _():
            dbg_ref[...] = jnp.zeros_like(dbg_ref)

        dmv = dm_ref[...]
        g0 = _sigmoid(gl0_ref[...] + bg_ref[:, :D])
        g1 = _sigmoid(gl1_ref[...] + bg_ref[:, D:])
        dp_ref[...] = (dmv * g0).astype(BF16)
        da_ref[...] = (dmv * g1).astype(BF16)
        dgl0 = dmv * p_ref[...] * (g0 * (1.0 - g0))
        dgl1 = dmv * a_ref[...] * (g1 * (1.0 - g1))
        dgl_ref[:, :D] = dgl0.astype(BF16)
        dgl_ref[:, D:] = dgl1.astype(BF16)
        dbg_ref[:, :D] += jnp.sum(dgl0, axis=0, keepdims=True)
        dbg_ref[:, D:] += jnp.sum(dgl1, axis=0, keepdims=True)

    row = lambda i: (i, 0)
    fixed = lambda i: (0, 0)
    return _pc(
        body, name="merge_bwd", grid=(T // tm,),
        out_shape=(jax.ShapeDtypeStruct((T, D), BF16), jax.ShapeDtypeStruct((T, D), BF16),
                   jax.ShapeDtypeStruct((T, 2 * D), BF16), jax.ShapeDtypeStruct((1, 2 * D), F32)),
        in_specs=[pl.BlockSpec((tm, D), row), pl.BlockSpec((tm, D), row), pl.BlockSpec((tm, D), row),
                  pl.BlockSpec((tm, D), lambda i: (i, gb)), pl.BlockSpec((tm, D), lambda i: (i, gb + 1)),
                  pl.BlockSpec((1, 2 * D), fixed)],
        out_specs=(pl.BlockSpec((tm, D), row), pl.BlockSpec((tm, D), row), pl.BlockSpec((tm, 2 * D), row),
                   pl.BlockSpec((1, 2 * D), fixed)),
        compiler_params=_params("arbitrary"),
    )(dm, p, a, proj, proj, b_gate)


def _pool_bwd(proj, dy, pool_w, scale, R):
    T = proj.shape[0]
    nb = T // R
    hb = R // HALO

    def body(u_ref, up_ref, z_ref, dy_ref, pw_ref, sc_ref, du_ref, dz_ref, dpw_ref, dsc_ref, halo_sc):
        i = pl.program_id(0)
        rb = nb - 1 - i

        @pl.when(i == 0)
        def _():
            halo_sc[...] = jnp.zeros_like(halo_sc)
            dpw_ref[...] = jnp.zeros_like(dpw_ref)
            dsc_ref[...] = jnp.zeros_like(dsc_ref)

        u = u_ref[...]
        row = rb * R + lax.broadcasted_iota(jnp.int32, (R, 1), 0)
        before = jnp.where(rb > 0, up_ref[...], 0.0)
        pooled = _pooled(u, before, row)
        pw = [pw_ref[g].astype(BF16) for g in range(len(POOL_WINDOWS))]
        mixed = jnp.concatenate(
            [jnp.dot(pooled[g].astype(BF16), pw[g], preferred_element_type=F32) for g in range(len(POOL_WINDOWS))],
            axis=1)
        sc = sc_ref[...]
        silu, dsilu = _silu_and_grad(z_ref[...])
        dyv = dy_ref[...]
        dmp = dyv * silu
        dz_ref[...] = (dyv * (mixed * sc) * dsilu).astype(BF16)
        dsc_ref[...] += jnp.sum(dmp * mixed, axis=0, keepdims=True)
        dmixed = (dmp * sc).astype(BF16)
        dpn = []
        dpooled = []
        for g, w in enumerate(POOL_WINDOWS):
            cols = slice(g * POOL_G, (g + 1) * POOL_G)
            dpw_ref[g] += lax.dot_general(pooled[g].astype(BF16), dmixed[:, cols], TN_DIMS,
                                          preferred_element_type=F32)
            dpg = lax.dot_general(dmixed[:, cols], pw[g], NT_DIMS, preferred_element_type=F32)
            dpooled.append(dpg)
            dpn.append(dpg / jnp.minimum(row + 1, w).astype(F32))
        dpn = jnp.concatenate(dpn, axis=1)
        sums = _window_sums(jnp.concatenate([dpn, halo_sc[...]], axis=0), False)
        du_ref[...] = jnp.concatenate(
            [sums[g][:R, g * POOL_G:(g + 1) * POOL_G] - dpooled[g] for g in range(len(POOL_WINDOWS))],
            axis=1).astype(BF16)
        halo_sc[...] = dpn[:HALO, :]

    rev = lambda i: (nb - 1 - i, 0)
    return _pc(
        body, name="pool_bwd", grid=(nb,),
        out_shape=(jax.ShapeDtypeStruct((T, POOL_W), BF16), jax.ShapeDtypeStruct((T, POOL_W), BF16),
                   jax.ShapeDtypeStruct((4, POOL_G, POOL_G), F32), jax.ShapeDtypeStruct((1, POOL_W), F32)),
        in_specs=[pl.BlockSpec((R, POOL_W), rev),
                  pl.BlockSpec((HALO, POOL_W), lambda i: (jnp.maximum((nb - 1 - i) * hb - 1, 0), 0)),
                  pl.BlockSpec((R, POOL_W), lambda i: (nb - 1 - i, 1)),
                  pl.BlockSpec((R, POOL_W), rev),
                  pl.BlockSpec((4, POOL_G, POOL_G), lambda i: (0, 0, 0)), pl.BlockSpec((1, POOL_W), lambda i: (0, 0))],
        out_specs=(pl.BlockSpec((R, POOL_W), rev), pl.BlockSpec((R, POOL_W), rev),
                   pl.BlockSpec((4, POOL_G, POOL_G), lambda i: (0, 0, 0)), pl.BlockSpec((1, POOL_W), lambda i: (0, 0))),
        scratch_shapes=[pltpu.VMEM((HALO, POOL_W), F32)],
        compiler_params=_params("arbitrary"),
    )(proj, proj, proj, dy, pool_w, scale)


def _attn_gate_bwd(dya, o, proj, tm):
    T = dya.shape[0]
    zb = (proj.shape[1] - ATTN_W) // ATTN_W

    def body(dy_ref, o_ref, za_ref, do_ref, dza_ref):
        silu, dsilu = _silu_and_grad(za_ref[...])
        dyv = dy_ref[...]
        do_ref[...] = (dyv * silu).astype(BF16)
        dza_ref[...] = (dyv * o_ref[...] * dsilu).astype(BF16)

    row = lambda i: (i, 0)
    return _pc(
        body, name="attn_gate_bwd", grid=(T // tm,),
        out_shape=(jax.ShapeDtypeStruct((T, ATTN_W), BF16), jax.ShapeDtypeStruct((T, ATTN_W), BF16)),
        in_specs=[pl.BlockSpec((tm, ATTN_W), row), pl.BlockSpec((tm, ATTN_W), row),
                  pl.BlockSpec((tm, ATTN_W), lambda i: (i, zb))],
        out_specs=(pl.BlockSpec((tm, ATTN_W), row), pl.BlockSpec((tm, ATTN_W), row)),
        compiler_params=_params("parallel"),
    )(dya, o, proj)


def _attn_bwd(projb, do, carries, tq):
    T = projb.shape[0]
    nq = T // tq
    tk, nk = tq, nq
    n_pairs = ATTN_W // LANE
    scale = HEAD_DIM ** -0.5

    def body(q_ref, k_ref, v_ref, do_ref, c_ref, dq_ref, dkT_ref, dvT_ref, kT_sc, vT_sc, km_sc, *per_head):
        f_sc, dq_sc, kn_sc, l_sc, dw_sc, dl_sc, w_sc = (per_head[2 * n:2 * n + 2] for n in range(7))
        i = pl.program_id(1)

        @pl.when(i == 0)
        def _():
            _fill_blocks(k_ref, nk, tk, transposed_sc=kT_sc, masked_sc=km_sc, norm_sc=kn_sc)
            _fill_blocks(v_ref, nk, tk, transposed_sc=vT_sc)
            dkT_ref[...] = jnp.zeros_like(dkT_ref)
            dvT_ref[...] = jnp.zeros_like(dvT_ref)

        qs = _head_masks(q_ref[...], scale)
        dos = _head_masks(do_ref[...], 1.0)
        qT = [x.astype(F32).T.astype(BF16) for x in qs]
        doT = [x.astype(F32).T.astype(BF16) for x in dos]
        lane = lax.broadcasted_iota(jnp.int32, (1, LANE), 1)
        valid = lax.broadcasted_iota(jnp.int32, (tq, tk), 1) < lax.broadcasted_iota(jnp.int32, (tq, tk), 0)
        kk0 = lax.broadcasted_iota(jnp.int32, (tk, tk), 0)
        kk1 = lax.broadcasted_iota(jnp.int32, (tk, tk), 1)
        suffix = (kk0 >= kk1).astype(BF16)
        prefix = (kk0 <= kk1).astype(BF16)
        for a in range(2):
            f_sc[a][...] = jnp.zeros_like(f_sc[a])
            dq_sc[a][...] = jnp.zeros_like(dq_sc[a])
            dl_sc[a][1] = jnp.zeros((tq, tk), BF16)
            w_sc[a][1] = jnp.zeros((tq, tk), BF16)
        live = lane == i
        for a, qn in enumerate(_max_row_norm2(qs[0] + qs[1], _head_lanes())):
            l_max = jnp.sqrt(qn * kn_sc[a][0:1, :])
            live = jnp.logical_or(live, jnp.min(c_ref[a, 0], axis=0, keepdims=True) - l_max <= ZERO_WEIGHT)
        t0 = jnp.min(jnp.where(jnp.logical_and(live, lane <= i), lane, i))
        n = i - t0

        def products(j, slot):
            kT = kT_sc[j]
            vT = vT_sc[j]
            for a in range(2):
                l_sc[a][slot] = jnp.dot(qs[a], kT, preferred_element_type=F32)
                dw_sc[a][slot] = jnp.dot(dos[a], vT, preferred_element_type=F32)

        def gradients(j, slot):
            rows = pl.ds(pl.multiple_of(j * tk, tk), tk)
            dkT = []
            dvT = []
            for a in range(2):
                dlb = dl_sc[a][slot]
                dq_sc[a][...] += jnp.dot(dlb, km_sc[a, rows, :], preferred_element_type=F32)
                dkT.append(jnp.dot(qT[a], dlb, preferred_element_type=F32))
                dvT.append(jnp.dot(doT[a], w_sc[a][slot], preferred_element_type=F32))
            dkT_ref[j] += dkT[0] + dkT[1]
            dvT_ref[j] += dvT[0] + dvT[1]

        def elementwise(j, slot, masked):
            sp, inc, e, beta, p = [None] * 2, [None] * 2, [None] * 2, [None] * 2, [None] * 2
            for a in range(2):
                x = _softplus(l_sc[a][slot])
                sp[a] = jnp.where(valid, x, 0.0) if masked else x
            for a in range(2):
                inc[a] = jnp.dot(sp[a].astype(BF16), suffix, preferred_element_type=F32)
            for a in range(2):
                l = l_sc[a][slot]
                c = jnp.sum(jnp.where(lane == j, c_ref[a, 0], 0.0), axis=1, keepdims=True)
                w = jnp.exp(l - inc[a] - c)
                if masked:
                    w = jnp.where(valid, w, 0.0)
                w_sc[a][slot] = w.astype(BF16)
                beta[a] = jnp.exp(l - sp[a])
                e[a] = w * dw_sc[a][slot]
            for a in range(2):
                p[a] = jnp.dot(e[a].astype(BF16), prefix, preferred_element_type=F32)
            for a in range(2):
                f = f_sc[a][...]
                dl = e[a] - beta[a] * (p[a] + _wide(f, tk))
                if masked:
                    dl = jnp.where(valid, dl, 0.0)
                dl_sc[a][slot] = dl.astype(BF16)
                f_sc[a][...] = f + p[a][:, tk - 1:tk]

        def step(r, slot):
            products(t0 + r + 1, 1 - slot)
            gradients(t0 + jnp.maximum(r - 1, 0), 1 - slot)
            elementwise(t0 + r, slot, False)

        def last(slot):
            gradients(t0 + jnp.maximum(n - 1, 0), 1 - slot)
            elementwise(i, slot, True)
            gradients(i, slot)

        products(t0, 0)

        def two_steps(tt, carry):
            step(2 * tt, 0)
            step(2 * tt + 1, 1)
            return carry

        lax.fori_loop(0, n // 2, two_steps, 0)

        @pl.when(n % 2 == 1)
        def _():
            step(n - 1, 0)
            last(1)

        @pl.when(n % 2 == 0)
        def _():
            last(0)

        dq_ref[...] = ((dq_sc[0][...] + dq_sc[1][...]) * scale).astype(BF16)

    return _pc(
        body, name="attn_bwd", grid=(n_pairs, nq),
        out_shape=(jax.ShapeDtypeStruct((T, ATTN_W), BF16), jax.ShapeDtypeStruct((n_pairs, nk, LANE, tk), F32),
                   jax.ShapeDtypeStruct((n_pairs, nk, LANE, tk), F32)),
        in_specs=[pl.BlockSpec((tq, LANE), lambda p, i: (i, p)),
                  pl.BlockSpec((T, LANE), lambda p, i: (0, n_pairs + p)),
                  pl.BlockSpec((T, LANE), lambda p, i: (0, 2 * n_pairs + p)),
                  pl.BlockSpec((tq, LANE), lambda p, i: (i, p)),
                  pl.BlockSpec((2, 1, tq, LANE), lambda p, i: (p, i, 0, 0))],
        out_specs=(pl.BlockSpec((tq, LANE), lambda p, i: (i, p)),
                   pl.BlockSpec((None, nk, LANE, tk), lambda p, i: (p, 0, 0, 0)),
                   pl.BlockSpec((None, nk, LANE, tk), lambda p, i: (p, 0, 0, 0))),
        scratch_shapes=[pltpu.VMEM((nk, LANE, tk), BF16), pltpu.VMEM((nk, LANE, tk), BF16),
                        pltpu.VMEM((2, T, LANE), BF16)]
        + [pltpu.VMEM((tq, LANE), F32)] * 4 + [pltpu.VMEM((8, LANE), F32)] * 2
        + [pltpu.VMEM((2, tq, tk), F32)] * 4 + [pltpu.VMEM((2, tq, tk), BF16)] * 4,
        compiler_params=_params("arbitrary", "arbitrary"),
    )(projb, projb, projb, do, carries)


def _norm_bwd(dh, x, g, dxo, tm):
    T, D = x.shape

    def body(dh_ref, x_ref, g_ref, dxo_ref, dx_ref, dg_ref):
        @pl.when(pl.program_id(0) == 0)
        def _():
            dg_ref[...] = jnp.zeros_like(dg_ref)

        xv = x_ref[...]
        r = lax.rsqrt(jnp.mean(xv * xv, axis=-1, keepdims=True) + RMS_EPS)
        xh = xv * r
        dhv = dh_ref[...]
        dg_ref[...] += jnp.sum(dhv * xh, axis=0, keepdims=True)
        dhg = dhv * g_ref[...]
        dx_ref[...] = dxo_ref[...] + r * (dhg - xh * jnp.mean(dhg * xh, axis=-1, keepdims=True))

    row = lambda i: (i, 0)
    return _pc(
        body, name="norm_bwd", grid=(T // tm,),
        out_shape=(jax.ShapeDtypeStruct((T, D), F32), jax.ShapeDtypeStruct((1, D), F32)),
        in_specs=[pl.BlockSpec((tm, D), row), pl.BlockSpec((tm, D), row), pl.BlockSpec((1, D), lambda i: (0, 0)),
                  pl.BlockSpec((tm, D), row)],
        out_specs=(pl.BlockSpec((tm, D), row), pl.BlockSpec((1, D), lambda i: (0, 0))),
        compiler_params=_params("arbitrary"),
    )(dh, x, g, dxo)


def _adamw(pieces, w, m, v, name):
    rows, cols = w.shape
    br = rows
    while br * cols > 65536 and br % 16 == 0:
        br //= 2
    c1 = 1.0 / (1.0 - ADAM_B1 ** ADAM_STEP)
    c2 = 1.0 / (1.0 - ADAM_B2 ** ADAM_STEP)

    def body(p_ref, w_ref, m_ref, v_ref, g_ref, d_ref, nm_ref, nv_ref):
        g = p_ref[0].astype(F32)
        for s in range(1, N_DEV):
            g = g + p_ref[s].astype(F32)
        nm = ADAM_B1 * m_ref[...] + (1.0 - ADAM_B1) * g
        nv = ADAM_B2 * v_ref[...] + (1.0 - ADAM_B2) * (g * g)
        g_ref[...] = g
        nm_ref[...] = nm
        nv_ref[...] = nv
        d_ref[...] = -ADAM_LR * ((nm * c1) / (jnp.sqrt(nv * c2) + ADAM_EPS) + ADAM_WD * w_ref[...])

    blk = pl.BlockSpec((br, cols), lambda i: (i, 0))
    shape = jax.ShapeDtypeStruct((rows, cols), F32)
    return _pc(
        body, name=name, grid=(rows // br,), out_shape=(shape, shape, shape, shape),
        in_specs=[pl.BlockSpec((N_DEV, br, cols), lambda i: (0, i, 0)), blk, blk, blk],
        out_specs=(blk, blk, blk, blk),
        compiler_params=_params("parallel"),
    )(pieces, w, m, v)


def _rows128(a):
    flat = a.reshape(-1)
    n = flat.shape[0]
    padded = -(-n // (8 * LANE)) * (8 * LANE)
    if padded != n:
        flat = jnp.concatenate([flat, jnp.zeros((padded - n,), flat.dtype)])
    return flat.reshape(-1, LANE)


def _pack(parts):
    return jnp.concatenate([_rows128(p) for p in parts], axis=0)


def _unpack(packed, like):
    out, r = [], 0
    for a in like:
        n = a.size
        nr = -(-n // (8 * LANE)) * 8
        out.append(packed[r:r + nr].reshape(-1)[:n].reshape(a.shape))
        r += nr
    return out


def kernel(x, norm_g, w_in, b_gate, pool_w, pool_scale, w_pool_up, w_attn_up, w_out, final_g, loss_target, m_norm_g, m_w_in, m_b_gate, m_pool_w, m_pool_scale, m_w_pool_up, m_w_attn_up, m_w_out, m_final_g, v_norm_g, v_w_in, v_b_gate, v_pool_w, v_pool_scale, v_w_pool_up, v_w_attn_up, v_w_out, v_final_g):
    L = norm_g.shape[0]
    T, D = x.shape[1], x.shape[2]
    NW = w_in.shape[2] * N_DEV
    assert NW == 2 * POOL_W + 4 * ATTN_W + 2 * D and x.shape[0] == 1
    tm = min(512, T)
    tq = min(256, T // 2)
    x0 = x.reshape(T, D)
    target = loss_target.reshape(T, D)

    g_in, g_pu, g_au, g_out = _exchange(
        [w_in.astype(BF16), w_pool_up.astype(BF16), w_attn_up.astype(BF16), w_out.astype(BF16)],
        [False] * 4, "gather_weights")
    win_full = jnp.transpose(g_in, (1, 2, 0, 3)).reshape(L, D, NW)
    wpu_full = jnp.transpose(g_pu, (1, 2, 0, 3)).reshape(L, POOL_W, D)
    wau_full = jnp.transpose(g_au, (1, 2, 0, 3)).reshape(L, ATTN_W, D)
    wout_full = jnp.transpose(g_out, (1, 0, 2, 3)).reshape(L, D, D)

    saved = []
    xl = x0
    for l in range(L):
        proj, projb, h = _norm_inproj(xl, norm_g[l:l + 1], win_full[l], min(256, T))
        y_pool = _pool_fwd(proj, pool_w[l], pool_scale[l:l + 1], tm)
        o, y_attn, carries = _attn_fwd(projb, proj, tq)
        merged, p, a = _merge_fwd(y_pool, y_attn, wpu_full[l], wau_full[l], proj, b_gate[l:l + 1], min(256, T))
        x_next = _mm_nn_res(merged, wout_full[l], xl, tm, "out_proj")
        saved.append((xl, proj, projb, h, y_pool, o, y_attn, carries, merged, p, a))
        xl = x_next

    dx, d_final_g, loss_part = _final_loss(xl, final_g.reshape(1, D), target, tm)

    d_norm_g, d_b_gate, d_pool_w, d_pool_scale = [None] * L, [None] * L, [None] * L, [None] * L
    d_win, d_wpu, d_wau, d_wout = [None] * L, [None] * L, [None] * L, [None] * L
    for l in reversed(range(L)):
        xin, proj, projb, h, y_pool, o, y_attn, carries, merged, p, a = saved[l]
        dm = _mm_nt(dx, wout_full[l], tm, D, "d_merged")
        d_wout[l] = _mm_tn(merged, dx, 1, D, min(1024, T), "d_w_out").reshape(N_DEV, D // N_DEV, D)
        dp, da, dgl, d_b_gate[l] = _merge_bwd(dm, p, a, proj, b_gate[l:l + 1], min(256, T))
        d_wpu[l] = _mm_tn(y_pool, dp, N_DEV, D, min(1024, T), "d_w_pool_up")
        d_wau[l] = _mm_tn(y_attn, da, N_DEV, D, min(1024, T), "d_w_attn_up")
        dyp = _mm_nt(dp, wpu_full[l], tm, D, "d_y_pool")
        dya = _mm_nt(da, wau_full[l], tm, D, "d_y_attn")
        du, dzp, d_pool_w[l], d_pool_scale[l] = _pool_bwd(proj, dyp, pool_w[l], pool_scale[l:l + 1], tm)
        do, dza = _attn_gate_bwd(dya, o, proj, tm)
        dq, dkT, dvT = _attn_bwd(projb, do, carries, tq)
        dk, dv = [jnp.transpose(t, (1, 3, 0, 2)).reshape(T, ATTN_W).astype(BF16) for t in (dkT, dvT)]
        dproj = jnp.concatenate([du, dzp, dq, dk, dv, dza, dgl], axis=1)
        d_win[l] = _mm_tn(h, dproj, N_DEV, NW // 2, min(1024, T), "d_w_in")
        dh = _mm_nt(dproj, win_full[l], min(256, T), NW, "d_h")
        dx, d_norm_g[l] = _norm_bwd(dh, xin, norm_g[l:l + 1], dx, tm)

    small_like = [norm_g, b_gate, pool_w, pool_scale, final_g, jnp.zeros((8, LANE), F32)]
    small = _pack([jnp.concatenate(d_norm_g, 0), jnp.concatenate(d_b_gate, 0), jnp.stack(d_pool_w, 0),
                   jnp.concatenate(d_pool_scale, 0), d_final_g, loss_part])
    sends = [small]
    flags = [False]
    for l in range(L):
        sends += [d_win[l], d_wpu[l], d_wau[l], d_wout[l]]
        flags += [True] * 4
    got = _exchange(sends, flags, "scatter_grads")
    r_small = got[0]
    r_in = jnp.stack([got[1 + 4 * l] for l in range(L)], axis=1)
    r_pu = jnp.stack([got[2 + 4 * l] for l in range(L)], axis=1)
    r_au = jnp.stack([got[3 + 4 * l] for l in range(L)], axis=1)
    r_out = jnp.stack([got[4 + 4 * l] for l in range(L)], axis=1)

    def update(pieces, w, m, v, name):
        cols = w.shape[-1]
        res = _adamw(pieces.reshape(N_DEV, -1, cols), w.reshape(-1, cols), m.reshape(-1, cols),
                     v.reshape(-1, cols), name)
        return [r.reshape(w.shape) for r in res]

    u_in = update(r_in, w_in, m_w_in, v_w_in, "adamw_w_in")
    u_pu = update(r_pu, w_pool_up, m_w_pool_up, v_w_pool_up, "adamw_w_pool_up")
    u_au = update(r_au, w_attn_up, m_w_attn_up, v_w_attn_up, "adamw_w_attn_up")
    u_out = update(r_out, w_out, m_w_out, v_w_out, "adamw_w_out")
    zeros = small_like[-1]
    smalls = _adamw(r_small,
                    _pack([norm_g, b_gate, pool_w, pool_scale, final_g, zeros]),
                    _pack([m_norm_g, m_b_gate, m_pool_w, m_pool_scale, m_final_g, zeros]),
                    _pack([v_norm_g, v_b_gate, v_pool_w, v_pool_scale, v_final_g, zeros]), "adamw_small")
    s_g, s_d, s_m, s_v = [_unpack(s, small_like) for s in smalls]
    loss = s_g[5][0, 0]

    def ordered(k):
        s = (s_g, s_d, s_m, s_v)[k]
        return [s[0], u_in[k], s[1], s[2], s[3], u_pu[k], u_au[k], u_out[k], s[4]]

    return (loss, dx.reshape(x.shape), *ordered(0), *ordered(1), *ordered(2), *ordered(3))
```

```python
import jax
import jax.numpy as jnp
from jax import lax
from jax.experimental import pallas as pl
from jax.experimental.pallas import tpu as pltpu

F32 = jnp.float32
BF16 = jnp.bfloat16

N_DEV = 8
HEAD_DIM = 64
ATTN_W = 512
POOL_W = 512
POOL_G = 128
POOL_WINDOWS = (2, 4, 8, 16)
HALO = 16
LANE = 128
RMS_EPS = 1e-6
ZERO_WEIGHT = 110.0
NO_CARRY = 3.0e38
ADAM_LR, ADAM_B1, ADAM_B2, ADAM_EPS, ADAM_WD, ADAM_STEP = 0.001, 0.9, 0.999, 1e-08, 0.01, 10
VMEM_LIMIT = 56 * 1024 * 1024

NT_DIMS = (((1,), (1,)), ((), ()))
TN_DIMS = (((0,), (0,)), ((), ()))


def _pc(body, **kw):
    return pl.pallas_call(body, **kw)


def _params(*sem):
    return pltpu.CompilerParams(dimension_semantics=sem, vmem_limit_bytes=VMEM_LIMIT)


def _sigmoid(z):
    return 1.0 / (1.0 + jnp.exp(-z))


def _silu_and_grad(z):
    s = _sigmoid(z)
    return z * s, s * (1.0 + z * (1.0 - s))


def _my_index():
    return 4 * lax.axis_index("x") + 2 * lax.axis_index("y") + lax.axis_index("c")


def _peer(k):
    x, y, c = lax.axis_index("x"), lax.axis_index("y"), lax.axis_index("c")
    px = lax.rem(x + ((k >> 2) & 1), 2)
    py = lax.rem(y + ((k >> 1) & 1), 2)
    pc = lax.rem(c + (k & 1), 2)
    return (px, py, pc), 4 * px + 2 * py + pc


def _exchange_copies(ins, outs, sems, scatter):
    send_sems, recv_sems, local_sems = sems
    n = len(ins)
    me = _my_index()

    def src(a, idx):
        return ins[a].at[idx] if scatter[a] else ins[a]

    local = [pltpu.make_async_copy(src(a, me), outs[a].at[me], local_sems.at[a]) for a in range(n)]
    sends, arrivals = [], []
    for k in (1, 2, 4, 3, 5, 6, 7):
        dev, pidx = _peer(k)
        for a in range(n):
            sem = dict(send_sem=send_sems.at[a * N_DEV + k], recv_sem=recv_sems.at[a * N_DEV + k],
                       device_id=dev, device_id_type=pl.DeviceIdType.MESH)
            sends.append(pltpu.make_async_remote_copy(src_ref=src(a, pidx), dst_ref=outs[a].at[me], **sem))
            arrivals.append(pltpu.make_async_remote_copy(src_ref=src(a, pidx), dst_ref=outs[a].at[pidx], **sem))
    return local, sends, arrivals


def _exchange_start(ins, outs, sems, scatter):
    local, sends, _ = _exchange_copies(ins, outs, sems, scatter)
    for cp in local + sends:
        cp.start()


def _exchange_wait(ins, outs, sems, scatter):
    local, sends, arrivals = _exchange_copies(ins, outs, sems, scatter)
    for cp in arrivals:
        cp.wait_recv()
    for cp in sends:
        cp.wait_send()
    for cp in local:
        cp.wait()


def _exchange_shapes(arrays, scatter):
    n = len(arrays)
    out_shape = [jax.ShapeDtypeStruct((N_DEV,) + tuple(a.shape[1:] if s else a.shape), a.dtype)
                 for a, s in zip(arrays, scatter)]
    sems = [pltpu.SemaphoreType.DMA((n * N_DEV,)), pltpu.SemaphoreType.DMA((n * N_DEV,)),
            pltpu.SemaphoreType.DMA((n,))]
    return out_shape, sems


def _exchange(arrays, scatter, name):
    n = len(arrays)

    def body(*refs):
        ins, outs, sems = refs[:n], refs[n:2 * n], refs[2 * n:]
        _exchange_start(ins, outs, sems, scatter)
        _exchange_wait(ins, outs, sems, scatter)

    out_shape, sems = _exchange_shapes(arrays, scatter)
    any_spec = pl.BlockSpec(memory_space=pl.ANY)
    return _pc(
        body, name=name, out_shape=tuple(out_shape),
        in_specs=[any_spec] * n, out_specs=tuple([any_spec] * n), scratch_shapes=sems,
    )(*arrays)


def _with_rider(body, n_in, n_out, n_scratch, rider, first, last):
    if rider is None:
        return body
    arrays, scatter = rider
    n = len(arrays)

    def wrapped(*refs):
        ins, r_ins = refs[:n_in], refs[n_in:n_in + n]
        outs = refs[n_in + n:n_in + n + n_out]
        r_outs = refs[n_in + n + n_out:n_in + 2 * n + n_out]
        scratch = refs[n_in + 2 * n + n_out:n_in + 2 * n + n_out + n_scratch]
        sems = refs[n_in + 2 * n + n_out + n_scratch:]

        @pl.when(first())
        def _():
            _exchange_start(r_ins, r_outs, sems, scatter)

        body(*ins, *outs, *scratch)

        @pl.when(last())
        def _():
            _exchange_wait(r_ins, r_outs, sems, scatter)

    return wrapped


def _rider_specs(rider):
    if rider is None:
        return [], [], [], [], []
    arrays, scatter = rider
    out_shape, sems = _exchange_shapes(arrays, scatter)
    any_spec = pl.BlockSpec(memory_space=pl.ANY)
    return list(arrays), [any_spec] * len(arrays), out_shape, [any_spec] * len(arrays), sems


def _mm_nn_res(a, b, res, tm, name):
    T, K = a.shape
    N = b.shape[1]

    def body(a_ref, b_ref, r_ref, o_ref):
        o_ref[...] = r_ref[...] + jnp.dot(a_ref[...], b_ref[...], preferred_element_type=F32)

    return _pc(
        body, name=name, grid=(T // tm,), out_shape=jax.ShapeDtypeStruct((T, N), F32),
        in_specs=[pl.BlockSpec((tm, K), lambda i: (i, 0)), pl.BlockSpec((K, N), lambda i: (0, 0)),
                  pl.BlockSpec((tm, N), lambda i: (i, 0))],
        out_specs=pl.BlockSpec((tm, N), lambda i: (i, 0)),
        compiler_params=_params("parallel"),
    )(a, b, res)


def _mm_nt(a, b, tm, tk, name, rider=None):
    T, K = a.shape
    N = b.shape[0]
    nk = K // tk
    nm = T // tm

    def body(a_ref, b_ref, o_ref):
        part = lax.dot_general(a_ref[...].astype(BF16), b_ref[...], NT_DIMS, preferred_element_type=F32)
        if nk == 1:
            o_ref[...] = part
        else:
            k = pl.program_id(1)

            @pl.when(k == 0)
            def _():
                o_ref[...] = part

            @pl.when(k > 0)
            def _():
                o_ref[...] += part

    r_in, r_in_specs, r_out, r_out_specs, r_sems = _rider_specs(rider)
    body = _with_rider(body, 2, 1, 0, rider,
                       lambda: jnp.logical_and(pl.program_id(0) == 0, pl.program_id(1) == 0),
                       lambda: jnp.logical_and(pl.program_id(0) == nm - 1, pl.program_id(1) == nk - 1))
    res = _pc(
        body, name=name, grid=(nm, nk), out_shape=tuple([jax.ShapeDtypeStruct((T, N), F32)] + r_out),
        in_specs=[pl.BlockSpec((tm, tk), lambda i, k: (i, k)), pl.BlockSpec((N, tk), lambda i, k: (0, k))] + r_in_specs,
        out_specs=tuple([pl.BlockSpec((tm, N), lambda i, k: (i, 0))] + r_out_specs),
        scratch_shapes=r_sems,
        compiler_params=_params("arbitrary", "arbitrary"),
    )(a, b, *r_in)
    return res[0] if rider is None else res


def _mm_tn(a, b, n_col_shards, tn, tk, name):
    T, M = a.shape
    N = b.shape[1]
    sw = N // n_col_shards
    per_step = tn // sw
    nk = T // tk

    def body(a_ref, b_ref, o_ref, acc_sc):
        k = pl.program_id(1)
        part = lax.dot_general(a_ref[...].astype(BF16), b_ref[...].astype(BF16), TN_DIMS,
                               preferred_element_type=F32)

        @pl.when(k == 0)
        def _():
            acc_sc[...] = part

        @pl.when(k > 0)
        def _():
            acc_sc[...] += part

        @pl.when(k == nk - 1)
        def _():
            for s in range(per_step):
                o_ref[s] = acc_sc[:, s * sw:(s + 1) * sw].astype(BF16)

    return _pc(
        body, name=name, grid=(N // tn, nk),
        out_shape=jax.ShapeDtypeStruct((n_col_shards, M, sw), BF16),
        in_specs=[pl.BlockSpec((tk, M), lambda j, k: (k, 0)), pl.BlockSpec((tk, tn), lambda j, k: (k, j))],
        out_specs=pl.BlockSpec((per_step, M, sw), lambda j, k: (j, 0, 0)),
        scratch_shapes=[pltpu.VMEM((M, tn), F32)],
        compiler_params=_params("parallel", "arbitrary"),
    )(a, b)


def _proj_layout(D):
    return {"u": 0, "z_pool": POOL_W, "gates": 2 * POOL_W, "z_attn": 2 * POOL_W + 2 * D, "width": 2 * POOL_W + 2 * D + ATTN_W}


def _norm_inproj(x, g, w, tm):
    T, D = x.shape
    NW = w.shape[1]
    lay = _proj_layout(D)
    qkv0, za0, gl0 = 2 * POOL_W, 2 * POOL_W + 3 * ATTN_W, 2 * POOL_W + 4 * ATTN_W

    def body(x_ref, g_ref, w_ref, proj_ref, qkv_ref, h_ref):
        xv = x_ref[...]
        r = lax.rsqrt(jnp.mean(xv * xv, axis=-1, keepdims=True) + RMS_EPS)
        h = ((xv * r) * g_ref[...]).astype(BF16)
        h_ref[...] = h

        def cols(lo, hi):
            return jnp.dot(h, w_ref[:, lo:hi], preferred_element_type=F32)

        proj_ref[:, :lay["gates"]] = cols(0, qkv0)
        qkv_ref[...] = cols(qkv0, za0).astype(BF16)
        proj_ref[:, lay["gates"]:lay["z_attn"]] = cols(gl0, NW)
        proj_ref[:, lay["z_attn"]:] = cols(za0, gl0)

    return _pc(
        body, name="norm_inproj", grid=(T // tm,),
        out_shape=(jax.ShapeDtypeStruct((T, lay["width"]), F32), jax.ShapeDtypeStruct((T, 3 * ATTN_W), BF16),
                   jax.ShapeDtypeStruct((T, D), BF16)),
        in_specs=[pl.BlockSpec((tm, D), lambda i: (i, 0)), pl.BlockSpec((1, D), lambda i: (0, 0)),
                  pl.BlockSpec((D, NW), lambda i: (0, 0))],
        out_specs=(pl.BlockSpec((tm, lay["width"]), lambda i: (i, 0)), pl.BlockSpec((tm, 3 * ATTN_W), lambda i: (i, 0)),
                   pl.BlockSpec((tm, D), lambda i: (i, 0))),
        compiler_params=_params("parallel"),
    )(x, g, w)


def _window_sums(xh, forward):
    n = xh.shape[0]
    sums, s, step = [], xh, 1
    for _ in POOL_WINDOWS:
        s = s + pltpu.roll(s, step if forward else n - step, 0)
        sums.append(s)
        step *= 2
    return sums


def _pooled(u, halo, row):
    sums = _window_sums(jnp.concatenate([halo, u], axis=0), True)
    out = []
    for g, w in enumerate(POOL_WINDOWS):
        cols = slice(g * POOL_G, (g + 1) * POOL_G)
        cnt = jnp.minimum(row + 1, w).astype(F32)
        out.append(sums[g][HALO:, cols] / cnt - u[:, cols])
    return out


def _pool_fwd(proj, pool_w, scale, R):
    T = proj.shape[0]

    def body(u_ref, z_ref, pw_ref, sc_ref, y_ref, halo_sc):
        i = pl.program_id(0)

        @pl.when(i == 0)
        def _():
            halo_sc[...] = jnp.zeros_like(halo_sc)

        u = u_ref[...]
        row = i * R + lax.broadcasted_iota(jnp.int32, (R, 1), 0)
        pooled = _pooled(u, halo_sc[...], row)
        mixed = jnp.concatenate(
            [jnp.dot(pooled[g].astype(BF16), pw_ref[g].astype(BF16), preferred_element_type=F32)
             for g in range(len(POOL_WINDOWS))], axis=1)
        z = z_ref[...]
        y_ref[...] = ((mixed * sc_ref[...]) * (z * _sigmoid(z))).astype(BF16)
        halo_sc[...] = u[R - HALO:, :]

    return _pc(
        body, name="pool_fwd", grid=(T // R,), out_shape=jax.ShapeDtypeStruct((T, POOL_W), BF16),
        in_specs=[pl.BlockSpec((R, POOL_W), lambda i: (i, 0)), pl.BlockSpec((R, POOL_W), lambda i: (i, 1)),
                  pl.BlockSpec((4, POOL_G, POOL_G), lambda i: (0, 0, 0)), pl.BlockSpec((1, POOL_W), lambda i: (0, 0))],
        out_specs=pl.BlockSpec((R, POOL_W), lambda i: (i, 0)),
        scratch_shapes=[pltpu.VMEM((HALO, POOL_W), F32)],
        compiler_params=_params("arbitrary"),
    )(proj, proj, pool_w, scale)


def _softplus(l):
    return jnp.maximum(l, 0.0) + jnp.log(1.0 + jnp.exp(-jnp.abs(l)))


def _first_last_step(n0, n1):
    return (lambda: jnp.logical_and(pl.program_id(0) == 0, pl.program_id(1) == 0),
            lambda: jnp.logical_and(pl.program_id(0) == n0 - 1, pl.program_id(1) == n1 - 1))


def _head_lanes():
    lane = lax.broadcasted_iota(jnp.int32, (1, LANE), 1)
    return [lane < HEAD_DIM, lane >= HEAD_DIM]


def _head_masks(q, scale):
    qf = q.astype(F32) * scale
    return [jnp.where(m, qf, 0.0).astype(BF16) for m in _head_lanes()]


def _wide(c, width):
    return jnp.concatenate([c] * (width // LANE), axis=1)


def _max_row_norm2(x, heads):
    sq = x.astype(F32) * x.astype(F32)
    return [jnp.max(jnp.sum(jnp.where(m, sq, 0.0), axis=1, keepdims=True), axis=0, keepdims=True) for m in heads]


def _fill_blocks(src_ref, nk, tk, transposed_sc=None, masked_sc=None, norm_sc=None):
    heads = _head_lanes()
    if norm_sc is not None:
        for a in range(2):
            norm_sc[a][...] = jnp.zeros_like(norm_sc[a])

    def step(j, carry):
        rows = pl.ds(pl.multiple_of(j * tk, tk), tk)
        blk = src_ref[rows, :]
        if norm_sc is not None:
            for a, n2 in enumerate(_max_row_norm2(blk, heads)):
                norm_sc[a][...] = jnp.maximum(norm_sc[a][...], n2)
        if transposed_sc is not None:
            transposed_sc[j] = blk.astype(F32).T.astype(BF16)
        if masked_sc is not None:
            for a in range(2):
                masked_sc[a, rows, :] = jnp.where(heads[a], blk, jnp.zeros_like(blk))
        return carry

    lax.fori_loop(0, nk, step, 0)


def _attn_fwd(projb, proj, tq, rider=None):
    T = projb.shape[0]
    nq = T // tq
    tk, nk = tq, nq
    assert nk <= LANE
    n_pairs = ATTN_W // LANE
    zb = (proj.shape[1] - ATTN_W) // LANE

    def body(q_ref, k_ref, v_ref, za_ref, o_ref, y_ref, c_ref, kT_sc, vm_sc, *per_head):
        c_sc, cm_sc, o_sc, kn_sc, l_sc, w_sc = (per_head[2 * n:2 * n + 2] for n in range(6))
        i = pl.program_id(1)

        @pl.when(i == 0)
        def _():
            _fill_blocks(k_ref, nk, tk, transposed_sc=kT_sc, norm_sc=kn_sc)
            _fill_blocks(v_ref, nk, tk, masked_sc=vm_sc)

        qs = _head_masks(q_ref[...], HEAD_DIM ** -0.5)
        lane = lax.broadcasted_iota(jnp.int32, (1, LANE), 1)
        valid = lax.broadcasted_iota(jnp.int32, (tq, tk), 1) < lax.broadcasted_iota(jnp.int32, (tq, tk), 0)
        suffix = (lax.broadcasted_iota(jnp.int32, (tk, tk), 0) >= lax.broadcasted_iota(jnp.int32, (tk, tk), 1)).astype(BF16)
        for a in range(2):
            c_sc[a][...] = jnp.zeros_like(c_sc[a])
            o_sc[a][...] = jnp.zeros_like(o_sc[a])
            cm_sc[a][...] = jnp.full(cm_sc[a].shape, NO_CARRY, F32)
        l_max = [jnp.sqrt(qn * kn_sc[a][...]) for a, qn in enumerate(_max_row_norm2(qs[0] + qs[1], _head_lanes()))]

        def logits(j, slot):
            kT = kT_sc[j]
            for a in range(2):
                l_sc[a][slot] = jnp.dot(qs[a], kT, preferred_element_type=F32)

        def values(j, slot):
            rows = pl.ds(pl.multiple_of(j * tk, tk), tk)
            for a in range(2):
                o_sc[a][...] += jnp.dot(w_sc[a][slot], vm_sc[a, rows, :], preferred_element_type=F32)

        def weights(j, slot, masked):
            sp = []
            for a in range(2):
                x = _softplus(l_sc[a][slot])
                sp.append((jnp.where(valid, x, 0.0) if masked else x).astype(BF16))
            inc = [jnp.dot(sp[a], suffix, preferred_element_type=F32) for a in range(2)]
            for a in range(2):
                c = c_sc[a][...]
                w = jnp.exp(l_sc[a][slot] - inc[a] - _wide(c, tk))
                if masked:
                    w = jnp.where(valid, w, 0.0)
                w_sc[a][slot] = w.astype(BF16)
                cm_sc[a][...] = jnp.where(lane == j, c, cm_sc[a][...])
                c_sc[a][...] = c + inc[a][:, 0:1]

        def more():
            live = [jnp.min(c_sc[a][...], axis=0, keepdims=True) - l_max[a][0:1, :] <= ZERO_WEIGHT for a in range(2)]
            return jnp.max(jnp.where(jnp.logical_or(live[0], live[1]), 1, 0))

        logits(i, 0)
        logits(jnp.maximum(i - 1, 0), 1)
        weights(i, 0, True)

        def step(t, slot):
            logits(jnp.maximum(i - t - 1, 0), 1 - slot)
            values(i - t + 1, 1 - slot)
            weights(i - t, slot, False)

        def two_steps(carry):
            tt, _ = carry
            step(2 * tt + 1, 1)
            step(2 * tt + 2, 0)
            return tt + 1, more()

        pairs, go = lax.while_loop(lambda c: jnp.logical_and(c[0] < i // 2, c[1] > 0), two_steps, (0, more()))
        done = 2 * pairs
        one_more = jnp.logical_and(done == i - 1, go > 0)

        @pl.when(one_more)
        def _():
            step(i, 1)
            values(0, 1)

        @pl.when(jnp.logical_not(one_more))
        def _():
            values(i - done, 0)

        o = o_sc[0][...] + o_sc[1][...]
        o_ref[...] = o
        za = za_ref[...]
        y_ref[...] = (o * (za * _sigmoid(za))).astype(BF16)
        c_ref[0, 0] = cm_sc[0][...]
        c_ref[1, 0] = cm_sc[1][...]

    scratch = ([pltpu.VMEM((nk, LANE, tk), BF16), pltpu.VMEM((2, T, LANE), BF16)]
               + [pltpu.VMEM((tq, LANE), F32)] * 6 + [pltpu.VMEM((8, LANE), F32)] * 2
               + [pltpu.VMEM((2, tq, tk), F32)] * 2 + [pltpu.VMEM((2, tq, tk), BF16)] * 2)
    r_in, r_in_specs, r_out, r_out_specs, r_sems = _rider_specs(rider)
    body = _with_rider(body, 4, 3, len(scratch), rider, *_first_last_step(n_pairs, nq))
    return _pc(
        body, name="attn_fwd", grid=(n_pairs, nq),
        out_shape=tuple([jax.ShapeDtypeStruct((T, ATTN_W), F32), jax.ShapeDtypeStruct((T, ATTN_W), BF16),
                         jax.ShapeDtypeStruct((2 * n_pairs, nq, tq, LANE), F32)] + r_out),
        in_specs=[pl.BlockSpec((tq, LANE), lambda p, i: (i, p)),
                  pl.BlockSpec((T, LANE), lambda p, i: (0, n_pairs + p)),
                  pl.BlockSpec((T, LANE), lambda p, i: (0, 2 * n_pairs + p)),
                  pl.BlockSpec((tq, LANE), lambda p, i: (i, zb + p))] + r_in_specs,
        out_specs=tuple([pl.BlockSpec((tq, LANE), lambda p, i: (i, p)), pl.BlockSpec((tq, LANE), lambda p, i: (i, p)),
                         pl.BlockSpec((2, 1, tq, LANE), lambda p, i: (p, i, 0, 0))] + r_out_specs),
        scratch_shapes=scratch + r_sems,
        compiler_params=_params("arbitrary", "arbitrary"),
    )(projb, projb, projb, proj, *r_in)


def _merge_fwd(y_pool, y_attn, w_pu, w_au, proj, b_gate, tm):
    T = y_pool.shape[0]
    D = w_pu.shape[1]
    gb = _proj_layout(D)["gates"] // D

    def body(yp_ref, ya_ref, wpu_ref, wau_ref, gl0_ref, gl1_ref, bg_ref, m_ref, p_ref, a_ref):
        p = jnp.dot(yp_ref[...], wpu_ref[...], preferred_element_type=F32)
        a = jnp.dot(ya_ref[...], wau_ref[...], preferred_element_type=F32)
        g0 = _sigmoid(gl0_ref[...] + bg_ref[:, :D])
        g1 = _sigmoid(gl1_ref[...] + bg_ref[:, D:])
        m_ref[...] = (g0 * p + g1 * a).astype(BF16)
        p_ref[...] = p
        a_ref[...] = a

    row = lambda i: (i, 0)
    fixed = lambda i: (0, 0)
    return _pc(
        body, name="merge_fwd", grid=(T // tm,),
        out_shape=(jax.ShapeDtypeStruct((T, D), BF16), jax.ShapeDtypeStruct((T, D), F32),
                   jax.ShapeDtypeStruct((T, D), F32)),
        in_specs=[pl.BlockSpec((tm, POOL_W), row), pl.BlockSpec((tm, ATTN_W), row),
                  pl.BlockSpec((POOL_W, D), fixed), pl.BlockSpec((ATTN_W, D), fixed),
                  pl.BlockSpec((tm, D), lambda i: (i, gb)), pl.BlockSpec((tm, D), lambda i: (i, gb + 1)),
                  pl.BlockSpec((1, 2 * D), fixed)],
        out_specs=(pl.BlockSpec((tm, D), row), pl.BlockSpec((tm, D), row), pl.BlockSpec((tm, D), row)),
        compiler_params=_params("parallel"),
    )(y_pool, y_attn, w_pu, w_au, proj, proj, b_gate)


def _final_loss(x, g, target, tm):
    T, D = x.shape

    def body(x_ref, g_ref, t_ref, dx_ref, dg_ref, loss_ref):
        @pl.when(pl.program_id(0) == 0)
        def _():
            dg_ref[...] = jnp.zeros_like(dg_ref)
            loss_ref[...] = jnp.zeros_like(loss_ref)

        xv, gv = x_ref[...], g_ref[...]
        r = lax.rsqrt(jnp.mean(xv * xv, axis=-1, keepdims=True) + RMS_EPS)
        xh = xv * r
        d = xh * gv - t_ref[...]
        loss_ref[...] += 0.5 * jnp.sum(jnp.mean(d * d, axis=-1, keepdims=True), axis=0, keepdims=True)
        dy = d * (1.0 / D)
        dg_ref[...] += jnp.sum(dy * xh, axis=0, keepdims=True)
        dh = dy * gv
        dx_ref[...] = r * (dh - xh * jnp.mean(dh * xh, axis=-1, keepdims=True))

    return _pc(
        body, name="final_loss", grid=(T // tm,),
        out_shape=(jax.ShapeDtypeStruct((T, D), F32), jax.ShapeDtypeStruct((1, D), F32),
                   jax.ShapeDtypeStruct((8, LANE), F32)),
        in_specs=[pl.BlockSpec((tm, D), lambda i: (i, 0)), pl.BlockSpec((1, D), lambda i: (0, 0)),
                  pl.BlockSpec((tm, D), lambda i: (i, 0))],
        out_specs=(pl.BlockSpec((tm, D), lambda i: (i, 0)), pl.BlockSpec((1, D), lambda i: (0, 0)),
                   pl.BlockSpec((8, LANE), lambda i: (0, 0))),
        compiler_params=_params("arbitrary"),
    )(x, g, target)


def _merge_bwd(dm, p, a, proj, b_gate, tm):
    T, D = dm.shape
    gb = _proj_layout(D)["gates"] // D

    def body(dm_ref, p_ref, a_ref, gl0_ref, gl1_ref, bg_ref, dp_ref, da_ref, dgl_ref, dbg_ref):
        @pl.when(pl.program_id(0) == 0)
        def _():
            dbg_ref[...] = jnp.zeros_like(dbg_ref)

        dmv = dm_ref[...]
        g0 = _sigmoid(gl0_ref[...] + bg_ref[:, :D])
        g1 = _sigmoid(gl1_ref[...] + bg_ref[:, D:])
        dp_ref[...] = (dmv * g0).astype(BF16)
        da_ref[...] = (dmv * g1).astype(BF16)
        dgl0 = dmv * p_ref[...] * (g0 * (1.0 - g0))
        dgl1 = dmv * a_ref[...] * (g1 * (1.0 - g1))
        dgl_ref[:, :D] = dgl0.astype(BF16)
        dgl_ref[:, D:] = dgl1.astype(BF16)
        dbg_ref[:, :D] += jnp.sum(dgl0, axis=0, keepdims=True)
        dbg_ref[:, D:] += jnp.sum(dgl1, axis=0, keepdims=True)

    row = lambda i: (i, 0)
    fixed = lambda i: (0, 0)
    return _pc(
        body, name="merge_bwd", grid=(T // tm,),
        out_shape=(jax.ShapeDtypeStruct((T, D), BF16), jax.ShapeDtypeStruct((T, D), BF16),
                   jax.ShapeDtypeStruct((T, 2 * D), BF16), jax.ShapeDtypeStruct((1, 2 * D), F32)),
        in_specs=[pl.BlockSpec((tm, D), row), pl.BlockSpec((tm, D), row), pl.BlockSpec((tm, D), row),
                  pl.BlockSpec((tm, D), lambda i: (i, gb)), pl.BlockSpec((tm, D), lambda i: (i, gb + 1)),
                  pl.BlockSpec((1, 2 * D), fixed)],
        out_specs=(pl.BlockSpec((tm, D), row), pl.BlockSpec((tm, D), row), pl.BlockSpec((tm, 2 * D), row),
                   pl.BlockSpec((1, 2 * D), fixed)),
        compiler_params=_params("arbitrary"),
    )(dm, p, a, proj, proj, b_gate)


def _pool_bwd(proj, dy, pool_w, scale, R):
    T = proj.shape[0]
    nb = T // R
    hb = R // HALO

    def body(u_ref, up_ref, z_ref, dy_ref, pw_ref, sc_ref, du_ref, dz_ref, dpw_ref, dsc_ref, halo_sc):
        i = pl.program_id(0)
        rb = nb - 1 - i

        @pl.when(i == 0)
        def _():
            halo_sc[...] = jnp.zeros_like(halo_sc)
            dpw_ref[...] = jnp.zeros_like(dpw_ref)
            dsc_ref[...] = jnp.zeros_like(dsc_ref)

        u = u_ref[...]
        row = rb * R + lax.broadcasted_iota(jnp.int32, (R, 1), 0)
        before = jnp.where(rb > 0, up_ref[...], 0.0)
        pooled = _pooled(u, before, row)
        pw = [pw_ref[g].astype(BF16) for g in range(len(POOL_WINDOWS))]
        mixed = jnp.concatenate(
            [jnp.dot(pooled[g].astype(BF16), pw[g], preferred_element_type=F32) for g in range(len(POOL_WINDOWS))],
            axis=1)
        sc = sc_ref[...]
        silu, dsilu = _silu_and_grad(z_ref[...])
        dyv = dy_ref[...]
        dmp = dyv * silu
        dz_ref[...] = (dyv * (mixed * sc) * dsilu).astype(BF16)
        dsc_ref[...] += jnp.sum(dmp * mixed, axis=0, keepdims=True)
        dmixed = (dmp * sc).astype(BF16)
        dpn = []
        dpooled = []
        for g, w in enumerate(POOL_WINDOWS):
            cols = slice(g * POOL_G, (g + 1) * POOL_G)
            dpw_ref[g] += lax.dot_general(pooled[g].astype(BF16), dmixed[:, cols], TN_DIMS,
                                          preferred_element_type=F32)
            dpg = lax.dot_general(dmixed[:, cols], pw[g], NT_DIMS, preferred_element_type=F32)
            dpooled.append(dpg)
            dpn.append(dpg / jnp.minimum(row + 1, w).astype(F32))
        dpn = jnp.concatenate(dpn, axis=1)
        sums = _window_sums(jnp.concatenate([dpn, halo_sc[...]], axis=0), False)
        du_ref[...] = jnp.concatenate(
            [sums[g][:R, g * POOL_G:(g + 1) * POOL_G] - dpooled[g] for g in range(len(POOL_WINDOWS))],
            axis=1).astype(BF16)
        halo_sc[...] = dpn[:HALO, :]

    rev = lambda i: (nb - 1 - i, 0)
    return _pc(
        body, name="pool_bwd", grid=(nb,),
        out_shape=(jax.ShapeDtypeStruct((T, POOL_W), BF16), jax.ShapeDtypeStruct((T, POOL_W), BF16),
                   jax.ShapeDtypeStruct((4, POOL_G, POOL_G), F32), jax.ShapeDtypeStruct((1, POOL_W), F32)),
        in_specs=[pl.BlockSpec((R, POOL_W), rev),
                  pl.BlockSpec((HALO, POOL_W), lambda i: (jnp.maximum((nb - 1 - i) * hb - 1, 0), 0)),
                  pl.BlockSpec((R, POOL_W), lambda i: (nb - 1 - i, 1)),
                  pl.BlockSpec((R, POOL_W), rev),
                  pl.BlockSpec((4, POOL_G, POOL_G), lambda i: (0, 0, 0)), pl.BlockSpec((1, POOL_W), lambda i: (0, 0))],
        out_specs=(pl.BlockSpec((R, POOL_W), rev), pl.BlockSpec((R, POOL_W), rev),
                   pl.BlockSpec((4, POOL_G, POOL_G), lambda i: (0, 0, 0)), pl.BlockSpec((1, POOL_W), lambda i: (0, 0))),
        scratch_shapes=[pltpu.VMEM((HALO, POOL_W), F32)],
        compiler_params=_params("arbitrary"),
    )(proj, proj, proj, dy, pool_w, scale)


def _attn_gate_bwd(dya, o, proj, tm):
    T = dya.shape[0]
    zb = (proj.shape[1] - ATTN_W) // ATTN_W

    def body(dy_ref, o_ref, za_ref, do_ref, dza_ref):
        silu, dsilu = _silu_and_grad(za_ref[...])
        dyv = dy_ref[...]
        do_ref[...] = (dyv * silu).astype(BF16)
        dza_ref[...] = (dyv * o_ref[...] * dsilu).astype(BF16)

    row = lambda i: (i, 0)
    return _pc(
        body, name="attn_gate_bwd", grid=(T // tm,),
        out_shape=(jax.ShapeDtypeStruct((T, ATTN_W), BF16), jax.ShapeDtypeStruct((T, ATTN_W), BF16)),
        in_specs=[pl.BlockSpec((tm, ATTN_W), row), pl.BlockSpec((tm, ATTN_W), row),
                  pl.BlockSpec((tm, ATTN_W), lambda i: (i, zb))],
        out_specs=(pl.BlockSpec((tm, ATTN_W), row), pl.BlockSpec((tm, ATTN_W), row)),
        compiler_params=_params("parallel"),
    )(dya, o, proj)


def _attn_bwd(projb, do, carries, tq, rider=None):
    T = projb.shape[0]
    nq = T // tq
    tk, nk = tq, nq
    n_pairs = ATTN_W // LANE
    scale = HEAD_DIM ** -0.5

    def body(q_ref, k_ref, v_ref, do_ref, c_ref, dq_ref, dkT_ref, dvT_ref, kT_sc, vT_sc, km_sc, *per_head):
        f_sc, dq_sc, kn_sc, l_sc, dw_sc, dl_sc, w_sc = (per_head[2 * n:2 * n + 2] for n in range(7))
        i = pl.program_id(1)

        @pl.when(i == 0)
        def _():
            _fill_blocks(k_ref, nk, tk, transposed_sc=kT_sc, masked_sc=km_sc, norm_sc=kn_sc)
            _fill_blocks(v_ref, nk, tk, transposed_sc=vT_sc)
            dkT_ref[...] = jnp.zeros_like(dkT_ref)
            dvT_ref[...] = jnp.zeros_like(dvT_ref)

        qs = _head_masks(q_ref[...], scale)
        dos = _head_masks(do_ref[...], 1.0)
        qT = [x.astype(F32).T.astype(BF16) for x in qs]
        doT = [x.astype(F32).T.astype(BF16) for x in dos]
        lane = lax.broadcasted_iota(jnp.int32, (1, LANE), 1)
        valid = lax.broadcasted_iota(jnp.int32, (tq, tk), 1) < lax.broadcasted_iota(jnp.int32, (tq, tk), 0)
        kk0 = lax.broadcasted_iota(jnp.int32, (tk, tk), 0)
        kk1 = lax.broadcasted_iota(jnp.int32, (tk, tk), 1)
        suffix = (kk0 >= kk1).astype(BF16)
        prefix = (kk0 <= kk1).astype(BF16)
        for a in range(2):
            f_sc[a][...] = jnp.zeros_like(f_sc[a])
            dq_sc[a][...] = jnp.zeros_like(dq_sc[a])
            dl_sc[a][1] = jnp.zeros((tq, tk), BF16)
            w_sc[a][1] = jnp.zeros((tq, tk), BF16)
        live = lane == i
        for a, qn in enumerate(_max_row_norm2(qs[0] + qs[1], _head_lanes())):
            l_max = jnp.sqrt(qn * kn_sc[a][0:1, :])
            live = jnp.logical_or(live, jnp.min(c_ref[a, 0], axis=0, keepdims=True) - l_max <= ZERO_WEIGHT)
        t0 = jnp.min(jnp.where(jnp.logical_and(live, lane <= i), lane, i))
        n = i - t0

        def products(j, slot):
            kT = kT_sc[j]
            vT = vT_sc[j]
            for a in range(2):
                l_sc[a][slot] = jnp.dot(qs[a], kT, preferred_element_type=F32)
                dw_sc[a][slot] = jnp.dot(dos[a], vT, preferred_element_type=F32)

        def gradients(j, slot):
            rows = pl.ds(pl.multiple_of(j * tk, tk), tk)
            dkT = []
            dvT = []
            for a in range(2):
                dlb = dl_sc[a][slot]
                dq_sc[a][...] += jnp.dot(dlb, km_sc[a, rows, :], preferred_element_type=F32)
                dkT.append(jnp.dot(qT[a], dlb, preferred_element_type=F32))
                dvT.append(jnp.dot(doT[a], w_sc[a][slot], preferred_element_type=F32))
            dkT_ref[j] += dkT[0] + dkT[1]
            dvT_ref[j] += dvT[0] + dvT[1]

        def elementwise(j, slot, masked):
            sp, inc, e, beta, p = [None] * 2, [None] * 2, [None] * 2, [None] * 2, [None] * 2
            for a in range(2):
                x = _softplus(l_sc[a][slot])
                sp[a] = jnp.where(valid, x, 0.0) if masked else x
            for a in range(2):
                inc[a] = jnp.dot(sp[a].astype(BF16), suffix, preferred_element_type=F32)
            for a in range(2):
                l = l_sc[a][slot]
                c = jnp.sum(jnp.where(lane == j, c_ref[a, 0], 0.0), axis=1, keepdims=True)
                w = jnp.exp(l - inc[a] - c)
                if masked:
                    w = jnp.where(valid, w, 0.0)
                w_sc[a][slot] = w.astype(BF16)
                beta[a] = jnp.exp(l - sp[a])
                e[a] = w * dw_sc[a][slot]
            for a in range(2):
                p[a] = jnp.dot(e[a].astype(BF16), prefix, preferred_element_type=F32)
            for a in range(2):
                f = f_sc[a][...]
                dl = e[a] - beta[a] * (p[a] + _wide(f, tk))
                if masked:
                    dl = jnp.where(valid, dl, 0.0)
                dl_sc[a][slot] = dl.astype(BF16)
                f_sc[a][...] = f + p[a][:, tk - 1:tk]

        def step(r, slot):
            products(t0 + r + 1, 1 - slot)
            gradients(t0 + jnp.maximum(r - 1, 0), 1 - slot)
            elementwise(t0 + r, slot, False)

        def last(slot):
            gradients(t0 + jnp.maximum(n - 1, 0), 1 - slot)
            elementwise(i, slot, True)
            gradients(i, slot)

        products(t0, 0)

        def two_steps(tt, carry):
            step(2 * tt, 0)
            step(2 * tt + 1, 1)
            return carry

        lax.fori_loop(0, n // 2, two_steps, 0)

        @pl.when(n % 2 == 1)
        def _():
            step(n - 1, 0)
            last(1)

        @pl.when(n % 2 == 0)
        def _():
            last(0)

        dq_ref[...] = ((dq_sc[0][...] + dq_sc[1][...]) * scale).astype(BF16)

    scratch = ([pltpu.VMEM((nk, LANE, tk), BF16), pltpu.VMEM((nk, LANE, tk), BF16), pltpu.VMEM((2, T, LANE), BF16)]
               + [pltpu.VMEM((tq, LANE), F32)] * 4 + [pltpu.VMEM((8, LANE), F32)] * 2
               + [pltpu.VMEM((2, tq, tk), F32)] * 4 + [pltpu.VMEM((2, tq, tk), BF16)] * 4)
    r_in, r_in_specs, r_out, r_out_specs, r_sems = _rider_specs(rider)
    body = _with_rider(body, 5, 3, len(scratch), rider, *_first_last_step(n_pairs, nq))
    return _pc(
        body, name="attn_bwd", grid=(n_pairs, nq),
        out_shape=tuple([jax.ShapeDtypeStruct((T, ATTN_W), BF16), jax.ShapeDtypeStruct((n_pairs, nk, LANE, tk), F32),
                         jax.ShapeDtypeStruct((n_pairs, nk, LANE, tk), F32)] + r_out),
        in_specs=[pl.BlockSpec((tq, LANE), lambda p, i: (i, p)),
                  pl.BlockSpec((T, LANE), lambda p, i: (0, n_pairs + p)),
                  pl.BlockSpec((T, LANE), lambda p, i: (0, 2 * n_pairs + p)),
                  pl.BlockSpec((tq, LANE), lambda p, i: (i, p)),
                  pl.BlockSpec((2, 1, tq, LANE), lambda p, i: (p, i, 0, 0))] + r_in_specs,
        out_specs=tuple([pl.BlockSpec((tq, LANE), lambda p, i: (i, p)),
                         pl.BlockSpec((None, nk, LANE, tk), lambda p, i: (p, 0, 0, 0)),
                         pl.BlockSpec((None, nk, LANE, tk), lambda p, i: (p, 0, 0, 0))] + r_out_specs),
        scratch_shapes=scratch + r_sems,
        compiler_params=_params("arbitrary", "arbitrary"),
    )(projb, projb, projb, do, carries, *r_in)


def _norm_bwd(dh, x, g, dxo, tm):
    T, D = x.shape

    def body(dh_ref, x_ref, g_ref, dxo_ref, dx_ref, dg_ref):
        @pl.when(pl.program_id(0) == 0)
        def _():
            dg_ref[...] = jnp.zeros_like(dg_ref)

        xv = x_ref[...]
        r = lax.rsqrt(jnp.mean(xv * xv, axis=-1, keepdims=True) + RMS_EPS)
        xh = xv * r
        dhv = dh_ref[...]
        dg_ref[...] += jnp.sum(dhv * xh, axis=0, keepdims=True)
        dhg = dhv * g_ref[...]
        dx_ref[...] = dxo_ref[...] + r * (dhg - xh * jnp.mean(dhg * xh, axis=-1, keepdims=True))

    row = lambda i: (i, 0)
    return _pc(
        body, name="norm_bwd", grid=(T // tm,),
        out_shape=(jax.ShapeDtypeStruct((T, D), F32), jax.ShapeDtypeStruct((1, D), F32)),
        in_specs=[pl.BlockSpec((tm, D), row), pl.BlockSpec((tm, D), row), pl.BlockSpec((1, D), lambda i: (0, 0)),
                  pl.BlockSpec((tm, D), row)],
        out_specs=(pl.BlockSpec((tm, D), row), pl.BlockSpec((1, D), lambda i: (0, 0))),
        compiler_params=_params("arbitrary"),
    )(dh, x, g, dxo)


def _adamw(pieces, w, m, v, name):
    rows, cols = w.shape
    br = rows
    while br * cols > 65536 and br % 16 == 0:
        br //= 2
    c1 = 1.0 / (1.0 - ADAM_B1 ** ADAM_STEP)
    c2 = 1.0 / (1.0 - ADAM_B2 ** ADAM_STEP)

    def body(p_ref, w_ref, m_ref, v_ref, g_ref, d_ref, nm_ref, nv_ref):
        g = p_ref[0].astype(F32)
        for s in range(1, N_DEV):
            g = g + p_ref[s].astype(F32)
        nm = ADAM_B1 * m_ref[...] + (1.0 - ADAM_B1) * g
        nv = ADAM_B2 * v_ref[...] + (1.0 - ADAM_B2) * (g * g)
        g_ref[...] = g
        nm_ref[...] = nm
        nv_ref[...] = nv
        d_ref[...] = -ADAM_LR * ((nm * c1) / (jnp.sqrt(nv * c2) + ADAM_EPS) + ADAM_WD * w_ref[...])

    blk = pl.BlockSpec((br, cols), lambda i: (i, 0))
    shape = jax.ShapeDtypeStruct((rows, cols), F32)
    return _pc(
        body, name=name, grid=(rows // br,), out_shape=(shape, shape, shape, shape),
        in_specs=[pl.BlockSpec((N_DEV, br, cols), lambda i: (0, i, 0)), blk, blk, blk],
        out_specs=(blk, blk, blk, blk),
        compiler_params=_params("parallel"),
    )(pieces, w, m, v)


def _rows128(a):
    flat = a.reshape(-1)
    n = flat.shape[0]
    padded = -(-n // (8 * LANE)) * (8 * LANE)
    if padded != n:
        flat = jnp.concatenate([flat, jnp.zeros((padded - n,), flat.dtype)])
    return flat.reshape(-1, LANE)


def _pack(parts):
    return jnp.concatenate([_rows128(p) for p in parts], axis=0)


def _unpack(packed, like):
    out, r = [], 0
    for a in like:
        n = a.size
        nr = -(-n // (8 * LANE)) * 8
        out.append(packed[r:r + nr].reshape(-1)[:n].reshape(a.shape))
        r += nr
    return out


def kernel(x, norm_g, w_in, b_gate, pool_w, pool_scale, w_pool_up, w_attn_up, w_out, final_g, loss_target, m_norm_g, m_w_in, m_b_gate, m_pool_w, m_pool_scale, m_w_pool_up, m_w_attn_up, m_w_out, m_final_g, v_norm_g, v_w_in, v_b_gate, v_pool_w, v_pool_scale, v_w_pool_up, v_w_attn_up, v_w_out, v_final_g):
    L = norm_g.shape[0]
    T, D = x.shape[1], x.shape[2]
    NW = w_in.shape[2] * N_DEV
    assert NW == 2 * POOL_W + 4 * ATTN_W + 2 * D and x.shape[0] == 1
    tm = min(512, T)
    tq = min(256, T // 2)
    x0 = x.reshape(T, D)
    target = loss_target.reshape(T, D)

    assert L >= 2
    win_first = jnp.transpose(_exchange([w_in[0].astype(BF16)], [False], "gather_w_in0")[0], (1, 0, 2)).reshape(D, NW)
    rest = [w_in[1:].astype(BF16), w_pool_up.astype(BF16), w_attn_up.astype(BF16), w_out.astype(BF16)]

    saved = []
    xl = x0
    for l in range(L):
        proj, projb, h = _norm_inproj(xl, norm_g[l:l + 1], win_first if l == 0 else win_rest[l - 1], min(256, T))
        y_pool = _pool_fwd(proj, pool_w[l], pool_scale[l:l + 1], tm)
        if l == 0:
            o, y_attn, carries, g_in, g_pu, g_au, g_out = _attn_fwd(projb, proj, tq, rider=(rest, [False] * 4))
            win_rest = jnp.transpose(g_in, (1, 2, 0, 3)).reshape(L - 1, D, NW)
            wpu_full = jnp.transpose(g_pu, (1, 2, 0, 3)).reshape(L, POOL_W, D)
            wau_full = jnp.transpose(g_au, (1, 2, 0, 3)).reshape(L, ATTN_W, D)
            wout_full = jnp.transpose(g_out, (1, 0, 2, 3)).reshape(L, D, D)
        else:
            o, y_attn, carries = _attn_fwd(projb, proj, tq)
        merged, p, a = _merge_fwd(y_pool, y_attn, wpu_full[l], wau_full[l], proj, b_gate[l:l + 1], min(256, T))
        x_next = _mm_nn_res(merged, wout_full[l], xl, tm, "out_proj")
        saved.append((xl, proj, projb, h, y_pool, o, y_attn, carries, merged, p, a))
        xl = x_next

    dx, d_final_g, loss_part = _final_loss(xl, final_g.reshape(1, D), target, tm)

    d_norm_g, d_b_gate, d_pool_w, d_pool_scale = [None] * L, [None] * L, [None] * L, [None] * L
    d_win, d_wpu, d_wau, d_wout = [None] * L, [None] * L, [None] * L, [None] * L
    small_like = [norm_g, b_gate, pool_w, pool_scale, final_g, jnp.zeros((8, LANE), F32)]
    for l in reversed(range(L)):
        xin, proj, projb, h, y_pool, o, y_attn, carries, merged, p, a = saved[l]
        win_l = win_first if l == 0 else win_rest[l - 1]
        dm = _mm_nt(dx, wout_full[l], tm, D, "d_merged")
        d_wout[l] = _mm_tn(merged, dx, 1, D, min(1024, T), "d_w_out").reshape(N_DEV, D // N_DEV, D)
        dp, da, dgl, d_b_gate[l] = _merge_bwd(dm, p, a, proj, b_gate[l:l + 1], min(256, T))
        d_wpu[l] = _mm_tn(y_pool, dp, N_DEV, D, min(1024, T), "d_w_pool_up")
        d_wau[l] = _mm_tn(y_attn, da, N_DEV, D, min(1024, T), "d_w_attn_up")
        dyp = _mm_nt(dp, wpu_full[l], tm, D, "d_y_pool")
        dya = _mm_nt(da, wau_full[l], tm, D, "d_y_attn")
        du, dzp, d_pool_w[l], d_pool_scale[l] = _pool_bwd(proj, dyp, pool_w[l], pool_scale[l:l + 1], tm)
        do, dza = _attn_gate_bwd(dya, o, proj, tm)
        if l == 0:
            early = d_win[1:] + d_wpu + d_wau + d_wout
            dq, dkT, dvT, *got_early = _attn_bwd(projb, do, carries, tq, rider=(early, [True] * len(early)))
        else:
            dq, dkT, dvT = _attn_bwd(projb, do, carries, tq)
        dk, dv = [jnp.transpose(t, (1, 3, 0, 2)).reshape(T, ATTN_W).astype(BF16) for t in (dkT, dvT)]
        dproj = jnp.concatenate([du, dzp, dq, dk, dv, dza, dgl], axis=1)
        d_win[l] = _mm_tn(h, dproj, N_DEV, NW // 2, min(1024, T), "d_w_in")
        if l == 0:
            small = _pack([jnp.concatenate([jnp.zeros((1, D), F32)] + d_norm_g[1:], 0), jnp.concatenate(d_b_gate, 0),
                           jnp.stack(d_pool_w, 0), jnp.concatenate(d_pool_scale, 0), d_final_g, loss_part])
            dh, got_win0, got_small = _mm_nt(dproj, win_l, min(256, T), NW, "d_h", rider=([d_win[0], small], [True, False]))
        else:
            dh = _mm_nt(dproj, win_l, min(256, T), NW, "d_h")
        dx, d_norm_g[l] = _norm_bwd(dh, xin, norm_g[l:l + 1], dx, tm)

    got_norm0 = _exchange([d_norm_g[0].reshape(-1, LANE)], [False], "gather_norm_grad")[0]
    r_small = jnp.concatenate([got_norm0, got_small[:, D // LANE:]], axis=1)
    r_in = jnp.stack([got_win0] + got_early[:L - 1], axis=1)
    r_pu = jnp.stack(got_early[L - 1:2 * L - 1], axis=1)
    r_au = jnp.stack(got_early[2 * L - 1:3 * L - 1], axis=1)
    r_out = jnp.stack(got_early[3 * L - 1:4 * L - 1], axis=1)

    def update(pieces, w, m, v, name):
        cols = w.shape[-1]
        res = _adamw(pieces.reshape(N_DEV, -1, cols), w.reshape(-1, cols), m.reshape(-1, cols),
                     v.reshape(-1, cols), name)
        return [r.reshape(w.shape) for r in res]

    u_in = update(r_in, w_in, m_w_in, v_w_in, "adamw_w_in")
    u_pu = update(r_pu, w_pool_up, m_w_pool_up, v_w_pool_up, "adamw_w_pool_up")
    u_au = update(r_au, w_attn_up, m_w_attn_up, v_w_attn_up, "adamw_w_attn_up")
    u_out = update(r_out, w_out, m_w_out, v_w_out, "adamw_w_out")
    zeros = small_like[-1]
    smalls = _adamw(r_small,
                    _pack([norm_g, b_gate, pool_w, pool_scale, final_g, zeros]),
                    _pack([m_norm_g, m_b_gate, m_pool_w, m_pool_scale, m_final_g, zeros]),
                    _pack([v_norm_g, v_b_gate, v_pool_w, v_pool_scale, v_final_g, zeros]), "adamw_small")
    s_g, s_d, s_m, s_v = [_unpack(s, small_like) for s in smalls]
    loss = s_g[5][0, 0]

    def ordered(k):
        s = (s_g, s_d, s_m, s_v)[k]
        return [s[0], u_in[k], s[1], s[2], s[3], u_pu[k], u_au[k], u_out[k], s[4]]

    return (loss, dx.reshape(x.shape), *ordered(0), *ordered(1), *ordered(2), *ordered(3))
```

```python
import jax
import jax.numpy as jnp
from jax import lax
from jax.experimental import pallas as pl
from jax.experimental.pallas import tpu as pltpu

F32 = jnp.float32
BF16 = jnp.bfloat16

N_DEV = 8
HEAD_DIM = 64
ATTN_W = 512
POOL_W = 512
POOL_G = 128
POOL_WINDOWS = (2, 4, 8, 16)
HALO = 16
LANE = 128
RMS_EPS = 1e-6
ZERO_WEIGHT = 110.0
NO_CARRY = 3.0e38
ADAM_LR, ADAM_B1, ADAM_B2, ADAM_EPS, ADAM_WD, ADAM_STEP = 0.001, 0.9, 0.999, 1e-08, 0.01, 10
VMEM_LIMIT = 56 * 1024 * 1024

NT_DIMS = (((1,), (1,)), ((), ()))
TN_DIMS = (((0,), (0,)), ((), ()))


def _pc(body, **kw):
    return pl.pallas_call(body, **kw)


def _params(*sem):
    return pltpu.CompilerParams(dimension_semantics=sem, vmem_limit_bytes=VMEM_LIMIT)


def _sigmoid(z):
    return 1.0 / (1.0 + jnp.exp(-z))


def _silu_and_grad(z):
    s = _sigmoid(z)
    return z * s, s * (1.0 + z * (1.0 - s))


def _my_index():
    return 4 * lax.axis_index("x") + 2 * lax.axis_index("y") + lax.axis_index("c")


def _peer(k):
    x, y, c = lax.axis_index("x"), lax.axis_index("y"), lax.axis_index("c")
    px = lax.rem(x + ((k >> 2) & 1), 2)
    py = lax.rem(y + ((k >> 1) & 1), 2)
    pc = lax.rem(c + (k & 1), 2)
    return (px, py, pc), 4 * px + 2 * py + pc


def _exchange_copies(ins, outs, sems, scatter):
    send_sems, recv_sems, local_sems = sems
    n = len(ins)
    me = _my_index()

    def src(a, idx):
        return ins[a].at[idx] if scatter[a] else ins[a]

    local = [pltpu.make_async_copy(src(a, me), outs[a].at[me], local_sems.at[a]) for a in range(n)]
    sends, arrivals = [], []
    for k in (1, 2, 4, 3, 5, 6, 7):
        dev, pidx = _peer(k)
        for a in range(n):
            sem = dict(send_sem=send_sems.at[a * N_DEV + k], recv_sem=recv_sems.at[a * N_DEV + k],
                       device_id=dev, device_id_type=pl.DeviceIdType.MESH)
            sends.append(pltpu.make_async_remote_copy(src_ref=src(a, pidx), dst_ref=outs[a].at[me], **sem))
            arrivals.append(pltpu.make_async_remote_copy(src_ref=src(a, pidx), dst_ref=outs[a].at[pidx], **sem))
    return local, sends, arrivals


def _exchange_start(ins, outs, sems, scatter):
    local, sends, _ = _exchange_copies(ins, outs, sems, scatter)
    for cp in local + sends:
        cp.start()


def _exchange_wait(ins, outs, sems, scatter):
    local, sends, arrivals = _exchange_copies(ins, outs, sems, scatter)
    for cp in arrivals:
        cp.wait_recv()
    for cp in sends:
        cp.wait_send()
    for cp in local:
        cp.wait()


def _exchange_shapes(arrays, scatter):
    n = len(arrays)
    out_shape = [jax.ShapeDtypeStruct((N_DEV,) + tuple(a.shape[1:] if s else a.shape), a.dtype)
                 for a, s in zip(arrays, scatter)]
    sems = [pltpu.SemaphoreType.DMA((n * N_DEV,)), pltpu.SemaphoreType.DMA((n * N_DEV,)),
            pltpu.SemaphoreType.DMA((n,))]
    return out_shape, sems


def _exchange(arrays, scatter, name):
    n = len(arrays)

    def body(*refs):
        ins, outs, sems = refs[:n], refs[n:2 * n], refs[2 * n:]
        _exchange_start(ins, outs, sems, scatter)
        _exchange_wait(ins, outs, sems, scatter)

    out_shape, sems = _exchange_shapes(arrays, scatter)
    any_spec = pl.BlockSpec(memory_space=pl.ANY)
    return _pc(
        body, name=name, out_shape=tuple(out_shape),
        in_specs=[any_spec] * n, out_specs=tuple([any_spec] * n), scratch_shapes=sems,
    )(*arrays)


def _with_rider(body, n_in, n_out, n_scratch, rider, first, last):
    if rider is None:
        return body
    arrays, scatter = rider
    n = len(arrays)

    def wrapped(*refs):
        ins, r_ins = refs[:n_in], refs[n_in:n_in + n]
        outs = refs[n_in + n:n_in + n + n_out]
        r_outs = refs[n_in + n + n_out:n_in + 2 * n + n_out]
        scratch = refs[n_in + 2 * n + n_out:n_in + 2 * n + n_out + n_scratch]
        sems = refs[n_in + 2 * n + n_out + n_scratch:]

        @pl.when(first())
        def _():
            _exchange_start(r_ins, r_outs, sems, scatter)

        body(*ins, *outs, *scratch)

        @pl.when(last())
        def _():
            _exchange_wait(r_ins, r_outs, sems, scatter)

    return wrapped


def _rider_specs(rider):
    if rider is None:
        return [], [], [], [], []
    arrays, scatter = rider
    out_shape, sems = _exchange_shapes(arrays, scatter)
    any_spec = pl.BlockSpec(memory_space=pl.ANY)
    return list(arrays), [any_spec] * len(arrays), out_shape, [any_spec] * len(arrays), sems


def _mm_nn_res(a, b, res, tm, name):
    T, K = a.shape
    N = b.shape[1]

    def body(a_ref, b_ref, r_ref, o_ref):
        o_ref[...] = r_ref[...] + jnp.dot(a_ref[...], b_ref[...], preferred_element_type=F32)

    return _pc(
        body, name=name, grid=(T // tm,), out_shape=jax.ShapeDtypeStruct((T, N), F32),
        in_specs=[pl.BlockSpec((tm, K), lambda i: (i, 0)), pl.BlockSpec((K, N), lambda i: (0, 0)),
                  pl.BlockSpec((tm, N), lambda i: (i, 0))],
        out_specs=pl.BlockSpec((tm, N), lambda i: (i, 0)),
        compiler_params=_params("parallel"),
    )(a, b, res)


def _mm_tn(a, b, n_col_shards, tn, tk, name):
    T, M = a.shape
    N = b.shape[1]
    sw = N // n_col_shards
    per_step = tn // sw
    nk = T // tk

    def body(a_ref, b_ref, o_ref, acc_sc):
        k = pl.program_id(1)
        part = lax.dot_general(a_ref[...].astype(BF16), b_ref[...].astype(BF16), TN_DIMS,
                               preferred_element_type=F32)

        @pl.when(k == 0)
        def _():
            acc_sc[...] = part

        @pl.when(k > 0)
        def _():
            acc_sc[...] += part

        @pl.when(k == nk - 1)
        def _():
            for s in range(per_step):
                o_ref[s] = acc_sc[:, s * sw:(s + 1) * sw].astype(BF16)

    return _pc(
        body, name=name, grid=(N // tn, nk),
        out_shape=jax.ShapeDtypeStruct((n_col_shards, M, sw), BF16),
        in_specs=[pl.BlockSpec((tk, M), lambda j, k: (k, 0)), pl.BlockSpec((tk, tn), lambda j, k: (k, j))],
        out_specs=pl.BlockSpec((per_step, M, sw), lambda j, k: (j, 0, 0)),
        scratch_shapes=[pltpu.VMEM((M, tn), F32)],
        compiler_params=_params("parallel", "arbitrary"),
    )(a, b)


def _proj_layout(D):
    return {"u": 0, "z_pool": POOL_W, "gates": 2 * POOL_W, "z_attn": 2 * POOL_W + 2 * D, "width": 2 * POOL_W + 2 * D + ATTN_W}


def _norm_inproj(x, g, w, tm):
    T, D = x.shape
    NW = w.shape[1]
    lay = _proj_layout(D)
    qkv0, za0, gl0 = 2 * POOL_W, 2 * POOL_W + 3 * ATTN_W, 2 * POOL_W + 4 * ATTN_W

    def body(x_ref, g_ref, w_ref, proj_ref, qkv_ref, h_ref):
        xv = x_ref[...]
        r = lax.rsqrt(jnp.mean(xv * xv, axis=-1, keepdims=True) + RMS_EPS)
        h = ((xv * r) * g_ref[...]).astype(BF16)
        h_ref[...] = h

        def cols(lo, hi):
            return jnp.dot(h, w_ref[:, lo:hi], preferred_element_type=F32)

        proj_ref[:, :lay["gates"]] = cols(0, qkv0)
        qkv_ref[...] = cols(qkv0, za0).astype(BF16)
        proj_ref[:, lay["gates"]:lay["z_attn"]] = cols(gl0, NW)
        proj_ref[:, lay["z_attn"]:] = cols(za0, gl0)

    return _pc(
        body, name="norm_inproj", grid=(T // tm,),
        out_shape=(jax.ShapeDtypeStruct((T, lay["width"]), F32), jax.ShapeDtypeStruct((T, 3 * ATTN_W), BF16),
                   jax.ShapeDtypeStruct((T, D), BF16)),
        in_specs=[pl.BlockSpec((tm, D), lambda i: (i, 0)), pl.BlockSpec((1, D), lambda i: (0, 0)),
                  pl.BlockSpec((D, NW), lambda i: (0, 0))],
        out_specs=(pl.BlockSpec((tm, lay["width"]), lambda i: (i, 0)), pl.BlockSpec((tm, 3 * ATTN_W), lambda i: (i, 0)),
                   pl.BlockSpec((tm, D), lambda i: (i, 0))),
        compiler_params=_params("parallel"),
    )(x, g, w)


def _window_sums(xh, forward):
    n = xh.shape[0]
    sums, s, step = [], xh, 1
    for _ in POOL_WINDOWS:
        s = s + pltpu.roll(s, step if forward else n - step, 0)
        sums.append(s)
        step *= 2
    return sums


def _pooled(u, halo, row):
    sums = _window_sums(jnp.concatenate([halo, u], axis=0), True)
    out = []
    for g, w in enumerate(POOL_WINDOWS):
        cols = slice(g * POOL_G, (g + 1) * POOL_G)
        cnt = jnp.minimum(row + 1, w).astype(F32)
        out.append(sums[g][HALO:, cols] / cnt - u[:, cols])
    return out


def _pool_fwd(proj, pool_w, scale, R):
    T = proj.shape[0]

    def body(u_ref, z_ref, pw_ref, sc_ref, y_ref, halo_sc):
        i = pl.program_id(0)

        @pl.when(i == 0)
        def _():
            halo_sc[...] = jnp.zeros_like(halo_sc)

        u = u_ref[...]
        row = i * R + lax.broadcasted_iota(jnp.int32, (R, 1), 0)
        pooled = _pooled(u, halo_sc[...], row)
        mixed = jnp.concatenate(
            [jnp.dot(pooled[g].astype(BF16), pw_ref[g].astype(BF16), preferred_element_type=F32)
             for g in range(len(POOL_WINDOWS))], axis=1)
        z = z_ref[...]
        y_ref[...] = ((mixed * sc_ref[...]) * (z * _sigmoid(z))).astype(BF16)
        halo_sc[...] = u[R - HALO:, :]

    return _pc(
        body, name="pool_fwd", grid=(T // R,), out_shape=jax.ShapeDtypeStruct((T, POOL_W), BF16),
        in_specs=[pl.BlockSpec((R, POOL_W), lambda i: (i, 0)), pl.BlockSpec((R, POOL_W), lambda i: (i, 1)),
                  pl.BlockSpec((4, POOL_G, POOL_G), lambda i: (0, 0, 0)), pl.BlockSpec((1, POOL_W), lambda i: (0, 0))],
        out_specs=pl.BlockSpec((R, POOL_W), lambda i: (i, 0)),
        scratch_shapes=[pltpu.VMEM((HALO, POOL_W), F32)],
        compiler_params=_params("arbitrary"),
    )(proj, proj, pool_w, scale)


def _softplus(l):
    return jnp.maximum(l, 0.0) + jnp.log(1.0 + jnp.exp(-jnp.abs(l)))


def _first_last_step(n0, n1):
    return (lambda: jnp.logical_and(pl.program_id(0) == 0, pl.program_id(1) == 0),
            lambda: jnp.logical_and(pl.program_id(0) == n0 - 1, pl.program_id(1) == n1 - 1))


def _head_lanes():
    lane = lax.broadcasted_iota(jnp.int32, (1, LANE), 1)
    return [lane < HEAD_DIM, lane >= HEAD_DIM]


def _head_masks(q, scale):
    qf = q.astype(F32) * scale
    return [jnp.where(m, qf, 0.0).astype(BF16) for m in _head_lanes()]


def _wide(c, width):
    return jnp.concatenate([c] * (width // LANE), axis=1)


def _max_row_norm2(x, heads):
    sq = x.astype(F32) * x.astype(F32)
    return [jnp.max(jnp.sum(jnp.where(m, sq, 0.0), axis=1, keepdims=True), axis=0, keepdims=True) for m in heads]


def _fill_blocks(src_ref, nk, tk, transposed_sc=None, masked_sc=None, norm_sc=None):
    heads = _head_lanes()
    if norm_sc is not None:
        for a in range(2):
            norm_sc[a][...] = jnp.zeros_like(norm_sc[a])

    def step(j, carry):
        rows = pl.ds(pl.multiple_of(j * tk, tk), tk)
        blk = src_ref[rows, :]
        if norm_sc is not None:
            for a, n2 in enumerate(_max_row_norm2(blk, heads)):
                norm_sc[a][...] = jnp.maximum(norm_sc[a][...], n2)
        if transposed_sc is not None:
            transposed_sc[j] = blk.astype(F32).T.astype(BF16)
        if masked_sc is not None:
            for a in range(2):
                masked_sc[a, rows, :] = jnp.where(heads[a], blk, jnp.zeros_like(blk))
        return carry

    lax.fori_loop(0, nk, step, 0)


def _attn_fwd(projb, proj, tq, rider=None):
    T = projb.shape[0]
    nq = T // tq
    tk, nk = tq, nq
    assert nk <= LANE
    n_pairs = ATTN_W // LANE
    zb = (proj.shape[1] - ATTN_W) // LANE

    def body(q_ref, k_ref, v_ref, za_ref, o_ref, y_ref, c_ref, kT_sc, vm_sc, *per_head):
        c_sc, cm_sc, o_sc, kn_sc, l_sc, w_sc = (per_head[2 * n:2 * n + 2] for n in range(6))
        i = pl.program_id(1)

        @pl.when(i == 0)
        def _():
            _fill_blocks(k_ref, nk, tk, transposed_sc=kT_sc, norm_sc=kn_sc)
            _fill_blocks(v_ref, nk, tk, masked_sc=vm_sc)

        qs = _head_masks(q_ref[...], HEAD_DIM ** -0.5)
        lane = lax.broadcasted_iota(jnp.int32, (1, LANE), 1)
        valid = lax.broadcasted_iota(jnp.int32, (tq, tk), 1) < lax.broadcasted_iota(jnp.int32, (tq, tk), 0)
        suffix = (lax.broadcasted_iota(jnp.int32, (tk, tk), 0) >= lax.broadcasted_iota(jnp.int32, (tk, tk), 1)).astype(BF16)
        for a in range(2):
            c_sc[a][...] = jnp.zeros_like(c_sc[a])
            o_sc[a][...] = jnp.zeros_like(o_sc[a])
            cm_sc[a][...] = jnp.full(cm_sc[a].shape, NO_CARRY, F32)
        l_max = [jnp.sqrt(qn * kn_sc[a][...]) for a, qn in enumerate(_max_row_norm2(qs[0] + qs[1], _head_lanes()))]

        def logits(j, slot):
            kT = kT_sc[j]
            for a in range(2):
                l_sc[a][slot] = jnp.dot(qs[a], kT, preferred_element_type=F32)

        def values(j, slot):
            rows = pl.ds(pl.multiple_of(j * tk, tk), tk)
            for a in range(2):
                o_sc[a][...] += jnp.dot(w_sc[a][slot], vm_sc[a, rows, :], preferred_element_type=F32)

        def weights(j, slot, masked):
            sp = []
            for a in range(2):
                x = _softplus(l_sc[a][slot])
                sp.append((jnp.where(valid, x, 0.0) if masked else x).astype(BF16))
            inc = [jnp.dot(sp[a], suffix, preferred_element_type=F32) for a in range(2)]
            for a in range(2):
                c = c_sc[a][...]
                w = jnp.exp(l_sc[a][slot] - inc[a] - _wide(c, tk))
                if masked:
                    w = jnp.where(valid, w, 0.0)
                w_sc[a][slot] = w.astype(BF16)
                cm_sc[a][...] = jnp.where(lane == j, c, cm_sc[a][...])
                c_sc[a][...] = c + inc[a][:, 0:1]

        def more():
            live = [jnp.min(c_sc[a][...], axis=0, keepdims=True) - l_max[a][0:1, :] <= ZERO_WEIGHT for a in range(2)]
            return jnp.max(jnp.where(jnp.logical_or(live[0], live[1]), 1, 0))

        logits(i, 0)
        logits(jnp.maximum(i - 1, 0), 1)
        weights(i, 0, True)

        def step(t, slot):
            logits(jnp.maximum(i - t - 1, 0), 1 - slot)
            values(i - t + 1, 1 - slot)
            weights(i - t, slot, False)

        def two_steps(carry):
            tt, _ = carry
            step(2 * tt + 1, 1)
            step(2 * tt + 2, 0)
            return tt + 1, more()

        pairs, go = lax.while_loop(lambda c: jnp.logical_and(c[0] < i // 2, c[1] > 0), two_steps, (0, more()))
        done = 2 * pairs
        one_more = jnp.logical_and(done == i - 1, go > 0)

        @pl.when(one_more)
        def _():
            step(i, 1)
            values(0, 1)

        @pl.when(jnp.logical_not(one_more))
        def _():
            values(i - done, 0)

        o = o_sc[0][...] + o_sc[1][...]
        o_ref[...] = o
        za = za_ref[...]
        y_ref[...] = (o * (za * _sigmoid(za))).astype(BF16)
        c_ref[0, 0] = cm_sc[0][...]
        c_ref[1, 0] = cm_sc[1][...]

    scratch = ([pltpu.VMEM((nk, LANE, tk), BF16), pltpu.VMEM((2, T, LANE), BF16)]
               + [pltpu.VMEM((tq, LANE), F32)] * 6 + [pltpu.VMEM((8, LANE), F32)] * 2
               + [pltpu.VMEM((2, tq, tk), F32)] * 2 + [pltpu.VMEM((2, tq, tk), BF16)] * 2)
    r_in, r_in_specs, r_out, r_out_specs, r_sems = _rider_specs(rider)
    body = _with_rider(body, 4, 3, len(scratch), rider, *_first_last_step(n_pairs, nq))
    return _pc(
        body, name="attn_fwd", grid=(n_pairs, nq),
        out_shape=tuple([jax.ShapeDtypeStruct((T, ATTN_W), F32), jax.ShapeDtypeStruct((T, ATTN_W), BF16),
                         jax.ShapeDtypeStruct((2 * n_pairs, nq, tq, LANE), F32)] + r_out),
        in_specs=[pl.BlockSpec((tq, LANE), lambda p, i: (i, p)),
                  pl.BlockSpec((T, LANE), lambda p, i: (0, n_pairs + p)),
                  pl.BlockSpec((T, LANE), lambda p, i: (0, 2 * n_pairs + p)),
                  pl.BlockSpec((tq, LANE), lambda p, i: (i, zb + p))] + r_in_specs,
        out_specs=tuple([pl.BlockSpec((tq, LANE), lambda p, i: (i, p)), pl.BlockSpec((tq, LANE), lambda p, i: (i, p)),
                         pl.BlockSpec((2, 1, tq, LANE), lambda p, i: (p, i, 0, 0))] + r_out_specs),
        scratch_shapes=scratch + r_sems,
        compiler_params=_params("arbitrary", "arbitrary"),
    )(projb, projb, projb, proj, *r_in)


def _gates(gl0, gl1, bg, D):
    return _sigmoid(gl0 + bg[:, :D]), _sigmoid(gl1 + bg[:, D:])


def _merge_fwd(y_pool, y_attn, w_pu, w_au, proj, b_gate, tm):
    T = y_pool.shape[0]
    D = w_pu.shape[1]
    gb = _proj_layout(D)["gates"] // D

    def body(yp_ref, ya_ref, wpu_ref, wau_ref, gl0_ref, gl1_ref, bg_ref, m_ref):
        p = jnp.dot(yp_ref[...], wpu_ref[...], preferred_element_type=F32)
        a = jnp.dot(ya_ref[...], wau_ref[...], preferred_element_type=F32)
        g0, g1 = _gates(gl0_ref[...], gl1_ref[...], bg_ref[...], D)
        m_ref[...] = (g0 * p + g1 * a).astype(BF16)

    row = lambda i: (i, 0)
    fixed = lambda i: (0, 0)
    return _pc(
        body, name="merge_fwd", grid=(T // tm,), out_shape=jax.ShapeDtypeStruct((T, D), BF16),
        in_specs=[pl.BlockSpec((tm, POOL_W), row), pl.BlockSpec((tm, ATTN_W), row),
                  pl.BlockSpec((POOL_W, D), fixed), pl.BlockSpec((ATTN_W, D), fixed),
                  pl.BlockSpec((tm, D), lambda i: (i, gb)), pl.BlockSpec((tm, D), lambda i: (i, gb + 1)),
                  pl.BlockSpec((1, 2 * D), fixed)],
        out_specs=pl.BlockSpec((tm, D), row),
        compiler_params=_params("parallel"),
    )(y_pool, y_attn, w_pu, w_au, proj, proj, b_gate)


def _final_loss(x, g, target, tm):
    T, D = x.shape

    def body(x_ref, g_ref, t_ref, dx_ref, dg_ref, loss_ref):
        @pl.when(pl.program_id(0) == 0)
        def _():
            dg_ref[...] = jnp.zeros_like(dg_ref)
            loss_ref[...] = jnp.zeros_like(loss_ref)

        xv, gv = x_ref[...], g_ref[...]
        r = lax.rsqrt(jnp.mean(xv * xv, axis=-1, keepdims=True) + RMS_EPS)
        xh = xv * r
        d = xh * gv - t_ref[...]
        loss_ref[...] += 0.5 * jnp.sum(jnp.mean(d * d, axis=-1, keepdims=True), axis=0, keepdims=True)
        dy = d * (1.0 / D)
        dg_ref[...] += jnp.sum(dy * xh, axis=0, keepdims=True)
        dh = dy * gv
        dx_ref[...] = r * (dh - xh * jnp.mean(dh * xh, axis=-1, keepdims=True))

    return _pc(
        body, name="final_loss", grid=(T // tm,),
        out_shape=(jax.ShapeDtypeStruct((T, D), F32), jax.ShapeDtypeStruct((1, D), F32),
                   jax.ShapeDtypeStruct((8, LANE), F32)),
        in_specs=[pl.BlockSpec((tm, D), lambda i: (i, 0)), pl.BlockSpec((1, D), lambda i: (0, 0)),
                  pl.BlockSpec((tm, D), lambda i: (i, 0))],
        out_specs=(pl.BlockSpec((tm, D), lambda i: (i, 0)), pl.BlockSpec((1, D), lambda i: (0, 0)),
                   pl.BlockSpec((8, LANE), lambda i: (0, 0))),
        compiler_params=_params("arbitrary"),
    )(x, g, target)


def _merge_bwd(dxo, w_out, y_pool, y_attn, w_pu, w_au, proj, b_gate, tm):
    T, D = dxo.shape
    gb = _proj_layout(D)["gates"] // D

    def body(dxo_ref, wout_ref, yp_ref, ya_ref, wpu_ref, wau_ref, gl0_ref, gl1_ref, bg_ref,
             dp_ref, da_ref, dgl_ref, dbg_ref):
        @pl.when(pl.program_id(0) == 0)
        def _():
            dbg_ref[...] = jnp.zeros_like(dbg_ref)

        dmv = lax.dot_general(dxo_ref[...].astype(BF16), wout_ref[...], NT_DIMS, preferred_element_type=F32)
        p = jnp.dot(yp_ref[...], wpu_ref[...], preferred_element_type=F32)
        a = jnp.dot(ya_ref[...], wau_ref[...], preferred_element_type=F32)
        g0, g1 = _gates(gl0_ref[...], gl1_ref[...], bg_ref[...], D)
        dp_ref[...] = (dmv * g0).astype(BF16)
        da_ref[...] = (dmv * g1).astype(BF16)
        dgl0 = dmv * p * (g0 * (1.0 - g0))
        dgl1 = dmv * a * (g1 * (1.0 - g1))
        dgl_ref[:, :D] = dgl0.astype(BF16)
        dgl_ref[:, D:] = dgl1.astype(BF16)
        dbg_ref[:, :D] += jnp.sum(dgl0, axis=0, keepdims=True)
        dbg_ref[:, D:] += jnp.sum(dgl1, axis=0, keepdims=True)

    row = lambda i: (i, 0)
    fixed = lambda i: (0, 0)
    return _pc(
        body, name="merge_bwd", grid=(T // tm,),
        out_shape=(jax.ShapeDtypeStruct((T, D), BF16), jax.ShapeDtypeStruct((T, D), BF16),
                   jax.ShapeDtypeStruct((T, 2 * D), BF16), jax.ShapeDtypeStruct((1, 2 * D), F32)),
        in_specs=[pl.BlockSpec((tm, D), row), pl.BlockSpec((D, D), fixed),
                  pl.BlockSpec((tm, POOL_W), row), pl.BlockSpec((tm, ATTN_W), row),
                  pl.BlockSpec((POOL_W, D), fixed), pl.BlockSpec((ATTN_W, D), fixed),
                  pl.BlockSpec((tm, D), lambda i: (i, gb)), pl.BlockSpec((tm, D), lambda i: (i, gb + 1)),
                  pl.BlockSpec((1, 2 * D), fixed)],
        out_specs=(pl.BlockSpec((tm, D), row), pl.BlockSpec((tm, D), row), pl.BlockSpec((tm, 2 * D), row),
                   pl.BlockSpec((1, 2 * D), fixed)),
        compiler_params=_params("arbitrary"),
    )(dxo, w_out, y_pool, y_attn, w_pu, w_au, proj, proj, b_gate)


def _pool_bwd(proj, dp, w_pu, pool_w, scale, R):
    T = proj.shape[0]
    D = w_pu.shape[1]
    nb = T // R
    hb = R // HALO

    def body(u_ref, up_ref, z_ref, dp_ref, wpu_ref, pw_ref, sc_ref, du_ref, dz_ref, dpw_ref, dsc_ref, halo_sc):
        i = pl.program_id(0)
        rb = nb - 1 - i

        @pl.when(i == 0)
        def _():
            halo_sc[...] = jnp.zeros_like(halo_sc)
            dpw_ref[...] = jnp.zeros_like(dpw_ref)
            dsc_ref[...] = jnp.zeros_like(dsc_ref)

        u = u_ref[...]
        row = rb * R + lax.broadcasted_iota(jnp.int32, (R, 1), 0)
        before = jnp.where(rb > 0, up_ref[...], 0.0)
        pooled = _pooled(u, before, row)
        pw = [pw_ref[g].astype(BF16) for g in range(len(POOL_WINDOWS))]
        mixed = jnp.concatenate(
            [jnp.dot(pooled[g].astype(BF16), pw[g], preferred_element_type=F32) for g in range(len(POOL_WINDOWS))],
            axis=1)
        sc = sc_ref[...]
        silu, dsilu = _silu_and_grad(z_ref[...])
        dyv = lax.dot_general(dp_ref[...], wpu_ref[...], NT_DIMS, preferred_element_type=F32)
        dmp = dyv * silu
        dz_ref[...] = (dyv * (mixed * sc) * dsilu).astype(BF16)
        dsc_ref[...] += jnp.sum(dmp * mixed, axis=0, keepdims=True)
        dmixed = (dmp * sc).astype(BF16)
        dpn = []
        dpooled = []
        for g, w in enumerate(POOL_WINDOWS):
            cols = slice(g * POOL_G, (g + 1) * POOL_G)
            dpw_ref[g] += lax.dot_general(pooled[g].astype(BF16), dmixed[:, cols], TN_DIMS,
                                          preferred_element_type=F32)
            dpg = lax.dot_general(dmixed[:, cols], pw[g], NT_DIMS, preferred_element_type=F32)
            dpooled.append(dpg)
            dpn.append(dpg / jnp.minimum(row + 1, w).astype(F32))
        dpn = jnp.concatenate(dpn, axis=1)
        sums = _window_sums(jnp.concatenate([dpn, halo_sc[...]], axis=0), False)
        du_ref[...] = jnp.concatenate(
            [sums[g][:R, g * POOL_G:(g + 1) * POOL_G] - dpooled[g] for g in range(len(POOL_WINDOWS))],
            axis=1).astype(BF16)
        halo_sc[...] = dpn[:HALO, :]

    rev = lambda i: (nb - 1 - i, 0)
    return _pc(
        body, name="pool_bwd", grid=(nb,),
        out_shape=(jax.ShapeDtypeStruct((T, POOL_W), BF16), jax.ShapeDtypeStruct((T, POOL_W), BF16),
                   jax.ShapeDtypeStruct((4, POOL_G, POOL_G), F32), jax.ShapeDtypeStruct((1, POOL_W), F32)),
        in_specs=[pl.BlockSpec((R, POOL_W), rev),
                  pl.BlockSpec((HALO, POOL_W), lambda i: (jnp.maximum((nb - 1 - i) * hb - 1, 0), 0)),
                  pl.BlockSpec((R, POOL_W), lambda i: (nb - 1 - i, 1)),
                  pl.BlockSpec((R, D), rev), pl.BlockSpec((POOL_W, D), lambda i: (0, 0)),
                  pl.BlockSpec((4, POOL_G, POOL_G), lambda i: (0, 0, 0)), pl.BlockSpec((1, POOL_W), lambda i: (0, 0))],
        out_specs=(pl.BlockSpec((R, POOL_W), rev), pl.BlockSpec((R, POOL_W), rev),
                   pl.BlockSpec((4, POOL_G, POOL_G), lambda i: (0, 0, 0)), pl.BlockSpec((1, POOL_W), lambda i: (0, 0))),
        scratch_shapes=[pltpu.VMEM((HALO, POOL_W), F32)],
        compiler_params=_params("arbitrary"),
    )(proj, proj, proj, dp, w_pu, pool_w, scale)


def _attn_gate_bwd(da, w_au, o, proj, tm):
    T, D = da.shape
    zb = (proj.shape[1] - ATTN_W) // ATTN_W

    def body(da_ref, wau_ref, o_ref, za_ref, do_ref, dza_ref):
        silu, dsilu = _silu_and_grad(za_ref[...])
        dyv = lax.dot_general(da_ref[...], wau_ref[...], NT_DIMS, preferred_element_type=F32)
        do_ref[...] = (dyv * silu).astype(BF16)
        dza_ref[...] = (dyv * o_ref[...] * dsilu).astype(BF16)

    row = lambda i: (i, 0)
    return _pc(
        body, name="attn_gate_bwd", grid=(T // tm,),
        out_shape=(jax.ShapeDtypeStruct((T, ATTN_W), BF16), jax.ShapeDtypeStruct((T, ATTN_W), BF16)),
        in_specs=[pl.BlockSpec((tm, D), row), pl.BlockSpec((ATTN_W, D), lambda i: (0, 0)),
                  pl.BlockSpec((tm, ATTN_W), row), pl.BlockSpec((tm, ATTN_W), lambda i: (i, zb))],
        out_specs=(pl.BlockSpec((tm, ATTN_W), row), pl.BlockSpec((tm, ATTN_W), row)),
        compiler_params=_params("parallel"),
    )(da, w_au, o, proj)


def _attn_bwd(projb, do, carries, tq, rider=None):
    T = projb.shape[0]
    nq = T // tq
    tk, nk = tq, nq
    n_pairs = ATTN_W // LANE
    scale = HEAD_DIM ** -0.5

    def body(q_ref, k_ref, v_ref, do_ref, c_ref, dq_ref, dk_ref, dv_ref, kT_sc, vT_sc, km_sc, dkT_ref, dvT_ref, *per_head):
        f_sc, dq_sc, kn_sc, l_sc, dw_sc, dl_sc, w_sc = (per_head[2 * n:2 * n + 2] for n in range(7))
        i = pl.program_id(1)

        @pl.when(i == 0)
        def _():
            _fill_blocks(k_ref, nk, tk, transposed_sc=kT_sc, masked_sc=km_sc, norm_sc=kn_sc)
            _fill_blocks(v_ref, nk, tk, transposed_sc=vT_sc)
            dkT_ref[...] = jnp.zeros_like(dkT_ref)
            dvT_ref[...] = jnp.zeros_like(dvT_ref)

        qs = _head_masks(q_ref[...], scale)
        dos = _head_masks(do_ref[...], 1.0)
        qT = [x.astype(F32).T.astype(BF16) for x in qs]
        doT = [x.astype(F32).T.astype(BF16) for x in dos]
        lane = lax.broadcasted_iota(jnp.int32, (1, LANE), 1)
        valid = lax.broadcasted_iota(jnp.int32, (tq, tk), 1) < lax.broadcasted_iota(jnp.int32, (tq, tk), 0)
        kk0 = lax.broadcasted_iota(jnp.int32, (tk, tk), 0)
        kk1 = lax.broadcasted_iota(jnp.int32, (tk, tk), 1)
        suffix = (kk0 >= kk1).astype(BF16)
        prefix = (kk0 <= kk1).astype(BF16)
        for a in range(2):
            f_sc[a][...] = jnp.zeros_like(f_sc[a])
            dq_sc[a][...] = jnp.zeros_like(dq_sc[a])
            dl_sc[a][1] = jnp.zeros((tq, tk), BF16)
            w_sc[a][1] = jnp.zeros((tq, tk), BF16)
        live = lane == i
        for a, qn in enumerate(_max_row_norm2(qs[0] + qs[1], _head_lanes())):
            l_max = jnp.sqrt(qn * kn_sc[a][0:1, :])
            live = jnp.logical_or(live, jnp.min(c_ref[a, 0], axis=0, keepdims=True) - l_max <= ZERO_WEIGHT)
        t0 = jnp.min(jnp.where(jnp.logical_and(live, lane <= i), lane, i))
        n = i - t0

        def products(j, slot):
            kT = kT_sc[j]
            vT = vT_sc[j]
            for a in range(2):
                l_sc[a][slot] = jnp.dot(qs[a], kT, preferred_element_type=F32)
                dw_sc[a][slot] = jnp.dot(dos[a], vT, preferred_element_type=F32)

        def gradients(j, slot):
            rows = pl.ds(pl.multiple_of(j * tk, tk), tk)
            dkT = []
            dvT = []
            for a in range(2):
                dlb = dl_sc[a][slot]
                dq_sc[a][...] += jnp.dot(dlb, km_sc[a, rows, :], preferred_element_type=F32)
                dkT.append(jnp.dot(qT[a], dlb, preferred_element_type=F32))
                dvT.append(jnp.dot(doT[a], w_sc[a][slot], preferred_element_type=F32))
            dkT_ref[j] += dkT[0] + dkT[1]
            dvT_ref[j] += dvT[0] + dvT[1]

        def elementwise(j, slot, masked):
            sp, inc, e, beta, p = [None] * 2, [None] * 2, [None] * 2, [None] * 2, [None] * 2
            for a in range(2):
                x = _softplus(l_sc[a][slot])
                sp[a] = jnp.where(valid, x, 0.0) if masked else x
            for a in range(2):
                inc[a] = jnp.dot(sp[a].astype(BF16), suffix, preferred_element_type=F32)
            for a in range(2):
                l = l_sc[a][slot]
                c = jnp.sum(jnp.where(lane == j, c_ref[a, 0], 0.0), axis=1, keepdims=True)
                w = jnp.exp(l - inc[a] - c)
                if masked:
                    w = jnp.where(valid, w, 0.0)
                w_sc[a][slot] = w.astype(BF16)
                beta[a] = jnp.exp(l - sp[a])
                e[a] = w * dw_sc[a][slot]
            for a in range(2):
                p[a] = jnp.dot(e[a].astype(BF16), prefix, preferred_element_type=F32)
            for a in range(2):
                f = f_sc[a][...]
                dl = e[a] - beta[a] * (p[a] + _wide(f, tk))
                if masked:
                    dl = jnp.where(valid, dl, 0.0)
                dl_sc[a][slot] = dl.astype(BF16)
                f_sc[a][...] = f + p[a][:, tk - 1:tk]

        def step(r, slot):
            products(t0 + r + 1, 1 - slot)
            gradients(t0 + jnp.maximum(r - 1, 0), 1 - slot)
            elementwise(t0 + r, slot, False)

        def last(slot):
            gradients(t0 + jnp.maximum(n - 1, 0), 1 - slot)
            elementwise(i, slot, True)
            gradients(i, slot)

        products(t0, 0)

        def two_steps(tt, carry):
            step(2 * tt, 0)
            step(2 * tt + 1, 1)
            return carry

        lax.fori_loop(0, n // 2, two_steps, 0)

        @pl.when(n % 2 == 1)
        def _():
            step(n - 1, 0)
            last(1)

        @pl.when(n % 2 == 0)
        def _():
            last(0)

        dq_ref[...] = ((dq_sc[0][...] + dq_sc[1][...]) * scale).astype(BF16)

        @pl.when(i == nq - 1)
        def _():
            def untranspose(j, carry):
                rows = pl.ds(pl.multiple_of(j * tk, tk), tk)
                dk_ref[rows, :] = dkT_ref[j].T.astype(BF16)
                dv_ref[rows, :] = dvT_ref[j].T.astype(BF16)
                return carry

            lax.fori_loop(0, nk, untranspose, 0)

    scratch = ([pltpu.VMEM((nk, LANE, tk), BF16), pltpu.VMEM((nk, LANE, tk), BF16), pltpu.VMEM((2, T, LANE), BF16),
                pltpu.VMEM((nk, LANE, tk), F32), pltpu.VMEM((nk, LANE, tk), F32)]
               + [pltpu.VMEM((tq, LANE), F32)] * 4 + [pltpu.VMEM((8, LANE), F32)] * 2
               + [pltpu.VMEM((2, tq, tk), F32)] * 4 + [pltpu.VMEM((2, tq, tk), BF16)] * 4)
    r_in, r_in_specs, r_out, r_out_specs, r_sems = _rider_specs(rider)
    body = _with_rider(body, 5, 3, len(scratch), rider, *_first_last_step(n_pairs, nq))
    return _pc(
        body, name="attn_bwd", grid=(n_pairs, nq),
        out_shape=tuple([jax.ShapeDtypeStruct((T, ATTN_W), BF16)] * 3 + r_out),
        in_specs=[pl.BlockSpec((tq, LANE), lambda p, i: (i, p)),
                  pl.BlockSpec((T, LANE), lambda p, i: (0, n_pairs + p)),
                  pl.BlockSpec((T, LANE), lambda p, i: (0, 2 * n_pairs + p)),
                  pl.BlockSpec((tq, LANE), lambda p, i: (i, p)),
                  pl.BlockSpec((2, 1, tq, LANE), lambda p, i: (p, i, 0, 0))] + r_in_specs,
        out_specs=tuple([pl.BlockSpec((tq, LANE), lambda p, i: (i, p)), pl.BlockSpec((T, LANE), lambda p, i: (0, p)),
                         pl.BlockSpec((T, LANE), lambda p, i: (0, p))] + r_out_specs),
        scratch_shapes=scratch + r_sems,
        compiler_params=_params("arbitrary", "arbitrary"),
    )(projb, projb, projb, do, carries, *r_in)


def _dh_norm_bwd(dproj, w, x, g, dxo, tm, rider=None):
    T, D = x.shape
    NW = w.shape[1]
    nm = T // tm

    def body(dp_ref, w_ref, x_ref, g_ref, dxo_ref, dx_ref, dg_ref):
        @pl.when(pl.program_id(0) == 0)
        def _():
            dg_ref[...] = jnp.zeros_like(dg_ref)

        dhv = lax.dot_general(dp_ref[...], w_ref[...], NT_DIMS, preferred_element_type=F32)
        xv = x_ref[...]
        r = lax.rsqrt(jnp.mean(xv * xv, axis=-1, keepdims=True) + RMS_EPS)
        xh = xv * r
        dg_ref[...] += jnp.sum(dhv * xh, axis=0, keepdims=True)
        dhg = dhv * g_ref[...]
        dx_ref[...] = dxo_ref[...] + r * (dhg - xh * jnp.mean(dhg * xh, axis=-1, keepdims=True))

    r_in, r_in_specs, r_out, r_out_specs, r_sems = _rider_specs(rider)
    body = _with_rider(body, 5, 2, 0, rider, lambda: pl.program_id(0) == 0, lambda: pl.program_id(0) == nm - 1)
    row = lambda i: (i, 0)
    fixed = lambda i: (0, 0)
    return _pc(
        body, name="d_h_norm_bwd", grid=(nm,),
        out_shape=tuple([jax.ShapeDtypeStruct((T, D), F32), jax.ShapeDtypeStruct((1, D), F32)] + r_out),
        in_specs=[pl.BlockSpec((tm, NW), row), pl.BlockSpec((D, NW), fixed), pl.BlockSpec((tm, D), row),
                  pl.BlockSpec((1, D), fixed), pl.BlockSpec((tm, D), row)] + r_in_specs,
        out_specs=tuple([pl.BlockSpec((tm, D), row), pl.BlockSpec((1, D), fixed)] + r_out_specs),
        scratch_shapes=r_sems,
        compiler_params=_params("arbitrary"),
    )(dproj, w, x, g, dxo, *r_in)


def _adamw(pieces, w, m, v, name):
    rows, cols = w.shape
    br = rows
    while br * cols > 65536 and br % 16 == 0:
        br //= 2
    c1 = 1.0 / (1.0 - ADAM_B1 ** ADAM_STEP)
    c2 = 1.0 / (1.0 - ADAM_B2 ** ADAM_STEP)

    def body(p_ref, w_ref, m_ref, v_ref, g_ref, d_ref, nm_ref, nv_ref):
        g = p_ref[0].astype(F32)
        for s in range(1, N_DEV):
            g = g + p_ref[s].astype(F32)
        nm = ADAM_B1 * m_ref[...] + (1.0 - ADAM_B1) * g
        nv = ADAM_B2 * v_ref[...] + (1.0 - ADAM_B2) * (g * g)
        g_ref[...] = g
        nm_ref[...] = nm
        nv_ref[...] = nv
        d_ref[...] = -ADAM_LR * ((nm * c1) / (jnp.sqrt(nv * c2) + ADAM_EPS) + ADAM_WD * w_ref[...])

    blk = pl.BlockSpec((br, cols), lambda i: (i, 0))
    shape = jax.ShapeDtypeStruct((rows, cols), F32)
    return _pc(
        body, name=name, grid=(rows // br,), out_shape=(shape, shape, shape, shape),
        in_specs=[pl.BlockSpec((N_DEV, br, cols), lambda i: (0, i, 0)), blk, blk, blk],
        out_specs=(blk, blk, blk, blk),
        compiler_params=_params("parallel"),
    )(pieces, w, m, v)


def _rows128(a):
    flat = a.reshape(-1)
    n = flat.shape[0]
    padded = -(-n // (8 * LANE)) * (8 * LANE)
    if padded != n:
        flat = jnp.concatenate([flat, jnp.zeros((padded - n,), flat.dtype)])
    return flat.reshape(-1, LANE)


def _pack(parts):
    return jnp.concatenate([_rows128(p) for p in parts], axis=0)


def _unpack(packed, like):
    out, r = [], 0
    for a in like:
        n = a.size
        nr = -(-n // (8 * LANE)) * 8
        out.append(packed[r:r + nr].reshape(-1)[:n].reshape(a.shape))
        r += nr
    return out


def kernel(x, norm_g, w_in, b_gate, pool_w, pool_scale, w_pool_up, w_attn_up, w_out, final_g, loss_target, m_norm_g, m_w_in, m_b_gate, m_pool_w, m_pool_scale, m_w_pool_up, m_w_attn_up, m_w_out, m_final_g, v_norm_g, v_w_in, v_b_gate, v_pool_w, v_pool_scale, v_w_pool_up, v_w_attn_up, v_w_out, v_final_g):
    L = norm_g.shape[0]
    T, D = x.shape[1], x.shape[2]
    NW = w_in.shape[2] * N_DEV
    assert NW == 2 * POOL_W + 4 * ATTN_W + 2 * D and x.shape[0] == 1
    tm = min(512, T)
    tq = min(256, T // 2)
    x0 = x.reshape(T, D)
    target = loss_target.reshape(T, D)

    assert L >= 2
    win_first = jnp.transpose(_exchange([w_in[0].astype(BF16)], [False], "gather_w_in0")[0], (1, 0, 2)).reshape(D, NW)
    rest = [w_in[1:].astype(BF16), w_pool_up.astype(BF16), w_attn_up.astype(BF16), w_out.astype(BF16)]

    saved = []
    xl = x0
    for l in range(L):
        proj, projb, h = _norm_inproj(xl, norm_g[l:l + 1], win_first if l == 0 else win_rest[l - 1], min(256, T))
        y_pool = _pool_fwd(proj, pool_w[l], pool_scale[l:l + 1], tm)
        if l == 0:
            o, y_attn, carries, g_in, g_pu, g_au, g_out = _attn_fwd(projb, proj, tq, rider=(rest, [False] * 4))
            win_rest = jnp.transpose(g_in, (1, 2, 0, 3)).reshape(L - 1, D, NW)
            wpu_full = jnp.transpose(g_pu, (1, 2, 0, 3)).reshape(L, POOL_W, D)
            wau_full = jnp.transpose(g_au, (1, 2, 0, 3)).reshape(L, ATTN_W, D)
            wout_full = jnp.transpose(g_out, (1, 0, 2, 3)).reshape(L, D, D)
        else:
            o, y_attn, carries = _attn_fwd(projb, proj, tq)
        merged = _merge_fwd(y_pool, y_attn, wpu_full[l], wau_full[l], proj, b_gate[l:l + 1], min(256, T))
        x_next = _mm_nn_res(merged, wout_full[l], xl, tm, "out_proj")
        saved.append((xl, proj, projb, h, y_pool, o, y_attn, carries, merged))
        xl = x_next

    dx, d_final_g, loss_part = _final_loss(xl, final_g.reshape(1, D), target, tm)

    d_norm_g, d_b_gate, d_pool_w, d_pool_scale = [None] * L, [None] * L, [None] * L, [None] * L
    d_win, d_wpu, d_wau, d_wout = [None] * L, [None] * L, [None] * L, [None] * L
    small_like = [norm_g, b_gate, pool_w, pool_scale, final_g, jnp.zeros((8, LANE), F32)]
    for l in reversed(range(L)):
        xin, proj, projb, h, y_pool, o, y_attn, carries, merged = saved[l]
        win_l = win_first if l == 0 else win_rest[l - 1]
        d_wout[l] = _mm_tn(merged, dx, 1, D, min(1024, T), "d_w_out").reshape(N_DEV, D // N_DEV, D)
        dp, da, dgl, d_b_gate[l] = _merge_bwd(dx, wout_full[l], y_pool, y_attn, wpu_full[l], wau_full[l], proj,
                                              b_gate[l:l + 1], min(256, T))
        d_wpu[l] = _mm_tn(y_pool, dp, N_DEV, D, min(1024, T), "d_w_pool_up")
        d_wau[l] = _mm_tn(y_attn, da, N_DEV, D, min(1024, T), "d_w_attn_up")
        du, dzp, d_pool_w[l], d_pool_scale[l] = _pool_bwd(proj, dp, wpu_full[l], pool_w[l], pool_scale[l:l + 1], tm)
        do, dza = _attn_gate_bwd(da, wau_full[l], o, proj, tm)
        if l == 0:
            small = _pack([jnp.concatenate([jnp.zeros((1, D), F32)] + d_norm_g[1:], 0), jnp.concatenate(d_b_gate, 0),
                           jnp.stack(d_pool_w, 0), jnp.concatenate(d_pool_scale, 0), d_final_g, loss_part])
            early = d_win[1:] + d_wpu + d_wau + d_wout
            dq, dk, dv, got_small, *got_early = _attn_bwd(
                projb, do, carries, tq, rider=([small] + early, [False] + [True] * len(early)))
        else:
            dq, dk, dv = _attn_bwd(projb, do, carries, tq)
        dproj = jnp.concatenate([du, dzp, dq, dk, dv, dza, dgl], axis=1)
        d_win[l] = _mm_tn(h, dproj, N_DEV, NW // 2, min(1024, T), "d_w_in")
        if l == 0:
            dx, d_norm_g[l], got_win0 = _dh_norm_bwd(dproj, win_l, xin, norm_g[l:l + 1], dx, min(256, T),
                                                     rider=([d_win[0]], [True]))
        else:
            dx, d_norm_g[l] = _dh_norm_bwd(dproj, win_l, xin, norm_g[l:l + 1], dx, min(256, T))

    got_norm0 = _exchange([d_norm_g[0].reshape(-1, LANE)], [False], "gather_norm_grad")[0]
    r_small = jnp.concatenate([got_norm0, got_small[:, D // LANE:]], axis=1)
    r_in = jnp.stack([got_win0] + got_early[:L - 1], axis=1)
    r_pu = jnp.stack(got_early[L - 1:2 * L - 1], axis=1)
    r_au = jnp.stack(got_early[2 * L - 1:3 * L - 1], axis=1)
    r_out = jnp.stack(got_early[3 * L - 1:4 * L - 1], axis=1)

    def update(pieces, w, m, v, name):
        cols = w.shape[-1]
        res = _adamw(pieces.reshape(N_DEV, -1, cols), w.reshape(-1, cols), m.reshape(-1, cols),
                     v.reshape(-1, cols), name)
        return [r.reshape(w.shape) for r in res]

    u_in = update(r_in, w_in, m_w_in, v_w_in, "adamw_w_in")
    u_pu = update(r_pu, w_pool_up, m_w_pool_up, v_w_pool_up, "adamw_w_pool_up")
    u_au = update(r_au, w_attn_up, m_w_attn_up, v_w_attn_up, "adamw_w_attn_up")
    u_out = update(r_out, w_out, m_w_out, v_w_out, "adamw_w_out")
    zeros = small_like[-1]
    smalls = _adamw(r_small,
                    _pack([norm_g, b_gate, pool_w, pool_scale, final_g, zeros]),
                    _pack([m_norm_g, m_b_gate, m_pool_w, m_pool_scale, m_final_g, zeros]),
                    _pack([v_norm_g, v_b_gate, v_pool_w, v_pool_scale, v_final_g, zeros]), "adamw_small")
    s_g, s_d, s_m, s_v = [_unpack(s, small_like) for s in smalls]
    loss = s_g[5][0, 0]

    def ordered(k):
        s = (s_g, s_d, s_m, s_v)[k]
        return [s[0], u_in[k], s[1], s[2], s[3], u_pu[k], u_au[k], u_out[k], s[4]]

    return (loss, dx.reshape(x.shape), *ordered(0), *ordered(1), *ordered(2), *ordered(3))
```

```python
import jax
import jax.numpy as jnp
from jax import lax
from jax.experimental import pallas as pl
from jax.experimental.pallas import tpu as pltpu

F32 = jnp.float32
BF16 = jnp.bfloat16

N_DEV = 8
HEAD_DIM = 64
ATTN_W = 512
POOL_W = 512
POOL_G = 128
POOL_WINDOWS = (2, 4, 8, 16)
HALO = 16
LANE = 128
RMS_EPS = 1e-6
ZERO_WEIGHT = 110.0
NO_CARRY = 3.0e38
ADAM_LR, ADAM_B1, ADAM_B2, ADAM_EPS, ADAM_WD, ADAM_STEP = 0.001, 0.9, 0.999, 1e-08, 0.01, 10
VMEM_LIMIT = 56 * 1024 * 1024

NT_DIMS = (((1,), (1,)), ((), ()))
TN_DIMS = (((0,), (0,)), ((), ()))


def _pc(body, **kw):
    return pl.pallas_call(body, **kw)


def _params(*sem):
    return pltpu.CompilerParams(dimension_semantics=sem, vmem_limit_bytes=VMEM_LIMIT)


def _sigmoid(z):
    return 1.0 / (1.0 + jnp.exp(-z))


def _silu_and_grad(z):
    s = _sigmoid(z)
    return z * s, s * (1.0 + z * (1.0 - s))


def _my_index():
    return 4 * lax.axis_index("x") + 2 * lax.axis_index("y") + lax.axis_index("c")


def _peer(k):
    x, y, c = lax.axis_index("x"), lax.axis_index("y"), lax.axis_index("c")
    px = lax.rem(x + ((k >> 2) & 1), 2)
    py = lax.rem(y + ((k >> 1) & 1), 2)
    pc = lax.rem(c + (k & 1), 2)
    return (px, py, pc), 4 * px + 2 * py + pc


def _exchange_copies(ins, outs, sems, scatter):
    send_sems, recv_sems, local_sems = sems
    n = len(ins)
    me = _my_index()

    def src(a, idx):
        return ins[a].at[idx] if scatter[a] else ins[a]

    local = [pltpu.make_async_copy(src(a, me), outs[a].at[me], local_sems.at[a]) for a in range(n)]
    sends, arrivals = [], []
    for k in (1, 2, 4, 3, 5, 6, 7):
        dev, pidx = _peer(k)
        for a in range(n):
            sem = dict(send_sem=send_sems.at[a * N_DEV + k], recv_sem=recv_sems.at[a * N_DEV + k],
                       device_id=dev, device_id_type=pl.DeviceIdType.MESH)
            sends.append(pltpu.make_async_remote_copy(src_ref=src(a, pidx), dst_ref=outs[a].at[me], **sem))
            arrivals.append(pltpu.make_async_remote_copy(src_ref=src(a, pidx), dst_ref=outs[a].at[pidx], **sem))
    return local, sends, arrivals


def _exchange_start(ins, outs, sems, scatter):
    local, sends, _ = _exchange_copies(ins, outs, sems, scatter)
    for cp in local + sends:
        cp.start()


def _exchange_wait(ins, outs, sems, scatter):
    local, sends, arrivals = _exchange_copies(ins, outs, sems, scatter)
    for cp in arrivals:
        cp.wait_recv()
    for cp in sends:
        cp.wait_send()
    for cp in local:
        cp.wait()


def _exchange_shapes(arrays, scatter):
    n = len(arrays)
    out_shape = [jax.ShapeDtypeStruct((N_DEV,) + tuple(a.shape[1:] if s else a.shape), a.dtype)
                 for a, s in zip(arrays, scatter)]
    sems = [pltpu.SemaphoreType.DMA((n * N_DEV,)), pltpu.SemaphoreType.DMA((n * N_DEV,)),
            pltpu.SemaphoreType.DMA((n,))]
    return out_shape, sems


def _exchange(arrays, scatter, name):
    n = len(arrays)

    def body(*refs):
        ins, outs, sems = refs[:n], refs[n:2 * n], refs[2 * n:]
        _exchange_start(ins, outs, sems, scatter)
        _exchange_wait(ins, outs, sems, scatter)

    out_shape, sems = _exchange_shapes(arrays, scatter)
    any_spec = pl.BlockSpec(memory_space=pl.ANY)
    return _pc(
        body, name=name, out_shape=tuple(out_shape),
        in_specs=[any_spec] * n, out_specs=tuple([any_spec] * n), scratch_shapes=sems,
    )(*arrays)


def _with_rider(body, n_in, n_out, n_scratch, rider, first, last):
    if rider is None:
        return body
    arrays, scatter = rider
    n = len(arrays)

    def wrapped(*refs):
        ins, r_ins = refs[:n_in], refs[n_in:n_in + n]
        outs = refs[n_in + n:n_in + n + n_out]
        r_outs = refs[n_in + n + n_out:n_in + 2 * n + n_out]
        scratch = refs[n_in + 2 * n + n_out:n_in + 2 * n + n_out + n_scratch]
        sems = refs[n_in + 2 * n + n_out + n_scratch:]

        @pl.when(first())
        def _():
            _exchange_start(r_ins, r_outs, sems, scatter)

        body(*ins, *outs, *scratch)

        @pl.when(last())
        def _():
            _exchange_wait(r_ins, r_outs, sems, scatter)

    return wrapped


def _rider_specs(rider):
    if rider is None:
        return [], [], [], [], []
    arrays, scatter = rider
    out_shape, sems = _exchange_shapes(arrays, scatter)
    any_spec = pl.BlockSpec(memory_space=pl.ANY)
    return list(arrays), [any_spec] * len(arrays), out_shape, [any_spec] * len(arrays), sems


def _mm_nn_res(a, b, res, tm, name):
    T, K = a.shape
    N = b.shape[1]

    def body(a_ref, b_ref, r_ref, o_ref):
        o_ref[...] = r_ref[...] + jnp.dot(a_ref[...], b_ref[...], preferred_element_type=F32)

    return _pc(
        body, name=name, grid=(T // tm,), out_shape=jax.ShapeDtypeStruct((T, N), F32),
        in_specs=[pl.BlockSpec((tm, K), lambda i: (i, 0)), pl.BlockSpec((K, N), lambda i: (0, 0)),
                  pl.BlockSpec((tm, N), lambda i: (i, 0))],
        out_specs=pl.BlockSpec((tm, N), lambda i: (i, 0)),
        compiler_params=_params("parallel"),
    )(a, b, res)


def _mm_tn(a, b, n_col_shards, tn, tk, name):
    T, M = a.shape
    N = b.shape[1]
    sw = N // n_col_shards
    per_step = tn // sw
    nk = T // tk

    def body(a_ref, b_ref, o_ref, acc_sc):
        k = pl.program_id(1)
        part = lax.dot_general(a_ref[...].astype(BF16), b_ref[...].astype(BF16), TN_DIMS,
                               preferred_element_type=F32)

        @pl.when(k == 0)
        def _():
            acc_sc[...] = part

        @pl.when(k > 0)
        def _():
            acc_sc[...] += part

        @pl.when(k == nk - 1)
        def _():
            for s in range(per_step):
                o_ref[s] = acc_sc[:, s * sw:(s + 1) * sw].astype(BF16)

    return _pc(
        body, name=name, grid=(N // tn, nk),
        out_shape=jax.ShapeDtypeStruct((n_col_shards, M, sw), BF16),
        in_specs=[pl.BlockSpec((tk, M), lambda j, k: (k, 0)), pl.BlockSpec((tk, tn), lambda j, k: (k, j))],
        out_specs=pl.BlockSpec((per_step, M, sw), lambda j, k: (j, 0, 0)),
        scratch_shapes=[pltpu.VMEM((M, tn), F32)],
        compiler_params=_params("parallel", "arbitrary"),
    )(a, b)


def _proj_layout(D):
    return {"u": 0, "z_pool": POOL_W, "gates": 2 * POOL_W, "z_attn": 2 * POOL_W + 2 * D, "width": 2 * POOL_W + 2 * D + ATTN_W}


def _norm_inproj(x, g, w, tm):
    T, D = x.shape
    NW = w.shape[1]
    lay = _proj_layout(D)
    qkv0, za0, gl0 = 2 * POOL_W, 2 * POOL_W + 3 * ATTN_W, 2 * POOL_W + 4 * ATTN_W

    def body(x_ref, g_ref, w_ref, proj_ref, qkv_ref, h_ref):
        xv = x_ref[...]
        r = lax.rsqrt(jnp.mean(xv * xv, axis=-1, keepdims=True) + RMS_EPS)
        h = ((xv * r) * g_ref[...]).astype(BF16)
        h_ref[...] = h

        def cols(lo, hi):
            return jnp.dot(h, w_ref[:, lo:hi], preferred_element_type=F32)

        proj_ref[:, :lay["gates"]] = cols(0, qkv0)
        qkv_ref[...] = cols(qkv0, za0).astype(BF16)
        proj_ref[:, lay["gates"]:lay["z_attn"]] = cols(gl0, NW)
        proj_ref[:, lay["z_attn"]:] = cols(za0, gl0)

    return _pc(
        body, name="norm_inproj", grid=(T // tm,),
        out_shape=(jax.ShapeDtypeStruct((T, lay["width"]), F32), jax.ShapeDtypeStruct((T, 3 * ATTN_W), BF16),
                   jax.ShapeDtypeStruct((T, D), BF16)),
        in_specs=[pl.BlockSpec((tm, D), lambda i: (i, 0)), pl.BlockSpec((1, D), lambda i: (0, 0)),
                  pl.BlockSpec((D, NW), lambda i: (0, 0))],
        out_specs=(pl.BlockSpec((tm, lay["width"]), lambda i: (i, 0)), pl.BlockSpec((tm, 3 * ATTN_W), lambda i: (i, 0)),
                   pl.BlockSpec((tm, D), lambda i: (i, 0))),
        compiler_params=_params("parallel"),
    )(x, g, w)


def _window_sums(xh, forward):
    n = xh.shape[0]
    sums, s, step = [], xh, 1
    for _ in POOL_WINDOWS:
        s = s + pltpu.roll(s, step if forward else n - step, 0)
        sums.append(s)
        step *= 2
    return sums


def _pooled(u, halo, row):
    sums = _window_sums(jnp.concatenate([halo, u], axis=0), True)
    out = []
    for g, w in enumerate(POOL_WINDOWS):
        cols = slice(g * POOL_G, (g + 1) * POOL_G)
        cnt = jnp.minimum(row + 1, w).astype(F32)
        out.append(sums[g][HALO:, cols] / cnt - u[:, cols])
    return out


def _pool_fwd(proj, pool_w, scale, R):
    T = proj.shape[0]

    def body(u_ref, z_ref, pw_ref, sc_ref, y_ref, halo_sc):
        i = pl.program_id(0)

        @pl.when(i == 0)
        def _():
            halo_sc[...] = jnp.zeros_like(halo_sc)

        u = u_ref[...]
        row = i * R + lax.broadcasted_iota(jnp.int32, (R, 1), 0)
        pooled = _pooled(u, halo_sc[...], row)
        mixed = jnp.concatenate(
            [jnp.dot(pooled[g].astype(BF16), pw_ref[g].astype(BF16), preferred_element_type=F32)
             for g in range(len(POOL_WINDOWS))], axis=1)
        z = z_ref[...]
        y_ref[...] = ((mixed * sc_ref[...]) * (z * _sigmoid(z))).astype(BF16)
        halo_sc[...] = u[R - HALO:, :]

    return _pc(
        body, name="pool_fwd", grid=(T // R,), out_shape=jax.ShapeDtypeStruct((T, POOL_W), BF16),
        in_specs=[pl.BlockSpec((R, POOL_W), lambda i: (i, 0)), pl.BlockSpec((R, POOL_W), lambda i: (i, 1)),
                  pl.BlockSpec((4, POOL_G, POOL_G), lambda i: (0, 0, 0)), pl.BlockSpec((1, POOL_W), lambda i: (0, 0))],
        out_specs=pl.BlockSpec((R, POOL_W), lambda i: (i, 0)),
        scratch_shapes=[pltpu.VMEM((HALO, POOL_W), F32)],
        compiler_params=_params("arbitrary"),
    )(proj, proj, pool_w, scale)


def _softplus(l):
    return jnp.maximum(l, 0.0) + jnp.log(1.0 + jnp.exp(-jnp.abs(l)))


def _first_last_step(n0, n1):
    return (lambda: jnp.logical_and(pl.program_id(0) == 0, pl.program_id(1) == 0),
            lambda: jnp.logical_and(pl.program_id(0) == n0 - 1, pl.program_id(1) == n1 - 1))


def _head_lanes():
    lane = lax.broadcasted_iota(jnp.int32, (1, LANE), 1)
    return [lane < HEAD_DIM, lane >= HEAD_DIM]


def _head_masks(q, scale):
    qf = q.astype(F32) * scale
    return [jnp.where(m, qf, 0.0).astype(BF16) for m in _head_lanes()]


def _wide(c, width):
    return jnp.concatenate([c] * (width // LANE), axis=1)


def _max_row_norm2(x, heads):
    sq = x.astype(F32) * x.astype(F32)
    return [jnp.max(jnp.sum(jnp.where(m, sq, 0.0), axis=1, keepdims=True), axis=0, keepdims=True) for m in heads]


def _fill_blocks(src_ref, nk, tk, transposed_sc=None, masked_sc=None, norm_sc=None):
    heads = _head_lanes()
    if norm_sc is not None:
        for a in range(2):
            norm_sc[a][...] = jnp.zeros_like(norm_sc[a])

    def step(j, carry):
        rows = pl.ds(pl.multiple_of(j * tk, tk), tk)
        blk = src_ref[rows, :]
        if norm_sc is not None:
            for a, n2 in enumerate(_max_row_norm2(blk, heads)):
                norm_sc[a][...] = jnp.maximum(norm_sc[a][...], n2)
        if transposed_sc is not None:
            transposed_sc[j] = blk.astype(F32).T.astype(BF16)
        if masked_sc is not None:
            for a in range(2):
                masked_sc[a, rows, :] = jnp.where(heads[a], blk, jnp.zeros_like(blk))
        return carry

    lax.fori_loop(0, nk, step, 0)


def _attn_fwd(projb, proj, tq, rider=None):
    T = projb.shape[0]
    nq = T // tq
    tk, nk = tq, nq
    assert nk <= LANE
    n_pairs = ATTN_W // LANE
    zb = (proj.shape[1] - ATTN_W) // LANE

    def body(q_ref, k_ref, v_ref, za_ref, o_ref, y_ref, c_ref, kT_sc, vm_sc, *per_head):
        c_sc, cm_sc, o_sc, kn_sc, l_sc, w_sc = (per_head[2 * n:2 * n + 2] for n in range(6))
        i = pl.program_id(1)

        @pl.when(i == 0)
        def _():
            _fill_blocks(k_ref, nk, tk, transposed_sc=kT_sc, norm_sc=kn_sc)
            _fill_blocks(v_ref, nk, tk, masked_sc=vm_sc)

        qs = _head_masks(q_ref[...], HEAD_DIM ** -0.5)
        lane = lax.broadcasted_iota(jnp.int32, (1, LANE), 1)
        valid = lax.broadcasted_iota(jnp.int32, (tq, tk), 1) < lax.broadcasted_iota(jnp.int32, (tq, tk), 0)
        suffix = (lax.broadcasted_iota(jnp.int32, (tk, tk), 0) >= lax.broadcasted_iota(jnp.int32, (tk, tk), 1)).astype(BF16)
        for a in range(2):
            c_sc[a][...] = jnp.zeros_like(c_sc[a])
            o_sc[a][...] = jnp.zeros_like(o_sc[a])
            cm_sc[a][...] = jnp.full(cm_sc[a].shape, NO_CARRY, F32)
        l_max = [jnp.sqrt(qn * kn_sc[a][...]) for a, qn in enumerate(_max_row_norm2(qs[0] + qs[1], _head_lanes()))]

        def logits(j, slot):
            kT = kT_sc[j]
            for a in range(2):
                l_sc[a][slot] = jnp.dot(qs[a], kT, preferred_element_type=F32)

        def values(j, slot):
            rows = pl.ds(pl.multiple_of(j * tk, tk), tk)
            for a in range(2):
                o_sc[a][...] += jnp.dot(w_sc[a][slot], vm_sc[a, rows, :], preferred_element_type=F32)

        def softplus_bf16(slot, masked):
            out = []
            for a in range(2):
                x = _softplus(l_sc[a][slot])
                out.append((jnp.where(valid, x, 0.0) if masked else x).astype(BF16))
            return out

        def finish_weights(j, slot, inc, mask, keep=None):
            for a in range(2):
                c = c_sc[a][...]
                w = jnp.exp(l_sc[a][slot] - inc[a] - _wide(c, tk))
                if mask is not None:
                    w = jnp.where(mask, w, 0.0)
                add = inc[a][:, 0:1]
                if keep is not None:
                    w = jnp.where(keep, w, 0.0)
                    add = jnp.where(keep, add, 0.0)
                w_sc[a][slot] = w.astype(BF16)
                cm_sc[a][...] = jnp.where(lane == j, c, cm_sc[a][...])
                c_sc[a][...] = c + add

        def weights(j, slot):
            sp = softplus_bf16(slot, False)
            inc = [jnp.dot(sp[a], suffix, preferred_element_type=F32) for a in range(2)]
            finish_weights(j, slot, inc, None)

        def more():
            live = [jnp.min(c_sc[a][...], axis=0, keepdims=True) - l_max[a][0:1, :] <= ZERO_WEIGHT for a in range(2)]
            return jnp.max(jnp.where(jnp.logical_or(live[0], live[1]), 1, 0))

        logits(i, 0)
        logits(jnp.maximum(i - 1, 0), 1)
        sp = softplus_bf16(0, True) + softplus_bf16(1, False)
        inc = [jnp.dot(x, suffix, preferred_element_type=F32) for x in sp]
        finish_weights(i, 0, inc[:2], valid)
        finish_weights(i - 1, 1, inc[2:], None, keep=i > 0)
        values(i, 0)
        logits(jnp.maximum(i - 2, 0), 0)

        def step(t, slot):
            logits(jnp.maximum(i - t - 1, 0), 1 - slot)
            values(i - t + 1, 1 - slot)
            weights(i - t, slot)

        def two_steps(carry):
            tt, _ = carry
            step(2 * tt + 2, 0)
            step(2 * tt + 3, 1)
            return tt + 1, more()

        pairs, go = lax.while_loop(lambda c: jnp.logical_and(2 * c[0] + 3 <= i, c[1] > 0), two_steps, (0, more()))
        done = 1 + 2 * pairs
        one_more = jnp.logical_and(done + 1 == i, go > 0)

        @pl.when(one_more)
        def _():
            step(i, 0)
            values(0, 0)

        @pl.when(jnp.logical_not(one_more))
        def _():
            values(jnp.maximum(i - done, 0), 1)

        o = o_sc[0][...] + o_sc[1][...]
        o_ref[...] = o
        za = za_ref[...]
        y_ref[...] = (o * (za * _sigmoid(za))).astype(BF16)
        c_ref[0, 0] = cm_sc[0][...]
        c_ref[1, 0] = cm_sc[1][...]

    scratch = ([pltpu.VMEM((nk, LANE, tk), BF16), pltpu.VMEM((2, T, LANE), BF16)]
               + [pltpu.VMEM((tq, LANE), F32)] * 6 + [pltpu.VMEM((8, LANE), F32)] * 2
               + [pltpu.VMEM((2, tq, tk), F32)] * 2 + [pltpu.VMEM((2, tq, tk), BF16)] * 2)
    r_in, r_in_specs, r_out, r_out_specs, r_sems = _rider_specs(rider)
    body = _with_rider(body, 4, 3, len(scratch), rider, *_first_last_step(n_pairs, nq))
    return _pc(
        body, name="attn_fwd", grid=(n_pairs, nq),
        out_shape=tuple([jax.ShapeDtypeStruct((T, ATTN_W), F32), jax.ShapeDtypeStruct((T, ATTN_W), BF16),
                         jax.ShapeDtypeStruct((2 * n_pairs, nq, tq, LANE), F32)] + r_out),
        in_specs=[pl.BlockSpec((tq, LANE), lambda p, i: (i, p)),
                  pl.BlockSpec((T, LANE), lambda p, i: (0, n_pairs + p)),
                  pl.BlockSpec((T, LANE), lambda p, i: (0, 2 * n_pairs + p)),
                  pl.BlockSpec((tq, LANE), lambda p, i: (i, zb + p))] + r_in_specs,
        out_specs=tuple([pl.BlockSpec((tq, LANE), lambda p, i: (i, p)), pl.BlockSpec((tq, LANE), lambda p, i: (i, p)),
                         pl.BlockSpec((2, 1, tq, LANE), lambda p, i: (p, i, 0, 0))] + r_out_specs),
        scratch_shapes=scratch + r_sems,
        compiler_params=_params("arbitrary", "arbitrary"),
    )(projb, projb, projb, proj, *r_in)


def _gates(gl0, gl1, bg, D):
    return _sigmoid(gl0 + bg[:, :D]), _sigmoid(gl1 + bg[:, D:])


def _merge_fwd(y_pool, y_attn, w_pu, w_au, proj, b_gate, tm):
    T = y_pool.shape[0]
    D = w_pu.shape[1]
    gb = _proj_layout(D)["gates"] // D

    def body(yp_ref, ya_ref, wpu_ref, wau_ref, gl0_ref, gl1_ref, bg_ref, m_ref):
        p = jnp.dot(yp_ref[...], wpu_ref[...], preferred_element_type=F32)
        a = jnp.dot(ya_ref[...], wau_ref[...], preferred_element_type=F32)
        g0, g1 = _gates(gl0_ref[...], gl1_ref[...], bg_ref[...], D)
        m_ref[...] = (g0 * p + g1 * a).astype(BF16)

    row = lambda i: (i, 0)
    fixed = lambda i: (0, 0)
    return _pc(
        body, name="merge_fwd", grid=(T // tm,), out_shape=jax.ShapeDtypeStruct((T, D), BF16),
        in_specs=[pl.BlockSpec((tm, POOL_W), row), pl.BlockSpec((tm, ATTN_W), row),
                  pl.BlockSpec((POOL_W, D), fixed), pl.BlockSpec((ATTN_W, D), fixed),
                  pl.BlockSpec((tm, D), lambda i: (i, gb)), pl.BlockSpec((tm, D), lambda i: (i, gb + 1)),
                  pl.BlockSpec((1, 2 * D), fixed)],
        out_specs=pl.BlockSpec((tm, D), row),
        compiler_params=_params("parallel"),
    )(y_pool, y_attn, w_pu, w_au, proj, proj, b_gate)


def _final_loss(x, g, target, tm):
    T, D = x.shape

    def body(x_ref, g_ref, t_ref, dx_ref, dg_ref, loss_ref):
        @pl.when(pl.program_id(0) == 0)
        def _():
            dg_ref[...] = jnp.zeros_like(dg_ref)
            loss_ref[...] = jnp.zeros_like(loss_ref)

        xv, gv = x_ref[...], g_ref[...]
        r = lax.rsqrt(jnp.mean(xv * xv, axis=-1, keepdims=True) + RMS_EPS)
        xh = xv * r
        d = xh * gv - t_ref[...]
        loss_ref[...] += 0.5 * jnp.sum(jnp.mean(d * d, axis=-1, keepdims=True), axis=0, keepdims=True)
        dy = d * (1.0 / D)
        dg_ref[...] += jnp.sum(dy * xh, axis=0, keepdims=True)
        dh = dy * gv
        dx_ref[...] = r * (dh - xh * jnp.mean(dh * xh, axis=-1, keepdims=True))

    return _pc(
        body, name="final_loss", grid=(T // tm,),
        out_shape=(jax.ShapeDtypeStruct((T, D), F32), jax.ShapeDtypeStruct((1, D), F32),
                   jax.ShapeDtypeStruct((8, LANE), F32)),
        in_specs=[pl.BlockSpec((tm, D), lambda i: (i, 0)), pl.BlockSpec((1, D), lambda i: (0, 0)),
                  pl.BlockSpec((tm, D), lambda i: (i, 0))],
        out_specs=(pl.BlockSpec((tm, D), lambda i: (i, 0)), pl.BlockSpec((1, D), lambda i: (0, 0)),
                   pl.BlockSpec((8, LANE), lambda i: (0, 0))),
        compiler_params=_params("arbitrary"),
    )(x, g, target)


def _merge_bwd(dxo, w_out, y_pool, y_attn, w_pu, w_au, proj, b_gate, tm):
    T, D = dxo.shape
    gb = _proj_layout(D)["gates"] // D

    def body(dxo_ref, wout_ref, yp_ref, ya_ref, wpu_ref, wau_ref, gl0_ref, gl1_ref, bg_ref,
             dp_ref, da_ref, dgl_ref, dbg_ref):
        @pl.when(pl.program_id(0) == 0)
        def _():
            dbg_ref[...] = jnp.zeros_like(dbg_ref)

        dmv = lax.dot_general(dxo_ref[...].astype(BF16), wout_ref[...], NT_DIMS, preferred_element_type=F32)
        p = jnp.dot(yp_ref[...], wpu_ref[...], preferred_element_type=F32)
        a = jnp.dot(ya_ref[...], wau_ref[...], preferred_element_type=F32)
        g0, g1 = _gates(gl0_ref[...], gl1_ref[...], bg_ref[...], D)
        dp_ref[...] = (dmv * g0).astype(BF16)
        da_ref[...] = (dmv * g1).astype(BF16)
        dgl0 = dmv * p * (g0 * (1.0 - g0))
        dgl1 = dmv * a * (g1 * (1.0 - g1))
        dgl_ref[:, :D] = dgl0.astype(BF16)
        dgl_ref[:, D:] = dgl1.astype(BF16)
        dbg_ref[:, :D] += jnp.sum(dgl0, axis=0, keepdims=True)
        dbg_ref[:, D:] += jnp.sum(dgl1, axis=0, keepdims=True)

    row = lambda i: (i, 0)
    fixed = lambda i: (0, 0)
    return _pc(
        body, name="merge_bwd", grid=(T // tm,),
        out_shape=(jax.ShapeDtypeStruct((T, D), BF16), jax.ShapeDtypeStruct((T, D), BF16),
                   jax.ShapeDtypeStruct((T, 2 * D), BF16), jax.ShapeDtypeStruct((1, 2 * D), F32)),
        in_specs=[pl.BlockSpec((tm, D), row), pl.BlockSpec((D, D), fixed),
                  pl.BlockSpec((tm, POOL_W), row), pl.BlockSpec((tm, ATTN_W), row),
                  pl.BlockSpec((POOL_W, D), fixed), pl.BlockSpec((ATTN_W, D), fixed),
                  pl.BlockSpec((tm, D), lambda i: (i, gb)), pl.BlockSpec((tm, D), lambda i: (i, gb + 1)),
                  pl.BlockSpec((1, 2 * D), fixed)],
        out_specs=(pl.BlockSpec((tm, D), row), pl.BlockSpec((tm, D), row), pl.BlockSpec((tm, 2 * D), row),
                   pl.BlockSpec((1, 2 * D), fixed)),
        compiler_params=_params("arbitrary"),
    )(dxo, w_out, y_pool, y_attn, w_pu, w_au, proj, proj, b_gate)


def _pool_bwd(proj, dp, w_pu, pool_w, scale, R):
    T = proj.shape[0]
    D = w_pu.shape[1]
    nb = T // R
    hb = R // HALO

    def body(u_ref, up_ref, z_ref, dp_ref, wpu_ref, pw_ref, sc_ref, du_ref, dz_ref, dpw_ref, dsc_ref, halo_sc):
        i = pl.program_id(0)
        rb = nb - 1 - i

        @pl.when(i == 0)
        def _():
            halo_sc[...] = jnp.zeros_like(halo_sc)
            dpw_ref[...] = jnp.zeros_like(dpw_ref)
            dsc_ref[...] = jnp.zeros_like(dsc_ref)

        u = u_ref[...]
        row = rb * R + lax.broadcasted_iota(jnp.int32, (R, 1), 0)
        before = jnp.where(rb > 0, up_ref[...], 0.0)
        pooled = _pooled(u, before, row)
        pw = [pw_ref[g].astype(BF16) for g in range(len(POOL_WINDOWS))]
        mixed = jnp.concatenate(
            [jnp.dot(pooled[g].astype(BF16), pw[g], preferred_element_type=F32) for g in range(len(POOL_WINDOWS))],
            axis=1)
        sc = sc_ref[...]
        silu, dsilu = _silu_and_grad(z_ref[...])
        dyv = lax.dot_general(dp_ref[...], wpu_ref[...], NT_DIMS, preferred_element_type=F32)
        dmp = dyv * silu
        dz_ref[...] = (dyv * (mixed * sc) * dsilu).astype(BF16)
        dsc_ref[...] += jnp.sum(dmp * mixed, axis=0, keepdims=True)
        dmixed = (dmp * sc).astype(BF16)
        dpn = []
        dpooled = []
        for g, w in enumerate(POOL_WINDOWS):
            cols = slice(g * POOL_G, (g + 1) * POOL_G)
            dpw_ref[g] += lax.dot_general(pooled[g].astype(BF16), dmixed[:, cols], TN_DIMS,
                                          preferred_element_type=F32)
            dpg = lax.dot_general(dmixed[:, cols], pw[g], NT_DIMS, preferred_element_type=F32)
            dpooled.append(dpg)
            dpn.append(dpg / jnp.minimum(row + 1, w).astype(F32))
        dpn = jnp.concatenate(dpn, axis=1)
        sums = _window_sums(jnp.concatenate([dpn, halo_sc[...]], axis=0), False)
        du_ref[...] = jnp.concatenate(
            [sums[g][:R, g * POOL_G:(g + 1) * POOL_G] - dpooled[g] for g in range(len(POOL_WINDOWS))],
            axis=1).astype(BF16)
        halo_sc[...] = dpn[:HALO, :]

    rev = lambda i: (nb - 1 - i, 0)
    return _pc(
        body, name="pool_bwd", grid=(nb,),
        out_shape=(jax.ShapeDtypeStruct((T, POOL_W), BF16), jax.ShapeDtypeStruct((T, POOL_W), BF16),
                   jax.ShapeDtypeStruct((4, POOL_G, POOL_G), F32), jax.ShapeDtypeStruct((1, POOL_W), F32)),
        in_specs=[pl.BlockSpec((R, POOL_W), rev),
                  pl.BlockSpec((HALO, POOL_W), lambda i: (jnp.maximum((nb - 1 - i) * hb - 1, 0), 0)),
                  pl.BlockSpec((R, POOL_W), lambda i: (nb - 1 - i, 1)),
                  pl.BlockSpec((R, D), rev), pl.BlockSpec((POOL_W, D), lambda i: (0, 0)),
                  pl.BlockSpec((4, POOL_G, POOL_G), lambda i: (0, 0, 0)), pl.BlockSpec((1, POOL_W), lambda i: (0, 0))],
        out_specs=(pl.BlockSpec((R, POOL_W), rev), pl.BlockSpec((R, POOL_W), rev),
                   pl.BlockSpec((4, POOL_G, POOL_G), lambda i: (0, 0, 0)), pl.BlockSpec((1, POOL_W), lambda i: (0, 0))),
        scratch_shapes=[pltpu.VMEM((HALO, POOL_W), F32)],
        compiler_params=_params("arbitrary"),
    )(proj, proj, proj, dp, w_pu, pool_w, scale)


def _attn_gate_bwd(da, w_au, o, proj, tm):
    T, D = da.shape
    zb = (proj.shape[1] - ATTN_W) // ATTN_W

    def body(da_ref, wau_ref, o_ref, za_ref, do_ref, dza_ref):
        silu, dsilu = _silu_and_grad(za_ref[...])
        dyv = lax.dot_general(da_ref[...], wau_ref[...], NT_DIMS, preferred_element_type=F32)
        do_ref[...] = (dyv * silu).astype(BF16)
        dza_ref[...] = (dyv * o_ref[...] * dsilu).astype(BF16)

    row = lambda i: (i, 0)
    return _pc(
        body, name="attn_gate_bwd", grid=(T // tm,),
        out_shape=(jax.ShapeDtypeStruct((T, ATTN_W), BF16), jax.ShapeDtypeStruct((T, ATTN_W), BF16)),
        in_specs=[pl.BlockSpec((tm, D), row), pl.BlockSpec((ATTN_W, D), lambda i: (0, 0)),
                  pl.BlockSpec((tm, ATTN_W), row), pl.BlockSpec((tm, ATTN_W), lambda i: (i, zb))],
        out_specs=(pl.BlockSpec((tm, ATTN_W), row), pl.BlockSpec((tm, ATTN_W), row)),
        compiler_params=_params("parallel"),
    )(da, w_au, o, proj)


def _attn_bwd(projb, do, carries, tq, rider=None):
    T = projb.shape[0]
    nq = T // tq
    tk, nk = tq, nq
    n_pairs = ATTN_W // LANE
    scale = HEAD_DIM ** -0.5

    def body(q_ref, k_ref, v_ref, do_ref, c_ref, dq_ref, dk_ref, dv_ref, kT_sc, vT_sc, km_sc, dkT_ref, dvT_ref, *per_head):
        f_sc, dq_sc, kn_sc, l_sc, dw_sc, dl_sc, w_sc = (per_head[2 * n:2 * n + 2] for n in range(7))
        i = pl.program_id(1)

        @pl.when(i == 0)
        def _():
            _fill_blocks(k_ref, nk, tk, transposed_sc=kT_sc, masked_sc=km_sc, norm_sc=kn_sc)
            _fill_blocks(v_ref, nk, tk, transposed_sc=vT_sc)
            dkT_ref[...] = jnp.zeros_like(dkT_ref)
            dvT_ref[...] = jnp.zeros_like(dvT_ref)

        qs = _head_masks(q_ref[...], scale)
        dos = _head_masks(do_ref[...], 1.0)
        qT = [x.astype(F32).T.astype(BF16) for x in qs]
        doT = [x.astype(F32).T.astype(BF16) for x in dos]
        lane = lax.broadcasted_iota(jnp.int32, (1, LANE), 1)
        valid = lax.broadcasted_iota(jnp.int32, (tq, tk), 1) < lax.broadcasted_iota(jnp.int32, (tq, tk), 0)
        kk0 = lax.broadcasted_iota(jnp.int32, (tk, tk), 0)
        kk1 = lax.broadcasted_iota(jnp.int32, (tk, tk), 1)
        suffix = (kk0 >= kk1).astype(BF16)
        prefix = (kk0 <= kk1).astype(BF16)
        for a in range(2):
            f_sc[a][...] = jnp.zeros_like(f_sc[a])
            dq_sc[a][...] = jnp.zeros_like(dq_sc[a])
            dl_sc[a][1] = jnp.zeros((tq, tk), BF16)
            w_sc[a][1] = jnp.zeros((tq, tk), BF16)
        live = lane == i
        for a, qn in enumerate(_max_row_norm2(qs[0] + qs[1], _head_lanes())):
            l_max = jnp.sqrt(qn * kn_sc[a][0:1, :])
            live = jnp.logical_or(live, jnp.min(c_ref[a, 0], axis=0, keepdims=True) - l_max <= ZERO_WEIGHT)
        t0 = jnp.min(jnp.where(jnp.logical_and(live, lane <= i), lane, i))
        n = i - t0

        def products(j, slot):
            kT = kT_sc[j]
            vT = vT_sc[j]
            for a in range(2):
                l_sc[a][slot] = jnp.dot(qs[a], kT, preferred_element_type=F32)
                dw_sc[a][slot] = jnp.dot(dos[a], vT, preferred_element_type=F32)

        def gradients(j, slot):
            rows = pl.ds(pl.multiple_of(j * tk, tk), tk)
            dkT = []
            dvT = []
            for a in range(2):
                dlb = dl_sc[a][slot]
                dq_sc[a][...] += jnp.dot(dlb, km_sc[a, rows, :], preferred_element_type=F32)
                dkT.append(jnp.dot(qT[a], dlb, preferred_element_type=F32))
                dvT.append(jnp.dot(doT[a], w_sc[a][slot], preferred_element_type=F32))
            dkT_ref[j] += dkT[0] + dkT[1]
            dvT_ref[j] += dvT[0] + dvT[1]

        def elementwise(j, slot, masked):
            sp, inc, e, beta, p = [None] * 2, [None] * 2, [None] * 2, [None] * 2, [None] * 2
            for a in range(2):
                x = _softplus(l_sc[a][slot])
                sp[a] = jnp.where(valid, x, 0.0) if masked else x
            for a in range(2):
                inc[a] = jnp.dot(sp[a].astype(BF16), suffix, preferred_element_type=F32)
            for a in range(2):
                l = l_sc[a][slot]
                c = jnp.sum(jnp.where(lane == j, c_ref[a, 0], 0.0), axis=1, keepdims=True)
                w = jnp.exp(l - inc[a] - c)
                if masked:
                    w = jnp.where(valid, w, 0.0)
                w_sc[a][slot] = w.astype(BF16)
                beta[a] = jnp.exp(l - sp[a])
                e[a] = w * dw_sc[a][slot]
            for a in range(2):
                p[a] = jnp.dot(e[a].astype(BF16), prefix, preferred_element_type=F32)
            for a in range(2):
                f = f_sc[a][...]
                dl = e[a] - beta[a] * (p[a] + _wide(f, tk))
                if masked:
                    dl = jnp.where(valid, dl, 0.0)
                dl_sc[a][slot] = dl.astype(BF16)
                f_sc[a][...] = f + p[a][:, tk - 1:tk]

        def step(r, slot):
            products(t0 + r + 1, 1 - slot)
            gradients(t0 + jnp.maximum(r - 1, 0), 1 - slot)
            elementwise(t0 + r, slot, False)

        def last(slot):
            gradients(t0 + jnp.maximum(n - 1, 0), 1 - slot)
            elementwise(i, slot, True)
            gradients(i, slot)

        products(t0, 0)

        def two_steps(tt, carry):
            step(2 * tt, 0)
            step(2 * tt + 1, 1)
            return carry

        lax.fori_loop(0, n // 2, two_steps, 0)

        @pl.when(n % 2 == 1)
        def _():
            step(n - 1, 0)
            last(1)

        @pl.when(n % 2 == 0)
        def _():
            last(0)

        dq_ref[...] = ((dq_sc[0][...] + dq_sc[1][...]) * scale).astype(BF16)

        @pl.when(i == nq - 1)
        def _():
            def untranspose(j, carry):
                rows = pl.ds(pl.multiple_of(j * tk, tk), tk)
                dk_ref[rows, :] = dkT_ref[j].T.astype(BF16)
                dv_ref[rows, :] = dvT_ref[j].T.astype(BF16)
                return carry

            lax.fori_loop(0, nk, untranspose, 0)

    scratch = ([pltpu.VMEM((nk, LANE, tk), BF16), pltpu.VMEM((nk, LANE, tk), BF16), pltpu.VMEM((2, T, LANE), BF16),
                pltpu.VMEM((nk, LANE, tk), F32), pltpu.VMEM((nk, LANE, tk), F32)]
               + [pltpu.VMEM((tq, LANE), F32)] * 4 + [pltpu.VMEM((8, LANE), F32)] * 2
               + [pltpu.VMEM((2, tq, tk), F32)] * 4 + [pltpu.VMEM((2, tq, tk), BF16)] * 4)
    r_in, r_in_specs, r_out, r_out_specs, r_sems = _rider_specs(rider)
    body = _with_rider(body, 5, 3, len(scratch), rider, *_first_last_step(n_pairs, nq))
    return _pc(
        body, name="attn_bwd", grid=(n_pairs, nq),
        out_shape=tuple([jax.ShapeDtypeStruct((T, ATTN_W), BF16)] * 3 + r_out),
        in_specs=[pl.BlockSpec((tq, LANE), lambda p, i: (i, p)),
                  pl.BlockSpec((T, LANE), lambda p, i: (0, n_pairs + p)),
                  pl.BlockSpec((T, LANE), lambda p, i: (0, 2 * n_pairs + p)),
                  pl.BlockSpec((tq, LANE), lambda p, i: (i, p)),
                  pl.BlockSpec((2, 1, tq, LANE), lambda p, i: (p, i, 0, 0))] + r_in_specs,
        out_specs=tuple([pl.BlockSpec((tq, LANE), lambda p, i: (i, p)), pl.BlockSpec((T, LANE), lambda p, i: (0, p)),
                         pl.BlockSpec((T, LANE), lambda p, i: (0, p))] + r_out_specs),
        scratch_shapes=scratch + r_sems,
        compiler_params=_params("arbitrary", "arbitrary"),
    )(projb, projb, projb, do, carries, *r_in)


def _dh_norm_bwd(dproj, w, x, g, dxo, tm, rider=None):
    T, D = x.shape
    NW = w.shape[1]
    nm = T // tm

    def body(dp_ref, w_ref, x_ref, g_ref, dxo_ref, dx_ref, dg_ref):
        @pl.when(pl.program_id(0) == 0)
        def _():
            dg_ref[...] = jnp.zeros_like(dg_ref)

        dhv = lax.dot_general(dp_ref[...], w_ref[...], NT_DIMS, preferred_element_type=F32)
        xv = x_ref[...]
        r = lax.rsqrt(jnp.mean(xv * xv, axis=-1, keepdims=True) + RMS_EPS)
        xh = xv * r
        dg_ref[...] += jnp.sum(dhv * xh, axis=0, keepdims=True)
        dhg = dhv * g_ref[...]
        dx_ref[...] = dxo_ref[...] + r * (dhg - xh * jnp.mean(dhg * xh, axis=-1, keepdims=True))

    r_in, r_in_specs, r_out, r_out_specs, r_sems = _rider_specs(rider)
    body = _with_rider(body, 5, 2, 0, rider, lambda: pl.program_id(0) == 0, lambda: pl.program_id(0) == nm - 1)
    row = lambda i: (i, 0)
    fixed = lambda i: (0, 0)
    return _pc(
        body, name="d_h_norm_bwd", grid=(nm,),
        out_shape=tuple([jax.ShapeDtypeStruct((T, D), F32), jax.ShapeDtypeStruct((1, D), F32)] + r_out),
        in_specs=[pl.BlockSpec((tm, NW), row), pl.BlockSpec((D, NW), fixed), pl.BlockSpec((tm, D), row),
                  pl.BlockSpec((1, D), fixed), pl.BlockSpec((tm, D), row)] + r_in_specs,
        out_specs=tuple([pl.BlockSpec((tm, D), row), pl.BlockSpec((1, D), fixed)] + r_out_specs),
        scratch_shapes=r_sems,
        compiler_params=_params("arbitrary"),
    )(dproj, w, x, g, dxo, *r_in)


def _adamw(pieces, w, m, v, name):
    rows, cols = w.shape
    br = rows
    while br * cols > 65536 and br % 16 == 0:
        br //= 2
    c1 = 1.0 / (1.0 - ADAM_B1 ** ADAM_STEP)
    c2 = 1.0 / (1.0 - ADAM_B2 ** ADAM_STEP)

    def body(p_ref, w_ref, m_ref, v_ref, g_ref, d_ref, nm_ref, nv_ref):
        g = p_ref[0].astype(F32)
        for s in range(1, N_DEV):
            g = g + p_ref[s].astype(F32)
        nm = ADAM_B1 * m_ref[...] + (1.0 - ADAM_B1) * g
        nv = ADAM_B2 * v_ref[...] + (1.0 - ADAM_B2) * (g * g)
        g_ref[...] = g
        nm_ref[...] = nm
        nv_ref[...] = nv
        d_ref[...] = -ADAM_LR * ((nm * c1) / (jnp.sqrt(nv * c2) + ADAM_EPS) + ADAM_WD * w_ref[...])

    blk = pl.BlockSpec((br, cols), lambda i: (i, 0))
    shape = jax.ShapeDtypeStruct((rows, cols), F32)
    return _pc(
        body, name=name, grid=(rows // br,), out_shape=(shape, shape, shape, shape),
        in_specs=[pl.BlockSpec((N_DEV, br, cols), lambda i: (0, i, 0)), blk, blk, blk],
        out_specs=(blk, blk, blk, blk),
        compiler_params=_params("parallel"),
    )(pieces, w, m, v)


def _rows128(a):
    flat = a.reshape(-1)
    n = flat.shape[0]
    padded = -(-n // (8 * LANE)) * (8 * LANE)
    if padded != n:
        flat = jnp.concatenate([flat, jnp.zeros((padded - n,), flat.dtype)])
    return flat.reshape(-1, LANE)


def _pack(parts):
    return jnp.concatenate([_rows128(p) for p in parts], axis=0)


def _unpack(packed, like):
    out, r = [], 0
    for a in like:
        n = a.size
        nr = -(-n // (8 * LANE)) * 8
        out.append(packed[r:r + nr].reshape(-1)[:n].reshape(a.shape))
        r += nr
    return out


def kernel(x, norm_g, w_in, b_gate, pool_w, pool_scale, w_pool_up, w_attn_up, w_out, final_g, loss_target, m_norm_g, m_w_in, m_b_gate, m_pool_w, m_pool_scale, m_w_pool_up, m_w_attn_up, m_w_out, m_final_g, v_norm_g, v_w_in, v_b_gate, v_pool_w, v_pool_scale, v_w_pool_up, v_w_attn_up, v_w_out, v_final_g):
    L = norm_g.shape[0]
    T, D = x.shape[1], x.shape[2]
    NW = w_in.shape[2] * N_DEV
    assert NW == 2 * POOL_W + 4 * ATTN_W + 2 * D and x.shape[0] == 1
    tm = min(512, T)
    tq = min(256, T // 2)
    x0 = x.reshape(T, D)
    target = loss_target.reshape(T, D)

    assert L >= 2
    win_first = jnp.transpose(_exchange([w_in[0].astype(BF16)], [False], "gather_w_in0")[0], (1, 0, 2)).reshape(D, NW)
    rest = [w_in[1:].astype(BF16), w_pool_up.astype(BF16), w_attn_up.astype(BF16), w_out.astype(BF16)]

    saved = []
    xl = x0
    for l in range(L):
        proj, projb, h = _norm_inproj(xl, norm_g[l:l + 1], win_first if l == 0 else win_rest[l - 1], min(256, T))
        y_pool = _pool_fwd(proj, pool_w[l], pool_scale[l:l + 1], tm)
        if l == 0:
            o, y_attn, carries, g_in, g_pu, g_au, g_out = _attn_fwd(projb, proj, tq, rider=(rest, [False] * 4))
            win_rest = jnp.transpose(g_in, (1, 2, 0, 3)).reshape(L - 1, D, NW)
            wpu_full = jnp.transpose(g_pu, (1, 2, 0, 3)).reshape(L, POOL_W, D)
            wau_full = jnp.transpose(g_au, (1, 2, 0, 3)).reshape(L, ATTN_W, D)
            wout_full = jnp.transpose(g_out, (1, 0, 2, 3)).reshape(L, D, D)
        else:
            o, y_attn, carries = _attn_fwd(projb, proj, tq)
        merged = _merge_fwd(y_pool, y_attn, wpu_full[l], wau_full[l], proj, b_gate[l:l + 1], min(256, T))
        x_next = _mm_nn_res(merged, wout_full[l], xl, tm, "out_proj")
        saved.append((xl, proj, projb, h, y_pool, o, y_attn, carries, merged))
        xl = x_next

    dx, d_final_g, loss_part = _final_loss(xl, final_g.reshape(1, D), target, tm)

    d_norm_g, d_b_gate, d_pool_w, d_pool_scale = [None] * L, [None] * L, [None] * L, [None] * L
    d_win, d_wpu, d_wau, d_wout = [None] * L, [None] * L, [None] * L, [None] * L
    small_like = [norm_g, b_gate, pool_w, pool_scale, final_g, jnp.zeros((8, LANE), F32)]
    for l in reversed(range(L)):
        xin, proj, projb, h, y_pool, o, y_attn, carries, merged = saved[l]
        win_l = win_first if l == 0 else win_rest[l - 1]
        d_wout[l] = _mm_tn(merged, dx, 1, D, min(1024, T), "d_w_out").reshape(N_DEV, D // N_DEV, D)
        dp, da, dgl, d_b_gate[l] = _merge_bwd(dx, wout_full[l], y_pool, y_attn, wpu_full[l], wau_full[l], proj,
                                              b_gate[l:l + 1], min(256, T))
        d_wpu[l] = _mm_tn(y_pool, dp, N_DEV, D, min(1024, T), "d_w_pool_up")
        d_wau[l] = _mm_tn(y_attn, da, N_DEV, D, min(1024, T), "d_w_attn_up")
        du, dzp, d_pool_w[l], d_pool_scale[l] = _pool_bwd(proj, dp, wpu_full[l], pool_w[l], pool_scale[l:l + 1], tm)
        do, dza = _attn_gate_bwd(da, wau_full[l], o, proj, tm)
        if l == 0:
            small = _pack([jnp.concatenate([jnp.zeros((1, D), F32)] + d_norm_g[1:], 0), jnp.concatenate(d_b_gate, 0),
                           jnp.stack(d_pool_w, 0), jnp.concatenate(d_pool_scale, 0), d_final_g, loss_part])
            early = d_win[1:] + d_wpu + d_wau + d_wout
            dq, dk, dv, got_small, *got_early = _attn_bwd(
                projb, do, carries, tq, rider=([small] + early, [False] + [True] * len(early)))
        else:
            dq, dk, dv = _attn_bwd(projb, do, carries, tq)
        dproj = jnp.concatenate([du, dzp, dq, dk, dv, dza, dgl], axis=1)
        d_win[l] = _mm_tn(h, dproj, N_DEV, NW // 2, min(1024, T), "d_w_in")
        if l == 0:
            dx, d_norm_g[l], got_win0 = _dh_norm_bwd(dproj, win_l, xin, norm_g[l:l + 1], dx, min(256, T),
                                                     rider=([d_win[0]], [True]))
        else:
            dx, d_norm_g[l] = _dh_norm_bwd(dproj, win_l, xin, norm_g[l:l + 1], dx, min(256, T))

    got_norm0 = _exchange([d_norm_g[0].reshape(-1, LANE)], [False], "gather_norm_grad")[0]
    r_small = jnp.concatenate([got_norm0, got_small[:, D // LANE:]], axis=1)
    r_in = jnp.stack([got_win0] + got_early[:L - 1], axis=1)
    r_pu = jnp.stack(got_early[L - 1:2 * L - 1], axis=1)
    r_au = jnp.stack(got_early[2 * L - 1:3 * L - 1], axis=1)
    r_out = jnp.stack(got_early[3 * L - 1:4 * L - 1], axis=1)

    def update(pieces, w, m, v, name):
        cols = w.shape[-1]
        res = _adamw(pieces.reshape(N_DEV, -1, cols), w.reshape(-1, cols), m.reshape(-1, cols),
                     v.reshape(-1, cols), name)
        return [r.reshape(w.shape) for r in res]

    u_in = update(r_in, w_in, m_w_in, v_w_in, "adamw_w_in")
    u_pu = update(r_pu, w_pool_up, m_w_pool_up, v_w_pool_up, "adamw_w_pool_up")
    u_au = update(r_au, w_attn_up, m_w_attn_up, v_w_attn_up, "adamw_w_attn_up")
    u_out = update(r_out, w_out, m_w_out, v_w_out, "adamw_w_out")
    zeros = small_like[-1]
    smalls = _adamw(r_small,
                    _pack([norm_g, b_gate, pool_w, pool_scale, final_g, zeros]),
                    _pack([m_norm_g, m_b_gate, m_pool_w, m_pool_scale, m_final_g, zeros]),
                    _pack([v_norm_g, v_b_gate, v_pool_w, v_pool_scale, v_final_g, zeros]), "adamw_small")
    s_g, s_d, s_m, s_v = [_unpack(s, small_like) for s in smalls]
    loss = s_g[5][0, 0]

    def ordered(k):
        s = (s_g, s_d, s_m, s_v)[k]
        return [s[0], u_in[k], s[1], s[2], s[3], u_pu[k], u_au[k], u_out[k], s[4]]

    return (loss, dx.reshape(x.shape), *ordered(0), *ordered(1), *ordered(2), *ordered(3))
```

```python
import jax
import jax.numpy as jnp
from jax import lax
from jax.experimental import pallas as pl
from jax.experimental.pallas import tpu as pltpu

F32 = jnp.float32
BF16 = jnp.bfloat16

N_DEV = 8
HEAD_DIM = 64
ATTN_W = 512
POOL_W = 512
POOL_G = 128
POOL_WINDOWS = (2, 4, 8, 16)
HALO = 16
LANE = 128
RMS_EPS = 1e-6
ZERO_WEIGHT = 110.0
NO_CARRY = 3.0e38
ADAM_LR, ADAM_B1, ADAM_B2, ADAM_EPS, ADAM_WD, ADAM_STEP = 0.001, 0.9, 0.999, 1e-08, 0.01, 10
VMEM_LIMIT = 56 * 1024 * 1024

NT_DIMS = (((1,), (1,)), ((), ()))
TN_DIMS = (((0,), (0,)), ((), ()))


def _pc(body, **kw):
    return pl.pallas_call(body, **kw)


def _params(*sem):
    return pltpu.CompilerParams(dimension_semantics=sem, vmem_limit_bytes=VMEM_LIMIT)


def _sigmoid(z):
    return 1.0 / (1.0 + jnp.exp(-z))


def _silu_and_grad(z):
    s = _sigmoid(z)
    return z * s, s * (1.0 + z * (1.0 - s))


def _my_index():
    return 4 * lax.axis_index("x") + 2 * lax.axis_index("y") + lax.axis_index("c")


def _peer(k):
    x, y, c = lax.axis_index("x"), lax.axis_index("y"), lax.axis_index("c")
    px = lax.rem(x + ((k >> 2) & 1), 2)
    py = lax.rem(y + ((k >> 1) & 1), 2)
    pc = lax.rem(c + (k & 1), 2)
    return (px, py, pc), 4 * px + 2 * py + pc


def _exchange_copies(ins, outs, sems, scatter):
    send_sems, recv_sems, local_sems = sems
    n = len(ins)
    me = _my_index()

    def src(a, idx):
        return ins[a].at[idx] if scatter[a] else ins[a]

    local = [pltpu.make_async_copy(src(a, me), outs[a].at[me], local_sems.at[a]) for a in range(n)]
    sends, arrivals = [], []
    for k in (1, 2, 4, 3, 5, 6, 7):
        dev, pidx = _peer(k)
        for a in range(n):
            sem = dict(send_sem=send_sems.at[a * N_DEV + k], recv_sem=recv_sems.at[a * N_DEV + k],
                       device_id=dev, device_id_type=pl.DeviceIdType.MESH)
            sends.append(pltpu.make_async_remote_copy(src_ref=src(a, pidx), dst_ref=outs[a].at[me], **sem))
            arrivals.append(pltpu.make_async_remote_copy(src_ref=src(a, pidx), dst_ref=outs[a].at[pidx], **sem))
    return local, sends, arrivals


def _exchange_start(ins, outs, sems, scatter):
    local, sends, _ = _exchange_copies(ins, outs, sems, scatter)
    for cp in local + sends:
        cp.start()


def _exchange_wait(ins, outs, sems, scatter):
    local, sends, arrivals = _exchange_copies(ins, outs, sems, scatter)
    for cp in arrivals:
        cp.wait_recv()
    for cp in sends:
        cp.wait_send()
    for cp in local:
        cp.wait()


def _exchange_shapes(arrays, scatter):
    n = len(arrays)
    out_shape = [jax.ShapeDtypeStruct((N_DEV,) + tuple(a.shape[1:] if s else a.shape), a.dtype)
                 for a, s in zip(arrays, scatter)]
    sems = [pltpu.SemaphoreType.DMA((n * N_DEV,)), pltpu.SemaphoreType.DMA((n * N_DEV,)),
            pltpu.SemaphoreType.DMA((n,))]
    return out_shape, sems


def _exchange(arrays, scatter, name):
    n = len(arrays)

    def body(*refs):
        ins, outs, sems = refs[:n], refs[n:2 * n], refs[2 * n:]
        _exchange_start(ins, outs, sems, scatter)
        _exchange_wait(ins, outs, sems, scatter)

    out_shape, sems = _exchange_shapes(arrays, scatter)
    any_spec = pl.BlockSpec(memory_space=pl.ANY)
    return _pc(
        body, name=name, out_shape=tuple(out_shape),
        in_specs=[any_spec] * n, out_specs=tuple([any_spec] * n), scratch_shapes=sems,
    )(*arrays)


def _with_rider(body, n_in, n_out, n_scratch, rider, first, last):
    if rider is None:
        return body
    arrays, scatter = rider
    n = len(arrays)

    def wrapped(*refs):
        ins, r_ins = refs[:n_in], refs[n_in:n_in + n]
        outs = refs[n_in + n:n_in + n + n_out]
        r_outs = refs[n_in + n + n_out:n_in + 2 * n + n_out]
        scratch = refs[n_in + 2 * n + n_out:n_in + 2 * n + n_out + n_scratch]
        sems = refs[n_in + 2 * n + n_out + n_scratch:]

        @pl.when(first())
        def _():
            _exchange_start(r_ins, r_outs, sems, scatter)

        body(*ins, *outs, *scratch)

        @pl.when(last())
        def _():
            _exchange_wait(r_ins, r_outs, sems, scatter)

    return wrapped


def _rider_specs(rider):
    if rider is None:
        return [], [], [], [], []
    arrays, scatter = rider
    out_shape, sems = _exchange_shapes(arrays, scatter)
    any_spec = pl.BlockSpec(memory_space=pl.ANY)
    return list(arrays), [any_spec] * len(arrays), out_shape, [any_spec] * len(arrays), sems


def _mm_nn_res(a, b, res, tm, name):
    T, K = a.shape
    N = b.shape[1]

    def body(a_ref, b_ref, r_ref, o_ref):
        o_ref[...] = r_ref[...] + jnp.dot(a_ref[...], b_ref[...], preferred_element_type=F32)

    return _pc(
        body, name=name, grid=(T // tm,), out_shape=jax.ShapeDtypeStruct((T, N), F32),
        in_specs=[pl.BlockSpec((tm, K), lambda i: (i, 0)), pl.BlockSpec((K, N), lambda i: (0, 0)),
                  pl.BlockSpec((tm, N), lambda i: (i, 0))],
        out_specs=pl.BlockSpec((tm, N), lambda i: (i, 0)),
        compiler_params=_params("parallel"),
    )(a, b, res)


def _mm_tn(a, b, n_col_shards, tn, tk, name):
    T, M = a.shape
    N = b.shape[1]
    sw = N // n_col_shards
    per_step = tn // sw
    nk = T // tk

    def body(a_ref, b_ref, o_ref, acc_sc):
        k = pl.program_id(1)
        part = lax.dot_general(a_ref[...].astype(BF16), b_ref[...].astype(BF16), TN_DIMS,
                               preferred_element_type=F32)

        @pl.when(k == 0)
        def _():
            acc_sc[...] = part

        @pl.when(k > 0)
        def _():
            acc_sc[...] += part

        @pl.when(k == nk - 1)
        def _():
            for s in range(per_step):
                o_ref[s] = acc_sc[:, s * sw:(s + 1) * sw].astype(BF16)

    return _pc(
        body, name=name, grid=(N // tn, nk),
        out_shape=jax.ShapeDtypeStruct((n_col_shards, M, sw), BF16),
        in_specs=[pl.BlockSpec((tk, M), lambda j, k: (k, 0)), pl.BlockSpec((tk, tn), lambda j, k: (k, j))],
        out_specs=pl.BlockSpec((per_step, M, sw), lambda j, k: (j, 0, 0)),
        scratch_shapes=[pltpu.VMEM((M, tn), F32)],
        compiler_params=_params("parallel", "arbitrary"),
    )(a, b)


def _proj_layout(D):
    return {"u": 0, "z_pool": POOL_W, "gates": 2 * POOL_W, "z_attn": 2 * POOL_W + 2 * D, "width": 2 * POOL_W + 2 * D + ATTN_W}


def _norm_inproj(x, g, w, tm):
    T, D = x.shape
    NW = w.shape[1]
    lay = _proj_layout(D)
    qkv0, za0, gl0 = 2 * POOL_W, 2 * POOL_W + 3 * ATTN_W, 2 * POOL_W + 4 * ATTN_W

    def body(x_ref, g_ref, w_ref, proj_ref, qkv_ref, h_ref):
        xv = x_ref[...]
        r = lax.rsqrt(jnp.mean(xv * xv, axis=-1, keepdims=True) + RMS_EPS)
        h = ((xv * r) * g_ref[...]).astype(BF16)
        h_ref[...] = h

        def cols(lo, hi):
            return jnp.dot(h, w_ref[:, lo:hi], preferred_element_type=F32)

        proj_ref[:, :lay["gates"]] = cols(0, qkv0)
        qkv_ref[...] = cols(qkv0, za0).astype(BF16)
        proj_ref[:, lay["gates"]:lay["z_attn"]] = cols(gl0, NW)
        proj_ref[:, lay["z_attn"]:] = cols(za0, gl0)

    return _pc(
        body, name="norm_inproj", grid=(T // tm,),
        out_shape=(jax.ShapeDtypeStruct((T, lay["width"]), F32), jax.ShapeDtypeStruct((T, 3 * ATTN_W), BF16),
                   jax.ShapeDtypeStruct((T, D), BF16)),
        in_specs=[pl.BlockSpec((tm, D), lambda i: (i, 0)), pl.BlockSpec((1, D), lambda i: (0, 0)),
                  pl.BlockSpec((D, NW), lambda i: (0, 0))],
        out_specs=(pl.BlockSpec((tm, lay["width"]), lambda i: (i, 0)), pl.BlockSpec((tm, 3 * ATTN_W), lambda i: (i, 0)),
                   pl.BlockSpec((tm, D), lambda i: (i, 0))),
        compiler_params=_params("parallel"),
    )(x, g, w)


def _window_sums(xh, forward):
    n = xh.shape[0]
    sums, s, step = [], xh, 1
    for _ in POOL_WINDOWS:
        s = s + pltpu.roll(s, step if forward else n - step, 0)
        sums.append(s)
        step *= 2
    return sums


def _pooled(u, halo, row):
    sums = _window_sums(jnp.concatenate([halo, u], axis=0), True)
    out = []
    for g, w in enumerate(POOL_WINDOWS):
        cols = slice(g * POOL_G, (g + 1) * POOL_G)
        cnt = jnp.minimum(row + 1, w).astype(F32)
        out.append(sums[g][HALO:, cols] / cnt - u[:, cols])
    return out


def _pool_fwd(proj, pool_w, scale, R):
    T = proj.shape[0]

    def body(u_ref, z_ref, pw_ref, sc_ref, y_ref, halo_sc):
        i = pl.program_id(0)

        @pl.when(i == 0)
        def _():
            halo_sc[...] = jnp.zeros_like(halo_sc)

        u = u_ref[...]
        row = i * R + lax.broadcasted_iota(jnp.int32, (R, 1), 0)
        pooled = _pooled(u, halo_sc[...], row)
        mixed = jnp.concatenate(
            [jnp.dot(pooled[g].astype(BF16), pw_ref[g].astype(BF16), preferred_element_type=F32)
             for g in range(len(POOL_WINDOWS))], axis=1)
        z = z_ref[...]
        y_ref[...] = ((mixed * sc_ref[...]) * (z * _sigmoid(z))).astype(BF16)
        halo_sc[...] = u[R - HALO:, :]

    return _pc(
        body, name="pool_fwd", grid=(T // R,), out_shape=jax.ShapeDtypeStruct((T, POOL_W), BF16),
        in_specs=[pl.BlockSpec((R, POOL_W), lambda i: (i, 0)), pl.BlockSpec((R, POOL_W), lambda i: (i, 1)),
                  pl.BlockSpec((4, POOL_G, POOL_G), lambda i: (0, 0, 0)), pl.BlockSpec((1, POOL_W), lambda i: (0, 0))],
        out_specs=pl.BlockSpec((R, POOL_W), lambda i: (i, 0)),
        scratch_shapes=[pltpu.VMEM((HALO, POOL_W), F32)],
        compiler_params=_params("arbitrary"),
    )(proj, proj, pool_w, scale)


def _softplus(l):
    return jnp.maximum(l, 0.0) + jnp.log(1.0 + jnp.exp(-jnp.abs(l)))


def _first_last_step(n0, n1):
    return (lambda: jnp.logical_and(pl.program_id(0) == 0, pl.program_id(1) == 0),
            lambda: jnp.logical_and(pl.program_id(0) == n0 - 1, pl.program_id(1) == n1 - 1))


def _head_lanes():
    lane = lax.broadcasted_iota(jnp.int32, (1, LANE), 1)
    return [lane < HEAD_DIM, lane >= HEAD_DIM]


def _head_masks(q, scale):
    qf = q.astype(F32) * scale
    return [jnp.where(m, qf, 0.0).astype(BF16) for m in _head_lanes()]


def _wide(c, width):
    return jnp.concatenate([c] * (width // LANE), axis=1)


def _max_row_norm2(x, heads):
    sq = x.astype(F32) * x.astype(F32)
    return [jnp.max(jnp.sum(jnp.where(m, sq, 0.0), axis=1, keepdims=True), axis=0, keepdims=True) for m in heads]


def _fill_blocks(src_ref, nk, tk, transposed_sc=None, masked_sc=None, norm_sc=None):
    heads = _head_lanes()
    if norm_sc is not None:
        for a in range(2):
            norm_sc[a][...] = jnp.zeros_like(norm_sc[a])

    def step(j, carry):
        rows = pl.ds(pl.multiple_of(j * tk, tk), tk)
        blk = src_ref[rows, :]
        if norm_sc is not None:
            for a, n2 in enumerate(_max_row_norm2(blk, heads)):
                norm_sc[a][...] = jnp.maximum(norm_sc[a][...], n2)
        if transposed_sc is not None:
            transposed_sc[j] = blk.astype(F32).T.astype(BF16)
        if masked_sc is not None:
            for a in range(2):
                masked_sc[a, rows, :] = jnp.where(heads[a], blk, jnp.zeros_like(blk))
        return carry

    lax.fori_loop(0, nk, step, 0)


def _attn_fwd(projb, proj, tq, rider=None):
    T = projb.shape[0]
    nq = T // tq
    tk, nk = tq, nq
    assert nk <= LANE
    n_pairs = ATTN_W // LANE
    zb = (proj.shape[1] - ATTN_W) // LANE

    def body(q_ref, k_ref, v_ref, za_ref, o_ref, y_ref, c_ref, ws_ref, bs_ref, kT_sc, vm_sc, *per_head):
        c_sc, cm_sc, o_sc, kn_sc, l_sc, w_sc = (per_head[2 * n:2 * n + 2] for n in range(6))
        i = pl.program_id(1)

        @pl.when(i == 0)
        def _():
            _fill_blocks(k_ref, nk, tk, transposed_sc=kT_sc, norm_sc=kn_sc)
            _fill_blocks(v_ref, nk, tk, masked_sc=vm_sc)

        qs = _head_masks(q_ref[...], HEAD_DIM ** -0.5)
        lane = lax.broadcasted_iota(jnp.int32, (1, LANE), 1)
        valid = lax.broadcasted_iota(jnp.int32, (tq, tk), 1) < lax.broadcasted_iota(jnp.int32, (tq, tk), 0)
        suffix = (lax.broadcasted_iota(jnp.int32, (tk, tk), 0) >= lax.broadcasted_iota(jnp.int32, (tk, tk), 1)).astype(BF16)
        for a in range(2):
            c_sc[a][...] = jnp.zeros_like(c_sc[a])
            o_sc[a][...] = jnp.zeros_like(o_sc[a])
            cm_sc[a][...] = jnp.full(cm_sc[a].shape, NO_CARRY, F32)
        l_max = [jnp.sqrt(qn * kn_sc[a][...]) for a, qn in enumerate(_max_row_norm2(qs[0] + qs[1], _head_lanes()))]

        def logits(j, slot):
            kT = kT_sc[j]
            for a in range(2):
                l_sc[a][slot] = jnp.dot(qs[a], kT, preferred_element_type=F32)

        def values(j, slot):
            rows = pl.ds(pl.multiple_of(j * tk, tk), tk)
            for a in range(2):
                o_sc[a][...] += jnp.dot(w_sc[a][slot], vm_sc[a, rows, :], preferred_element_type=F32)

        def softplus(slot, masked):
            out = []
            for a in range(2):
                x = _softplus(l_sc[a][slot])
                out.append(jnp.where(valid, x, 0.0) if masked else x)
            return out

        def finish_weights(j, slot, sp, inc, mask, keep=None, cols=None):
            for a in range(2):
                c = c_sc[a][...]
                l = l_sc[a][slot]
                w = jnp.exp(l - inc[a] - _wide(c, tk))
                if mask is not None:
                    w = jnp.where(mask, w, 0.0)
                add = inc[a][:, 0:1]
                if keep is not None:
                    w = jnp.where(keep, w, 0.0)
                    add = jnp.where(keep, add, 0.0)
                w_sc[a][slot] = w.astype(BF16)
                if cols is not None:
                    beta = jnp.exp(l - sp[a])
                    ws_ref[a, 0, :, cols] = w
                    bs_ref[a, 0, :, cols] = beta if mask is None else jnp.where(mask, beta, 0.0)
                cm_sc[a][...] = jnp.where(lane == j, c, cm_sc[a][...])
                c_sc[a][...] = c + add

        def weights(j, slot):
            sp = softplus(slot, False)
            inc = [jnp.dot(sp[a].astype(BF16), suffix, preferred_element_type=F32) for a in range(2)]
            finish_weights(j, slot, sp, inc, None)

        def more():
            live = [jnp.min(c_sc[a][...], axis=0, keepdims=True) - l_max[a][0:1, :] <= ZERO_WEIGHT for a in range(2)]
            return jnp.max(jnp.where(jnp.logical_or(live[0], live[1]), 1, 0))

        logits(i, 0)
        logits(jnp.maximum(i - 1, 0), 1)
        sp = softplus(0, True) + softplus(1, False)
        inc = [jnp.dot(x.astype(BF16), suffix, preferred_element_type=F32) for x in sp]
        finish_weights(i, 0, sp[:2], inc[:2], valid, cols=slice(tk, 2 * tk))
        finish_weights(i - 1, 1, sp[2:], inc[2:], None, keep=i > 0, cols=slice(0, tk))
        values(i, 0)
        logits(jnp.maximum(i - 2, 0), 0)

        def step(t, slot):
            logits(jnp.maximum(i - t - 1, 0), 1 - slot)
            values(i - t + 1, 1 - slot)
            weights(i - t, slot)

        def two_steps(carry):
            tt, _ = carry
            step(2 * tt + 2, 0)
            step(2 * tt + 3, 1)
            return tt + 1, more()

        pairs, go = lax.while_loop(lambda c: jnp.logical_and(2 * c[0] + 3 <= i, c[1] > 0), two_steps, (0, more()))
        done = 1 + 2 * pairs
        one_more = jnp.logical_and(done + 1 == i, go > 0)

        @pl.when(one_more)
        def _():
            step(i, 0)
            values(0, 0)

        @pl.when(jnp.logical_not(one_more))
        def _():
            values(jnp.maximum(i - done, 0), 1)

        o = o_sc[0][...] + o_sc[1][...]
        o_ref[...] = o
        za = za_ref[...]
        y_ref[...] = (o * (za * _sigmoid(za))).astype(BF16)
        c_ref[0, 0] = cm_sc[0][...]
        c_ref[1, 0] = cm_sc[1][...]

    scratch = ([pltpu.VMEM((nk, LANE, tk), BF16), pltpu.VMEM((2, T, LANE), BF16)]
               + [pltpu.VMEM((tq, LANE), F32)] * 6 + [pltpu.VMEM((8, LANE), F32)] * 2
               + [pltpu.VMEM((2, tq, tk), F32)] * 2 + [pltpu.VMEM((2, tq, tk), BF16)] * 2)
    r_in, r_in_specs, r_out, r_out_specs, r_sems = _rider_specs(rider)
    body = _with_rider(body, 4, 5, len(scratch), rider, *_first_last_step(n_pairs, nq))
    kept = jax.ShapeDtypeStruct((2 * n_pairs, nq, tq, 2 * tk), F32)
    kept_spec = pl.BlockSpec((2, 1, tq, 2 * tk), lambda p, i: (p, i, 0, 0))
    return _pc(
        body, name="attn_fwd", grid=(n_pairs, nq),
        out_shape=tuple([jax.ShapeDtypeStruct((T, ATTN_W), F32), jax.ShapeDtypeStruct((T, ATTN_W), BF16),
                         jax.ShapeDtypeStruct((2 * n_pairs, nq, tq, LANE), F32), kept, kept] + r_out),
        in_specs=[pl.BlockSpec((tq, LANE), lambda p, i: (i, p)),
                  pl.BlockSpec((T, LANE), lambda p, i: (0, n_pairs + p)),
                  pl.BlockSpec((T, LANE), lambda p, i: (0, 2 * n_pairs + p)),
                  pl.BlockSpec((tq, LANE), lambda p, i: (i, zb + p))] + r_in_specs,
        out_specs=tuple([pl.BlockSpec((tq, LANE), lambda p, i: (i, p)), pl.BlockSpec((tq, LANE), lambda p, i: (i, p)),
                         pl.BlockSpec((2, 1, tq, LANE), lambda p, i: (p, i, 0, 0)), kept_spec, kept_spec] + r_out_specs),
        scratch_shapes=scratch + r_sems,
        compiler_params=_params("arbitrary", "arbitrary"),
    )(projb, projb, projb, proj, *r_in)


def _gates(gl0, gl1, bg, D):
    return _sigmoid(gl0 + bg[:, :D]), _sigmoid(gl1 + bg[:, D:])


def _merge_fwd(y_pool, y_attn, w_pu, w_au, proj, b_gate, tm):
    T = y_pool.shape[0]
    D = w_pu.shape[1]
    gb = _proj_layout(D)["gates"] // D

    def body(yp_ref, ya_ref, wpu_ref, wau_ref, gl0_ref, gl1_ref, bg_ref, m_ref):
        p = jnp.dot(yp_ref[...], wpu_ref[...], preferred_element_type=F32)
        a = jnp.dot(ya_ref[...], wau_ref[...], preferred_element_type=F32)
        g0, g1 = _gates(gl0_ref[...], gl1_ref[...], bg_ref[...], D)
        m_ref[...] = (g0 * p + g1 * a).astype(BF16)

    row = lambda i: (i, 0)
    fixed = lambda i: (0, 0)
    return _pc(
        body, name="merge_fwd", grid=(T // tm,), out_shape=jax.ShapeDtypeStruct((T, D), BF16),
        in_specs=[pl.BlockSpec((tm, POOL_W), row), pl.BlockSpec((tm, ATTN_W), row),
                  pl.BlockSpec((POOL_W, D), fixed), pl.BlockSpec((ATTN_W, D), fixed),
                  pl.BlockSpec((tm, D), lambda i: (i, gb)), pl.BlockSpec((tm, D), lambda i: (i, gb + 1)),
                  pl.BlockSpec((1, 2 * D), fixed)],
        out_specs=pl.BlockSpec((tm, D), row),
        compiler_params=_params("parallel"),
    )(y_pool, y_attn, w_pu, w_au, proj, proj, b_gate)


def _final_loss(x, g, target, tm):
    T, D = x.shape

    def body(x_ref, g_ref, t_ref, dx_ref, dg_ref, loss_ref):
        @pl.when(pl.program_id(0) == 0)
        def _():
            dg_ref[...] = jnp.zeros_like(dg_ref)
            loss_ref[...] = jnp.zeros_like(loss_ref)

        xv, gv = x_ref[...], g_ref[...]
        r = lax.rsqrt(jnp.mean(xv * xv, axis=-1, keepdims=True) + RMS_EPS)
        xh = xv * r
        d = xh * gv - t_ref[...]
        loss_ref[...] += 0.5 * jnp.sum(jnp.mean(d * d, axis=-1, keepdims=True), axis=0, keepdims=True)
        dy = d * (1.0 / D)
        dg_ref[...] += jnp.sum(dy * xh, axis=0, keepdims=True)
        dh = dy * gv
        dx_ref[...] = r * (dh - xh * jnp.mean(dh * xh, axis=-1, keepdims=True))

    return _pc(
        body, name="final_loss", grid=(T // tm,),
        out_shape=(jax.ShapeDtypeStruct((T, D), F32), jax.ShapeDtypeStruct((1, D), F32),
                   jax.ShapeDtypeStruct((8, LANE), F32)),
        in_specs=[pl.BlockSpec((tm, D), lambda i: (i, 0)), pl.BlockSpec((1, D), lambda i: (0, 0)),
                  pl.BlockSpec((tm, D), lambda i: (i, 0))],
        out_specs=(pl.BlockSpec((tm, D), lambda i: (i, 0)), pl.BlockSpec((1, D), lambda i: (0, 0)),
                   pl.BlockSpec((8, LANE), lambda i: (0, 0))),
        compiler_params=_params("arbitrary"),
    )(x, g, target)


def _merge_bwd(dxo, w_out, y_pool, y_attn, w_pu, w_au, proj, b_gate, tm):
    T, D = dxo.shape
    gb = _proj_layout(D)["gates"] // D

    def body(dxo_ref, wout_ref, yp_ref, ya_ref, wpu_ref, wau_ref, gl0_ref, gl1_ref, bg_ref,
             dp_ref, da_ref, dgl_ref, dbg_ref):
        @pl.when(pl.program_id(0) == 0)
        def _():
            dbg_ref[...] = jnp.zeros_like(dbg_ref)

        dmv = lax.dot_general(dxo_ref[...].astype(BF16), wout_ref[...], NT_DIMS, preferred_element_type=F32)
        p = jnp.dot(yp_ref[...], wpu_ref[...], preferred_element_type=F32)
        a = jnp.dot(ya_ref[...], wau_ref[...], preferred_element_type=F32)
        g0, g1 = _gates(gl0_ref[...], gl1_ref[...], bg_ref[...], D)
        dp_ref[...] = (dmv * g0).astype(BF16)
        da_ref[...] = (dmv * g1).astype(BF16)
        dgl0 = dmv * p * (g0 * (1.0 - g0))
        dgl1 = dmv * a * (g1 * (1.0 - g1))
        dgl_ref[:, :D] = dgl0.astype(BF16)
        dgl_ref[:, D:] = dgl1.astype(BF16)
        dbg_ref[:, :D] += jnp.sum(dgl0, axis=0, keepdims=True)
        dbg_ref[:, D:] += jnp.sum(dgl1, axis=0, keepdims=True)

    row = lambda i: (i, 0)
    fixed = lambda i: (0, 0)
    return _pc(
        body, name="merge_bwd", grid=(T // tm,),
        out_shape=(jax.ShapeDtypeStruct((T, D), BF16), jax.ShapeDtypeStruct((T, D), BF16),
                   jax.ShapeDtypeStruct((T, 2 * D), BF16), jax.ShapeDtypeStruct((1, 2 * D), F32)),
        in_specs=[pl.BlockSpec((tm, D), row), pl.BlockSpec((D, D), fixed),
                  pl.BlockSpec((tm, POOL_W), row), pl.BlockSpec((tm, ATTN_W), row),
                  pl.BlockSpec((POOL_W, D), fixed), pl.BlockSpec((ATTN_W, D), fixed),
                  pl.BlockSpec((tm, D), lambda i: (i, gb)), pl.BlockSpec((tm, D), lambda i: (i, gb + 1)),
                  pl.BlockSpec((1, 2 * D), fixed)],
        out_specs=(pl.BlockSpec((tm, D), row), pl.BlockSpec((tm, D), row), pl.BlockSpec((tm, 2 * D), row),
                   pl.BlockSpec((1, 2 * D), fixed)),
        compiler_params=_params("arbitrary"),
    )(dxo, w_out, y_pool, y_attn, w_pu, w_au, proj, proj, b_gate)


def _pool_bwd(proj, dp, w_pu, pool_w, scale, R):
    T = proj.shape[0]
    D = w_pu.shape[1]
    nb = T // R
    hb = R // HALO

    def body(u_ref, up_ref, z_ref, dp_ref, wpu_ref, pw_ref, sc_ref, du_ref, dz_ref, dpw_ref, dsc_ref, halo_sc):
        i = pl.program_id(0)
        rb = nb - 1 - i

        @pl.when(i == 0)
        def _():
            halo_sc[...] = jnp.zeros_like(halo_sc)
            dpw_ref[...] = jnp.zeros_like(dpw_ref)
            dsc_ref[...] = jnp.zeros_like(dsc_ref)

        u = u_ref[...]
        row = rb * R + lax.broadcasted_iota(jnp.int32, (R, 1), 0)
        before = jnp.where(rb > 0, up_ref[...], 0.0)
        pooled = _pooled(u, before, row)
        pw = [pw_ref[g].astype(BF16) for g in range(len(POOL_WINDOWS))]
        mixed = jnp.concatenate(
            [jnp.dot(pooled[g].astype(BF16), pw[g], preferred_element_type=F32) for g in range(len(POOL_WINDOWS))],
            axis=1)
        sc = sc_ref[...]
        silu, dsilu = _silu_and_grad(z_ref[...])
        dyv = lax.dot_general(dp_ref[...], wpu_ref[...], NT_DIMS, preferred_element_type=F32)
        dmp = dyv * silu
        dz_ref[...] = (dyv * (mixed * sc) * dsilu).astype(BF16)
        dsc_ref[...] += jnp.sum(dmp * mixed, axis=0, keepdims=True)
        dmixed = (dmp * sc).astype(BF16)
        dpn = []
        dpooled = []
        for g, w in enumerate(POOL_WINDOWS):
            cols = slice(g * POOL_G, (g + 1) * POOL_G)
            dpw_ref[g] += lax.dot_general(pooled[g].astype(BF16), dmixed[:, cols], TN_DIMS,
                                          preferred_element_type=F32)
            dpg = lax.dot_general(dmixed[:, cols], pw[g], NT_DIMS, preferred_element_type=F32)
            dpooled.append(dpg)
            dpn.append(dpg / jnp.minimum(row + 1, w).astype(F32))
        dpn = jnp.concatenate(dpn, axis=1)
        sums = _window_sums(jnp.concatenate([dpn, halo_sc[...]], axis=0), False)
        du_ref[...] = jnp.concatenate(
            [sums[g][:R, g * POOL_G:(g + 1) * POOL_G] - dpooled[g] for g in range(len(POOL_WINDOWS))],
            axis=1).astype(BF16)
        halo_sc[...] = dpn[:HALO, :]

    rev = lambda i: (nb - 1 - i, 0)
    return _pc(
        body, name="pool_bwd", grid=(nb,),
        out_shape=(jax.ShapeDtypeStruct((T, POOL_W), BF16), jax.ShapeDtypeStruct((T, POOL_W), BF16),
                   jax.ShapeDtypeStruct((4, POOL_G, POOL_G), F32), jax.ShapeDtypeStruct((1, POOL_W), F32)),
        in_specs=[pl.BlockSpec((R, POOL_W), rev),
                  pl.BlockSpec((HALO, POOL_W), lambda i: (jnp.maximum((nb - 1 - i) * hb - 1, 0), 0)),
                  pl.BlockSpec((R, POOL_W), lambda i: (nb - 1 - i, 1)),
                  pl.BlockSpec((R, D), rev), pl.BlockSpec((POOL_W, D), lambda i: (0, 0)),
                  pl.BlockSpec((4, POOL_G, POOL_G), lambda i: (0, 0, 0)), pl.BlockSpec((1, POOL_W), lambda i: (0, 0))],
        out_specs=(pl.BlockSpec((R, POOL_W), rev), pl.BlockSpec((R, POOL_W), rev),
                   pl.BlockSpec((4, POOL_G, POOL_G), lambda i: (0, 0, 0)), pl.BlockSpec((1, POOL_W), lambda i: (0, 0))),
        scratch_shapes=[pltpu.VMEM((HALO, POOL_W), F32)],
        compiler_params=_params("arbitrary"),
    )(proj, proj, proj, dp, w_pu, pool_w, scale)


def _attn_gate_bwd(da, w_au, o, proj, tm):
    T, D = da.shape
    zb = (proj.shape[1] - ATTN_W) // ATTN_W

    def body(da_ref, wau_ref, o_ref, za_ref, do_ref, dza_ref):
        silu, dsilu = _silu_and_grad(za_ref[...])
        dyv = lax.dot_general(da_ref[...], wau_ref[...], NT_DIMS, preferred_element_type=F32)
        do_ref[...] = (dyv * silu).astype(BF16)
        dza_ref[...] = (dyv * o_ref[...] * dsilu).astype(BF16)

    row = lambda i: (i, 0)
    return _pc(
        body, name="attn_gate_bwd", grid=(T // tm,),
        out_shape=(jax.ShapeDtypeStruct((T, ATTN_W), BF16), jax.ShapeDtypeStruct((T, ATTN_W), BF16)),
        in_specs=[pl.BlockSpec((tm, D), row), pl.BlockSpec((ATTN_W, D), lambda i: (0, 0)),
                  pl.BlockSpec((tm, ATTN_W), row), pl.BlockSpec((tm, ATTN_W), lambda i: (i, zb))],
        out_specs=(pl.BlockSpec((tm, ATTN_W), row), pl.BlockSpec((tm, ATTN_W), row)),
        compiler_params=_params("parallel"),
    )(da, w_au, o, proj)


def _attn_bwd(projb, do, carries, kept_w, kept_beta, tq, rider=None):
    T = projb.shape[0]
    nq = T // tq
    tk, nk = tq, nq
    n_pairs = ATTN_W // LANE
    scale = HEAD_DIM ** -0.5

    def body(q_ref, k_ref, v_ref, do_ref, c_ref, ws_ref, bs_ref, dq_ref, dk_ref, dv_ref, kT_sc, vT_sc, km_sc, dkT_ref, dvT_ref,
             *per_head):
        f_sc, dq_sc, kn_sc, l_sc, dw_sc, dl_sc, w_sc = (per_head[2 * n:2 * n + 2] for n in range(7))
        i = pl.program_id(1)

        @pl.when(i == 0)
        def _():
            _fill_blocks(k_ref, nk, tk, transposed_sc=kT_sc, masked_sc=km_sc, norm_sc=kn_sc)
            _fill_blocks(v_ref, nk, tk, transposed_sc=vT_sc)
            dkT_ref[...] = jnp.zeros_like(dkT_ref)
            dvT_ref[...] = jnp.zeros_like(dvT_ref)

        qs = _head_masks(q_ref[...], scale)
        dos = _head_masks(do_ref[...], 1.0)
        qT = [x.astype(F32).T.astype(BF16) for x in qs]
        doT = [x.astype(F32).T.astype(BF16) for x in dos]
        lane = lax.broadcasted_iota(jnp.int32, (1, LANE), 1)
        valid = lax.broadcasted_iota(jnp.int32, (tq, tk), 1) < lax.broadcasted_iota(jnp.int32, (tq, tk), 0)
        kk0 = lax.broadcasted_iota(jnp.int32, (tk, tk), 0)
        kk1 = lax.broadcasted_iota(jnp.int32, (tk, tk), 1)
        suffix = (kk0 >= kk1).astype(BF16)
        prefix = (kk0 <= kk1).astype(BF16)
        for a in range(2):
            f_sc[a][...] = jnp.zeros_like(f_sc[a])
            dq_sc[a][...] = jnp.zeros_like(dq_sc[a])
            dl_sc[a][1] = jnp.zeros((tq, tk), BF16)
            w_sc[a][1] = jnp.zeros((tq, tk), BF16)
        live = lane == i
        for a, qn in enumerate(_max_row_norm2(qs[0] + qs[1], _head_lanes())):
            l_max = jnp.sqrt(qn * kn_sc[a][0:1, :])
            live = jnp.logical_or(live, jnp.min(c_ref[a, 0], axis=0, keepdims=True) - l_max <= ZERO_WEIGHT)
        t0 = jnp.min(jnp.where(jnp.logical_and(live, lane <= i), lane, i))
        n = i - t0

        def products(j, slot, with_logits=True):
            kT = kT_sc[j]
            vT = vT_sc[j]
            for a in range(2):
                if with_logits:
                    l_sc[a][slot] = jnp.dot(qs[a], kT, preferred_element_type=F32)
                dw_sc[a][slot] = jnp.dot(dos[a], vT, preferred_element_type=F32)

        def gradients(j, slot):
            rows = pl.ds(pl.multiple_of(j * tk, tk), tk)
            dkT = []
            dvT = []
            for a in range(2):
                dlb = dl_sc[a][slot]
                dq_sc[a][...] += jnp.dot(dlb, km_sc[a, rows, :], preferred_element_type=F32)
                dkT.append(jnp.dot(qT[a], dlb, preferred_element_type=F32))
                dvT.append(jnp.dot(doT[a], w_sc[a][slot], preferred_element_type=F32))
            dkT_ref[j] += dkT[0] + dkT[1]
            dvT_ref[j] += dvT[0] + dvT[1]

        def elementwise(j, slot, masked):
            sp, inc, e, beta, p = [None] * 2, [None] * 2, [None] * 2, [None] * 2, [None] * 2
            for a in range(2):
                x = _softplus(l_sc[a][slot])
                sp[a] = jnp.where(valid, x, 0.0) if masked else x
            for a in range(2):
                inc[a] = jnp.dot(sp[a].astype(BF16), suffix, preferred_element_type=F32)
            for a in range(2):
                l = l_sc[a][slot]
                c = jnp.sum(jnp.where(lane == j, c_ref[a, 0], 0.0), axis=1, keepdims=True)
                w = jnp.exp(l - inc[a] - c)
                if masked:
                    w = jnp.where(valid, w, 0.0)
                w_sc[a][slot] = w.astype(BF16)
                beta[a] = jnp.exp(l - sp[a])
                e[a] = w * dw_sc[a][slot]
            for a in range(2):
                p[a] = jnp.dot(e[a].astype(BF16), prefix, preferred_element_type=F32)
            for a in range(2):
                f = f_sc[a][...]
                dl = e[a] - beta[a] * (p[a] + _wide(f, tk))
                if masked:
                    dl = jnp.where(valid, dl, 0.0)
                dl_sc[a][slot] = dl.astype(BF16)
                f_sc[a][...] = f + p[a][:, tk - 1:tk]

        def kept_blocks(tiles):
            keys = [(n_, a) for n_ in range(len(tiles)) for a in range(2)]
            for j, slot, cols in tiles:
                products(j, slot, with_logits=False)
            e, p = {}, {}
            for n_, a in keys:
                j, slot, cols = tiles[n_]
                w = ws_ref[a, 0, :, cols]
                w_sc[a][slot] = w.astype(BF16)
                e[n_, a] = w * dw_sc[a][slot]
            for k in keys:
                p[k] = jnp.dot(e[k].astype(BF16), prefix, preferred_element_type=F32)
            for a in range(2):
                f = f_sc[a][...]
                for n_, (j, slot, cols) in enumerate(tiles):
                    dl = e[n_, a] - bs_ref[a, 0, :, cols] * (p[n_, a] + _wide(f, tk))
                    dl_sc[a][slot] = dl.astype(BF16)
                    f = f + p[n_, a][:, tk - 1:tk]
                f_sc[a][...] = f

        def step(r, slot):
            products(t0 + r + 1, 1 - slot)
            gradients(t0 + jnp.maximum(r - 1, 0), 1 - slot)
            elementwise(t0 + r, slot, False)

        def last_two(slot):
            gradients(t0 + jnp.maximum(n - 2, 0), slot)
            kept_blocks([(i - 1, 1 - slot, slice(0, tk)), (i, slot, slice(tk, 2 * tk))])
            gradients(i - 1, 1 - slot)
            gradients(i, slot)

        @pl.when(n >= 2)
        def _():
            products(t0, 0)

        def two_steps(tt, carry):
            step(2 * tt, 0)
            step(2 * tt + 1, 1)
            return carry

        lax.fori_loop(0, jnp.maximum(n - 1, 0) // 2, two_steps, 0)

        @pl.when(n == 0)
        def _():
            kept_blocks([(i, 0, slice(tk, 2 * tk))])
            gradients(i, 0)

        @pl.when(n % 2 == 1)
        def _():
            last_two(1)

        @pl.when(jnp.logical_and(n > 0, n % 2 == 0))
        def _():
            step(n - 2, 0)
            last_two(0)

        dq_ref[...] = ((dq_sc[0][...] + dq_sc[1][...]) * scale).astype(BF16)

        @pl.when(i == nq - 1)
        def _():
            def untranspose(j, carry):
                rows = pl.ds(pl.multiple_of(j * tk, tk), tk)
                dk_ref[rows, :] = dkT_ref[j].T.astype(BF16)
                dv_ref[rows, :] = dvT_ref[j].T.astype(BF16)
                return carry

            lax.fori_loop(0, nk, untranspose, 0)

    scratch = ([pltpu.VMEM((nk, LANE, tk), BF16), pltpu.VMEM((nk, LANE, tk), BF16), pltpu.VMEM((2, T, LANE), BF16),
                pltpu.VMEM((nk, LANE, tk), F32), pltpu.VMEM((nk, LANE, tk), F32)]
               + [pltpu.VMEM((tq, LANE), F32)] * 4 + [pltpu.VMEM((8, LANE), F32)] * 2
               + [pltpu.VMEM((2, tq, tk), F32)] * 4 + [pltpu.VMEM((2, tq, tk), BF16)] * 4)
    r_in, r_in_specs, r_out, r_out_specs, r_sems = _rider_specs(rider)
    body = _with_rider(body, 7, 3, len(scratch), rider, *_first_last_step(n_pairs, nq))
    kept_spec = pl.BlockSpec((2, 1, tq, 2 * tk), lambda p, i: (p, i, 0, 0))
    return _pc(
        body, name="attn_bwd", grid=(n_pairs, nq),
        out_shape=tuple([jax.ShapeDtypeStruct((T, ATTN_W), BF16)] * 3 + r_out),
        in_specs=[pl.BlockSpec((tq, LANE), lambda p, i: (i, p)),
                  pl.BlockSpec((T, LANE), lambda p, i: (0, n_pairs + p)),
                  pl.BlockSpec((T, LANE), lambda p, i: (0, 2 * n_pairs + p)),
                  pl.BlockSpec((tq, LANE), lambda p, i: (i, p)),
                  pl.BlockSpec((2, 1, tq, LANE), lambda p, i: (p, i, 0, 0)), kept_spec, kept_spec] + r_in_specs,
        out_specs=tuple([pl.BlockSpec((tq, LANE), lambda p, i: (i, p)), pl.BlockSpec((T, LANE), lambda p, i: (0, p)),
                         pl.BlockSpec((T, LANE), lambda p, i: (0, p))] + r_out_specs),
        scratch_shapes=scratch + r_sems,
        compiler_params=_params("arbitrary", "arbitrary"),
    )(projb, projb, projb, do, carries, kept_w, kept_beta, *r_in)


def _dh_norm_bwd(dproj, w, x, g, dxo, tm, rider=None):
    T, D = x.shape
    NW = w.shape[1]
    nm = T // tm

    def body(dp_ref, w_ref, x_ref, g_ref, dxo_ref, dx_ref, dg_ref):
        @pl.when(pl.program_id(0) == 0)
        def _():
            dg_ref[...] = jnp.zeros_like(dg_ref)

        dhv = lax.dot_general(dp_ref[...], w_ref[...], NT_DIMS, preferred_element_type=F32)
        xv = x_ref[...]
        r = lax.rsqrt(jnp.mean(xv * xv, axis=-1, keepdims=True) + RMS_EPS)
        xh = xv * r
        dg_ref[...] += jnp.sum(dhv * xh, axis=0, keepdims=True)
        dhg = dhv * g_ref[...]
        dx_ref[...] = dxo_ref[...] + r * (dhg - xh * jnp.mean(dhg * xh, axis=-1, keepdims=True))

    r_in, r_in_specs, r_out, r_out_specs, r_sems = _rider_specs(rider)
    body = _with_rider(body, 5, 2, 0, rider, lambda: pl.program_id(0) == 0, lambda: pl.program_id(0) == nm - 1)
    row = lambda i: (i, 0)
    fixed = lambda i: (0, 0)
    return _pc(
        body, name="d_h_norm_bwd", grid=(nm,),
        out_shape=tuple([jax.ShapeDtypeStruct((T, D), F32), jax.ShapeDtypeStruct((1, D), F32)] + r_out),
        in_specs=[pl.BlockSpec((tm, NW), row), pl.BlockSpec((D, NW), fixed), pl.BlockSpec((tm, D), row),
                  pl.BlockSpec((1, D), fixed), pl.BlockSpec((tm, D), row)] + r_in_specs,
        out_specs=tuple([pl.BlockSpec((tm, D), row), pl.BlockSpec((1, D), fixed)] + r_out_specs),
        scratch_shapes=r_sems,
        compiler_params=_params("arbitrary"),
    )(dproj, w, x, g, dxo, *r_in)


def _adamw(pieces, w, m, v, name):
    rows, cols = w.shape
    br = rows
    while br * cols > 65536 and br % 16 == 0:
        br //= 2
    c1 = 1.0 / (1.0 - ADAM_B1 ** ADAM_STEP)
    c2 = 1.0 / (1.0 - ADAM_B2 ** ADAM_STEP)

    def body(p_ref, w_ref, m_ref, v_ref, g_ref, d_ref, nm_ref, nv_ref):
        g = p_ref[0].astype(F32)
        for s in range(1, N_DEV):
            g = g + p_ref[s].astype(F32)
        nm = ADAM_B1 * m_ref[...] + (1.0 - ADAM_B1) * g
        nv = ADAM_B2 * v_ref[...] + (1.0 - ADAM_B2) * (g * g)
        g_ref[...] = g
        nm_ref[...] = nm
        nv_ref[...] = nv
        d_ref[...] = -ADAM_LR * ((nm * c1) / (jnp.sqrt(nv * c2) + ADAM_EPS) + ADAM_WD * w_ref[...])

    blk = pl.BlockSpec((br, cols), lambda i: (i, 0))
    shape = jax.ShapeDtypeStruct((rows, cols), F32)
    return _pc(
        body, name=name, grid=(rows // br,), out_shape=(shape, shape, shape, shape),
        in_specs=[pl.BlockSpec((N_DEV, br, cols), lambda i: (0, i, 0)), blk, blk, blk],
        out_specs=(blk, blk, blk, blk),
        compiler_params=_params("parallel"),
    )(pieces, w, m, v)


def _rows128(a):
    flat = a.reshape(-1)
    n = flat.shape[0]
    padded = -(-n // (8 * LANE)) * (8 * LANE)
    if padded != n:
        flat = jnp.concatenate([flat, jnp.zeros((padded - n,), flat.dtype)])
    return flat.reshape(-1, LANE)


def _pack(parts):
    return jnp.concatenate([_rows128(p) for p in parts], axis=0)


def _unpack(packed, like):
    out, r = [], 0
    for a in like:
        n = a.size
        nr = -(-n // (8 * LANE)) * 8
        out.append(packed[r:r + nr].reshape(-1)[:n].reshape(a.shape))
        r += nr
    return out


def kernel(x, norm_g, w_in, b_gate, pool_w, pool_scale, w_pool_up, w_attn_up, w_out, final_g, loss_target, m_norm_g, m_w_in, m_b_gate, m_pool_w, m_pool_scale, m_w_pool_up, m_w_attn_up, m_w_out, m_final_g, v_norm_g, v_w_in, v_b_gate, v_pool_w, v_pool_scale, v_w_pool_up, v_w_attn_up, v_w_out, v_final_g):
    L = norm_g.shape[0]
    T, D = x.shape[1], x.shape[2]
    NW = w_in.shape[2] * N_DEV
    assert NW == 2 * POOL_W + 4 * ATTN_W + 2 * D and x.shape[0] == 1
    tm = min(512, T)
    tq = min(256, T // 2)
    x0 = x.reshape(T, D)
    target = loss_target.reshape(T, D)

    assert L >= 2
    win_first = jnp.transpose(_exchange([w_in[0].astype(BF16)], [False], "gather_w_in0")[0], (1, 0, 2)).reshape(D, NW)
    rest = [w_in[1:].astype(BF16), w_pool_up.astype(BF16), w_attn_up.astype(BF16), w_out.astype(BF16)]

    saved = []
    xl = x0
    for l in range(L):
        proj, projb, h = _norm_inproj(xl, norm_g[l:l + 1], win_first if l == 0 else win_rest[l - 1], min(256, T))
        y_pool = _pool_fwd(proj, pool_w[l], pool_scale[l:l + 1], tm)
        if l == 0:
            o, y_attn, carries, kept_w, kept_b, g_in, g_pu, g_au, g_out = _attn_fwd(projb, proj, tq, rider=(rest, [False] * 4))
            win_rest = jnp.transpose(g_in, (1, 2, 0, 3)).reshape(L - 1, D, NW)
            wpu_full = jnp.transpose(g_pu, (1, 2, 0, 3)).reshape(L, POOL_W, D)
            wau_full = jnp.transpose(g_au, (1, 2, 0, 3)).reshape(L, ATTN_W, D)
            wout_full = jnp.transpose(g_out, (1, 0, 2, 3)).reshape(L, D, D)
        else:
            o, y_attn, carries, kept_w, kept_b = _attn_fwd(projb, proj, tq)
        merged = _merge_fwd(y_pool, y_attn, wpu_full[l], wau_full[l], proj, b_gate[l:l + 1], min(256, T))
        x_next = _mm_nn_res(merged, wout_full[l], xl, tm, "out_proj")
        saved.append((xl, proj, projb, h, y_pool, o, y_attn, (carries, kept_w, kept_b), merged))
        xl = x_next

    dx, d_final_g, loss_part = _final_loss(xl, final_g.reshape(1, D), target, tm)

    d_norm_g, d_b_gate, d_pool_w, d_pool_scale = [None] * L, [None] * L, [None] * L, [None] * L
    d_win, d_wpu, d_wau, d_wout = [None] * L, [None] * L, [None] * L, [None] * L
    small_like = [norm_g, b_gate, pool_w, pool_scale, final_g, jnp.zeros((8, LANE), F32)]
    for l in reversed(range(L)):
        xin, proj, projb, h, y_pool, o, y_attn, carries, merged = saved[l]
        win_l = win_first if l == 0 else win_rest[l - 1]
        d_wout[l] = _mm_tn(merged, dx, 1, D, min(1024, T), "d_w_out").reshape(N_DEV, D // N_DEV, D)
        dp, da, dgl, d_b_gate[l] = _merge_bwd(dx, wout_full[l], y_pool, y_attn, wpu_full[l], wau_full[l], proj,
                                              b_gate[l:l + 1], min(256, T))
        d_wpu[l] = _mm_tn(y_pool, dp, N_DEV, D, min(1024, T), "d_w_pool_up")
        d_wau[l] = _mm_tn(y_attn, da, N_DEV, D, min(1024, T), "d_w_attn_up")
        du, dzp, d_pool_w[l], d_pool_scale[l] = _pool_bwd(proj, dp, wpu_full[l], pool_w[l], pool_scale[l:l + 1], tm)
        do, dza = _attn_gate_bwd(da, wau_full[l], o, proj, tm)
        if l == 0:
            small = _pack([jnp.concatenate([jnp.zeros((1, D), F32)] + d_norm_g[1:], 0), jnp.concatenate(d_b_gate, 0),
                           jnp.stack(d_pool_w, 0), jnp.concatenate(d_pool_scale, 0), d_final_g, loss_part])
            early = d_win[1:] + d_wpu + d_wau + d_wout
            dq, dk, dv, got_small, *got_early = _attn_bwd(
                projb, do, *carries, tq, rider=([small] + early, [False] + [True] * len(early)))
        else:
            dq, dk, dv = _attn_bwd(projb, do, *carries, tq)
        dproj = jnp.concatenate([du, dzp, dq, dk, dv, dza, dgl], axis=1)
        d_win[l] = _mm_tn(h, dproj, N_DEV, NW // 2, min(1024, T), "d_w_in")
        if l == 0:
            dx, d_norm_g[l], got_win0 = _dh_norm_bwd(dproj, win_l, xin, norm_g[l:l + 1], dx, min(256, T),
                                                     rider=([d_win[0]], [True]))
        else:
            dx, d_norm_g[l] = _dh_norm_bwd(dproj, win_l, xin, norm_g[l:l + 1], dx, min(256, T))

    got_norm0 = _exchange([d_norm_g[0].reshape(-1, LANE)], [False], "gather_norm_grad")[0]
    r_small = jnp.concatenate([got_norm0, got_small[:, D // LANE:]], axis=1)
    r_in = jnp.stack([got_win0] + got_early[:L - 1], axis=1)
    r_pu = jnp.stack(got_early[L - 1:2 * L - 1], axis=1)
    r_au = jnp.stack(got_early[2 * L - 1:3 * L - 1], axis=1)
    r_out = jnp.stack(got_early[3 * L - 1:4 * L - 1], axis=1)

    def update(pieces, w, m, v, name):
        cols = w.shape[-1]
        res = _adamw(pieces.reshape(N_DEV, -1, cols), w.reshape(-1, cols), m.reshape(-1, cols),
                     v.reshape(-1, cols), name)
        return [r.reshape(w.shape) for r in res]

    u_in = update(r_in, w_in, m_w_in, v_w_in, "adamw_w_in")
    u_pu = update(r_pu, w_pool_up, m_w_pool_up, v_w_pool_up, "adamw_w_pool_up")
    u_au = update(r_au, w_attn_up, m_w_attn_up, v_w_attn_up, "adamw_w_attn_up")
    u_out = update(r_out, w_out, m_w_out, v_w_out, "adamw_w_out")
    zeros = small_like[-1]
    smalls = _adamw(r_small,
                    _pack([norm_g, b_gate, pool_w, pool_scale, final_g, zeros]),
                    _pack([m_norm_g, m_b_gate, m_pool_w, m_pool_scale, m_final_g, zeros]),
                    _pack([v_norm_g, v_b_gate, v_pool_w, v_pool_scale, v_final_g, zeros]), "adamw_small")
    s_g, s_d, s_m, s_v = [_unpack(s, small_like) for s in smalls]
    loss = s_g[5][0, 0]

    def ordered(k):
        s = (s_g, s_d, s_m, s_v)[k]
        return [s[0], u_in[k], s[1], s[2], s[3], u_pu[k], u_au[k], u_out[k], s[4]]

    return (loss, dx.reshape(x.shape), *ordered(0), *ordered(1), *ordered(2), *ordered(3))
```

```python
import jax
import jax.numpy as jnp
from jax import lax
from jax.experimental import pallas as pl
from jax.experimental.pallas import tpu as pltpu

F32 = jnp.float32
BF16 = jnp.bfloat16

N_DEV = 8
HEAD_DIM = 64
ATTN_W = 512
POOL_W = 512
POOL_G = 128
POOL_WINDOWS = (2, 4, 8, 16)
HALO = 16
LANE = 128
RMS_EPS = 1e-6
ZERO_WEIGHT = 110.0
NO_CARRY = 3.0e38
ADAM_LR, ADAM_B1, ADAM_B2, ADAM_EPS, ADAM_WD, ADAM_STEP = 0.001, 0.9, 0.999, 1e-08, 0.01, 10
VMEM_LIMIT = 56 * 1024 * 1024

NT_DIMS = (((1,), (1,)), ((), ()))
TN_DIMS = (((0,), (0,)), ((), ()))


def _pc(body, **kw):
    return pl.pallas_call(body, **kw)


def _params(*sem):
    return pltpu.CompilerParams(dimension_semantics=sem, vmem_limit_bytes=VMEM_LIMIT)


def _sigmoid(z):
    return 1.0 / (1.0 + jnp.exp(-z))


def _silu_and_grad(z):
    s = _sigmoid(z)
    return z * s, s * (1.0 + z * (1.0 - s))


def _my_index():
    return 4 * lax.axis_index("x") + 2 * lax.axis_index("y") + lax.axis_index("c")


def _peer(k):
    x, y, c = lax.axis_index("x"), lax.axis_index("y"), lax.axis_index("c")
    px = lax.rem(x + ((k >> 2) & 1), 2)
    py = lax.rem(y + ((k >> 1) & 1), 2)
    pc = lax.rem(c + (k & 1), 2)
    return (px, py, pc), 4 * px + 2 * py + pc


def _exchange_copies(ins, outs, sems, scatter):
    send_sems, recv_sems, local_sems = sems
    n = len(ins)
    me = _my_index()

    def src(a, idx):
        return ins[a].at[idx] if scatter[a] else ins[a]

    local = [pltpu.make_async_copy(src(a, me), outs[a].at[me], local_sems.at[a]) for a in range(n)]
    sends, arrivals = [], []
    for k in (1, 2, 4, 3, 5, 6, 7):
        dev, pidx = _peer(k)
        for a in range(n):
            sem = dict(send_sem=send_sems.at[a * N_DEV + k], recv_sem=recv_sems.at[a * N_DEV + k],
                       device_id=dev, device_id_type=pl.DeviceIdType.MESH)
            sends.append(pltpu.make_async_remote_copy(src_ref=src(a, pidx), dst_ref=outs[a].at[me], **sem))
            arrivals.append(pltpu.make_async_remote_copy(src_ref=src(a, pidx), dst_ref=outs[a].at[pidx], **sem))
    return local, sends, arrivals


def _exchange_start(ins, outs, sems, scatter):
    local, sends, _ = _exchange_copies(ins, outs, sems, scatter)
    for cp in local + sends:
        cp.start()


def _exchange_wait(ins, outs, sems, scatter):
    local, sends, arrivals = _exchange_copies(ins, outs, sems, scatter)
    for cp in arrivals:
        cp.wait_recv()
    for cp in sends:
        cp.wait_send()
    for cp in local:
        cp.wait()


def _exchange_shapes(arrays, scatter):
    n = len(arrays)
    out_shape = [jax.ShapeDtypeStruct((N_DEV,) + tuple(a.shape[1:] if s else a.shape), a.dtype)
                 for a, s in zip(arrays, scatter)]
    sems = [pltpu.SemaphoreType.DMA((n * N_DEV,)), pltpu.SemaphoreType.DMA((n * N_DEV,)),
            pltpu.SemaphoreType.DMA((n,))]
    return out_shape, sems


def _exchange(arrays, scatter, name):
    n = len(arrays)

    def body(*refs):
        ins, outs, sems = refs[:n], refs[n:2 * n], refs[2 * n:]
        _exchange_start(ins, outs, sems, scatter)
        _exchange_wait(ins, outs, sems, scatter)

    out_shape, sems = _exchange_shapes(arrays, scatter)
    any_spec = pl.BlockSpec(memory_space=pl.ANY)
    return _pc(
        body, name=name, out_shape=tuple(out_shape),
        in_specs=[any_spec] * n, out_specs=tuple([any_spec] * n), scratch_shapes=sems,
    )(*arrays)


def _with_rider(body, n_in, n_out, n_scratch, rider, first, last):
    if rider is None:
        return body
    arrays, scatter = rider
    n = len(arrays)

    def wrapped(*refs):
        ins, r_ins = refs[:n_in], refs[n_in:n_in + n]
        outs = refs[n_in + n:n_in + n + n_out]
        r_outs = refs[n_in + n + n_out:n_in + 2 * n + n_out]
        scratch = refs[n_in + 2 * n + n_out:n_in + 2 * n + n_out + n_scratch]
        sems = refs[n_in + 2 * n + n_out + n_scratch:]

        @pl.when(first())
        def _():
            _exchange_start(r_ins, r_outs, sems, scatter)

        body(*ins, *outs, *scratch)

        @pl.when(last())
        def _():
            _exchange_wait(r_ins, r_outs, sems, scatter)

    return wrapped


def _rider_specs(rider):
    if rider is None:
        return [], [], [], [], []
    arrays, scatter = rider
    out_shape, sems = _exchange_shapes(arrays, scatter)
    any_spec = pl.BlockSpec(memory_space=pl.ANY)
    return list(arrays), [any_spec] * len(arrays), out_shape, [any_spec] * len(arrays), sems


def _mm_nn_res(a, b, res, tm, name):
    T, K = a.shape
    N = b.shape[1]

    def body(a_ref, b_ref, r_ref, o_ref):
        o_ref[...] = r_ref[...] + jnp.dot(a_ref[...], b_ref[...], preferred_element_type=F32)

    return _pc(
        body, name=name, grid=(T // tm,), out_shape=jax.ShapeDtypeStruct((T, N), F32),
        in_specs=[pl.BlockSpec((tm, K), lambda i: (i, 0)), pl.BlockSpec((K, N), lambda i: (0, 0)),
                  pl.BlockSpec((tm, N), lambda i: (i, 0))],
        out_specs=pl.BlockSpec((tm, N), lambda i: (i, 0)),
        compiler_params=_params("parallel"),
    )(a, b, res)


def _mm_tn(a, b, n_col_shards, tn, tk, name):
    T, M = a.shape
    N = b.shape[1]
    sw = N // n_col_shards
    per_step = tn // sw
    nk = T // tk

    def body(a_ref, b_ref, o_ref, acc_sc):
        k = pl.program_id(1)
        part = lax.dot_general(a_ref[...].astype(BF16), b_ref[...].astype(BF16), TN_DIMS,
                               preferred_element_type=F32)

        @pl.when(k == 0)
        def _():
            acc_sc[...] = part

        @pl.when(k > 0)
        def _():
            acc_sc[...] += part

        @pl.when(k == nk - 1)
        def _():
            for s in range(per_step):
                o_ref[s] = acc_sc[:, s * sw:(s + 1) * sw].astype(BF16)

    return _pc(
        body, name=name, grid=(N // tn, nk),
        out_shape=jax.ShapeDtypeStruct((n_col_shards, M, sw), BF16),
        in_specs=[pl.BlockSpec((tk, M), lambda j, k: (k, 0)), pl.BlockSpec((tk, tn), lambda j, k: (k, j))],
        out_specs=pl.BlockSpec((per_step, M, sw), lambda j, k: (j, 0, 0)),
        scratch_shapes=[pltpu.VMEM((M, tn), F32)],
        compiler_params=_params("parallel", "arbitrary"),
    )(a, b)


def _mm_tn_segs(a, segs, sw, tk, name):
    T, M = a.shape
    N = sum(x.shape[1] for x in segs)
    assert N % sw == 0
    n_seg = len(segs)
    nk = T // tk

    def body(a_ref, *rest):
        seg_refs, o_ref, acc_sc = rest[:n_seg], rest[n_seg], rest[n_seg + 1]
        k = pl.program_id(0)
        slab = jnp.concatenate([r[...] for r in seg_refs], axis=1)
        part = lax.dot_general(a_ref[...], slab, TN_DIMS, preferred_element_type=F32)

        @pl.when(k == 0)
        def _():
            acc_sc[...] = part

        @pl.when(k > 0)
        def _():
            acc_sc[...] += part

        @pl.when(k == nk - 1)
        def _():
            for j in range(N // sw):
                o_ref[j] = acc_sc[:, j * sw:(j + 1) * sw].astype(BF16)

    return _pc(
        body, name=name, grid=(nk,), out_shape=jax.ShapeDtypeStruct((N // sw, M, sw), BF16),
        in_specs=[pl.BlockSpec((tk, M), lambda k: (k, 0))] + [pl.BlockSpec((tk, x.shape[1]), lambda k: (k, 0)) for x in segs],
        out_specs=pl.BlockSpec((N // sw, M, sw), lambda k: (0, 0, 0)),
        scratch_shapes=[pltpu.VMEM((M, N), F32)],
        compiler_params=_params("arbitrary"),
    )(a, *segs)


def _proj_layout(D):
    return {"u": 0, "z_pool": POOL_W, "gates": 2 * POOL_W, "z_attn": 2 * POOL_W + 2 * D, "width": 2 * POOL_W + 2 * D + ATTN_W}


def _norm_inproj(x, g, w, tm):
    T, D = x.shape
    NW = w.shape[1]
    lay = _proj_layout(D)
    qkv0, za0, gl0 = 2 * POOL_W, 2 * POOL_W + 3 * ATTN_W, 2 * POOL_W + 4 * ATTN_W

    def body(x_ref, g_ref, w_ref, proj_ref, qkv_ref, h_ref):
        xv = x_ref[...]
        r = lax.rsqrt(jnp.mean(xv * xv, axis=-1, keepdims=True) + RMS_EPS)
        h = ((xv * r) * g_ref[...]).astype(BF16)
        h_ref[...] = h

        def cols(lo, hi):
            return jnp.dot(h, w_ref[:, lo:hi], preferred_element_type=F32)

        proj_ref[:, :lay["gates"]] = cols(0, qkv0)
        qkv_ref[...] = cols(qkv0, za0).astype(BF16)
        proj_ref[:, lay["gates"]:lay["z_attn"]] = cols(gl0, NW)
        proj_ref[:, lay["z_attn"]:] = cols(za0, gl0)

    return _pc(
        body, name="norm_inproj", grid=(T // tm,),
        out_shape=(jax.ShapeDtypeStruct((T, lay["width"]), F32), jax.ShapeDtypeStruct((T, 3 * ATTN_W), BF16),
                   jax.ShapeDtypeStruct((T, D), BF16)),
        in_specs=[pl.BlockSpec((tm, D), lambda i: (i, 0)), pl.BlockSpec((1, D), lambda i: (0, 0)),
                  pl.BlockSpec((D, NW), lambda i: (0, 0))],
        out_specs=(pl.BlockSpec((tm, lay["width"]), lambda i: (i, 0)), pl.BlockSpec((tm, 3 * ATTN_W), lambda i: (i, 0)),
                   pl.BlockSpec((tm, D), lambda i: (i, 0))),
        compiler_params=_params("parallel"),
    )(x, g, w)


def _window_sums(xh, forward):
    n = xh.shape[0]
    sums, s, step = [], xh, 1
    for _ in POOL_WINDOWS:
        s = s + pltpu.roll(s, step if forward else n - step, 0)
        sums.append(s)
        step *= 2
    return sums


def _pooled(u, halo, row):
    sums = _window_sums(jnp.concatenate([halo, u], axis=0), True)
    out = []
    for g, w in enumerate(POOL_WINDOWS):
        cols = slice(g * POOL_G, (g + 1) * POOL_G)
        cnt = jnp.minimum(row + 1, w).astype(F32)
        out.append(sums[g][HALO:, cols] / cnt - u[:, cols])
    return out


def _pool_fwd(proj, pool_w, scale, R):
    T = proj.shape[0]

    def body(u_ref, z_ref, pw_ref, sc_ref, y_ref, halo_sc):
        i = pl.program_id(0)

        @pl.when(i == 0)
        def _():
            halo_sc[...] = jnp.zeros_like(halo_sc)

        u = u_ref[...]
        row = i * R + lax.broadcasted_iota(jnp.int32, (R, 1), 0)
        pooled = _pooled(u, halo_sc[...], row)
        mixed = jnp.concatenate(
            [jnp.dot(pooled[g].astype(BF16), pw_ref[g].astype(BF16), preferred_element_type=F32)
             for g in range(len(POOL_WINDOWS))], axis=1)
        z = z_ref[...]
        y_ref[...] = ((mixed * sc_ref[...]) * (z * _sigmoid(z))).astype(BF16)
        halo_sc[...] = u[R - HALO:, :]

    return _pc(
        body, name="pool_fwd", grid=(T // R,), out_shape=jax.ShapeDtypeStruct((T, POOL_W), BF16),
        in_specs=[pl.BlockSpec((R, POOL_W), lambda i: (i, 0)), pl.BlockSpec((R, POOL_W), lambda i: (i, 1)),
                  pl.BlockSpec((4, POOL_G, POOL_G), lambda i: (0, 0, 0)), pl.BlockSpec((1, POOL_W), lambda i: (0, 0))],
        out_specs=pl.BlockSpec((R, POOL_W), lambda i: (i, 0)),
        scratch_shapes=[pltpu.VMEM((HALO, POOL_W), F32)],
        compiler_params=_params("arbitrary"),
    )(proj, proj, pool_w, scale)


def _softplus(l):
    return jnp.maximum(l, 0.0) + jnp.log(1.0 + jnp.exp(-jnp.abs(l)))


def _first_last_step(n0, n1):
    return (lambda: jnp.logical_and(pl.program_id(0) == 0, pl.program_id(1) == 0),
            lambda: jnp.logical_and(pl.program_id(0) == n0 - 1, pl.program_id(1) == n1 - 1))


def _head_lanes():
    lane = lax.broadcasted_iota(jnp.int32, (1, LANE), 1)
    return [lane < HEAD_DIM, lane >= HEAD_DIM]


def _head_masks(q, scale):
    qf = q.astype(F32) * scale
    return [jnp.where(m, qf, 0.0).astype(BF16) for m in _head_lanes()]


def _wide(c, width):
    return jnp.concatenate([c] * (width // LANE), axis=1)


def _max_row_norm2(x, heads):
    sq = x.astype(F32) * x.astype(F32)
    return [jnp.max(jnp.sum(jnp.where(m, sq, 0.0), axis=1, keepdims=True), axis=0, keepdims=True) for m in heads]


def _fill_blocks(src_ref, nk, tk, transposed_sc=None, masked_sc=None, norm_sc=None):
    heads = _head_lanes()
    if norm_sc is not None:
        for a in range(2):
            norm_sc[a][...] = jnp.zeros_like(norm_sc[a])

    def step(j, carry):
        rows = pl.ds(pl.multiple_of(j * tk, tk), tk)
        blk = src_ref[rows, :]
        if norm_sc is not None:
            for a, n2 in enumerate(_max_row_norm2(blk, heads)):
                norm_sc[a][...] = jnp.maximum(norm_sc[a][...], n2)
        if transposed_sc is not None:
            transposed_sc[j] = blk.astype(F32).T.astype(BF16)
        if masked_sc is not None:
            for a in range(2):
                masked_sc[a, rows, :] = jnp.where(heads[a], blk, jnp.zeros_like(blk))
        return carry

    lax.fori_loop(0, nk, step, 0)


def _attn_fwd(projb, proj, tq, rider=None):
    T = projb.shape[0]
    nq = T // tq
    tk, nk = tq, nq
    assert nk <= LANE
    n_pairs = ATTN_W // LANE
    zb = (proj.shape[1] - ATTN_W) // LANE

    def body(q_ref, k_ref, v_ref, za_ref, o_ref, y_ref, c_ref, ws_ref, bs_ref, kT_sc, vm_sc, *per_head):
        c_sc, cm_sc, o_sc, kn_sc, l_sc, w_sc = (per_head[2 * n:2 * n + 2] for n in range(6))
        i = pl.program_id(1)

        @pl.when(i == 0)
        def _():
            _fill_blocks(k_ref, nk, tk, transposed_sc=kT_sc, norm_sc=kn_sc)
            _fill_blocks(v_ref, nk, tk, masked_sc=vm_sc)

        qs = _head_masks(q_ref[...], HEAD_DIM ** -0.5)
        lane = lax.broadcasted_iota(jnp.int32, (1, LANE), 1)
        valid = lax.broadcasted_iota(jnp.int32, (tq, tk), 1) < lax.broadcasted_iota(jnp.int32, (tq, tk), 0)
        suffix = (lax.broadcasted_iota(jnp.int32, (tk, tk), 0) >= lax.broadcasted_iota(jnp.int32, (tk, tk), 1)).astype(BF16)
        for a in range(2):
            c_sc[a][...] = jnp.zeros_like(c_sc[a])
            o_sc[a][...] = jnp.zeros_like(o_sc[a])
            cm_sc[a][...] = jnp.full(cm_sc[a].shape, NO_CARRY, F32)
        l_max = [jnp.sqrt(qn * kn_sc[a][...]) for a, qn in enumerate(_max_row_norm2(qs[0] + qs[1], _head_lanes()))]

        def logits(j, slot):
            kT = kT_sc[j]
            for a in range(2):
                l_sc[a][slot] = jnp.dot(qs[a], kT, preferred_element_type=F32)

        def values(j, slot):
            rows = pl.ds(pl.multiple_of(j * tk, tk), tk)
            for a in range(2):
                o_sc[a][...] += jnp.dot(w_sc[a][slot], vm_sc[a, rows, :], preferred_element_type=F32)

        def softplus(slot, masked):
            out = []
            for a in range(2):
                x = _softplus(l_sc[a][slot])
                out.append(jnp.where(valid, x, 0.0) if masked else x)
            return out

        def finish_weights(j, slot, sp, inc, mask, keep=None, cols=None):
            for a in range(2):
                c = c_sc[a][...]
                l = l_sc[a][slot]
                w = jnp.exp(l - inc[a] - _wide(c, tk))
                if mask is not None:
                    w = jnp.where(mask, w, 0.0)
                add = inc[a][:, 0:1]
                if keep is not None:
                    w = jnp.where(keep, w, 0.0)
                    add = jnp.where(keep, add, 0.0)
                w_sc[a][slot] = w.astype(BF16)
                if cols is not None:
                    beta = jnp.exp(l - sp[a])
                    ws_ref[a, 0, :, cols] = w
                    bs_ref[a, 0, :, cols] = beta if mask is None else jnp.where(mask, beta, 0.0)
                cm_sc[a][...] = jnp.where(lane == j, c, cm_sc[a][...])
                c_sc[a][...] = c + add

        def weights(j, slot):
            sp = softplus(slot, False)
            inc = [jnp.dot(sp[a].astype(BF16), suffix, preferred_element_type=F32) for a in range(2)]
            finish_weights(j, slot, sp, inc, None)

        def more():
            live = [jnp.min(c_sc[a][...], axis=0, keepdims=True) - l_max[a][0:1, :] <= ZERO_WEIGHT for a in range(2)]
            return jnp.max(jnp.where(jnp.logical_or(live[0], live[1]), 1, 0))

        logits(i, 0)
        logits(jnp.maximum(i - 1, 0), 1)
        sp = softplus(0, True) + softplus(1, False)
        inc = [jnp.dot(x.astype(BF16), suffix, preferred_element_type=F32) for x in sp]
        finish_weights(i, 0, sp[:2], inc[:2], valid, cols=slice(tk, 2 * tk))
        finish_weights(i - 1, 1, sp[2:], inc[2:], None, keep=i > 0, cols=slice(0, tk))
        values(i, 0)
        logits(jnp.maximum(i - 2, 0), 0)

        def step(t, slot):
            logits(jnp.maximum(i - t - 1, 0), 1 - slot)
            values(i - t + 1, 1 - slot)
            weights(i - t, slot)

        def two_steps(carry):
            tt, _ = carry
            step(2 * tt + 2, 0)
            step(2 * tt + 3, 1)
            return tt + 1, more()

        pairs, go = lax.while_loop(lambda c: jnp.logical_and(2 * c[0] + 3 <= i, c[1] > 0), two_steps, (0, more()))
        done = 1 + 2 * pairs
        one_more = jnp.logical_and(done + 1 == i, go > 0)

        @pl.when(one_more)
        def _():
            step(i, 0)
            values(0, 0)

        @pl.when(jnp.logical_not(one_more))
        def _():
            values(jnp.maximum(i - done, 0), 1)

        o = o_sc[0][...] + o_sc[1][...]
        o_ref[...] = o
        za = za_ref[...]
        y_ref[...] = (o * (za * _sigmoid(za))).astype(BF16)
        c_ref[0, 0] = cm_sc[0][...]
        c_ref[1, 0] = cm_sc[1][...]

    scratch = ([pltpu.VMEM((nk, LANE, tk), BF16), pltpu.VMEM((2, T, LANE), BF16)]
               + [pltpu.VMEM((tq, LANE), F32)] * 6 + [pltpu.VMEM((8, LANE), F32)] * 2
               + [pltpu.VMEM((2, tq, tk), F32)] * 2 + [pltpu.VMEM((2, tq, tk), BF16)] * 2)
    r_in, r_in_specs, r_out, r_out_specs, r_sems = _rider_specs(rider)
    body = _with_rider(body, 4, 5, len(scratch), rider, *_first_last_step(n_pairs, nq))
    kept = jax.ShapeDtypeStruct((2 * n_pairs, nq, tq, 2 * tk), F32)
    kept_spec = pl.BlockSpec((2, 1, tq, 2 * tk), lambda p, i: (p, i, 0, 0))
    return _pc(
        body, name="attn_fwd", grid=(n_pairs, nq),
        out_shape=tuple([jax.ShapeDtypeStruct((T, ATTN_W), F32), jax.ShapeDtypeStruct((T, ATTN_W), BF16),
                         jax.ShapeDtypeStruct((2 * n_pairs, nq, tq, LANE), F32), kept, kept] + r_out),
        in_specs=[pl.BlockSpec((tq, LANE), lambda p, i: (i, p)),
                  pl.BlockSpec((T, LANE), lambda p, i: (0, n_pairs + p)),
                  pl.BlockSpec((T, LANE), lambda p, i: (0, 2 * n_pairs + p)),
                  pl.BlockSpec((tq, LANE), lambda p, i: (i, zb + p))] + r_in_specs,
        out_specs=tuple([pl.BlockSpec((tq, LANE), lambda p, i: (i, p)), pl.BlockSpec((tq, LANE), lambda p, i: (i, p)),
                         pl.BlockSpec((2, 1, tq, LANE), lambda p, i: (p, i, 0, 0)), kept_spec, kept_spec] + r_out_specs),
        scratch_shapes=scratch + r_sems,
        compiler_params=_params("arbitrary", "arbitrary"),
    )(projb, projb, projb, proj, *r_in)


def _gates(gl0, gl1, bg, D):
    return _sigmoid(gl0 + bg[:, :D]), _sigmoid(gl1 + bg[:, D:])


def _merge_fwd(y_pool, y_attn, w_pu, w_au, proj, b_gate, tm):
    T = y_pool.shape[0]
    D = w_pu.shape[1]
    gb = _proj_layout(D)["gates"] // D

    def body(yp_ref, ya_ref, wpu_ref, wau_ref, gl0_ref, gl1_ref, bg_ref, m_ref):
        p = jnp.dot(yp_ref[...], wpu_ref[...], preferred_element_type=F32)
        a = jnp.dot(ya_ref[...], wau_ref[...], preferred_element_type=F32)
        g0, g1 = _gates(gl0_ref[...], gl1_ref[...], bg_ref[...], D)
        m_ref[...] = (g0 * p + g1 * a).astype(BF16)

    row = lambda i: (i, 0)
    fixed = lambda i: (0, 0)
    return _pc(
        body, name="merge_fwd", grid=(T // tm,), out_shape=jax.ShapeDtypeStruct((T, D), BF16),
        in_specs=[pl.BlockSpec((tm, POOL_W), row), pl.BlockSpec((tm, ATTN_W), row),
                  pl.BlockSpec((POOL_W, D), fixed), pl.BlockSpec((ATTN_W, D), fixed),
                  pl.BlockSpec((tm, D), lambda i: (i, gb)), pl.BlockSpec((tm, D), lambda i: (i, gb + 1)),
                  pl.BlockSpec((1, 2 * D), fixed)],
        out_specs=pl.BlockSpec((tm, D), row),
        compiler_params=_params("parallel"),
    )(y_pool, y_attn, w_pu, w_au, proj, proj, b_gate)


def _final_loss(x, g, target, tm):
    T, D = x.shape

    def body(x_ref, g_ref, t_ref, dx_ref, dg_ref, loss_ref):
        @pl.when(pl.program_id(0) == 0)
        def _():
            dg_ref[...] = jnp.zeros_like(dg_ref)
            loss_ref[...] = jnp.zeros_like(loss_ref)

        xv, gv = x_ref[...], g_ref[...]
        r = lax.rsqrt(jnp.mean(xv * xv, axis=-1, keepdims=True) + RMS_EPS)
        xh = xv * r
        d = xh * gv - t_ref[...]
        loss_ref[...] += 0.5 * jnp.sum(jnp.mean(d * d, axis=-1, keepdims=True), axis=0, keepdims=True)
        dy = d * (1.0 / D)
        dg_ref[...] += jnp.sum(dy * xh, axis=0, keepdims=True)
        dh = dy * gv
        dx_ref[...] = r * (dh - xh * jnp.mean(dh * xh, axis=-1, keepdims=True))

    return _pc(
        body, name="final_loss", grid=(T // tm,),
        out_shape=(jax.ShapeDtypeStruct((T, D), F32), jax.ShapeDtypeStruct((1, D), F32),
                   jax.ShapeDtypeStruct((8, LANE), F32)),
        in_specs=[pl.BlockSpec((tm, D), lambda i: (i, 0)), pl.BlockSpec((1, D), lambda i: (0, 0)),
                  pl.BlockSpec((tm, D), lambda i: (i, 0))],
        out_specs=(pl.BlockSpec((tm, D), lambda i: (i, 0)), pl.BlockSpec((1, D), lambda i: (0, 0)),
                   pl.BlockSpec((8, LANE), lambda i: (0, 0))),
        compiler_params=_params("arbitrary"),
    )(x, g, target)


def _merge_bwd(dxo, w_out, y_pool, y_attn, w_pu, w_au, proj, b_gate, tm):
    T, D = dxo.shape
    gb = _proj_layout(D)["gates"] // D

    def body(dxo_ref, wout_ref, yp_ref, ya_ref, wpu_ref, wau_ref, gl0_ref, gl1_ref, bg_ref,
             dp_ref, da_ref, dgl_ref, dbg_ref):
        @pl.when(pl.program_id(0) == 0)
        def _():
            dbg_ref[...] = jnp.zeros_like(dbg_ref)

        dmv = lax.dot_general(dxo_ref[...].astype(BF16), wout_ref[...], NT_DIMS, preferred_element_type=F32)
        p = jnp.dot(yp_ref[...], wpu_ref[...], preferred_element_type=F32)
        a = jnp.dot(ya_ref[...], wau_ref[...], preferred_element_type=F32)
        g0, g1 = _gates(gl0_ref[...], gl1_ref[...], bg_ref[...], D)
        dp_ref[...] = (dmv * g0).astype(BF16)
        da_ref[...] = (dmv * g1).astype(BF16)
        dgl0 = dmv * p * (g0 * (1.0 - g0))
        dgl1 = dmv * a * (g1 * (1.0 - g1))
        dgl_ref[:, :D] = dgl0.astype(BF16)
        dgl_ref[:, D:] = dgl1.astype(BF16)
        dbg_ref[:, :D] += jnp.sum(dgl0, axis=0, keepdims=True)
        dbg_ref[:, D:] += jnp.sum(dgl1, axis=0, keepdims=True)

    row = lambda i: (i, 0)
    fixed = lambda i: (0, 0)
    return _pc(
        body, name="merge_bwd", grid=(T // tm,),
        out_shape=(jax.ShapeDtypeStruct((T, D), BF16), jax.ShapeDtypeStruct((T, D), BF16),
                   jax.ShapeDtypeStruct((T, 2 * D), BF16), jax.ShapeDtypeStruct((1, 2 * D), F32)),
        in_specs=[pl.BlockSpec((tm, D), row), pl.BlockSpec((D, D), fixed),
                  pl.BlockSpec((tm, POOL_W), row), pl.BlockSpec((tm, ATTN_W), row),
                  pl.BlockSpec((POOL_W, D), fixed), pl.BlockSpec((ATTN_W, D), fixed),
                  pl.BlockSpec((tm, D), lambda i: (i, gb)), pl.BlockSpec((tm, D), lambda i: (i, gb + 1)),
                  pl.BlockSpec((1, 2 * D), fixed)],
        out_specs=(pl.BlockSpec((tm, D), row), pl.BlockSpec((tm, D), row), pl.BlockSpec((tm, 2 * D), row),
                   pl.BlockSpec((1, 2 * D), fixed)),
        compiler_params=_params("arbitrary"),
    )(dxo, w_out, y_pool, y_attn, w_pu, w_au, proj, proj, b_gate)


def _pool_bwd(proj, dp, w_pu, pool_w, scale, R):
    T = proj.shape[0]
    D = w_pu.shape[1]
    nb = T // R
    hb = R // HALO

    def body(u_ref, up_ref, z_ref, dp_ref, wpu_ref, pw_ref, sc_ref, du_ref, dz_ref, dpw_ref, dsc_ref, halo_sc):
        i = pl.program_id(0)
        rb = nb - 1 - i

        @pl.when(i == 0)
        def _():
            halo_sc[...] = jnp.zeros_like(halo_sc)
            dpw_ref[...] = jnp.zeros_like(dpw_ref)
            dsc_ref[...] = jnp.zeros_like(dsc_ref)

        u = u_ref[...]
        row = rb * R + lax.broadcasted_iota(jnp.int32, (R, 1), 0)
        before = jnp.where(rb > 0, up_ref[...], 0.0)
        pooled = _pooled(u, before, row)
        pw = [pw_ref[g].astype(BF16) for g in range(len(POOL_WINDOWS))]
        mixed = jnp.concatenate(
            [jnp.dot(pooled[g].astype(BF16), pw[g], preferred_element_type=F32) for g in range(len(POOL_WINDOWS))],
            axis=1)
        sc = sc_ref[...]
        silu, dsilu = _silu_and_grad(z_ref[...])
        dyv = lax.dot_general(dp_ref[...], wpu_ref[...], NT_DIMS, preferred_element_type=F32)
        dmp = dyv * silu
        dz_ref[...] = (dyv * (mixed * sc) * dsilu).astype(BF16)
        dsc_ref[...] += jnp.sum(dmp * mixed, axis=0, keepdims=True)
        dmixed = (dmp * sc).astype(BF16)
        dpn = []
        dpooled = []
        for g, w in enumerate(POOL_WINDOWS):
            cols = slice(g * POOL_G, (g + 1) * POOL_G)
            dpw_ref[g] += lax.dot_general(pooled[g].astype(BF16), dmixed[:, cols], TN_DIMS,
                                          preferred_element_type=F32)
            dpg = lax.dot_general(dmixed[:, cols], pw[g], NT_DIMS, preferred_element_type=F32)
            dpooled.append(dpg)
            dpn.append(dpg / jnp.minimum(row + 1, w).astype(F32))
        dpn = jnp.concatenate(dpn, axis=1)
        sums = _window_sums(jnp.concatenate([dpn, halo_sc[...]], axis=0), False)
        du_ref[...] = jnp.concatenate(
            [sums[g][:R, g * POOL_G:(g + 1) * POOL_G] - dpooled[g] for g in range(len(POOL_WINDOWS))],
            axis=1).astype(BF16)
        halo_sc[...] = dpn[:HALO, :]

    rev = lambda i: (nb - 1 - i, 0)
    return _pc(
        body, name="pool_bwd", grid=(nb,),
        out_shape=(jax.ShapeDtypeStruct((T, POOL_W), BF16), jax.ShapeDtypeStruct((T, POOL_W), BF16),
                   jax.ShapeDtypeStruct((4, POOL_G, POOL_G), F32), jax.ShapeDtypeStruct((1, POOL_W), F32)),
        in_specs=[pl.BlockSpec((R, POOL_W), rev),
                  pl.BlockSpec((HALO, POOL_W), lambda i: (jnp.maximum((nb - 1 - i) * hb - 1, 0), 0)),
                  pl.BlockSpec((R, POOL_W), lambda i: (nb - 1 - i, 1)),
                  pl.BlockSpec((R, D), rev), pl.BlockSpec((POOL_W, D), lambda i: (0, 0)),
                  pl.BlockSpec((4, POOL_G, POOL_G), lambda i: (0, 0, 0)), pl.BlockSpec((1, POOL_W), lambda i: (0, 0))],
        out_specs=(pl.BlockSpec((R, POOL_W), rev), pl.BlockSpec((R, POOL_W), rev),
                   pl.BlockSpec((4, POOL_G, POOL_G), lambda i: (0, 0, 0)), pl.BlockSpec((1, POOL_W), lambda i: (0, 0))),
        scratch_shapes=[pltpu.VMEM((HALO, POOL_W), F32)],
        compiler_params=_params("arbitrary"),
    )(proj, proj, proj, dp, w_pu, pool_w, scale)


def _attn_gate_bwd(da, w_au, o, proj, tm):
    T, D = da.shape
    zb = (proj.shape[1] - ATTN_W) // ATTN_W

    def body(da_ref, wau_ref, o_ref, za_ref, do_ref, dza_ref):
        silu, dsilu = _silu_and_grad(za_ref[...])
        dyv = lax.dot_general(da_ref[...], wau_ref[...], NT_DIMS, preferred_element_type=F32)
        do_ref[...] = (dyv * silu).astype(BF16)
        dza_ref[...] = (dyv * o_ref[...] * dsilu).astype(BF16)

    row = lambda i: (i, 0)
    return _pc(
        body, name="attn_gate_bwd", grid=(T // tm,),
        out_shape=(jax.ShapeDtypeStruct((T, ATTN_W), BF16), jax.ShapeDtypeStruct((T, ATTN_W), BF16)),
        in_specs=[pl.BlockSpec((tm, D), row), pl.BlockSpec((ATTN_W, D), lambda i: (0, 0)),
                  pl.BlockSpec((tm, ATTN_W), row), pl.BlockSpec((tm, ATTN_W), lambda i: (i, zb))],
        out_specs=(pl.BlockSpec((tm, ATTN_W), row), pl.BlockSpec((tm, ATTN_W), row)),
        compiler_params=_params("parallel"),
    )(da, w_au, o, proj)


def _attn_bwd(projb, do, carries, kept_w, kept_beta, tq, rider=None):
    T = projb.shape[0]
    nq = T // tq
    tk, nk = tq, nq
    n_pairs = ATTN_W // LANE
    scale = HEAD_DIM ** -0.5

    def body(q_ref, k_ref, v_ref, do_ref, c_ref, ws_ref, bs_ref, dq_ref, dk_ref, dv_ref, kT_sc, vT_sc, km_sc, dkT_ref, dvT_ref,
             *per_head):
        f_sc, dq_sc, kn_sc, l_sc, dw_sc, dl_sc, w_sc = (per_head[2 * n:2 * n + 2] for n in range(7))
        i = pl.program_id(1)

        @pl.when(i == 0)
        def _():
            _fill_blocks(k_ref, nk, tk, transposed_sc=kT_sc, masked_sc=km_sc, norm_sc=kn_sc)
            _fill_blocks(v_ref, nk, tk, transposed_sc=vT_sc)
            dkT_ref[...] = jnp.zeros_like(dkT_ref)
            dvT_ref[...] = jnp.zeros_like(dvT_ref)

        qs = _head_masks(q_ref[...], scale)
        dos = _head_masks(do_ref[...], 1.0)
        qT = [x.astype(F32).T.astype(BF16) for x in qs]
        doT = [x.astype(F32).T.astype(BF16) for x in dos]
        lane = lax.broadcasted_iota(jnp.int32, (1, LANE), 1)
        valid = lax.broadcasted_iota(jnp.int32, (tq, tk), 1) < lax.broadcasted_iota(jnp.int32, (tq, tk), 0)
        kk0 = lax.broadcasted_iota(jnp.int32, (tk, tk), 0)
        kk1 = lax.broadcasted_iota(jnp.int32, (tk, tk), 1)
        suffix = (kk0 >= kk1).astype(BF16)
        prefix = (kk0 <= kk1).astype(BF16)
        for a in range(2):
            f_sc[a][...] = jnp.zeros_like(f_sc[a])
            dq_sc[a][...] = jnp.zeros_like(dq_sc[a])
        live = lane == i
        for a, qn in enumerate(_max_row_norm2(qs[0] + qs[1], _head_lanes())):
            l_max = jnp.sqrt(qn * kn_sc[a][0:1, :])
            live = jnp.logical_or(live, jnp.min(c_ref[a, 0], axis=0, keepdims=True) - l_max <= ZERO_WEIGHT)
        t0 = jnp.min(jnp.where(jnp.logical_and(live, lane <= i), lane, i))
        n = i - t0

        def products(j, slot, with_logits=True):
            kT = kT_sc[j]
            vT = vT_sc[j]
            for a in range(2):
                if with_logits:
                    l_sc[a][slot] = jnp.dot(qs[a], kT, preferred_element_type=F32)
                dw_sc[a][slot] = jnp.dot(dos[a], vT, preferred_element_type=F32)

        def gradients(j, slot):
            rows = pl.ds(pl.multiple_of(j * tk, tk), tk)
            dkT = []
            dvT = []
            for a in range(2):
                dlb = dl_sc[a][slot]
                dq_sc[a][...] += jnp.dot(dlb, km_sc[a, rows, :], preferred_element_type=F32)
                dkT.append(jnp.dot(qT[a], dlb, preferred_element_type=F32))
                dvT.append(jnp.dot(doT[a], w_sc[a][slot], preferred_element_type=F32))
            dkT_ref[j] += dkT[0] + dkT[1]
            dvT_ref[j] += dvT[0] + dvT[1]

        def elementwise(j, slot, masked):
            sp, inc, e, beta, p = [None] * 2, [None] * 2, [None] * 2, [None] * 2, [None] * 2
            for a in range(2):
                x = _softplus(l_sc[a][slot])
                sp[a] = jnp.where(valid, x, 0.0) if masked else x
            for a in range(2):
                inc[a] = jnp.dot(sp[a].astype(BF16), suffix, preferred_element_type=F32)
            for a in range(2):
                l = l_sc[a][slot]
                c = jnp.sum(jnp.where(lane == j, c_ref[a, 0], 0.0), axis=1, keepdims=True)
                w = jnp.exp(l - inc[a] - c)
                if masked:
                    w = jnp.where(valid, w, 0.0)
                w_sc[a][slot] = w.astype(BF16)
                beta[a] = jnp.exp(l - sp[a])
                e[a] = w * dw_sc[a][slot]
            for a in range(2):
                p[a] = jnp.dot(e[a].astype(BF16), prefix, preferred_element_type=F32)
            for a in range(2):
                f = f_sc[a][...]
                dl = e[a] - beta[a] * (p[a] + _wide(f, tk))
                if masked:
                    dl = jnp.where(valid, dl, 0.0)
                dl_sc[a][slot] = dl.astype(BF16)
                f_sc[a][...] = f + p[a][:, tk - 1:tk]

        def kept_blocks(tiles):
            keys = [(n_, a) for n_ in range(len(tiles)) for a in range(2)]
            for j, slot, cols in tiles:
                products(j, slot, with_logits=False)
            e, p = {}, {}
            for n_, a in keys:
                j, slot, cols = tiles[n_]
                w = ws_ref[a, 0, :, cols]
                w_sc[a][slot] = w.astype(BF16)
                e[n_, a] = w * dw_sc[a][slot]
            for k in keys:
                p[k] = jnp.dot(e[k].astype(BF16), prefix, preferred_element_type=F32)
            for a in range(2):
                f = f_sc[a][...]
                for n_, (j, slot, cols) in enumerate(tiles):
                    dl = e[n_, a] - bs_ref[a, 0, :, cols] * (p[n_, a] + _wide(f, tk))
                    dl_sc[a][slot] = dl.astype(BF16)
                    f = f + p[n_, a][:, tk - 1:tk]
                f_sc[a][...] = f

        def step(r, slot):
            products(t0 + r + 1, 1 - slot)
            gradients(t0 + jnp.maximum(r - 1, 0), 1 - slot)
            elementwise(t0 + r, slot, False)

        def last_two(slot, pending=True):
            if pending:
                gradients(i - 2, slot)
            kept_blocks([(i - 1, 1 - slot, slice(0, tk)), (i, slot, slice(tk, 2 * tk))])
            gradients(i - 1, 1 - slot)
            gradients(i, slot)

        @pl.when(n >= 2)
        def _():
            for a in range(2):
                dl_sc[a][1] = jnp.zeros((tq, tk), BF16)
                w_sc[a][1] = jnp.zeros((tq, tk), BF16)
            products(t0, 0)

        def two_steps(tt, carry):
            step(2 * tt, 0)
            step(2 * tt + 1, 1)
            return carry

        lax.fori_loop(0, jnp.maximum(n - 1, 0) // 2, two_steps, 0)

        @pl.when(n == 0)
        def _():
            kept_blocks([(i, 0, slice(tk, 2 * tk))])
            gradients(i, 0)

        @pl.when(n == 1)
        def _():
            last_two(1, pending=False)

        @pl.when(jnp.logical_and(n > 1, n % 2 == 1))
        def _():
            last_two(1)

        @pl.when(jnp.logical_and(n > 0, n % 2 == 0))
        def _():
            step(n - 2, 0)
            last_two(0)

        dq_ref[...] = ((dq_sc[0][...] + dq_sc[1][...]) * scale).astype(BF16)

        @pl.when(i == nq - 1)
        def _():
            def untranspose(j, carry):
                rows = pl.ds(pl.multiple_of(j * tk, tk), tk)
                dk_ref[rows, :] = dkT_ref[j].T.astype(BF16)
                dv_ref[rows, :] = dvT_ref[j].T.astype(BF16)
                return carry

            lax.fori_loop(0, nk, untranspose, 0)

    scratch = ([pltpu.VMEM((nk, LANE, tk), BF16), pltpu.VMEM((nk, LANE, tk), BF16), pltpu.VMEM((2, T, LANE), BF16),
                pltpu.VMEM((nk, LANE, tk), F32), pltpu.VMEM((nk, LANE, tk), F32)]
               + [pltpu.VMEM((tq, LANE), F32)] * 4 + [pltpu.VMEM((8, LANE), F32)] * 2
               + [pltpu.VMEM((2, tq, tk), F32)] * 4 + [pltpu.VMEM((2, tq, tk), BF16)] * 4)
    r_in, r_in_specs, r_out, r_out_specs, r_sems = _rider_specs(rider)
    body = _with_rider(body, 7, 3, len(scratch), rider, *_first_last_step(n_pairs, nq))
    kept_spec = pl.BlockSpec((2, 1, tq, 2 * tk), lambda p, i: (p, i, 0, 0))
    return _pc(
        body, name="attn_bwd", grid=(n_pairs, nq),
        out_shape=tuple([jax.ShapeDtypeStruct((T, ATTN_W), BF16)] * 3 + r_out),
        in_specs=[pl.BlockSpec((tq, LANE), lambda p, i: (i, p)),
                  pl.BlockSpec((T, LANE), lambda p, i: (0, n_pairs + p)),
                  pl.BlockSpec((T, LANE), lambda p, i: (0, 2 * n_pairs + p)),
                  pl.BlockSpec((tq, LANE), lambda p, i: (i, p)),
                  pl.BlockSpec((2, 1, tq, LANE), lambda p, i: (p, i, 0, 0)), kept_spec, kept_spec] + r_in_specs,
        out_specs=tuple([pl.BlockSpec((tq, LANE), lambda p, i: (i, p)), pl.BlockSpec((T, LANE), lambda p, i: (0, p)),
                         pl.BlockSpec((T, LANE), lambda p, i: (0, p))] + r_out_specs),
        scratch_shapes=scratch + r_sems,
        compiler_params=_params("arbitrary", "arbitrary"),
    )(projb, projb, projb, do, carries, kept_w, kept_beta, *r_in)


def _dh_norm_bwd(segs, w, x, g, dxo, tm, rider=None):
    T, D = x.shape
    nm = T // tm
    n_seg = len(segs)
    offs = [sum(y.shape[1] for y in segs[:n]) for n in range(n_seg + 1)]
    assert offs[-1] == w.shape[1]

    def body(*refs):
        seg_refs = refs[:n_seg]
        w_ref, x_ref, g_ref, dxo_ref, dx_ref, dg_ref = refs[n_seg:]

        @pl.when(pl.program_id(0) == 0)
        def _():
            dg_ref[...] = jnp.zeros_like(dg_ref)

        dhv = None
        for n in range(n_seg):
            part = lax.dot_general(seg_refs[n][...], w_ref[:, offs[n]:offs[n + 1]], NT_DIMS, preferred_element_type=F32)
            dhv = part if dhv is None else dhv + part
        xv = x_ref[...]
        r = lax.rsqrt(jnp.mean(xv * xv, axis=-1, keepdims=True) + RMS_EPS)
        xh = xv * r
        dg_ref[...] += jnp.sum(dhv * xh, axis=0, keepdims=True)
        dhg = dhv * g_ref[...]
        dx_ref[...] = dxo_ref[...] + r * (dhg - xh * jnp.mean(dhg * xh, axis=-1, keepdims=True))

    r_in, r_in_specs, r_out, r_out_specs, r_sems = _rider_specs(rider)
    body = _with_rider(body, n_seg + 4, 2, 0, rider, lambda: pl.program_id(0) == 0, lambda: pl.program_id(0) == nm - 1)
    row = lambda i: (i, 0)
    fixed = lambda i: (0, 0)
    return _pc(
        body, name="d_h_norm_bwd", grid=(nm,),
        out_shape=tuple([jax.ShapeDtypeStruct((T, D), F32), jax.ShapeDtypeStruct((1, D), F32)] + r_out),
        in_specs=[pl.BlockSpec((tm, y.shape[1]), row) for y in segs]
        + [pl.BlockSpec(w.shape, fixed), pl.BlockSpec((tm, D), row), pl.BlockSpec((1, D), fixed), pl.BlockSpec((tm, D), row)]
        + r_in_specs,
        out_specs=tuple([pl.BlockSpec((tm, D), row), pl.BlockSpec((1, D), fixed)] + r_out_specs),
        scratch_shapes=r_sems,
        compiler_params=_params("arbitrary"),
    )(*segs, w, x, g, dxo, *r_in)


def _adamw(pieces, w, m, v, name):
    rows, cols = w.shape
    br = rows
    while br * cols > 65536 and br % 16 == 0:
        br //= 2
    c1 = 1.0 / (1.0 - ADAM_B1 ** ADAM_STEP)
    c2 = 1.0 / (1.0 - ADAM_B2 ** ADAM_STEP)

    def body(p_ref, w_ref, m_ref, v_ref, g_ref, d_ref, nm_ref, nv_ref):
        g = p_ref[0].astype(F32)
        for s in range(1, N_DEV):
            g = g + p_ref[s].astype(F32)
        nm = ADAM_B1 * m_ref[...] + (1.0 - ADAM_B1) * g
        nv = ADAM_B2 * v_ref[...] + (1.0 - ADAM_B2) * (g * g)
        g_ref[...] = g
        nm_ref[...] = nm
        nv_ref[...] = nv
        d_ref[...] = -ADAM_LR * ((nm * c1) / (jnp.sqrt(nv * c2) + ADAM_EPS) + ADAM_WD * w_ref[...])

    blk = pl.BlockSpec((br, cols), lambda i: (i, 0))
    shape = jax.ShapeDtypeStruct((rows, cols), F32)
    return _pc(
        body, name=name, grid=(rows // br,), out_shape=(shape, shape, shape, shape),
        in_specs=[pl.BlockSpec((N_DEV, br, cols), lambda i: (0, i, 0)), blk, blk, blk],
        out_specs=(blk, blk, blk, blk),
        compiler_params=_params("parallel"),
    )(pieces, w, m, v)


def _rows128(a):
    flat = a.reshape(-1)
    n = flat.shape[0]
    padded = -(-n // (8 * LANE)) * (8 * LANE)
    if padded != n:
        flat = jnp.concatenate([flat, jnp.zeros((padded - n,), flat.dtype)])
    return flat.reshape(-1, LANE)


def _pack(parts):
    return jnp.concatenate([_rows128(p) for p in parts], axis=0)


def _unpack(packed, like):
    out, r = [], 0
    for a in like:
        n = a.size
        nr = -(-n // (8 * LANE)) * 8
        out.append(packed[r:r + nr].reshape(-1)[:n].reshape(a.shape))
        r += nr
    return out


def kernel(x, norm_g, w_in, b_gate, pool_w, pool_scale, w_pool_up, w_attn_up, w_out, final_g, loss_target, m_norm_g, m_w_in, m_b_gate, m_pool_w, m_pool_scale, m_w_pool_up, m_w_attn_up, m_w_out, m_final_g, v_norm_g, v_w_in, v_b_gate, v_pool_w, v_pool_scale, v_w_pool_up, v_w_attn_up, v_w_out, v_final_g):
    L = norm_g.shape[0]
    T, D = x.shape[1], x.shape[2]
    NW = w_in.shape[2] * N_DEV
    assert NW == 2 * POOL_W + 4 * ATTN_W + 2 * D and x.shape[0] == 1
    tm = min(512, T)
    tq = min(256, T // 2)
    x0 = x.reshape(T, D)
    target = loss_target.reshape(T, D)

    assert L >= 2
    win_first = jnp.transpose(_exchange([w_in[0].astype(BF16)], [False], "gather_w_in0")[0], (1, 0, 2)).reshape(D, NW)
    rest = [w_in[1:].astype(BF16), w_pool_up.astype(BF16), w_attn_up.astype(BF16), w_out.astype(BF16)]

    saved = []
    xl = x0
    for l in range(L):
        proj, projb, h = _norm_inproj(xl, norm_g[l:l + 1], win_first if l == 0 else win_rest[l - 1], min(256, T))
        y_pool = _pool_fwd(proj, pool_w[l], pool_scale[l:l + 1], tm)
        if l == 0:
            o, y_attn, carries, kept_w, kept_b, g_in, g_pu, g_au, g_out = _attn_fwd(projb, proj, tq, rider=(rest, [False] * 4))
            win_rest = jnp.transpose(g_in, (1, 2, 0, 3)).reshape(L - 1, D, NW)
            wpu_full = jnp.transpose(g_pu, (1, 2, 0, 3)).reshape(L, POOL_W, D)
            wau_full = jnp.transpose(g_au, (1, 2, 0, 3)).reshape(L, ATTN_W, D)
            wout_full = jnp.transpose(g_out, (1, 0, 2, 3)).reshape(L, D, D)
        else:
            o, y_attn, carries, kept_w, kept_b = _attn_fwd(projb, proj, tq)
        merged = _merge_fwd(y_pool, y_attn, wpu_full[l], wau_full[l], proj, b_gate[l:l + 1], min(256, T))
        x_next = _mm_nn_res(merged, wout_full[l], xl, tm, "out_proj")
        saved.append((xl, proj, projb, h, y_pool, o, y_attn, (carries, kept_w, kept_b), merged))
        xl = x_next

    dx, d_final_g, loss_part = _final_loss(xl, final_g.reshape(1, D), target, tm)

    d_norm_g, d_b_gate, d_pool_w, d_pool_scale = [None] * L, [None] * L, [None] * L, [None] * L
    d_win, d_wpu, d_wau, d_wout = [None] * L, [None] * L, [None] * L, [None] * L
    small_like = [norm_g, b_gate, pool_w, pool_scale, final_g, jnp.zeros((8, LANE), F32)]
    for l in reversed(range(L)):
        xin, proj, projb, h, y_pool, o, y_attn, carries, merged = saved[l]
        win_l = win_first if l == 0 else win_rest[l - 1]
        d_wout[l] = _mm_tn(merged, dx, 1, D, min(1024, T), "d_w_out").reshape(N_DEV, D // N_DEV, D)
        dp, da, dgl, d_b_gate[l] = _merge_bwd(dx, wout_full[l], y_pool, y_attn, wpu_full[l], wau_full[l], proj,
                                              b_gate[l:l + 1], min(256, T))
        d_wpu[l] = _mm_tn(y_pool, dp, N_DEV, D, min(1024, T), "d_w_pool_up")
        d_wau[l] = _mm_tn(y_attn, da, N_DEV, D, min(1024, T), "d_w_attn_up")
        du, dzp, d_pool_w[l], d_pool_scale[l] = _pool_bwd(proj, dp, wpu_full[l], pool_w[l], pool_scale[l:l + 1], tm)
        do, dza = _attn_gate_bwd(da, wau_full[l], o, proj, tm)
        if l == 0:
            small = _pack([jnp.concatenate([jnp.zeros((1, D), F32)] + d_norm_g[1:], 0), jnp.concatenate(d_b_gate, 0),
                           jnp.stack(d_pool_w, 0), jnp.concatenate(d_pool_scale, 0), d_final_g, loss_part])
            early = d_win[1:] + d_wpu + d_wau + d_wout
            dq, dk, dv, got_small, *got_early = _attn_bwd(
                projb, do, *carries, tq, rider=([small] + early, [False] + [True] * len(early)))
        else:
            dq, dk, dv = _attn_bwd(projb, do, *carries, tq)
        segs = [du, dzp, dq, dk, dv, dza, dgl]
        sw = NW // N_DEV
        d_win[l] = jnp.concatenate([_mm_tn_segs(h, segs[:5], sw, min(1024, T), "d_w_in_a"),
                                    _mm_tn_segs(h, segs[5:], sw, min(1024, T), "d_w_in_b")], axis=0)
        if l == 0:
            dx, d_norm_g[l], got_win0 = _dh_norm_bwd(segs, win_l, xin, norm_g[l:l + 1], dx, min(256, T),
                                                     rider=([d_win[0]], [True]))
        else:
            dx, d_norm_g[l] = _dh_norm_bwd(segs, win_l, xin, norm_g[l:l + 1], dx, min(256, T))

    got_norm0 = _exchange([d_norm_g[0].reshape(-1, LANE)], [False], "gather_norm_grad")[0]
    r_small = jnp.concatenate([got_norm0, got_small[:, D // LANE:]], axis=1)
    r_in = jnp.stack([got_win0] + got_early[:L - 1], axis=1)
    r_pu = jnp.stack(got_early[L - 1:2 * L - 1], axis=1)
    r_au = jnp.stack(got_early[2 * L - 1:3 * L - 1], axis=1)
    r_out = jnp.stack(got_early[3 * L - 1:4 * L - 1], axis=1)

    def update(pieces, w, m, v, name):
        cols = w.shape[-1]
        res = _adamw(pieces.reshape(N_DEV, -1, cols), w.reshape(-1, cols), m.reshape(-1, cols),
                     v.reshape(-1, cols), name)
        return [r.reshape(w.shape) for r in res]

    u_in = update(r_in, w_in, m_w_in, v_w_in, "adamw_w_in")
    u_pu = update(r_pu, w_pool_up, m_w_pool_up, v_w_pool_up, "adamw_w_pool_up")
    u_au = update(r_au, w_attn_up, m_w_attn_up, v_w_attn_up, "adamw_w_attn_up")
    u_out = update(r_out, w_out, m_w_out, v_w_out, "adamw_w_out")
    zeros = small_like[-1]
    smalls = _adamw(r_small,
                    _pack([norm_g, b_gate, pool_w, pool_scale, final_g, zeros]),
                    _pack([m_norm_g, m_b_gate, m_pool_w, m_pool_scale, m_final_g, zeros]),
                    _pack([v_norm_g, v_b_gate, v_pool_w, v_pool_scale, v_final_g, zeros]), "adamw_small")
    s_g, s_d, s_m, s_v = [_unpack(s, small_like) for s in smalls]
    loss = s_g[5][0, 0]

    def ordered(k):
        s = (s_g, s_d, s_m, s_v)[k]
        return [s[0], u_in[k], s[1], s[2], s[3], u_pu[k], u_au[k], u_out[k], s[4]]

    return (loss, dx.reshape(x.shape), *ordered(0), *ordered(1), *ordered(2), *ordered(3))
```

```python
import jax
import jax.numpy as jnp
from jax import lax
from jax.experimental import pallas as pl
from jax.experimental.pallas import tpu as pltpu

F32 = jnp.float32
BF16 = jnp.bfloat16

N_DEV = 8
HEAD_DIM = 64
ATTN_W = 512
POOL_W = 512
POOL_G = 128
POOL_WINDOWS = (2, 4, 8, 16)
HALO = 16
LANE = 128
RMS_EPS = 1e-6
ZERO_WEIGHT = 110.0
NO_CARRY = 3.0e38
ADAM_LR, ADAM_B1, ADAM_B2, ADAM_EPS, ADAM_WD, ADAM_STEP = 0.001, 0.9, 0.999, 1e-08, 0.01, 10
VMEM_LIMIT = 56 * 1024 * 1024

NT_DIMS = (((1,), (1,)), ((), ()))
TN_DIMS = (((0,), (0,)), ((), ()))


def _pc(body, **kw):
    return pl.pallas_call(body, **kw)


def _params(*sem):
    return pltpu.CompilerParams(dimension_semantics=sem, vmem_limit_bytes=VMEM_LIMIT)


def _sigmoid(z):
    return 1.0 / (1.0 + jnp.exp(-z))


def _silu_and_grad(z):
    s = _sigmoid(z)
    return z * s, s * (1.0 + z * (1.0 - s))


def _my_index():
    return 4 * lax.axis_index("x") + 2 * lax.axis_index("y") + lax.axis_index("c")


def _peer(k):
    x, y, c = lax.axis_index("x"), lax.axis_index("y"), lax.axis_index("c")
    px = lax.rem(x + ((k >> 2) & 1), 2)
    py = lax.rem(y + ((k >> 1) & 1), 2)
    pc = lax.rem(c + (k & 1), 2)
    return (px, py, pc), 4 * px + 2 * py + pc


def _exchange_copies(ins, outs, sems, scatter):
    send_sems, recv_sems, local_sems = sems
    n = len(ins)
    me = _my_index()

    def src(a, idx):
        return ins[a].at[idx] if scatter[a] else ins[a]

    local = [pltpu.make_async_copy(src(a, me), outs[a].at[me], local_sems.at[a]) for a in range(n)]
    sends, arrivals = [], []
    for k in (1, 2, 4, 3, 5, 6, 7):
        dev, pidx = _peer(k)
        for a in range(n):
            sem = dict(send_sem=send_sems.at[a * N_DEV + k], recv_sem=recv_sems.at[a * N_DEV + k],
                       device_id=dev, device_id_type=pl.DeviceIdType.MESH)
            sends.append(pltpu.make_async_remote_copy(src_ref=src(a, pidx), dst_ref=outs[a].at[me], **sem))
            arrivals.append(pltpu.make_async_remote_copy(src_ref=src(a, pidx), dst_ref=outs[a].at[pidx], **sem))
    return local, sends, arrivals


def _exchange_start(ins, outs, sems, scatter):
    local, sends, _ = _exchange_copies(ins, outs, sems, scatter)
    for cp in local + sends:
        cp.start()


def _exchange_wait(ins, outs, sems, scatter):
    local, sends, arrivals = _exchange_copies(ins, outs, sems, scatter)
    for cp in arrivals:
        cp.wait_recv()
    for cp in sends:
        cp.wait_send()
    for cp in local:
        cp.wait()


def _exchange_shapes(arrays, scatter):
    n = len(arrays)
    out_shape = [jax.ShapeDtypeStruct((N_DEV,) + tuple(a.shape[1:] if s else a.shape), a.dtype)
                 for a, s in zip(arrays, scatter)]
    sems = [pltpu.SemaphoreType.DMA((n * N_DEV,)), pltpu.SemaphoreType.DMA((n * N_DEV,)),
            pltpu.SemaphoreType.DMA((n,))]
    return out_shape, sems


def _exchange(arrays, scatter, name):
    n = len(arrays)

    def body(*refs):
        ins, outs, sems = refs[:n], refs[n:2 * n], refs[2 * n:]
        _exchange_start(ins, outs, sems, scatter)
        _exchange_wait(ins, outs, sems, scatter)

    out_shape, sems = _exchange_shapes(arrays, scatter)
    any_spec = pl.BlockSpec(memory_space=pl.ANY)
    return _pc(
        body, name=name, out_shape=tuple(out_shape),
        in_specs=[any_spec] * n, out_specs=tuple([any_spec] * n), scratch_shapes=sems,
    )(*arrays)


def _gather_two_level(x, name):
    def body(x_ref, out_ref, send_sems, recv_sems, local_sem):
        X, Y, C = lax.axis_index("x"), lax.axis_index("y"), lax.axis_index("c")
        me, sibling = (X, Y, C), (X, Y, 1 - C)
        chips = [(1 - X, Y), (X, 1 - Y), (1 - X, 1 - Y)]

        def slot(px, py, pc):
            return out_ref.at[4 * px + 2 * py + pc]

        def copy(k, block, to, src=None):
            return pltpu.make_async_remote_copy(
                src_ref=slot(*block) if src is None else src, dst_ref=slot(*block),
                send_sem=send_sems.at[k], recv_sem=recv_sems.at[k], device_id=to, device_id_type=pl.DeviceIdType.MESH)

        mine = pltpu.make_async_copy(x_ref, slot(*me), local_sem)
        mine.start()
        first = [copy(0, me, sibling, src=x_ref)] + [copy(1 + j, me, (*chip, C), src=x_ref) for j, chip in enumerate(chips)]
        for cp in first:
            cp.start()
        passed = [copy(4 + j, (*chip, C), sibling) for j, chip in enumerate(chips)]
        for j, chip in enumerate(chips):
            copy(1 + j, (*chip, C), me).wait_recv()
            passed[j].start()
        copy(0, sibling, me).wait_recv()
        for j, chip in enumerate(chips):
            copy(4 + j, (*chip, 1 - C), me).wait_recv()
        for cp in first + passed:
            cp.wait_send()
        mine.wait()

    any_spec = pl.BlockSpec(memory_space=pl.ANY)
    return _pc(
        body, name=name, out_shape=jax.ShapeDtypeStruct((N_DEV,) + x.shape, x.dtype),
        in_specs=[any_spec], out_specs=any_spec,
        scratch_shapes=[pltpu.SemaphoreType.DMA((7,)), pltpu.SemaphoreType.DMA((7,)), pltpu.SemaphoreType.DMA(())],
    )(x)


def _with_rider(body, n_in, n_out, n_scratch, rider, first, last):
    if rider is None:
        return body
    arrays, scatter = rider
    n = len(arrays)

    def wrapped(*refs):
        ins, r_ins = refs[:n_in], refs[n_in:n_in + n]
        outs = refs[n_in + n:n_in + n + n_out]
        r_outs = refs[n_in + n + n_out:n_in + 2 * n + n_out]
        scratch = refs[n_in + 2 * n + n_out:n_in + 2 * n + n_out + n_scratch]
        sems = refs[n_in + 2 * n + n_out + n_scratch:]

        @pl.when(first())
        def _():
            _exchange_start(r_ins, r_outs, sems, scatter)

        body(*ins, *outs, *scratch)

        @pl.when(last())
        def _():
            _exchange_wait(r_ins, r_outs, sems, scatter)

    return wrapped


def _rider_specs(rider):
    if rider is None:
        return [], [], [], [], []
    arrays, scatter = rider
    out_shape, sems = _exchange_shapes(arrays, scatter)
    any_spec = pl.BlockSpec(memory_space=pl.ANY)
    return list(arrays), [any_spec] * len(arrays), out_shape, [any_spec] * len(arrays), sems


def _mm_nn_res(a, b, res, tm, name):
    T, K = a.shape
    N = b.shape[1]

    def body(a_ref, b_ref, r_ref, o_ref):
        o_ref[...] = r_ref[...] + jnp.dot(a_ref[...], b_ref[...], preferred_element_type=F32)

    return _pc(
        body, name=name, grid=(T // tm,), out_shape=jax.ShapeDtypeStruct((T, N), F32),
        in_specs=[pl.BlockSpec((tm, K), lambda i: (i, 0)), pl.BlockSpec((K, N), lambda i: (0, 0)),
                  pl.BlockSpec((tm, N), lambda i: (i, 0))],
        out_specs=pl.BlockSpec((tm, N), lambda i: (i, 0)),
        compiler_params=_params("parallel"),
    )(a, b, res)


def _mm_tn(a, b, n_col_shards, tn, tk, name):
    T, M = a.shape
    N = b.shape[1]
    sw = N // n_col_shards
    per_step = tn // sw
    nk = T // tk

    def body(a_ref, b_ref, o_ref, acc_sc):
        k = pl.program_id(1)
        part = lax.dot_general(a_ref[...].astype(BF16), b_ref[...].astype(BF16), TN_DIMS,
                               preferred_element_type=F32)

        @pl.when(k == 0)
        def _():
            acc_sc[...] = part

        @pl.when(k > 0)
        def _():
            acc_sc[...] += part

        @pl.when(k == nk - 1)
        def _():
            for s in range(per_step):
                o_ref[s] = acc_sc[:, s * sw:(s + 1) * sw].astype(BF16)

    return _pc(
        body, name=name, grid=(N // tn, nk),
        out_shape=jax.ShapeDtypeStruct((n_col_shards, M, sw), BF16),
        in_specs=[pl.BlockSpec((tk, M), lambda j, k: (k, 0)), pl.BlockSpec((tk, tn), lambda j, k: (k, j))],
        out_specs=pl.BlockSpec((per_step, M, sw), lambda j, k: (j, 0, 0)),
        scratch_shapes=[pltpu.VMEM((M, tn), F32)],
        compiler_params=_params("parallel", "arbitrary"),
    )(a, b)


def _mm_tn_segs(a, segs, sw, tk, name, into=None, first_shard=0):
    T, M = a.shape
    N = sum(x.shape[1] for x in segs)
    assert N % sw == 0 and first_shard % (N // sw) == 0
    n_seg = len(segs)
    nk = T // tk
    ns = N // sw

    def body(a_ref, *rest):
        seg_refs, o_ref, acc_sc = rest[:n_seg], rest[-2], rest[-1]
        k = pl.program_id(0)
        slab = jnp.concatenate([r[...] for r in seg_refs], axis=1)
        part = lax.dot_general(a_ref[...], slab, TN_DIMS, preferred_element_type=F32)

        @pl.when(k == 0)
        def _():
            acc_sc[...] = part

        @pl.when(k > 0)
        def _():
            acc_sc[...] += part

        @pl.when(k == nk - 1)
        def _():
            for j in range(ns):
                o_ref[j] = acc_sc[:, j * sw:(j + 1) * sw].astype(BF16)

    extra, extra_specs, alias = [], [], {}
    out_shape = jax.ShapeDtypeStruct((ns, M, sw), BF16)
    if into is not None:
        extra, extra_specs, alias = [into], [pl.BlockSpec(memory_space=pl.ANY)], {1 + n_seg: 0}
        out_shape = jax.ShapeDtypeStruct(into.shape, BF16)
    blk = first_shard // ns
    return _pc(
        body, name=name, grid=(nk,), out_shape=out_shape,
        in_specs=[pl.BlockSpec((tk, M), lambda k: (k, 0))] + [pl.BlockSpec((tk, x.shape[1]), lambda k: (k, 0)) for x in segs]
        + extra_specs,
        out_specs=pl.BlockSpec((ns, M, sw), lambda k: (blk, 0, 0)),
        scratch_shapes=[pltpu.VMEM((M, N), F32)], input_output_aliases=alias,
        compiler_params=_params("arbitrary"),
    )(a, *segs, *extra)


def _proj_layout(D):
    return {"u": 0, "z_pool": POOL_W, "gates": 2 * POOL_W, "z_attn": 2 * POOL_W + 2 * D, "width": 2 * POOL_W + 2 * D + ATTN_W}


def _norm_inproj(x, g, w, tm):
    T, D = x.shape
    NW = w.shape[1]
    lay = _proj_layout(D)
    qkv0, za0, gl0 = 2 * POOL_W, 2 * POOL_W + 3 * ATTN_W, 2 * POOL_W + 4 * ATTN_W

    def body(x_ref, g_ref, w_ref, proj_ref, qkv_ref, h_ref):
        xv = x_ref[...]
        r = lax.rsqrt(jnp.mean(xv * xv, axis=-1, keepdims=True) + RMS_EPS)
        h = ((xv * r) * g_ref[...]).astype(BF16)
        h_ref[...] = h

        def cols(lo, hi):
            return jnp.dot(h, w_ref[:, lo:hi], preferred_element_type=F32)

        proj_ref[:, :lay["gates"]] = cols(0, qkv0)
        qkv_ref[...] = cols(qkv0, za0).astype(BF16)
        proj_ref[:, lay["gates"]:lay["z_attn"]] = cols(gl0, NW)
        proj_ref[:, lay["z_attn"]:] = cols(za0, gl0)

    return _pc(
        body, name="norm_inproj", grid=(T // tm,),
        out_shape=(jax.ShapeDtypeStruct((T, lay["width"]), F32), jax.ShapeDtypeStruct((T, 3 * ATTN_W), BF16),
                   jax.ShapeDtypeStruct((T, D), BF16)),
        in_specs=[pl.BlockSpec((tm, D), lambda i: (i, 0)), pl.BlockSpec((1, D), lambda i: (0, 0)),
                  pl.BlockSpec((D, NW), lambda i: (0, 0))],
        out_specs=(pl.BlockSpec((tm, lay["width"]), lambda i: (i, 0)), pl.BlockSpec((tm, 3 * ATTN_W), lambda i: (i, 0)),
                   pl.BlockSpec((tm, D), lambda i: (i, 0))),
        compiler_params=_params("parallel"),
    )(x, g, w)


def _window_sums(xh, forward):
    n = xh.shape[0]
    sums, s, step = [], xh, 1
    for _ in POOL_WINDOWS:
        s = s + pltpu.roll(s, step if forward else n - step, 0)
        sums.append(s)
        step *= 2
    return sums


def _pooled(u, halo, row):
    sums = _window_sums(jnp.concatenate([halo, u], axis=0), True)
    out = []
    for g, w in enumerate(POOL_WINDOWS):
        cols = slice(g * POOL_G, (g + 1) * POOL_G)
        cnt = jnp.minimum(row + 1, w).astype(F32)
        out.append(sums[g][HALO:, cols] / cnt - u[:, cols])
    return out


def _pool_fwd(proj, pool_w, scale, R):
    T = proj.shape[0]

    def body(u_ref, z_ref, pw_ref, sc_ref, y_ref, halo_sc):
        i = pl.program_id(0)

        @pl.when(i == 0)
        def _():
            halo_sc[...] = jnp.zeros_like(halo_sc)

        u = u_ref[...]
        row = i * R + lax.broadcasted_iota(jnp.int32, (R, 1), 0)
        pooled = _pooled(u, halo_sc[...], row)
        mixed = jnp.concatenate(
            [jnp.dot(pooled[g].astype(BF16), pw_ref[g].astype(BF16), preferred_element_type=F32)
             for g in range(len(POOL_WINDOWS))], axis=1)
        z = z_ref[...]
        y_ref[...] = ((mixed * sc_ref[...]) * (z * _sigmoid(z))).astype(BF16)
        halo_sc[...] = u[R - HALO:, :]

    return _pc(
        body, name="pool_fwd", grid=(T // R,), out_shape=jax.ShapeDtypeStruct((T, POOL_W), BF16),
        in_specs=[pl.BlockSpec((R, POOL_W), lambda i: (i, 0)), pl.BlockSpec((R, POOL_W), lambda i: (i, 1)),
                  pl.BlockSpec((4, POOL_G, POOL_G), lambda i: (0, 0, 0)), pl.BlockSpec((1, POOL_W), lambda i: (0, 0))],
        out_specs=pl.BlockSpec((R, POOL_W), lambda i: (i, 0)),
        scratch_shapes=[pltpu.VMEM((HALO, POOL_W), F32)],
        compiler_params=_params("arbitrary"),
    )(proj, proj, pool_w, scale)


def _softplus(l):
    return jnp.maximum(l, 0.0) + jnp.log(1.0 + jnp.exp(-jnp.abs(l)))


def _first_last_step(n0, n1):
    return (lambda: jnp.logical_and(pl.program_id(0) == 0, pl.program_id(1) == 0),
            lambda: jnp.logical_and(pl.program_id(0) == n0 - 1, pl.program_id(1) == n1 - 1))


def _head_lanes():
    lane = lax.broadcasted_iota(jnp.int32, (1, LANE), 1)
    return [lane < HEAD_DIM, lane >= HEAD_DIM]


def _head_masks(q, scale):
    qf = q.astype(F32) * scale
    return [jnp.where(m, qf, 0.0).astype(BF16) for m in _head_lanes()]


def _wide(c, width):
    return jnp.concatenate([c] * (width // LANE), axis=1)


def _max_row_norm2(x, heads):
    sq = x.astype(F32) * x.astype(F32)
    return [jnp.max(jnp.sum(jnp.where(m, sq, 0.0), axis=1, keepdims=True), axis=0, keepdims=True) for m in heads]


def _fill_blocks(src_ref, nk, tk, transposed_sc=None, masked_sc=None, norm_sc=None):
    heads = _head_lanes()
    if norm_sc is not None:
        for a in range(2):
            norm_sc[a][...] = jnp.zeros_like(norm_sc[a])

    def step(j, carry):
        rows = pl.ds(pl.multiple_of(j * tk, tk), tk)
        blk = src_ref[rows, :]
        if norm_sc is not None:
            for a, n2 in enumerate(_max_row_norm2(blk, heads)):
                norm_sc[a][...] = jnp.maximum(norm_sc[a][...], n2)
        if transposed_sc is not None:
            transposed_sc[j] = blk.astype(F32).T.astype(BF16)
        if masked_sc is not None:
            for a in range(2):
                masked_sc[a, rows, :] = jnp.where(heads[a], blk, jnp.zeros_like(blk))
        return carry

    lax.fori_loop(0, nk, step, 0)


def _attn_fwd(projb, proj, tq, rider=None):
    T = projb.shape[0]
    nq = T // tq
    tk, nk = tq, nq
    assert nk <= LANE
    n_pairs = ATTN_W // LANE
    zb = (proj.shape[1] - ATTN_W) // LANE

    def body(q_ref, k_ref, v_ref, za_ref, o_ref, y_ref, c_ref, ws_ref, bs_ref, kT_sc, vm_sc, *per_head):
        c_sc, cm_sc, o_sc, kn_sc, l_sc, w_sc = (per_head[2 * n:2 * n + 2] for n in range(6))
        i = pl.program_id(1)

        @pl.when(i == 0)
        def _():
            _fill_blocks(k_ref, nk, tk, transposed_sc=kT_sc, norm_sc=kn_sc)
            _fill_blocks(v_ref, nk, tk, masked_sc=vm_sc)

        qs = _head_masks(q_ref[...], HEAD_DIM ** -0.5)
        lane = lax.broadcasted_iota(jnp.int32, (1, LANE), 1)
        valid = lax.broadcasted_iota(jnp.int32, (tq, tk), 1) < lax.broadcasted_iota(jnp.int32, (tq, tk), 0)
        suffix = (lax.broadcasted_iota(jnp.int32, (tk, tk), 0) >= lax.broadcasted_iota(jnp.int32, (tk, tk), 1)).astype(BF16)
        for a in range(2):
            c_sc[a][...] = jnp.zeros_like(c_sc[a])
            o_sc[a][...] = jnp.zeros_like(o_sc[a])
            cm_sc[a][...] = jnp.full(cm_sc[a].shape, NO_CARRY, F32)
        l_max = [jnp.sqrt(qn * kn_sc[a][...]) for a, qn in enumerate(_max_row_norm2(qs[0] + qs[1], _head_lanes()))]

        def logits(j, slot):
            kT = kT_sc[j]
            for a in range(2):
                l_sc[a][slot] = jnp.dot(qs[a], kT, preferred_element_type=F32)

        def values(j, slot):
            rows = pl.ds(pl.multiple_of(j * tk, tk), tk)
            for a in range(2):
                o_sc[a][...] += jnp.dot(w_sc[a][slot], vm_sc[a, rows, :], preferred_element_type=F32)

        def softplus(slot, masked):
            out = []
            for a in range(2):
                x = _softplus(l_sc[a][slot])
                out.append(jnp.where(valid, x, 0.0) if masked else x)
            return out

        def finish_weights(j, slot, sp, inc, mask, keep=None, cols=None):
            for a in range(2):
                c = c_sc[a][...]
                l = l_sc[a][slot]
                w = jnp.exp(l - inc[a] - _wide(c, tk))
                if mask is not None:
                    w = jnp.where(mask, w, 0.0)
                add = inc[a][:, 0:1]
                if keep is not None:
                    w = jnp.where(keep, w, 0.0)
                    add = jnp.where(keep, add, 0.0)
                w_sc[a][slot] = w.astype(BF16)
                if cols is not None:
                    beta = jnp.exp(l - sp[a])
                    ws_ref[a, 0, :, cols] = w
                    bs_ref[a, 0, :, cols] = beta if mask is None else jnp.where(mask, beta, 0.0)
                cm_sc[a][...] = jnp.where(lane == j, c, cm_sc[a][...])
                c_sc[a][...] = c + add

        def weights(j, slot):
            sp = softplus(slot, False)
            inc = [jnp.dot(sp[a].astype(BF16), suffix, preferred_element_type=F32) for a in range(2)]
            finish_weights(j, slot, sp, inc, None)

        def more():
            live = [jnp.min(c_sc[a][...], axis=0, keepdims=True) - l_max[a][0:1, :] <= ZERO_WEIGHT for a in range(2)]
            return jnp.max(jnp.where(jnp.logical_or(live[0], live[1]), 1, 0))

        logits(i, 0)
        logits(jnp.maximum(i - 1, 0), 1)
        sp = softplus(0, True) + softplus(1, False)
        inc = [jnp.dot(x.astype(BF16), suffix, preferred_element_type=F32) for x in sp]
        finish_weights(i, 0, sp[:2], inc[:2], valid, cols=slice(tk, 2 * tk))
        finish_weights(i - 1, 1, sp[2:], inc[2:], None, keep=i > 0, cols=slice(0, tk))
        values(i, 0)
        logits(jnp.maximum(i - 2, 0), 0)

        def step(t, slot):
            logits(jnp.maximum(i - t - 1, 0), 1 - slot)
            values(i - t + 1, 1 - slot)
            weights(i - t, slot)

        def two_steps(carry):
            tt, _ = carry
            step(2 * tt + 2, 0)
            step(2 * tt + 3, 1)
            return tt + 1, more()

        pairs, go = lax.while_loop(lambda c: jnp.logical_and(2 * c[0] + 3 <= i, c[1] > 0), two_steps, (0, more()))
        done = 1 + 2 * pairs
        one_more = jnp.logical_and(done + 1 == i, go > 0)

        @pl.when(one_more)
        def _():
            step(i, 0)
            values(0, 0)

        @pl.when(jnp.logical_not(one_more))
        def _():
            values(jnp.maximum(i - done, 0), 1)

        o = o_sc[0][...] + o_sc[1][...]
        o_ref[...] = o
        za = za_ref[...]
        y_ref[...] = (o * (za * _sigmoid(za))).astype(BF16)
        c_ref[0, 0] = cm_sc[0][...]
        c_ref[1, 0] = cm_sc[1][...]

    scratch = ([pltpu.VMEM((nk, LANE, tk), BF16), pltpu.VMEM((2, T, LANE), BF16)]
               + [pltpu.VMEM((tq, LANE), F32)] * 6 + [pltpu.VMEM((8, LANE), F32)] * 2
               + [pltpu.VMEM((2, tq, tk), F32)] * 2 + [pltpu.VMEM((2, tq, tk), BF16)] * 2)
    r_in, r_in_specs, r_out, r_out_specs, r_sems = _rider_specs(rider)
    body = _with_rider(body, 4, 5, len(scratch), rider, *_first_last_step(n_pairs, nq))
    kept = jax.ShapeDtypeStruct((2 * n_pairs, nq, tq, 2 * tk), F32)
    kept_spec = pl.BlockSpec((2, 1, tq, 2 * tk), lambda p, i: (p, i, 0, 0))
    return _pc(
        body, name="attn_fwd", grid=(n_pairs, nq),
        out_shape=tuple([jax.ShapeDtypeStruct((T, ATTN_W), F32), jax.ShapeDtypeStruct((T, ATTN_W), BF16),
                         jax.ShapeDtypeStruct((2 * n_pairs, nq, tq, LANE), F32), kept, kept] + r_out),
        in_specs=[pl.BlockSpec((tq, LANE), lambda p, i: (i, p)),
                  pl.BlockSpec((T, LANE), lambda p, i: (0, n_pairs + p)),
                  pl.BlockSpec((T, LANE), lambda p, i: (0, 2 * n_pairs + p)),
                  pl.BlockSpec((tq, LANE), lambda p, i: (i, zb + p))] + r_in_specs,
        out_specs=tuple([pl.BlockSpec((tq, LANE), lambda p, i: (i, p)), pl.BlockSpec((tq, LANE), lambda p, i: (i, p)),
                         pl.BlockSpec((2, 1, tq, LANE), lambda p, i: (p, i, 0, 0)), kept_spec, kept_spec] + r_out_specs),
        scratch_shapes=scratch + r_sems,
        compiler_params=_params("arbitrary", "arbitrary"),
    )(projb, projb, projb, proj, *r_in)


def _gates(gl0, gl1, bg, D):
    return _sigmoid(gl0 + bg[:, :D]), _sigmoid(gl1 + bg[:, D:])


def _merge_fwd(y_pool, y_attn, w_pu, w_au, proj, b_gate, tm):
    T = y_pool.shape[0]
    D = w_pu.shape[1]
    gb = _proj_layout(D)["gates"] // D

    def body(yp_ref, ya_ref, wpu_ref, wau_ref, gl0_ref, gl1_ref, bg_ref, m_ref):
        p = jnp.dot(yp_ref[...], wpu_ref[...], preferred_element_type=F32)
        a = jnp.dot(ya_ref[...], wau_ref[...], preferred_element_type=F32)
        g0, g1 = _gates(gl0_ref[...], gl1_ref[...], bg_ref[...], D)
        m_ref[...] = (g0 * p + g1 * a).astype(BF16)

    row = lambda i: (i, 0)
    fixed = lambda i: (0, 0)
    return _pc(
        body, name="merge_fwd", grid=(T // tm,), out_shape=jax.ShapeDtypeStruct((T, D), BF16),
        in_specs=[pl.BlockSpec((tm, POOL_W), row), pl.BlockSpec((tm, ATTN_W), row),
                  pl.BlockSpec((POOL_W, D), fixed), pl.BlockSpec((ATTN_W, D), fixed),
                  pl.BlockSpec((tm, D), lambda i: (i, gb)), pl.BlockSpec((tm, D), lambda i: (i, gb + 1)),
                  pl.BlockSpec((1, 2 * D), fixed)],
        out_specs=pl.BlockSpec((tm, D), row),
        compiler_params=_params("parallel"),
    )(y_pool, y_attn, w_pu, w_au, proj, proj, b_gate)


def _final_loss(x, g, target, tm):
    T, D = x.shape

    def body(x_ref, g_ref, t_ref, dx_ref, dg_ref, loss_ref):
        @pl.when(pl.program_id(0) == 0)
        def _():
            dg_ref[...] = jnp.zeros_like(dg_ref)
            loss_ref[...] = jnp.zeros_like(loss_ref)

        xv, gv = x_ref[...], g_ref[...]
        r = lax.rsqrt(jnp.mean(xv * xv, axis=-1, keepdims=True) + RMS_EPS)
        xh = xv * r
        d = xh * gv - t_ref[...]
        loss_ref[...] += 0.5 * jnp.sum(jnp.mean(d * d, axis=-1, keepdims=True), axis=0, keepdims=True)
        dy = d * (1.0 / D)
        dg_ref[...] += jnp.sum(dy * xh, axis=0, keepdims=True)
        dh = dy * gv
        dx_ref[...] = r * (dh - xh * jnp.mean(dh * xh, axis=-1, keepdims=True))

    return _pc(
        body, name="final_loss", grid=(T // tm,),
        out_shape=(jax.ShapeDtypeStruct((T, D), F32), jax.ShapeDtypeStruct((1, D), F32),
                   jax.ShapeDtypeStruct((8, LANE), F32)),
        in_specs=[pl.BlockSpec((tm, D), lambda i: (i, 0)), pl.BlockSpec((1, D), lambda i: (0, 0)),
                  pl.BlockSpec((tm, D), lambda i: (i, 0))],
        out_specs=(pl.BlockSpec((tm, D), lambda i: (i, 0)), pl.BlockSpec((1, D), lambda i: (0, 0)),
                   pl.BlockSpec((8, LANE), lambda i: (0, 0))),
        compiler_params=_params("arbitrary"),
    )(x, g, target)


def _merge_bwd(dxo, w_out, y_pool, y_attn, w_pu, w_au, proj, b_gate, tm):
    T, D = dxo.shape
    gb = _proj_layout(D)["gates"] // D

    def body(dxo_ref, wout_ref, yp_ref, ya_ref, wpu_ref, wau_ref, gl0_ref, gl1_ref, bg_ref,
             dp_ref, da_ref, dgl_ref, dbg_ref):
        @pl.when(pl.program_id(0) == 0)
        def _():
            dbg_ref[...] = jnp.zeros_like(dbg_ref)

        dmv = lax.dot_general(dxo_ref[...].astype(BF16), wout_ref[...], NT_DIMS, preferred_element_type=F32)
        p = jnp.dot(yp_ref[...], wpu_ref[...], preferred_element_type=F32)
        a = jnp.dot(ya_ref[...], wau_ref[...], preferred_element_type=F32)
        g0, g1 = _gates(gl0_ref[...], gl1_ref[...], bg_ref[...], D)
        dp_ref[...] = (dmv * g0).astype(BF16)
        da_ref[...] = (dmv * g1).astype(BF16)
        dgl0 = dmv * p * (g0 * (1.0 - g0))
        dgl1 = dmv * a * (g1 * (1.0 - g1))
        dgl_ref[:, :D] = dgl0.astype(BF16)
        dgl_ref[:, D:] = dgl1.astype(BF16)
        dbg_ref[:, :D] += jnp.sum(dgl0, axis=0, keepdims=True)
        dbg_ref[:, D:] += jnp.sum(dgl1, axis=0, keepdims=True)

    row = lambda i: (i, 0)
    fixed = lambda i: (0, 0)
    return _pc(
        body, name="merge_bwd", grid=(T // tm,),
        out_shape=(jax.ShapeDtypeStruct((T, D), BF16), jax.ShapeDtypeStruct((T, D), BF16),
                   jax.ShapeDtypeStruct((T, 2 * D), BF16), jax.ShapeDtypeStruct((1, 2 * D), F32)),
        in_specs=[pl.BlockSpec((tm, D), row), pl.BlockSpec((D, D), fixed),
                  pl.BlockSpec((tm, POOL_W), row), pl.BlockSpec((tm, ATTN_W), row),
                  pl.BlockSpec((POOL_W, D), fixed), pl.BlockSpec((ATTN_W, D), fixed),
                  pl.BlockSpec((tm, D), lambda i: (i, gb)), pl.BlockSpec((tm, D), lambda i: (i, gb + 1)),
                  pl.BlockSpec((1, 2 * D), fixed)],
        out_specs=(pl.BlockSpec((tm, D), row), pl.BlockSpec((tm, D), row), pl.BlockSpec((tm, 2 * D), row),
                   pl.BlockSpec((1, 2 * D), fixed)),
        compiler_params=_params("arbitrary"),
    )(dxo, w_out, y_pool, y_attn, w_pu, w_au, proj, proj, b_gate)


def _pool_bwd(proj, dp, w_pu, pool_w, scale, R):
    T = proj.shape[0]
    D = w_pu.shape[1]
    nb = T // R
    hb = R // HALO

    def body(u_ref, up_ref, z_ref, dp_ref, wpu_ref, pw_ref, sc_ref, du_ref, dz_ref, dpw_ref, dsc_ref, halo_sc):
        i = pl.program_id(0)
        rb = nb - 1 - i

        @pl.when(i == 0)
        def _():
            halo_sc[...] = jnp.zeros_like(halo_sc)
            dpw_ref[...] = jnp.zeros_like(dpw_ref)
            dsc_ref[...] = jnp.zeros_like(dsc_ref)

        u = u_ref[...]
        row = rb * R + lax.broadcasted_iota(jnp.int32, (R, 1), 0)
        before = jnp.where(rb > 0, up_ref[...], 0.0)
        pooled = _pooled(u, before, row)
        pw = [pw_ref[g].astype(BF16) for g in range(len(POOL_WINDOWS))]
        mixed = jnp.concatenate(
            [jnp.dot(pooled[g].astype(BF16), pw[g], preferred_element_type=F32) for g in range(len(POOL_WINDOWS))],
            axis=1)
        sc = sc_ref[...]
        silu, dsilu = _silu_and_grad(z_ref[...])
        dyv = lax.dot_general(dp_ref[...], wpu_ref[...], NT_DIMS, preferred_element_type=F32)
        dmp = dyv * silu
        dz_ref[...] = (dyv * (mixed * sc) * dsilu).astype(BF16)
        dsc_ref[...] += jnp.sum(dmp * mixed, axis=0, keepdims=True)
        dmixed = (dmp * sc).astype(BF16)
        dpn = []
        dpooled = []
        for g, w in enumerate(POOL_WINDOWS):
            cols = slice(g * POOL_G, (g + 1) * POOL_G)
            dpw_ref[g] += lax.dot_general(pooled[g].astype(BF16), dmixed[:, cols], TN_DIMS,
                                          preferred_element_type=F32)
            dpg = lax.dot_general(dmixed[:, cols], pw[g], NT_DIMS, preferred_element_type=F32)
            dpooled.append(dpg)
            dpn.append(dpg / jnp.minimum(row + 1, w).astype(F32))
        dpn = jnp.concatenate(dpn, axis=1)
        sums = _window_sums(jnp.concatenate([dpn, halo_sc[...]], axis=0), False)
        du_ref[...] = jnp.concatenate(
            [sums[g][:R, g * POOL_G:(g + 1) * POOL_G] - dpooled[g] for g in range(len(POOL_WINDOWS))],
            axis=1).astype(BF16)
        halo_sc[...] = dpn[:HALO, :]

    rev = lambda i: (nb - 1 - i, 0)
    return _pc(
        body, name="pool_bwd", grid=(nb,),
        out_shape=(jax.ShapeDtypeStruct((T, POOL_W), BF16), jax.ShapeDtypeStruct((T, POOL_W), BF16),
                   jax.ShapeDtypeStruct((4, POOL_G, POOL_G), F32), jax.ShapeDtypeStruct((1, POOL_W), F32)),
        in_specs=[pl.BlockSpec((R, POOL_W), rev),
                  pl.BlockSpec((HALO, POOL_W), lambda i: (jnp.maximum((nb - 1 - i) * hb - 1, 0), 0)),
                  pl.BlockSpec((R, POOL_W), lambda i: (nb - 1 - i, 1)),
                  pl.BlockSpec((R, D), rev), pl.BlockSpec((POOL_W, D), lambda i: (0, 0)),
                  pl.BlockSpec((4, POOL_G, POOL_G), lambda i: (0, 0, 0)), pl.BlockSpec((1, POOL_W), lambda i: (0, 0))],
        out_specs=(pl.BlockSpec((R, POOL_W), rev), pl.BlockSpec((R, POOL_W), rev),
                   pl.BlockSpec((4, POOL_G, POOL_G), lambda i: (0, 0, 0)), pl.BlockSpec((1, POOL_W), lambda i: (0, 0))),
        scratch_shapes=[pltpu.VMEM((HALO, POOL_W), F32)],
        compiler_params=_params("arbitrary"),
    )(proj, proj, proj, dp, w_pu, pool_w, scale)


def _attn_gate_bwd(da, w_au, o, proj, tm):
    T, D = da.shape
    zb = (proj.shape[1] - ATTN_W) // ATTN_W

    def body(da_ref, wau_ref, o_ref, za_ref, do_ref, dza_ref):
        silu, dsilu = _silu_and_grad(za_ref[...])
        dyv = lax.dot_general(da_ref[...], wau_ref[...], NT_DIMS, preferred_element_type=F32)
        do_ref[...] = (dyv * silu).astype(BF16)
        dza_ref[...] = (dyv * o_ref[...] * dsilu).astype(BF16)

    row = lambda i: (i, 0)
    return _pc(
        body, name="attn_gate_bwd", grid=(T // tm,),
        out_shape=(jax.ShapeDtypeStruct((T, ATTN_W), BF16), jax.ShapeDtypeStruct((T, ATTN_W), BF16)),
        in_specs=[pl.BlockSpec((tm, D), row), pl.BlockSpec((ATTN_W, D), lambda i: (0, 0)),
                  pl.BlockSpec((tm, ATTN_W), row), pl.BlockSpec((tm, ATTN_W), lambda i: (i, zb))],
        out_specs=(pl.BlockSpec((tm, ATTN_W), row), pl.BlockSpec((tm, ATTN_W), row)),
        compiler_params=_params("parallel"),
    )(da, w_au, o, proj)


def _attn_bwd(projb, do, carries, kept_w, kept_beta, tq, rider=None):
    T = projb.shape[0]
    nq = T // tq
    tk, nk = tq, nq
    n_pairs = ATTN_W // LANE
    scale = HEAD_DIM ** -0.5

    def body(q_ref, k_ref, v_ref, do_ref, c_ref, ws_ref, bs_ref, dq_ref, dk_ref, dv_ref, kT_sc, vT_sc, km_sc, dkT_ref, dvT_ref,
             *per_head):
        f_sc, dq_sc, kn_sc, l_sc, dw_sc, dl_sc, w_sc = (per_head[2 * n:2 * n + 2] for n in range(7))
        i = pl.program_id(1)

        @pl.when(i == 0)
        def _():
            _fill_blocks(k_ref, nk, tk, transposed_sc=kT_sc, masked_sc=km_sc, norm_sc=kn_sc)
            _fill_blocks(v_ref, nk, tk, transposed_sc=vT_sc)
            dkT_ref[...] = jnp.zeros_like(dkT_ref)
            dvT_ref[...] = jnp.zeros_like(dvT_ref)

        qs = _head_masks(q_ref[...], scale)
        dos = _head_masks(do_ref[...], 1.0)
        qT = [x.astype(F32).T.astype(BF16) for x in qs]
        doT = [x.astype(F32).T.astype(BF16) for x in dos]
        lane = lax.broadcasted_iota(jnp.int32, (1, LANE), 1)
        valid = lax.broadcasted_iota(jnp.int32, (tq, tk), 1) < lax.broadcasted_iota(jnp.int32, (tq, tk), 0)
        kk0 = lax.broadcasted_iota(jnp.int32, (tk, tk), 0)
        kk1 = lax.broadcasted_iota(jnp.int32, (tk, tk), 1)
        suffix = (kk0 >= kk1).astype(BF16)
        prefix = (kk0 <= kk1).astype(BF16)
        for a in range(2):
            f_sc[a][...] = jnp.zeros_like(f_sc[a])
            dq_sc[a][...] = jnp.zeros_like(dq_sc[a])
        live = lane == i
        for a, qn in enumerate(_max_row_norm2(qs[0] + qs[1], _head_lanes())):
            l_max = jnp.sqrt(qn * kn_sc[a][0:1, :])
            live = jnp.logical_or(live, jnp.min(c_ref[a, 0], axis=0, keepdims=True) - l_max <= ZERO_WEIGHT)
        t0 = jnp.min(jnp.where(jnp.logical_and(live, lane <= i), lane, i))
        n = i - t0

        def products(j, slot, with_logits=True):
            kT = kT_sc[j]
            vT = vT_sc[j]
            for a in range(2):
                if with_logits:
                    l_sc[a][slot] = jnp.dot(qs[a], kT, preferred_element_type=F32)
                dw_sc[a][slot] = jnp.dot(dos[a], vT, preferred_element_type=F32)

        def gradients(j, slot):
            rows = pl.ds(pl.multiple_of(j * tk, tk), tk)
            dkT = []
            dvT = []
            for a in range(2):
                dlb = dl_sc[a][slot]
                dq_sc[a][...] += jnp.dot(dlb, km_sc[a, rows, :], preferred_element_type=F32)
                dkT.append(jnp.dot(qT[a], dlb, preferred_element_type=F32))
                dvT.append(jnp.dot(doT[a], w_sc[a][slot], preferred_element_type=F32))
            dkT_ref[j] += dkT[0] + dkT[1]
            dvT_ref[j] += dvT[0] + dvT[1]

        def elementwise(j, slot, masked):
            sp, inc, e, beta, p = [None] * 2, [None] * 2, [None] * 2, [None] * 2, [None] * 2
            for a in range(2):
                x = _softplus(l_sc[a][slot])
                sp[a] = jnp.where(valid, x, 0.0) if masked else x
            for a in range(2):
                inc[a] = jnp.dot(sp[a].astype(BF16), suffix, preferred_element_type=F32)
            for a in range(2):
                l = l_sc[a][slot]
                c = jnp.sum(jnp.where(lane == j, c_ref[a, 0], 0.0), axis=1, keepdims=True)
                w = jnp.exp(l - inc[a] - c)
                if masked:
                    w = jnp.where(valid, w, 0.0)
                w_sc[a][slot] = w.astype(BF16)
                beta[a] = jnp.exp(l - sp[a])
                e[a] = w * dw_sc[a][slot]
            for a in range(2):
                p[a] = jnp.dot(e[a].astype(BF16), prefix, preferred_element_type=F32)
            for a in range(2):
                f = f_sc[a][...]
                dl = e[a] - beta[a] * (p[a] + _wide(f, tk))
                if masked:
                    dl = jnp.where(valid, dl, 0.0)
                dl_sc[a][slot] = dl.astype(BF16)
                f_sc[a][...] = f + p[a][:, tk - 1:tk]

        def kept_blocks(tiles):
            keys = [(n_, a) for n_ in range(len(tiles)) for a in range(2)]
            for j, slot, cols in tiles:
                products(j, slot, with_logits=False)
            e, p = {}, {}
            for n_, a in keys:
                j, slot, cols = tiles[n_]
                w = ws_ref[a, 0, :, cols]
                w_sc[a][slot] = w.astype(BF16)
                e[n_, a] = w * dw_sc[a][slot]
            for k in keys:
                p[k] = jnp.dot(e[k].astype(BF16), prefix, preferred_element_type=F32)
            for a in range(2):
                f = f_sc[a][...]
                for n_, (j, slot, cols) in enumerate(tiles):
                    dl = e[n_, a] - bs_ref[a, 0, :, cols] * (p[n_, a] + _wide(f, tk))
                    dl_sc[a][slot] = dl.astype(BF16)
                    f = f + p[n_, a][:, tk - 1:tk]
                f_sc[a][...] = f

        def step(r, slot):
            products(t0 + r + 1, 1 - slot)
            gradients(t0 + jnp.maximum(r - 1, 0), 1 - slot)
            elementwise(t0 + r, slot, False)

        def last_two(slot, pending=True):
            if pending:
                gradients(i - 2, slot)
            kept_blocks([(i - 1, 1 - slot, slice(0, tk)), (i, slot, slice(tk, 2 * tk))])
            gradients(i - 1, 1 - slot)
            gradients(i, slot)

        @pl.when(n >= 2)
        def _():
            for a in range(2):
                dl_sc[a][1] = jnp.zeros((tq, tk), BF16)
                w_sc[a][1] = jnp.zeros((tq, tk), BF16)
            products(t0, 0)

        def two_steps(tt, carry):
            step(2 * tt, 0)
            step(2 * tt + 1, 1)
            return carry

        lax.fori_loop(0, jnp.maximum(n - 1, 0) // 2, two_steps, 0)

        @pl.when(n == 0)
        def _():
            kept_blocks([(i, 0, slice(tk, 2 * tk))])
            gradients(i, 0)

        @pl.when(n == 1)
        def _():
            last_two(1, pending=False)

        @pl.when(jnp.logical_and(n > 1, n % 2 == 1))
        def _():
            last_two(1)

        @pl.when(jnp.logical_and(n > 0, n % 2 == 0))
        def _():
            step(n - 2, 0)
            last_two(0)

        dq_ref[...] = ((dq_sc[0][...] + dq_sc[1][...]) * scale).astype(BF16)

        @pl.when(i == nq - 1)
        def _():
            def untranspose(j, carry):
                rows = pl.ds(pl.multiple_of(j * tk, tk), tk)
                dk_ref[rows, :] = dkT_ref[j].T.astype(BF16)
                dv_ref[rows, :] = dvT_ref[j].T.astype(BF16)
                return carry

            lax.fori_loop(0, nk, untranspose, 0)

    scratch = ([pltpu.VMEM((nk, LANE, tk), BF16), pltpu.VMEM((nk, LANE, tk), BF16), pltpu.VMEM((2, T, LANE), BF16),
                pltpu.VMEM((nk, LANE, tk), F32), pltpu.VMEM((nk, LANE, tk), F32)]
               + [pltpu.VMEM((tq, LANE), F32)] * 4 + [pltpu.VMEM((8, LANE), F32)] * 2
               + [pltpu.VMEM((2, tq, tk), F32)] * 4 + [pltpu.VMEM((2, tq, tk), BF16)] * 4)
    r_in, r_in_specs, r_out, r_out_specs, r_sems = _rider_specs(rider)
    body = _with_rider(body, 7, 3, len(scratch), rider, *_first_last_step(n_pairs, nq))
    kept_spec = pl.BlockSpec((2, 1, tq, 2 * tk), lambda p, i: (p, i, 0, 0))
    return _pc(
        body, name="attn_bwd", grid=(n_pairs, nq),
        out_shape=tuple([jax.ShapeDtypeStruct((T, ATTN_W), BF16)] * 3 + r_out),
        in_specs=[pl.BlockSpec((tq, LANE), lambda p, i: (i, p)),
                  pl.BlockSpec((T, LANE), lambda p, i: (0, n_pairs + p)),
                  pl.BlockSpec((T, LANE), lambda p, i: (0, 2 * n_pairs + p)),
                  pl.BlockSpec((tq, LANE), lambda p, i: (i, p)),
                  pl.BlockSpec((2, 1, tq, LANE), lambda p, i: (p, i, 0, 0)), kept_spec, kept_spec] + r_in_specs,
        out_specs=tuple([pl.BlockSpec((tq, LANE), lambda p, i: (i, p)), pl.BlockSpec((T, LANE), lambda p, i: (0, p)),
                         pl.BlockSpec((T, LANE), lambda p, i: (0, p))] + r_out_specs),
        scratch_shapes=scratch + r_sems,
        compiler_params=_params("arbitrary", "arbitrary"),
    )(projb, projb, projb, do, carries, kept_w, kept_beta, *r_in)


def _dh_norm_bwd(segs, w, x, g, dxo, tm, rider=None):
    T, D = x.shape
    nm = T // tm
    n_seg = len(segs)
    offs = [sum(y.shape[1] for y in segs[:n]) for n in range(n_seg + 1)]
    assert offs[-1] == w.shape[1]

    def body(*refs):
        seg_refs = refs[:n_seg]
        w_ref, x_ref, g_ref, dxo_ref, dx_ref, dg_ref = refs[n_seg:]

        @pl.when(pl.program_id(0) == 0)
        def _():
            dg_ref[...] = jnp.zeros_like(dg_ref)

        dhv = None
        for n in range(n_seg):
            part = lax.dot_general(seg_refs[n][...], w_ref[:, offs[n]:offs[n + 1]], NT_DIMS, preferred_element_type=F32)
            dhv = part if dhv is None else dhv + part
        xv = x_ref[...]
        r = lax.rsqrt(jnp.mean(xv * xv, axis=-1, keepdims=True) + RMS_EPS)
        xh = xv * r
        dg_ref[...] += jnp.sum(dhv * xh, axis=0, keepdims=True)
        dhg = dhv * g_ref[...]
        dx_ref[...] = dxo_ref[...] + r * (dhg - xh * jnp.mean(dhg * xh, axis=-1, keepdims=True))

    r_in, r_in_specs, r_out, r_out_specs, r_sems = _rider_specs(rider)
    body = _with_rider(body, n_seg + 4, 2, 0, rider, lambda: pl.program_id(0) == 0, lambda: pl.program_id(0) == nm - 1)
    row = lambda i: (i, 0)
    fixed = lambda i: (0, 0)
    return _pc(
        body, name="d_h_norm_bwd", grid=(nm,),
        out_shape=tuple([jax.ShapeDtypeStruct((T, D), F32), jax.ShapeDtypeStruct((1, D), F32)] + r_out),
        in_specs=[pl.BlockSpec((tm, y.shape[1]), row) for y in segs]
        + [pl.BlockSpec(w.shape, fixed), pl.BlockSpec((tm, D), row), pl.BlockSpec((1, D), fixed), pl.BlockSpec((tm, D), row)]
        + r_in_specs,
        out_specs=tuple([pl.BlockSpec((tm, D), row), pl.BlockSpec((1, D), fixed)] + r_out_specs),
        scratch_shapes=r_sems,
        compiler_params=_params("arbitrary"),
    )(*segs, w, x, g, dxo, *r_in)


def _adamw(pieces, w, m, v, name):
    G = len(pieces)
    rows, cols = pieces[0].shape[1:]
    assert w.shape == (G * rows, cols)
    br = rows
    while br * cols > 65536 and br % 16 == 0:
        br //= 2
    nb = rows // br
    c1 = 1.0 / (1.0 - ADAM_B1 ** ADAM_STEP)
    c2 = 1.0 / (1.0 - ADAM_B2 ** ADAM_STEP)

    def body(*refs):
        p_refs = refs[:G]
        w_ref, m_ref, v_ref, g_ref, d_ref, nm_ref, nv_ref = refs[G:]
        g = None
        for n, p_ref in enumerate(p_refs):
            gn = p_ref[0].astype(F32)
            for s in range(1, N_DEV):
                gn = gn + p_ref[s].astype(F32)
            g = gn if g is None else jnp.where(pl.program_id(0) == n, gn, g)
        nm = ADAM_B1 * m_ref[...] + (1.0 - ADAM_B1) * g
        nv = ADAM_B2 * v_ref[...] + (1.0 - ADAM_B2) * (g * g)
        g_ref[...] = g
        nm_ref[...] = nm
        nv_ref[...] = nv
        d_ref[...] = -ADAM_LR * ((nm * c1) / (jnp.sqrt(nv * c2) + ADAM_EPS) + ADAM_WD * w_ref[...])

    blk = pl.BlockSpec((br, cols), lambda n, i: (n * nb + i, 0))
    shape = jax.ShapeDtypeStruct((G * rows, cols), F32)
    return _pc(
        body, name=name, grid=(G, nb), out_shape=(shape, shape, shape, shape),
        in_specs=[pl.BlockSpec((N_DEV, br, cols), lambda n, i: (0, i, 0))] * G + [blk, blk, blk],
        out_specs=(blk, blk, blk, blk),
        compiler_params=_params("parallel", "parallel"),
    )(*pieces, w, m, v)


def _rows128(a):
    flat = a.reshape(-1)
    n = flat.shape[0]
    padded = -(-n // (8 * LANE)) * (8 * LANE)
    if padded != n:
        flat = jnp.concatenate([flat, jnp.zeros((padded - n,), flat.dtype)])
    return flat.reshape(-1, LANE)


def _pack(parts):
    return jnp.concatenate([_rows128(p) for p in parts], axis=0)


def _unpack(packed, like):
    out, r = [], 0
    for a in like:
        n = a.size
        nr = -(-n // (8 * LANE)) * 8
        out.append(packed[r:r + nr].reshape(-1)[:n].reshape(a.shape))
        r += nr
    return out


def kernel(x, norm_g, w_in, b_gate, pool_w, pool_scale, w_pool_up, w_attn_up, w_out, final_g, loss_target, m_norm_g, m_w_in, m_b_gate, m_pool_w, m_pool_scale, m_w_pool_up, m_w_attn_up, m_w_out, m_final_g, v_norm_g, v_w_in, v_b_gate, v_pool_w, v_pool_scale, v_w_pool_up, v_w_attn_up, v_w_out, v_final_g):
    L = norm_g.shape[0]
    T, D = x.shape[1], x.shape[2]
    NW = w_in.shape[2] * N_DEV
    assert NW == 2 * POOL_W + 4 * ATTN_W + 2 * D and x.shape[0] == 1
    tm = min(512, T)
    tq = min(256, T // 2)
    x0 = x.reshape(T, D)
    target = loss_target.reshape(T, D)

    assert L >= 2
    win_first = jnp.transpose(_gather_two_level(w_in[0].astype(BF16), "gather_w_in0"), (1, 0, 2)).reshape(D, NW)
    rest = [w_in[1:].astype(BF16), w_pool_up.astype(BF16), w_attn_up.astype(BF16), w_out.astype(BF16)]

    saved = []
    xl = x0
    for l in range(L):
        proj, projb, h = _norm_inproj(xl, norm_g[l:l + 1], win_first if l == 0 else win_rest[l - 1], min(256, T))
        y_pool = _pool_fwd(proj, pool_w[l], pool_scale[l:l + 1], tm)
        if l == 0:
            o, y_attn, carries, kept_w, kept_b, g_in, g_pu, g_au, g_out = _attn_fwd(projb, proj, tq, rider=(rest, [False] * 4))
            win_rest = jnp.transpose(g_in, (1, 2, 0, 3)).reshape(L - 1, D, NW)
            wpu_full = jnp.transpose(g_pu, (1, 2, 0, 3)).reshape(L, POOL_W, D)
            wau_full = jnp.transpose(g_au, (1, 2, 0, 3)).reshape(L, ATTN_W, D)
            wout_full = jnp.transpose(g_out, (1, 0, 2, 3)).reshape(L, D, D)
        else:
            o, y_attn, carries, kept_w, kept_b = _attn_fwd(projb, proj, tq)
        merged = _merge_fwd(y_pool, y_attn, wpu_full[l], wau_full[l], proj, b_gate[l:l + 1], min(256, T))
        x_next = _mm_nn_res(merged, wout_full[l], xl, tm, "out_proj")
        saved.append((xl, proj, projb, h, y_pool, o, y_attn, (carries, kept_w, kept_b), merged))
        xl = x_next

    dx, d_final_g, loss_part = _final_loss(xl, final_g.reshape(1, D), target, tm)

    d_norm_g, d_b_gate, d_pool_w, d_pool_scale = [None] * L, [None] * L, [None] * L, [None] * L
    d_win, d_wpu, d_wau, d_wout = [None] * L, [None] * L, [None] * L, [None] * L
    small_like = [norm_g, b_gate, pool_w, pool_scale, final_g, jnp.zeros((8, LANE), F32)]
    for l in reversed(range(L)):
        xin, proj, projb, h, y_pool, o, y_attn, carries, merged = saved[l]
        win_l = win_first if l == 0 else win_rest[l - 1]
        d_wout[l] = _mm_tn(merged, dx, 1, D, min(1024, T), "d_w_out").reshape(N_DEV, D // N_DEV, D)
        dp, da, dgl, d_b_gate[l] = _merge_bwd(dx, wout_full[l], y_pool, y_attn, wpu_full[l], wau_full[l], proj,
                                              b_gate[l:l + 1], min(256, T))
        d_wpu[l] = _mm_tn(y_pool, dp, N_DEV, D, min(1024, T), "d_w_pool_up")
        d_wau[l] = _mm_tn(y_attn, da, N_DEV, D, min(1024, T), "d_w_attn_up")
        du, dzp, d_pool_w[l], d_pool_scale[l] = _pool_bwd(proj, dp, wpu_full[l], pool_w[l], pool_scale[l:l + 1], tm)
        do, dza = _attn_gate_bwd(da, wau_full[l], o, proj, tm)
        if l == 0:
            small = _pack([jnp.concatenate([jnp.zeros((1, D), F32)] + d_norm_g[1:], 0), jnp.concatenate(d_b_gate, 0),
                           jnp.stack(d_pool_w, 0), jnp.concatenate(d_pool_scale, 0), d_final_g, loss_part])
            early = d_win[1:] + d_wpu + d_wau + d_wout
            dq, dk, dv, got_small, *got_early = _attn_bwd(
                projb, do, *carries, tq, rider=([small] + early, [False] + [True] * len(early)))
        else:
            dq, dk, dv = _attn_bwd(projb, do, *carries, tq)
        segs = [du, dzp, dq, dk, dv, dza, dgl]
        sw = NW // N_DEV
        half = _mm_tn_segs(h, segs[5:], sw, min(1024, T), "d_w_in_b", into=jnp.zeros((N_DEV, D, sw), BF16), first_shard=N_DEV // 2)
        d_win[l] = _mm_tn_segs(h, segs[:5], sw, min(1024, T), "d_w_in_a", into=half)
        if l == 0:
            dx, d_norm_g[l], got_win0 = _dh_norm_bwd(segs, win_l, xin, norm_g[l:l + 1], dx, min(256, T),
                                                     rider=([d_win[0]], [True]))
        else:
            dx, d_norm_g[l] = _dh_norm_bwd(segs, win_l, xin, norm_g[l:l + 1], dx, min(256, T))

    got_norm0 = _exchange([d_norm_g[0].reshape(-1, LANE)], [False], "gather_norm_grad")[0]
    r_small = jnp.concatenate([got_norm0, got_small[:, D // LANE:]], axis=1)

    def update(pieces, w, m, v, name):
        cols = w.shape[-1]
        res = _adamw([p.reshape(N_DEV, -1, cols) for p in pieces], w.reshape(-1, cols), m.reshape(-1, cols),
                     v.reshape(-1, cols), name)
        return [r.reshape(w.shape) for r in res]

    u_in = update([got_win0] + got_early[:L - 1], w_in, m_w_in, v_w_in, "adamw_w_in")
    u_pu = update(got_early[L - 1:2 * L - 1], w_pool_up, m_w_pool_up, v_w_pool_up, "adamw_w_pool_up")
    u_au = update(got_early[2 * L - 1:3 * L - 1], w_attn_up, m_w_attn_up, v_w_attn_up, "adamw_w_attn_up")
    u_out = update(got_early[3 * L - 1:4 * L - 1], w_out, m_w_out, v_w_out, "adamw_w_out")
    zeros = small_like[-1]
    smalls = _adamw([r_small],
                    _pack([norm_g, b_gate, pool_w, pool_scale, final_g, zeros]),
                    _pack([m_norm_g, m_b_gate, m_pool_w, m_pool_scale, m_final_g, zeros]),
                    _pack([v_norm_g, v_b_gate, v_pool_w, v_pool_scale, v_final_g, zeros]), "adamw_small")
    s_g, s_d, s_m, s_v = [_unpack(s, small_like) for s in smalls]
    loss = s_g[5][0, 0]

    def ordered(k):
        s = (s_g, s_d, s_m, s_v)[k]
        return [s[0], u_in[k], s[1], s[2], s[3], u_pu[k], u_au[k], u_out[k], s[4]]

    return (loss, dx.reshape(x.shape), *ordered(0), *ordered(1), *ordered(2), *ordered(3))
```

```python
import jax
import jax.numpy as jnp
from jax import lax
from jax.experimental import pallas as pl
from jax.experimental.pallas import tpu as pltpu

F32 = jnp.float32
BF16 = jnp.bfloat16

N_DEV = 8
HEAD_DIM = 64
ATTN_W = 512
POOL_W = 512
POOL_G = 128
POOL_WINDOWS = (2, 4, 8, 16)
HALO = 16
LANE = 128
RMS_EPS = 1e-6
ZERO_WEIGHT = 110.0
NO_CARRY = 3.0e38
ADAM_LR, ADAM_B1, ADAM_B2, ADAM_EPS, ADAM_WD, ADAM_STEP = 0.001, 0.9, 0.999, 1e-08, 0.01, 10
VMEM_LIMIT = 56 * 1024 * 1024

NT_DIMS = (((1,), (1,)), ((), ()))
TN_DIMS = (((0,), (0,)), ((), ()))


def _pc(body, **kw):
    return pl.pallas_call(body, **kw)


def _params(*sem):
    return pltpu.CompilerParams(dimension_semantics=sem, vmem_limit_bytes=VMEM_LIMIT)


def _sigmoid(z):
    return 1.0 / (1.0 + jnp.exp(-z))


def _silu_and_grad(z):
    s = _sigmoid(z)
    return z * s, s * (1.0 + z * (1.0 - s))


def _my_index():
    return 4 * lax.axis_index("x") + 2 * lax.axis_index("y") + lax.axis_index("c")


def _peer(k):
    x, y, c = lax.axis_index("x"), lax.axis_index("y"), lax.axis_index("c")
    px = lax.rem(x + ((k >> 2) & 1), 2)
    py = lax.rem(y + ((k >> 1) & 1), 2)
    pc = lax.rem(c + (k & 1), 2)
    return (px, py, pc), 4 * px + 2 * py + pc


def _exchange_copies(ins, outs, sems, scatter):
    send_sems, recv_sems, local_sems = sems
    n = len(ins)
    me = _my_index()

    def src(a, idx):
        return ins[a].at[idx] if scatter[a] else ins[a]

    local = [pltpu.make_async_copy(src(a, me), outs[a].at[me], local_sems.at[a]) for a in range(n)]
    sends, arrivals = [], []
    for k in (1, 2, 4, 3, 5, 6, 7):
        dev, pidx = _peer(k)
        for a in range(n):
            sem = dict(send_sem=send_sems.at[a * N_DEV + k], recv_sem=recv_sems.at[a * N_DEV + k],
                       device_id=dev, device_id_type=pl.DeviceIdType.MESH)
            sends.append(pltpu.make_async_remote_copy(src_ref=src(a, pidx), dst_ref=outs[a].at[me], **sem))
            arrivals.append(pltpu.make_async_remote_copy(src_ref=src(a, pidx), dst_ref=outs[a].at[pidx], **sem))
    return local, sends, arrivals


def _exchange_start(ins, outs, sems, scatter):
    local, sends, _ = _exchange_copies(ins, outs, sems, scatter)
    for cp in local + sends:
        cp.start()


def _exchange_wait(ins, outs, sems, scatter):
    local, sends, arrivals = _exchange_copies(ins, outs, sems, scatter)
    for cp in arrivals:
        cp.wait_recv()
    for cp in sends:
        cp.wait_send()
    for cp in local:
        cp.wait()


def _exchange_shapes(arrays, scatter):
    n = len(arrays)
    out_shape = [jax.ShapeDtypeStruct((N_DEV,) + tuple(a.shape[1:] if s else a.shape), a.dtype)
                 for a, s in zip(arrays, scatter)]
    sems = [pltpu.SemaphoreType.DMA((n * N_DEV,)), pltpu.SemaphoreType.DMA((n * N_DEV,)),
            pltpu.SemaphoreType.DMA((n,))]
    return out_shape, sems


def _exchange(arrays, scatter, name):
    n = len(arrays)

    def body(*refs):
        ins, outs, sems = refs[:n], refs[n:2 * n], refs[2 * n:]
        _exchange_start(ins, outs, sems, scatter)
        _exchange_wait(ins, outs, sems, scatter)

    out_shape, sems = _exchange_shapes(arrays, scatter)
    any_spec = pl.BlockSpec(memory_space=pl.ANY)
    return _pc(
        body, name=name, out_shape=tuple(out_shape),
        in_specs=[any_spec] * n, out_specs=tuple([any_spec] * n), scratch_shapes=sems,
    )(*arrays)


def _gather_two_level(x, name):
    def body(x_ref, out_ref, send_sems, recv_sems, local_sem):
        X, Y, C = lax.axis_index("x"), lax.axis_index("y"), lax.axis_index("c")
        me, sibling = (X, Y, C), (X, Y, 1 - C)
        chips = [(1 - X, Y), (X, 1 - Y), (1 - X, 1 - Y)]

        def slot(px, py, pc):
            return out_ref.at[4 * px + 2 * py + pc]

        def copy(k, block, to, src=None):
            return pltpu.make_async_remote_copy(
                src_ref=slot(*block) if src is None else src, dst_ref=slot(*block),
                send_sem=send_sems.at[k], recv_sem=recv_sems.at[k], device_id=to, device_id_type=pl.DeviceIdType.MESH)

        mine = pltpu.make_async_copy(x_ref, slot(*me), local_sem)
        mine.start()
        first = [copy(0, me, sibling, src=x_ref)] + [copy(1 + j, me, (*chip, C), src=x_ref) for j, chip in enumerate(chips)]
        for cp in first:
            cp.start()
        passed = [copy(4 + j, (*chip, C), sibling) for j, chip in enumerate(chips)]
        for j, chip in enumerate(chips):
            copy(1 + j, (*chip, C), me).wait_recv()
            passed[j].start()
        copy(0, sibling, me).wait_recv()
        for j, chip in enumerate(chips):
            copy(4 + j, (*chip, 1 - C), me).wait_recv()
        for cp in first + passed:
            cp.wait_send()
        mine.wait()

    any_spec = pl.BlockSpec(memory_space=pl.ANY)
    return _pc(
        body, name=name, out_shape=jax.ShapeDtypeStruct((N_DEV,) + x.shape, x.dtype),
        in_specs=[any_spec], out_specs=any_spec,
        scratch_shapes=[pltpu.SemaphoreType.DMA((7,)), pltpu.SemaphoreType.DMA((7,)), pltpu.SemaphoreType.DMA(())],
    )(x)


def _with_rider(body, n_in, n_out, n_scratch, rider, first, last):
    if rider is None:
        return body
    arrays, scatter = rider
    n = len(arrays)

    def wrapped(*refs):
        ins, r_ins = refs[:n_in], refs[n_in:n_in + n]
        outs = refs[n_in + n:n_in + n + n_out]
        r_outs = refs[n_in + n + n_out:n_in + 2 * n + n_out]
        scratch = refs[n_in + 2 * n + n_out:n_in + 2 * n + n_out + n_scratch]
        sems = refs[n_in + 2 * n + n_out + n_scratch:]

        @pl.when(first())
        def _():
            _exchange_start(r_ins, r_outs, sems, scatter)

        body(*ins, *outs, *scratch)

        @pl.when(last())
        def _():
            _exchange_wait(r_ins, r_outs, sems, scatter)

    return wrapped


def _rider_specs(rider):
    if rider is None:
        return [], [], [], [], []
    arrays, scatter = rider
    out_shape, sems = _exchange_shapes(arrays, scatter)
    any_spec = pl.BlockSpec(memory_space=pl.ANY)
    return list(arrays), [any_spec] * len(arrays), out_shape, [any_spec] * len(arrays), sems


def _mm_nn_res(a, b, res, tm, name):
    T, K = a.shape
    N = b.shape[1]

    def body(a_ref, b_ref, r_ref, o_ref):
        o_ref[...] = r_ref[...] + jnp.dot(a_ref[...], b_ref[...], preferred_element_type=F32)

    return _pc(
        body, name=name, grid=(T // tm,), out_shape=jax.ShapeDtypeStruct((T, N), F32),
        in_specs=[pl.BlockSpec((tm, K), lambda i: (i, 0)), pl.BlockSpec((K, N), lambda i: (0, 0)),
                  pl.BlockSpec((tm, N), lambda i: (i, 0))],
        out_specs=pl.BlockSpec((tm, N), lambda i: (i, 0)),
        compiler_params=_params("parallel"),
    )(a, b, res)


def _mm_tn(a, b, n_col_shards, tn, tk, name):
    T, M = a.shape
    N = b.shape[1]
    sw = N // n_col_shards
    per_step = tn // sw
    nk = T // tk

    def body(a_ref, b_ref, o_ref, acc_sc):
        k = pl.program_id(1)
        part = lax.dot_general(a_ref[...].astype(BF16), b_ref[...].astype(BF16), TN_DIMS,
                               preferred_element_type=F32)

        @pl.when(k == 0)
        def _():
            acc_sc[...] = part

        @pl.when(k > 0)
        def _():
            acc_sc[...] += part

        @pl.when(k == nk - 1)
        def _():
            for s in range(per_step):
                o_ref[s] = acc_sc[:, s * sw:(s + 1) * sw].astype(BF16)

    return _pc(
        body, name=name, grid=(N // tn, nk),
        out_shape=jax.ShapeDtypeStruct((n_col_shards, M, sw), BF16),
        in_specs=[pl.BlockSpec((tk, M), lambda j, k: (k, 0)), pl.BlockSpec((tk, tn), lambda j, k: (k, j))],
        out_specs=pl.BlockSpec((per_step, M, sw), lambda j, k: (j, 0, 0)),
        scratch_shapes=[pltpu.VMEM((M, tn), F32)],
        compiler_params=_params("parallel", "arbitrary"),
    )(a, b)


def _mm_tn_segs(a, segs, sw, tk, name, into=None, first_shard=0):
    T, M = a.shape
    N = sum(x.shape[1] for x in segs)
    assert N % sw == 0 and first_shard % (N // sw) == 0
    n_seg = len(segs)
    nk = T // tk
    ns = N // sw

    def body(a_ref, *rest):
        seg_refs, o_ref, acc_sc = rest[:n_seg], rest[-2], rest[-1]
        k = pl.program_id(0)
        slab = jnp.concatenate([r[...] for r in seg_refs], axis=1)
        part = lax.dot_general(a_ref[...], slab, TN_DIMS, preferred_element_type=F32)

        @pl.when(k == 0)
        def _():
            acc_sc[...] = part

        @pl.when(k > 0)
        def _():
            acc_sc[...] += part

        @pl.when(k == nk - 1)
        def _():
            for j in range(ns):
                o_ref[j] = acc_sc[:, j * sw:(j + 1) * sw].astype(BF16)

    extra, extra_specs, alias = [], [], {}
    out_shape = jax.ShapeDtypeStruct((ns, M, sw), BF16)
    if into is not None:
        extra, extra_specs, alias = [into], [pl.BlockSpec(memory_space=pl.ANY)], {1 + n_seg: 0}
        out_shape = jax.ShapeDtypeStruct(into.shape, BF16)
    blk = first_shard // ns
    return _pc(
        body, name=name, grid=(nk,), out_shape=out_shape,
        in_specs=[pl.BlockSpec((tk, M), lambda k: (k, 0))] + [pl.BlockSpec((tk, x.shape[1]), lambda k: (k, 0)) for x in segs]
        + extra_specs,
        out_specs=pl.BlockSpec((ns, M, sw), lambda k: (blk, 0, 0)),
        scratch_shapes=[pltpu.VMEM((M, N), F32)], input_output_aliases=alias,
        compiler_params=_params("arbitrary"),
    )(a, *segs, *extra)


def _proj_layout(D):
    return {"u": 0, "z_pool": POOL_W, "gates": 2 * POOL_W, "z_attn": 2 * POOL_W + 2 * D, "width": 2 * POOL_W + 2 * D + ATTN_W}


def _norm_inproj(x, g, w, tm):
    T, D = x.shape
    NW = w.shape[1]
    lay = _proj_layout(D)
    qkv0, za0, gl0 = 2 * POOL_W, 2 * POOL_W + 3 * ATTN_W, 2 * POOL_W + 4 * ATTN_W

    def body(x_ref, g_ref, w_ref, proj_ref, qkv_ref, h_ref):
        xv = x_ref[...]
        r = lax.rsqrt(jnp.mean(xv * xv, axis=-1, keepdims=True) + RMS_EPS)
        h = ((xv * r) * g_ref[...]).astype(BF16)
        h_ref[...] = h

        def cols(lo, hi):
            return jnp.dot(h, w_ref[:, lo:hi], preferred_element_type=F32)

        proj_ref[:, :lay["gates"]] = cols(0, qkv0)
        qkv_ref[...] = cols(qkv0, za0).astype(BF16)
        proj_ref[:, lay["gates"]:lay["z_attn"]] = cols(gl0, NW)
        proj_ref[:, lay["z_attn"]:] = cols(za0, gl0)

    return _pc(
        body, name="norm_inproj", grid=(T // tm,),
        out_shape=(jax.ShapeDtypeStruct((T, lay["width"]), F32), jax.ShapeDtypeStruct((T, 3 * ATTN_W), BF16),
                   jax.ShapeDtypeStruct((T, D), BF16)),
        in_specs=[pl.BlockSpec((tm, D), lambda i: (i, 0)), pl.BlockSpec((1, D), lambda i: (0, 0)),
                  pl.BlockSpec((D, NW), lambda i: (0, 0))],
        out_specs=(pl.BlockSpec((tm, lay["width"]), lambda i: (i, 0)), pl.BlockSpec((tm, 3 * ATTN_W), lambda i: (i, 0)),
                   pl.BlockSpec((tm, D), lambda i: (i, 0))),
        compiler_params=_params("parallel"),
    )(x, g, w)


def _window_sums(xh, forward):
    n = xh.shape[0]
    sums, s, step = [], xh, 1
    for _ in POOL_WINDOWS:
        s = s + pltpu.roll(s, step if forward else n - step, 0)
        sums.append(s)
        step *= 2
    return sums


def _pooled(u, halo, row):
    sums = _window_sums(jnp.concatenate([halo, u], axis=0), True)
    out = []
    for g, w in enumerate(POOL_WINDOWS):
        cols = slice(g * POOL_G, (g + 1) * POOL_G)
        cnt = jnp.minimum(row + 1, w).astype(F32)
        out.append(sums[g][HALO:, cols] / cnt - u[:, cols])
    return out


def _pool_fwd(proj, pool_w, scale, R):
    T = proj.shape[0]

    def body(u_ref, z_ref, pw_ref, sc_ref, y_ref, halo_sc):
        i = pl.program_id(0)

        @pl.when(i == 0)
        def _():
            halo_sc[...] = jnp.zeros_like(halo_sc)

        u = u_ref[...]
        row = i * R + lax.broadcasted_iota(jnp.int32, (R, 1), 0)
        pooled = _pooled(u, halo_sc[...], row)
        mixed = jnp.concatenate(
            [jnp.dot(pooled[g].astype(BF16), pw_ref[g].astype(BF16), preferred_element_type=F32)
             for g in range(len(POOL_WINDOWS))], axis=1)
        z = z_ref[...]
        y_ref[...] = ((mixed * sc_ref[...]) * (z * _sigmoid(z))).astype(BF16)
        halo_sc[...] = u[R - HALO:, :]

    return _pc(
        body, name="pool_fwd", grid=(T // R,), out_shape=jax.ShapeDtypeStruct((T, POOL_W), BF16),
        in_specs=[pl.BlockSpec((R, POOL_W), lambda i: (i, 0)), pl.BlockSpec((R, POOL_W), lambda i: (i, 1)),
                  pl.BlockSpec((4, POOL_G, POOL_G), lambda i: (0, 0, 0)), pl.BlockSpec((1, POOL_W), lambda i: (0, 0))],
        out_specs=pl.BlockSpec((R, POOL_W), lambda i: (i, 0)),
        scratch_shapes=[pltpu.VMEM((HALO, POOL_W), F32)],
        compiler_params=_params("arbitrary"),
    )(proj, proj, pool_w, scale)


def _softplus(l):
    return jnp.maximum(l, 0.0) + jnp.log(1.0 + jnp.exp(-jnp.abs(l)))


def _first_last_step(n0, n1):
    return (lambda: jnp.logical_and(pl.program_id(0) == 0, pl.program_id(1) == 0),
            lambda: jnp.logical_and(pl.program_id(0) == n0 - 1, pl.program_id(1) == n1 - 1))


def _head_lanes():
    lane = lax.broadcasted_iota(jnp.int32, (1, LANE), 1)
    return [lane < HEAD_DIM, lane >= HEAD_DIM]


def _head_masks(q, scale):
    qf = q.astype(F32) * scale
    return [jnp.where(m, qf, 0.0).astype(BF16) for m in _head_lanes()]


def _wide(c, width):
    return jnp.concatenate([c] * (width // LANE), axis=1)


def _max_row_norm2(x, heads):
    sq = x.astype(F32) * x.astype(F32)
    return [jnp.max(jnp.sum(jnp.where(m, sq, 0.0), axis=1, keepdims=True), axis=0, keepdims=True) for m in heads]


def _fill_blocks(src_ref, nk, tk, transposed_sc=None, masked_sc=None, norm_sc=None):
    heads = _head_lanes()
    if norm_sc is not None:
        for a in range(2):
            norm_sc[a][...] = jnp.zeros_like(norm_sc[a])

    def step(j, carry):
        rows = pl.ds(pl.multiple_of(j * tk, tk), tk)
        blk = src_ref[rows, :]
        if norm_sc is not None:
            for a, n2 in enumerate(_max_row_norm2(blk, heads)):
                norm_sc[a][...] = jnp.maximum(norm_sc[a][...], n2)
        if transposed_sc is not None:
            transposed_sc[j] = blk.astype(F32).T.astype(BF16)
        if masked_sc is not None:
            for a in range(2):
                masked_sc[a, rows, :] = jnp.where(heads[a], blk, jnp.zeros_like(blk))
        return carry

    lax.fori_loop(0, nk, step, 0)


def _attn_fwd(projb, proj, tq, rider=None):
    T = projb.shape[0]
    nq = T // tq
    tk, nk = tq, nq
    assert nk <= LANE
    n_pairs = ATTN_W // LANE
    zb = (proj.shape[1] - ATTN_W) // LANE

    def body(q_ref, k_ref, v_ref, za_ref, o_ref, y_ref, c_ref, ws_ref, bs_ref, kT_sc, vm_sc, *per_head):
        c_sc, cm_sc, o_sc, kn_sc, l_sc, w_sc = (per_head[2 * n:2 * n + 2] for n in range(6))
        i = pl.program_id(1)

        @pl.when(i == 0)
        def _():
            _fill_blocks(k_ref, nk, tk, transposed_sc=kT_sc, norm_sc=kn_sc)
            _fill_blocks(v_ref, nk, tk, masked_sc=vm_sc)

        qs = _head_masks(q_ref[...], HEAD_DIM ** -0.5)
        lane = lax.broadcasted_iota(jnp.int32, (1, LANE), 1)
        valid = lax.broadcasted_iota(jnp.int32, (tq, tk), 1) < lax.broadcasted_iota(jnp.int32, (tq, tk), 0)
        suffix = (lax.broadcasted_iota(jnp.int32, (tk, tk), 0) >= lax.broadcasted_iota(jnp.int32, (tk, tk), 1)).astype(BF16)
        for a in range(2):
            c_sc[a][...] = jnp.zeros_like(c_sc[a])
            o_sc[a][...] = jnp.zeros_like(o_sc[a])
            cm_sc[a][...] = jnp.full(cm_sc[a].shape, NO_CARRY, F32)
        l_max = [jnp.sqrt(qn * kn_sc[a][...]) for a, qn in enumerate(_max_row_norm2(qs[0] + qs[1], _head_lanes()))]

        def logits(j, slot):
            kT = kT_sc[j]
            for a in range(2):
                l_sc[a][slot] = jnp.dot(qs[a], kT, preferred_element_type=F32)

        def values(j, slot):
            rows = pl.ds(pl.multiple_of(j * tk, tk), tk)
            for a in range(2):
                o_sc[a][...] += jnp.dot(w_sc[a][slot], vm_sc[a, rows, :], preferred_element_type=F32)

        def softplus(slot, masked):
            out = []
            for a in range(2):
                x = _softplus(l_sc[a][slot])
                out.append(jnp.where(valid, x, 0.0) if masked else x)
            return out

        def finish_weights(j, slot, sp, inc, mask, keep=None, cols=None):
            for a in range(2):
                c = c_sc[a][...]
                l = l_sc[a][slot]
                w = jnp.exp(l - inc[a] - _wide(c, tk))
                if mask is not None:
                    w = jnp.where(mask, w, 0.0)
                add = inc[a][:, 0:1]
                if keep is not None:
                    w = jnp.where(keep, w, 0.0)
                    add = jnp.where(keep, add, 0.0)
                w_sc[a][slot] = w.astype(BF16)
                if cols is not None:
                    beta = jnp.exp(l - sp[a])
                    ws_ref[a, 0, :, cols] = w
                    bs_ref[a, 0, :, cols] = beta if mask is None else jnp.where(mask, beta, 0.0)
                cm_sc[a][...] = jnp.where(lane == j, c, cm_sc[a][...])
                c_sc[a][...] = c + add

        def weights(j, slot):
            sp = softplus(slot, False)
            inc = [jnp.dot(sp[a].astype(BF16), suffix, preferred_element_type=F32) for a in range(2)]
            finish_weights(j, slot, sp, inc, None)

        def more():
            live = [jnp.min(c_sc[a][...], axis=0, keepdims=True) - l_max[a][0:1, :] <= ZERO_WEIGHT for a in range(2)]
            return jnp.max(jnp.where(jnp.logical_or(live[0], live[1]), 1, 0))

        logits(i, 0)
        logits(jnp.maximum(i - 1, 0), 1)
        sp = softplus(0, True) + softplus(1, False)
        inc = [jnp.dot(x.astype(BF16), suffix, preferred_element_type=F32) for x in sp]
        finish_weights(i, 0, sp[:2], inc[:2], valid, cols=slice(tk, 2 * tk))
        finish_weights(i - 1, 1, sp[2:], inc[2:], None, keep=i > 0, cols=slice(0, tk))
        values(i, 0)
        logits(jnp.maximum(i - 2, 0), 0)

        def step(t, slot):
            logits(jnp.maximum(i - t - 1, 0), 1 - slot)
            values(i - t + 1, 1 - slot)
            weights(i - t, slot)

        def two_steps(carry):
            tt, _ = carry
            step(2 * tt + 2, 0)
            step(2 * tt + 3, 1)
            return tt + 1, more()

        pairs, go = lax.while_loop(lambda c: jnp.logical_and(2 * c[0] + 3 <= i, c[1] > 0), two_steps, (0, more()))
        done = 1 + 2 * pairs
        one_more = jnp.logical_and(done + 1 == i, go > 0)

        @pl.when(one_more)
        def _():
            step(i, 0)
            values(0, 0)

        @pl.when(jnp.logical_not(one_more))
        def _():
            values(jnp.maximum(i - done, 0), 1)

        o = o_sc[0][...] + o_sc[1][...]
        o_ref[...] = o
        za = za_ref[...]
        y_ref[...] = (o * (za * _sigmoid(za))).astype(BF16)
        c_ref[0, 0] = cm_sc[0][...]
        c_ref[1, 0] = cm_sc[1][...]

    scratch = ([pltpu.VMEM((nk, LANE, tk), BF16), pltpu.VMEM((2, T, LANE), BF16)]
               + [pltpu.VMEM((tq, LANE), F32)] * 6 + [pltpu.VMEM((8, LANE), F32)] * 2
               + [pltpu.VMEM((2, tq, tk), F32)] * 2 + [pltpu.VMEM((2, tq, tk), BF16)] * 2)
    r_in, r_in_specs, r_out, r_out_specs, r_sems = _rider_specs(rider)
    body = _with_rider(body, 4, 5, len(scratch), rider, *_first_last_step(n_pairs, nq))
    kept = jax.ShapeDtypeStruct((2 * n_pairs, nq, tq, 2 * tk), F32)
    kept_spec = pl.BlockSpec((2, 1, tq, 2 * tk), lambda p, i: (p, i, 0, 0))
    return _pc(
        body, name="attn_fwd", grid=(n_pairs, nq),
        out_shape=tuple([jax.ShapeDtypeStruct((T, ATTN_W), F32), jax.ShapeDtypeStruct((T, ATTN_W), BF16),
                         jax.ShapeDtypeStruct((2 * n_pairs, nq, tq, LANE), F32), kept, kept] + r_out),
        in_specs=[pl.BlockSpec((tq, LANE), lambda p, i: (i, p)),
                  pl.BlockSpec((T, LANE), lambda p, i: (0, n_pairs + p)),
                  pl.BlockSpec((T, LANE), lambda p, i: (0, 2 * n_pairs + p)),
                  pl.BlockSpec((tq, LANE), lambda p, i: (i, zb + p))] + r_in_specs,
        out_specs=tuple([pl.BlockSpec((tq, LANE), lambda p, i: (i, p)), pl.BlockSpec((tq, LANE), lambda p, i: (i, p)),
                         pl.BlockSpec((2, 1, tq, LANE), lambda p, i: (p, i, 0, 0)), kept_spec, kept_spec] + r_out_specs),
        scratch_shapes=scratch + r_sems,
        compiler_params=_params("arbitrary", "arbitrary"),
    )(projb, projb, projb, proj, *r_in)


def _gates(gl0, gl1, bg, D):
    return _sigmoid(gl0 + bg[:, :D]), _sigmoid(gl1 + bg[:, D:])


def _merge_fwd(y_pool, y_attn, w_pu, w_au, proj, b_gate, tm):
    T = y_pool.shape[0]
    D = w_pu.shape[1]
    gb = _proj_layout(D)["gates"] // D

    def body(yp_ref, ya_ref, wpu_ref, wau_ref, gl0_ref, gl1_ref, bg_ref, m_ref):
        p = jnp.dot(yp_ref[...], wpu_ref[...], preferred_element_type=F32)
        a = jnp.dot(ya_ref[...], wau_ref[...], preferred_element_type=F32)
        g0, g1 = _gates(gl0_ref[...], gl1_ref[...], bg_ref[...], D)
        m_ref[...] = (g0 * p + g1 * a).astype(BF16)

    row = lambda i: (i, 0)
    fixed = lambda i: (0, 0)
    return _pc(
        body, name="merge_fwd", grid=(T // tm,), out_shape=jax.ShapeDtypeStruct((T, D), BF16),
        in_specs=[pl.BlockSpec((tm, POOL_W), row), pl.BlockSpec((tm, ATTN_W), row),
                  pl.BlockSpec((POOL_W, D), fixed), pl.BlockSpec((ATTN_W, D), fixed),
                  pl.BlockSpec((tm, D), lambda i: (i, gb)), pl.BlockSpec((tm, D), lambda i: (i, gb + 1)),
                  pl.BlockSpec((1, 2 * D), fixed)],
        out_specs=pl.BlockSpec((tm, D), row),
        compiler_params=_params("parallel"),
    )(y_pool, y_attn, w_pu, w_au, proj, proj, b_gate)


def _out_proj_final_loss(merged, w_out, x, g, target, tm):
    T, D = x.shape

    def body(m_ref, w_ref, x_ref, g_ref, t_ref, dx_ref, dg_ref, loss_ref):
        @pl.when(pl.program_id(0) == 0)
        def _():
            dg_ref[...] = jnp.zeros_like(dg_ref)
            loss_ref[...] = jnp.zeros_like(loss_ref)

        xv = x_ref[...] + jnp.dot(m_ref[...], w_ref[...], preferred_element_type=F32)
        gv = g_ref[...]
        r = lax.rsqrt(jnp.mean(xv * xv, axis=-1, keepdims=True) + RMS_EPS)
        xh = xv * r
        d = xh * gv - t_ref[...]
        loss_ref[...] += 0.5 * jnp.sum(jnp.mean(d * d, axis=-1, keepdims=True), axis=0, keepdims=True)
        dy = d * (1.0 / D)
        dg_ref[...] += jnp.sum(dy * xh, axis=0, keepdims=True)
        dh = dy * gv
        dx_ref[...] = r * (dh - xh * jnp.mean(dh * xh, axis=-1, keepdims=True))

    row = lambda i: (i, 0)
    fixed = lambda i: (0, 0)
    return _pc(
        body, name="out_proj_final_loss", grid=(T // tm,),
        out_shape=(jax.ShapeDtypeStruct((T, D), F32), jax.ShapeDtypeStruct((1, D), F32),
                   jax.ShapeDtypeStruct((8, LANE), F32)),
        in_specs=[pl.BlockSpec((tm, D), row), pl.BlockSpec((D, D), fixed), pl.BlockSpec((tm, D), row),
                  pl.BlockSpec((1, D), fixed), pl.BlockSpec((tm, D), row)],
        out_specs=(pl.BlockSpec((tm, D), row), pl.BlockSpec((1, D), fixed), pl.BlockSpec((8, LANE), fixed)),
        compiler_params=_params("arbitrary"),
    )(merged, w_out, x, g, target)


def _merge_bwd(dxo, w_out, y_pool, y_attn, w_pu, w_au, proj, b_gate, tm):
    T, D = dxo.shape
    gb = _proj_layout(D)["gates"] // D

    def body(dxo_ref, wout_ref, yp_ref, ya_ref, wpu_ref, wau_ref, gl0_ref, gl1_ref, bg_ref,
             dp_ref, da_ref, dgl_ref, dbg_ref):
        @pl.when(pl.program_id(0) == 0)
        def _():
            dbg_ref[...] = jnp.zeros_like(dbg_ref)

        dmv = lax.dot_general(dxo_ref[...].astype(BF16), wout_ref[...], NT_DIMS, preferred_element_type=F32)
        p = jnp.dot(yp_ref[...], wpu_ref[...], preferred_element_type=F32)
        a = jnp.dot(ya_ref[...], wau_ref[...], preferred_element_type=F32)
        g0, g1 = _gates(gl0_ref[...], gl1_ref[...], bg_ref[...], D)
        dp_ref[...] = (dmv * g0).astype(BF16)
        da_ref[...] = (dmv * g1).astype(BF16)
        dgl0 = dmv * p * (g0 * (1.0 - g0))
        dgl1 = dmv * a * (g1 * (1.0 - g1))
        dgl_ref[:, :D] = dgl0.astype(BF16)
        dgl_ref[:, D:] = dgl1.astype(BF16)
        dbg_ref[:, :D] += jnp.sum(dgl0, axis=0, keepdims=True)
        dbg_ref[:, D:] += jnp.sum(dgl1, axis=0, keepdims=True)

    row = lambda i: (i, 0)
    fixed = lambda i: (0, 0)
    return _pc(
        body, name="merge_bwd", grid=(T // tm,),
        out_shape=(jax.ShapeDtypeStruct((T, D), BF16), jax.ShapeDtypeStruct((T, D), BF16),
                   jax.ShapeDtypeStruct((T, 2 * D), BF16), jax.ShapeDtypeStruct((1, 2 * D), F32)),
        in_specs=[pl.BlockSpec((tm, D), row), pl.BlockSpec((D, D), fixed),
                  pl.BlockSpec((tm, POOL_W), row), pl.BlockSpec((tm, ATTN_W), row),
                  pl.BlockSpec((POOL_W, D), fixed), pl.BlockSpec((ATTN_W, D), fixed),
                  pl.BlockSpec((tm, D), lambda i: (i, gb)), pl.BlockSpec((tm, D), lambda i: (i, gb + 1)),
                  pl.BlockSpec((1, 2 * D), fixed)],
        out_specs=(pl.BlockSpec((tm, D), row), pl.BlockSpec((tm, D), row), pl.BlockSpec((tm, 2 * D), row),
                   pl.BlockSpec((1, 2 * D), fixed)),
        compiler_params=_params("arbitrary"),
    )(dxo, w_out, y_pool, y_attn, w_pu, w_au, proj, proj, b_gate)


def _pool_bwd(proj, dp, w_pu, pool_w, scale, R):
    T = proj.shape[0]
    D = w_pu.shape[1]
    nb = T // R
    hb = R // HALO

    def body(u_ref, up_ref, z_ref, dp_ref, wpu_ref, pw_ref, sc_ref, du_ref, dz_ref, dpw_ref, dsc_ref, halo_sc):
        i = pl.program_id(0)
        rb = nb - 1 - i

        @pl.when(i == 0)
        def _():
            halo_sc[...] = jnp.zeros_like(halo_sc)
            dpw_ref[...] = jnp.zeros_like(dpw_ref)
            dsc_ref[...] = jnp.zeros_like(dsc_ref)

        u = u_ref[...]
        row = rb * R + lax.broadcasted_iota(jnp.int32, (R, 1), 0)
        before = jnp.where(rb > 0, up_ref[...], 0.0)
        pooled = _pooled(u, before, row)
        pw = [pw_ref[g].astype(BF16) for g in range(len(POOL_WINDOWS))]
        mixed = jnp.concatenate(
            [jnp.dot(pooled[g].astype(BF16), pw[g], preferred_element_type=F32) for g in range(len(POOL_WINDOWS))],
            axis=1)
        sc = sc_ref[...]
        silu, dsilu = _silu_and_grad(z_ref[...])
        dyv = lax.dot_general(dp_ref[...], wpu_ref[...], NT_DIMS, preferred_element_type=F32)
        dmp = dyv * silu
        dz_ref[...] = (dyv * (mixed * sc) * dsilu).astype(BF16)
        dsc_ref[...] += jnp.sum(dmp * mixed, axis=0, keepdims=True)
        dmixed = (dmp * sc).astype(BF16)
        dpn = []
        dpooled = []
        for g, w in enumerate(POOL_WINDOWS):
            cols = slice(g * POOL_G, (g + 1) * POOL_G)
            dpw_ref[g] += lax.dot_general(pooled[g].astype(BF16), dmixed[:, cols], TN_DIMS,
                                          preferred_element_type=F32)
            dpg = lax.dot_general(dmixed[:, cols], pw[g], NT_DIMS, preferred_element_type=F32)
            dpooled.append(dpg)
            dpn.append(dpg / jnp.minimum(row + 1, w).astype(F32))
        dpn = jnp.concatenate(dpn, axis=1)
        sums = _window_sums(jnp.concatenate([dpn, halo_sc[...]], axis=0), False)
        du_ref[...] = jnp.concatenate(
            [sums[g][:R, g * POOL_G:(g + 1) * POOL_G] - dpooled[g] for g in range(len(POOL_WINDOWS))],
            axis=1).astype(BF16)
        halo_sc[...] = dpn[:HALO, :]

    rev = lambda i: (nb - 1 - i, 0)
    return _pc(
        body, name="pool_bwd", grid=(nb,),
        out_shape=(jax.ShapeDtypeStruct((T, POOL_W), BF16), jax.ShapeDtypeStruct((T, POOL_W), BF16),
                   jax.ShapeDtypeStruct((4, POOL_G, POOL_G), F32), jax.ShapeDtypeStruct((1, POOL_W), F32)),
        in_specs=[pl.BlockSpec((R, POOL_W), rev),
                  pl.BlockSpec((HALO, POOL_W), lambda i: (jnp.maximum((nb - 1 - i) * hb - 1, 0), 0)),
                  pl.BlockSpec((R, POOL_W), lambda i: (nb - 1 - i, 1)),
                  pl.BlockSpec((R, D), rev), pl.BlockSpec((POOL_W, D), lambda i: (0, 0)),
                  pl.BlockSpec((4, POOL_G, POOL_G), lambda i: (0, 0, 0)), pl.BlockSpec((1, POOL_W), lambda i: (0, 0))],
        out_specs=(pl.BlockSpec((R, POOL_W), rev), pl.BlockSpec((R, POOL_W), rev),
                   pl.BlockSpec((4, POOL_G, POOL_G), lambda i: (0, 0, 0)), pl.BlockSpec((1, POOL_W), lambda i: (0, 0))),
        scratch_shapes=[pltpu.VMEM((HALO, POOL_W), F32)],
        compiler_params=_params("arbitrary"),
    )(proj, proj, proj, dp, w_pu, pool_w, scale)


def _attn_gate_bwd(da, w_au, o, proj, tm):
    T, D = da.shape
    zb = (proj.shape[1] - ATTN_W) // ATTN_W

    def body(da_ref, wau_ref, o_ref, za_ref, do_ref, dza_ref):
        silu, dsilu = _silu_and_grad(za_ref[...])
        dyv = lax.dot_general(da_ref[...], wau_ref[...], NT_DIMS, preferred_element_type=F32)
        do_ref[...] = (dyv * silu).astype(BF16)
        dza_ref[...] = (dyv * o_ref[...] * dsilu).astype(BF16)

    row = lambda i: (i, 0)
    return _pc(
        body, name="attn_gate_bwd", grid=(T // tm,),
        out_shape=(jax.ShapeDtypeStruct((T, ATTN_W), BF16), jax.ShapeDtypeStruct((T, ATTN_W), BF16)),
        in_specs=[pl.BlockSpec((tm, D), row), pl.BlockSpec((ATTN_W, D), lambda i: (0, 0)),
                  pl.BlockSpec((tm, ATTN_W), row), pl.BlockSpec((tm, ATTN_W), lambda i: (i, zb))],
        out_specs=(pl.BlockSpec((tm, ATTN_W), row), pl.BlockSpec((tm, ATTN_W), row)),
        compiler_params=_params("parallel"),
    )(da, w_au, o, proj)


def _attn_bwd(projb, do, carries, kept_w, kept_beta, tq, rider=None):
    T = projb.shape[0]
    nq = T // tq
    tk, nk = tq, nq
    n_pairs = ATTN_W // LANE
    scale = HEAD_DIM ** -0.5

    def body(q_ref, k_ref, v_ref, do_ref, c_ref, ws_ref, bs_ref, dq_ref, dk_ref, dv_ref, kT_sc, vT_sc, km_sc, dkT_ref, dvT_ref,
             *per_head):
        f_sc, dq_sc, kn_sc, l_sc, dw_sc, dl_sc, w_sc = (per_head[2 * n:2 * n + 2] for n in range(7))
        i = pl.program_id(1)

        @pl.when(i == 0)
        def _():
            _fill_blocks(k_ref, nk, tk, transposed_sc=kT_sc, masked_sc=km_sc, norm_sc=kn_sc)
            _fill_blocks(v_ref, nk, tk, transposed_sc=vT_sc)
            dkT_ref[...] = jnp.zeros_like(dkT_ref)
            dvT_ref[...] = jnp.zeros_like(dvT_ref)

        qs = _head_masks(q_ref[...], scale)
        dos = _head_masks(do_ref[...], 1.0)
        qT = [x.astype(F32).T.astype(BF16) for x in qs]
        doT = [x.astype(F32).T.astype(BF16) for x in dos]
        lane = lax.broadcasted_iota(jnp.int32, (1, LANE), 1)
        valid = lax.broadcasted_iota(jnp.int32, (tq, tk), 1) < lax.broadcasted_iota(jnp.int32, (tq, tk), 0)
        kk0 = lax.broadcasted_iota(jnp.int32, (tk, tk), 0)
        kk1 = lax.broadcasted_iota(jnp.int32, (tk, tk), 1)
        suffix = (kk0 >= kk1).astype(BF16)
        prefix = (kk0 <= kk1).astype(BF16)
        for a in range(2):
            f_sc[a][...] = jnp.zeros_like(f_sc[a])
            dq_sc[a][...] = jnp.zeros_like(dq_sc[a])
        live = lane == i
        for a, qn in enumerate(_max_row_norm2(qs[0] + qs[1], _head_lanes())):
            l_max = jnp.sqrt(qn * kn_sc[a][0:1, :])
            live = jnp.logical_or(live, jnp.min(c_ref[a, 0], axis=0, keepdims=True) - l_max <= ZERO_WEIGHT)
        t0 = jnp.min(jnp.where(jnp.logical_and(live, lane <= i), lane, i))
        n = i - t0

        def products(j, slot, with_logits=True):
            kT = kT_sc[j]
            vT = vT_sc[j]
            for a in range(2):
                if with_logits:
                    l_sc[a][slot] = jnp.dot(qs[a], kT, preferred_element_type=F32)
                dw_sc[a][slot] = jnp.dot(dos[a], vT, preferred_element_type=F32)

        def gradients(j, slot):
            rows = pl.ds(pl.multiple_of(j * tk, tk), tk)
            dkT = []
            dvT = []
            for a in range(2):
                dlb = dl_sc[a][slot]
                dq_sc[a][...] += jnp.dot(dlb, km_sc[a, rows, :], preferred_element_type=F32)
                dkT.append(jnp.dot(qT[a], dlb, preferred_element_type=F32))
                dvT.append(jnp.dot(doT[a], w_sc[a][slot], preferred_element_type=F32))
            dkT_ref[j] += dkT[0] + dkT[1]
            dvT_ref[j] += dvT[0] + dvT[1]

        def elementwise(j, slot, masked):
            sp, inc, e, beta, p = [None] * 2, [None] * 2, [None] * 2, [None] * 2, [None] * 2
            for a in range(2):
                x = _softplus(l_sc[a][slot])
                sp[a] = jnp.where(valid, x, 0.0) if masked else x
            for a in range(2):
                inc[a] = jnp.dot(sp[a].astype(BF16), suffix, preferred_element_type=F32)
            for a in range(2):
                l = l_sc[a][slot]
                c = jnp.sum(jnp.where(lane == j, c_ref[a, 0], 0.0), axis=1, keepdims=True)
                w = jnp.exp(l - inc[a] - c)
                if masked:
                    w = jnp.where(valid, w, 0.0)
                w_sc[a][slot] = w.astype(BF16)
                beta[a] = jnp.exp(l - sp[a])
                e[a] = w * dw_sc[a][slot]
            for a in range(2):
                p[a] = jnp.dot(e[a].astype(BF16), prefix, preferred_element_type=F32)
            for a in range(2):
                f = f_sc[a][...]
                dl = e[a] - beta[a] * (p[a] + _wide(f, tk))
                if masked:
                    dl = jnp.where(valid, dl, 0.0)
                dl_sc[a][slot] = dl.astype(BF16)
                f_sc[a][...] = f + p[a][:, tk - 1:tk]

        def kept_blocks(tiles):
            keys = [(n_, a) for n_ in range(len(tiles)) for a in range(2)]
            for j, slot, cols in tiles:
                products(j, slot, with_logits=False)
            e, p = {}, {}
            for n_, a in keys:
                j, slot, cols = tiles[n_]
                w = ws_ref[a, 0, :, cols]
                w_sc[a][slot] = w.astype(BF16)
                e[n_, a] = w * dw_sc[a][slot]
            for k in keys:
                p[k] = jnp.dot(e[k].astype(BF16), prefix, preferred_element_type=F32)
            for a in range(2):
                f = f_sc[a][...]
                for n_, (j, slot, cols) in enumerate(tiles):
                    dl = e[n_, a] - bs_ref[a, 0, :, cols] * (p[n_, a] + _wide(f, tk))
                    dl_sc[a][slot] = dl.astype(BF16)
                    f = f + p[n_, a][:, tk - 1:tk]
                f_sc[a][...] = f

        def step(r, slot):
            products(t0 + r + 1, 1 - slot)
            gradients(t0 + jnp.maximum(r - 1, 0), 1 - slot)
            elementwise(t0 + r, slot, False)

        def last_two(slot, pending=True):
            if pending:
                gradients(i - 2, slot)
            kept_blocks([(i - 1, 1 - slot, slice(0, tk)), (i, slot, slice(tk, 2 * tk))])
            gradients(i - 1, 1 - slot)
            gradients(i, slot)

        @pl.when(n >= 2)
        def _():
            for a in range(2):
                dl_sc[a][1] = jnp.zeros((tq, tk), BF16)
                w_sc[a][1] = jnp.zeros((tq, tk), BF16)
            products(t0, 0)

        def two_steps(tt, carry):
            step(2 * tt, 0)
            step(2 * tt + 1, 1)
            return carry

        lax.fori_loop(0, jnp.maximum(n - 1, 0) // 2, two_steps, 0)

        @pl.when(n == 0)
        def _():
            kept_blocks([(i, 0, slice(tk, 2 * tk))])
            gradients(i, 0)

        @pl.when(n == 1)
        def _():
            last_two(1, pending=False)

        @pl.when(jnp.logical_and(n > 1, n % 2 == 1))
        def _():
            last_two(1)

        @pl.when(jnp.logical_and(n > 0, n % 2 == 0))
        def _():
            step(n - 2, 0)
            last_two(0)

        dq_ref[...] = ((dq_sc[0][...] + dq_sc[1][...]) * scale).astype(BF16)

        @pl.when(i == nq - 1)
        def _():
            def untranspose(j, carry):
                rows = pl.ds(pl.multiple_of(j * tk, tk), tk)
                dk_ref[rows, :] = dkT_ref[j].T.astype(BF16)
                dv_ref[rows, :] = dvT_ref[j].T.astype(BF16)
                return carry

            lax.fori_loop(0, nk, untranspose, 0)

    scratch = ([pltpu.VMEM((nk, LANE, tk), BF16), pltpu.VMEM((nk, LANE, tk), BF16), pltpu.VMEM((2, T, LANE), BF16),
                pltpu.VMEM((nk, LANE, tk), F32), pltpu.VMEM((nk, LANE, tk), F32)]
               + [pltpu.VMEM((tq, LANE), F32)] * 4 + [pltpu.VMEM((8, LANE), F32)] * 2
               + [pltpu.VMEM((2, tq, tk), F32)] * 4 + [pltpu.VMEM((2, tq, tk), BF16)] * 4)
    r_in, r_in_specs, r_out, r_out_specs, r_sems = _rider_specs(rider)
    body = _with_rider(body, 7, 3, len(scratch), rider, *_first_last_step(n_pairs, nq))
    kept_spec = pl.BlockSpec((2, 1, tq, 2 * tk), lambda p, i: (p, i, 0, 0))
    return _pc(
        body, name="attn_bwd", grid=(n_pairs, nq),
        out_shape=tuple([jax.ShapeDtypeStruct((T, ATTN_W), BF16)] * 3 + r_out),
        in_specs=[pl.BlockSpec((tq, LANE), lambda p, i: (i, p)),
                  pl.BlockSpec((T, LANE), lambda p, i: (0, n_pairs + p)),
                  pl.BlockSpec((T, LANE), lambda p, i: (0, 2 * n_pairs + p)),
                  pl.BlockSpec((tq, LANE), lambda p, i: (i, p)),
                  pl.BlockSpec((2, 1, tq, LANE), lambda p, i: (p, i, 0, 0)), kept_spec, kept_spec] + r_in_specs,
        out_specs=tuple([pl.BlockSpec((tq, LANE), lambda p, i: (i, p)), pl.BlockSpec((T, LANE), lambda p, i: (0, p)),
                         pl.BlockSpec((T, LANE), lambda p, i: (0, p))] + r_out_specs),
        scratch_shapes=scratch + r_sems,
        compiler_params=_params("arbitrary", "arbitrary"),
    )(projb, projb, projb, do, carries, kept_w, kept_beta, *r_in)


def _dh_norm_bwd(segs, w, x, g, dxo, tm, rider=None):
    T, D = x.shape
    nm = T // tm
    n_seg = len(segs)
    offs = [sum(y.shape[1] for y in segs[:n]) for n in range(n_seg + 1)]
    assert offs[-1] == w.shape[1]

    def body(*refs):
        seg_refs = refs[:n_seg]
        w_ref, x_ref, g_ref, dxo_ref, dx_ref, dg_ref = refs[n_seg:]

        @pl.when(pl.program_id(0) == 0)
        def _():
            dg_ref[...] = jnp.zeros_like(dg_ref)

        dhv = None
        for n in range(n_seg):
            part = lax.dot_general(seg_refs[n][...], w_ref[:, offs[n]:offs[n + 1]], NT_DIMS, preferred_element_type=F32)
            dhv = part if dhv is None else dhv + part
        xv = x_ref[...]
        r = lax.rsqrt(jnp.mean(xv * xv, axis=-1, keepdims=True) + RMS_EPS)
        xh = xv * r
        dg_ref[...] += jnp.sum(dhv * xh, axis=0, keepdims=True)
        dhg = dhv * g_ref[...]
        dx_ref[...] = dxo_ref[...] + r * (dhg - xh * jnp.mean(dhg * xh, axis=-1, keepdims=True))

    r_in, r_in_specs, r_out, r_out_specs, r_sems = _rider_specs(rider)
    body = _with_rider(body, n_seg + 4, 2, 0, rider, lambda: pl.program_id(0) == 0, lambda: pl.program_id(0) == nm - 1)
    row = lambda i: (i, 0)
    fixed = lambda i: (0, 0)
    return _pc(
        body, name="d_h_norm_bwd", grid=(nm,),
        out_shape=tuple([jax.ShapeDtypeStruct((T, D), F32), jax.ShapeDtypeStruct((1, D), F32)] + r_out),
        in_specs=[pl.BlockSpec((tm, y.shape[1]), row) for y in segs]
        + [pl.BlockSpec(w.shape, fixed), pl.BlockSpec((tm, D), row), pl.BlockSpec((1, D), fixed), pl.BlockSpec((tm, D), row)]
        + r_in_specs,
        out_specs=tuple([pl.BlockSpec((tm, D), row), pl.BlockSpec((1, D), fixed)] + r_out_specs),
        scratch_shapes=r_sems,
        compiler_params=_params("arbitrary"),
    )(*segs, w, x, g, dxo, *r_in)


def _adamw(pieces, w, m, v, name):
    G = len(pieces)
    rows, cols = pieces[0].shape[1:]
    assert w.shape == (G * rows, cols)
    br = rows
    while br * cols > 65536 and br % 16 == 0:
        br //= 2
    nb = rows // br
    c1 = 1.0 / (1.0 - ADAM_B1 ** ADAM_STEP)
    c2 = 1.0 / (1.0 - ADAM_B2 ** ADAM_STEP)

    def body(*refs):
        p_refs = refs[:G]
        w_ref, m_ref, v_ref, g_ref, d_ref, nm_ref, nv_ref = refs[G:]
        g = None
        for n, p_ref in enumerate(p_refs):
            gn = p_ref[0].astype(F32)
            for s in range(1, N_DEV):
                gn = gn + p_ref[s].astype(F32)
            g = gn if g is None else jnp.where(pl.program_id(0) == n, gn, g)
        nm = ADAM_B1 * m_ref[...] + (1.0 - ADAM_B1) * g
        nv = ADAM_B2 * v_ref[...] + (1.0 - ADAM_B2) * (g * g)
        g_ref[...] = g
        nm_ref[...] = nm
        nv_ref[...] = nv
        d_ref[...] = -ADAM_LR * ((nm * c1) / (jnp.sqrt(nv * c2) + ADAM_EPS) + ADAM_WD * w_ref[...])

    blk = pl.BlockSpec((br, cols), lambda n, i: (n * nb + i, 0))
    shape = jax.ShapeDtypeStruct((G * rows, cols), F32)
    return _pc(
        body, name=name, grid=(G, nb), out_shape=(shape, shape, shape, shape),
        in_specs=[pl.BlockSpec((N_DEV, br, cols), lambda n, i: (0, i, 0))] * G + [blk, blk, blk],
        out_specs=(blk, blk, blk, blk),
        compiler_params=_params("parallel", "parallel"),
    )(*pieces, w, m, v)


def _rows128(a):
    flat = a.reshape(-1)
    n = flat.shape[0]
    padded = -(-n // (8 * LANE)) * (8 * LANE)
    if padded != n:
        flat = jnp.concatenate([flat, jnp.zeros((padded - n,), flat.dtype)])
    return flat.reshape(-1, LANE)


def _pack(parts):
    return jnp.concatenate([_rows128(p) for p in parts], axis=0)


def _unpack(packed, like):
    out, r = [], 0
    for a in like:
        n = a.size
        nr = -(-n // (8 * LANE)) * 8
        out.append(packed[r:r + nr].reshape(-1)[:n].reshape(a.shape))
        r += nr
    return out


def kernel(x, norm_g, w_in, b_gate, pool_w, pool_scale, w_pool_up, w_attn_up, w_out, final_g, loss_target, m_norm_g, m_w_in, m_b_gate, m_pool_w, m_pool_scale, m_w_pool_up, m_w_attn_up, m_w_out, m_final_g, v_norm_g, v_w_in, v_b_gate, v_pool_w, v_pool_scale, v_w_pool_up, v_w_attn_up, v_w_out, v_final_g):
    L = norm_g.shape[0]
    T, D = x.shape[1], x.shape[2]
    NW = w_in.shape[2] * N_DEV
    assert NW == 2 * POOL_W + 4 * ATTN_W + 2 * D and x.shape[0] == 1
    tm = min(512, T)
    tq = min(256, T // 2)
    x0 = x.reshape(T, D)
    target = loss_target.reshape(T, D)

    assert L >= 2
    win_first = jnp.transpose(_gather_two_level(w_in[0].astype(BF16), "gather_w_in0"), (1, 0, 2)).reshape(D, NW)
    rest = [w_in[1:].astype(BF16), w_pool_up.astype(BF16), w_attn_up.astype(BF16), w_out.astype(BF16)]

    saved = []
    xl = x0
    for l in range(L):
        proj, projb, h = _norm_inproj(xl, norm_g[l:l + 1], win_first if l == 0 else win_rest[l - 1], min(256, T))
        y_pool = _pool_fwd(proj, pool_w[l], pool_scale[l:l + 1], tm)
        if l == 0:
            o, y_attn, carries, kept_w, kept_b, g_in, g_pu, g_au, g_out = _attn_fwd(projb, proj, tq, rider=(rest, [False] * 4))
            win_rest = jnp.transpose(g_in, (1, 2, 0, 3)).reshape(L - 1, D, NW)
            wpu_full = jnp.transpose(g_pu, (1, 2, 0, 3)).reshape(L, POOL_W, D)
            wau_full = jnp.transpose(g_au, (1, 2, 0, 3)).reshape(L, ATTN_W, D)
            wout_full = jnp.transpose(g_out, (1, 0, 2, 3)).reshape(L, D, D)
        else:
            o, y_attn, carries, kept_w, kept_b = _attn_fwd(projb, proj, tq)
        merged = _merge_fwd(y_pool, y_attn, wpu_full[l], wau_full[l], proj, b_gate[l:l + 1], min(256, T))
        saved.append((xl, proj, projb, h, y_pool, o, y_attn, (carries, kept_w, kept_b), merged))
        if l < L - 1:
            xl = _mm_nn_res(merged, wout_full[l], xl, tm, "out_proj")

    dx, d_final_g, loss_part = _out_proj_final_loss(merged, wout_full[L - 1], xl, final_g.reshape(1, D), target, tm)

    d_norm_g, d_b_gate, d_pool_w, d_pool_scale = [None] * L, [None] * L, [None] * L, [None] * L
    d_win, d_wpu, d_wau, d_wout = [None] * L, [None] * L, [None] * L, [None] * L
    small_like = [norm_g, b_gate, pool_w, pool_scale, final_g, jnp.zeros((8, LANE), F32)]
    for l in reversed(range(L)):
        xin, proj, projb, h, y_pool, o, y_attn, carries, merged = saved[l]
        win_l = win_first if l == 0 else win_rest[l - 1]
        d_wout[l] = _mm_tn(merged, dx, 1, D, min(1024, T), "d_w_out").reshape(N_DEV, D // N_DEV, D)
        dp, da, dgl, d_b_gate[l] = _merge_bwd(dx, wout_full[l], y_pool, y_attn, wpu_full[l], wau_full[l], proj,
                                              b_gate[l:l + 1], min(256, T))
        d_wpu[l] = _mm_tn(y_pool, dp, N_DEV, D, min(1024, T), "d_w_pool_up")
        d_wau[l] = _mm_tn(y_attn, da, N_DEV, D, min(1024, T), "d_w_attn_up")
        du, dzp, d_pool_w[l], d_pool_scale[l] = _pool_bwd(proj, dp, wpu_full[l], pool_w[l], pool_scale[l:l + 1], tm)
        do, dza = _attn_gate_bwd(da, wau_full[l], o, proj, tm)
        if l == 0:
            small = _pack([jnp.concatenate([jnp.zeros((1, D), F32)] + d_norm_g[1:], 0), jnp.concatenate(d_b_gate, 0),
                           jnp.stack(d_pool_w, 0), jnp.concatenate(d_pool_scale, 0), d_final_g, loss_part])
            early = d_win[1:] + d_wpu + d_wau + d_wout
            dq, dk, dv, got_small, *got_early = _attn_bwd(
                projb, do, *carries, tq, rider=([small] + early, [False] + [True] * len(early)))
        else:
            dq, dk, dv = _attn_bwd(projb, do, *carries, tq)
        segs = [du, dzp, dq, dk, dv, dza, dgl]
        sw = NW // N_DEV
        half = _mm_tn_segs(h, segs[5:], sw, min(1024, T), "d_w_in_b", into=jnp.zeros((N_DEV, D, sw), BF16), first_shard=N_DEV // 2)
        d_win[l] = _mm_tn_segs(h, segs[:5], sw, min(1024, T), "d_w_in_a", into=half)
        if l == 0:
            dx, d_norm_g[l], got_win0 = _dh_norm_bwd(segs, win_l, xin, norm_g[l:l + 1], dx, min(256, T),
                                                     rider=([d_win[0]], [True]))
        else:
            dx, d_norm_g[l] = _dh_norm_bwd(segs, win_l, xin, norm_g[l:l + 1], dx, min(256, T))

    got_norm0 = _exchange([d_norm_g[0].reshape(-1, LANE)], [False], "gather_norm_grad")[0]
    r_small = jnp.concatenate([got_norm0, got_small[:, D // LANE:]], axis=1)

    def update(pieces, w, m, v, name):
        cols = w.shape[-1]
        res = _adamw([p.reshape(N_DEV, -1, cols) for p in pieces], w.reshape(-1, cols), m.reshape(-1, cols),
                     v.reshape(-1, cols), name)
        return [r.reshape(w.shape) for r in res]

    u_in = update([got_win0] + got_early[:L - 1], w_in, m_w_in, v_w_in, "adamw_w_in")
    u_pu = update(got_early[L - 1:2 * L - 1], w_pool_up, m_w_pool_up, v_w_pool_up, "adamw_w_pool_up")
    u_au = update(got_early[2 * L - 1:3 * L - 1], w_attn_up, m_w_attn_up, v_w_attn_up, "adamw_w_attn_up")
    u_out = update(got_early[3 * L - 1:4 * L - 1], w_out, m_w_out, v_w_out, "adamw_w_out")
    zeros = small_like[-1]
    smalls = _adamw([r_small],
                    _pack([norm_g, b_gate, pool_w, pool_scale, final_g, zeros]),
                    _pack([m_norm_g, m_b_gate, m_pool_w, m_pool_scale, m_final_g, zeros]),
                    _pack([v_norm_g, v_b_gate, v_pool_w, v_pool_scale, v_final_g, zeros]), "adamw_small")
    s_g, s_d, s_m, s_v = [_unpack(s, small_like) for s in smalls]
    loss = s_g[5][0, 0]

    def ordered(k):
        s = (s_g, s_d, s_m, s_v)[k]
        return [s[0], u_in[k], s[1], s[2], s[3], u_pu[k], u_au[k], u_out[k], s[4]]

    return (loss, dx.reshape(x.shape), *ordered(0), *ordered(1), *ordered(2), *ordered(3))
```

```python
import jax
import jax.numpy as jnp
from jax import lax
from jax.experimental import pallas as pl
from jax.experimental.pallas import tpu as pltpu

F32 = jnp.float32
BF16 = jnp.bfloat16

N_DEV = 8
HEAD_DIM = 64
ATTN_W = 512
POOL_W = 512
POOL_G = 128
POOL_WINDOWS = (2, 4, 8, 16)
HALO = 16
LANE = 128
RMS_EPS = 1e-6
ZERO_WEIGHT = 110.0
NO_CARRY = 3.0e38
ADAM_LR, ADAM_B1, ADAM_B2, ADAM_EPS, ADAM_WD, ADAM_STEP = 0.001, 0.9, 0.999, 1e-08, 0.01, 10
VMEM_LIMIT = 56 * 1024 * 1024

NT_DIMS = (((1,), (1,)), ((), ()))
TN_DIMS = (((0,), (0,)), ((), ()))


def _pc(body, **kw):
    return pl.pallas_call(body, **kw)


def _params(*sem):
    return pltpu.CompilerParams(dimension_semantics=sem, vmem_limit_bytes=VMEM_LIMIT)


def _sigmoid(z):
    return 1.0 / (1.0 + jnp.exp(-z))


def _silu_and_grad(z):
    s = _sigmoid(z)
    return z * s, s * (1.0 + z * (1.0 - s))


def _my_index():
    return 4 * lax.axis_index("x") + 2 * lax.axis_index("y") + lax.axis_index("c")


def _peer(k):
    x, y, c = lax.axis_index("x"), lax.axis_index("y"), lax.axis_index("c")
    px = lax.rem(x + ((k >> 2) & 1), 2)
    py = lax.rem(y + ((k >> 1) & 1), 2)
    pc = lax.rem(c + (k & 1), 2)
    return (px, py, pc), 4 * px + 2 * py + pc


def _exchange_copies(ins, outs, sems, scatter):
    send_sems, recv_sems, local_sems = sems
    n = len(ins)
    me = _my_index()

    def src(a, idx):
        return ins[a].at[idx] if scatter[a] else ins[a]

    local = [pltpu.make_async_copy(src(a, me), outs[a].at[me], local_sems.at[a]) for a in range(n)]
    sends, arrivals = [], []
    for k in (1, 2, 4, 3, 5, 6, 7):
        dev, pidx = _peer(k)
        for a in range(n):
            sem = dict(send_sem=send_sems.at[a * N_DEV + k], recv_sem=recv_sems.at[a * N_DEV + k],
                       device_id=dev, device_id_type=pl.DeviceIdType.MESH)
            sends.append(pltpu.make_async_remote_copy(src_ref=src(a, pidx), dst_ref=outs[a].at[me], **sem))
            arrivals.append(pltpu.make_async_remote_copy(src_ref=src(a, pidx), dst_ref=outs[a].at[pidx], **sem))
    return local, sends, arrivals


def _exchange_start(ins, outs, sems, scatter):
    local, sends, _ = _exchange_copies(ins, outs, sems, scatter)
    for cp in local + sends:
        cp.start()


def _exchange_wait(ins, outs, sems, scatter):
    local, sends, arrivals = _exchange_copies(ins, outs, sems, scatter)
    for cp in arrivals:
        cp.wait_recv()
    for cp in sends:
        cp.wait_send()
    for cp in local:
        cp.wait()


def _exchange_shapes(arrays, scatter):
    n = len(arrays)
    out_shape = [jax.ShapeDtypeStruct((N_DEV,) + tuple(a.shape[1:] if s else a.shape), a.dtype)
                 for a, s in zip(arrays, scatter)]
    sems = [pltpu.SemaphoreType.DMA((n * N_DEV,)), pltpu.SemaphoreType.DMA((n * N_DEV,)),
            pltpu.SemaphoreType.DMA((n,))]
    return out_shape, sems


def _exchange(arrays, scatter, name):
    n = len(arrays)

    def body(*refs):
        ins, outs, sems = refs[:n], refs[n:2 * n], refs[2 * n:]
        _exchange_start(ins, outs, sems, scatter)
        _exchange_wait(ins, outs, sems, scatter)

    out_shape, sems = _exchange_shapes(arrays, scatter)
    any_spec = pl.BlockSpec(memory_space=pl.ANY)
    return _pc(
        body, name=name, out_shape=tuple(out_shape),
        in_specs=[any_spec] * n, out_specs=tuple([any_spec] * n), scratch_shapes=sems,
    )(*arrays)


def _gather_two_level(x, name):
    def body(x_ref, out_ref, send_sems, recv_sems, local_sem):
        X, Y, C = lax.axis_index("x"), lax.axis_index("y"), lax.axis_index("c")
        me, sibling = (X, Y, C), (X, Y, 1 - C)
        chips = [(1 - X, Y), (X, 1 - Y), (1 - X, 1 - Y)]

        def slot(px, py, pc):
            return out_ref.at[4 * px + 2 * py + pc]

        def copy(k, block, to, src=None):
            return pltpu.make_async_remote_copy(
                src_ref=slot(*block) if src is None else src, dst_ref=slot(*block),
                send_sem=send_sems.at[k], recv_sem=recv_sems.at[k], device_id=to, device_id_type=pl.DeviceIdType.MESH)

        mine = pltpu.make_async_copy(x_ref, slot(*me), local_sem)
        mine.start()
        first = [copy(0, me, sibling, src=x_ref)] + [copy(1 + j, me, (*chip, C), src=x_ref) for j, chip in enumerate(chips)]
        for cp in first:
            cp.start()
        passed = [copy(4 + j, (*chip, C), sibling) for j, chip in enumerate(chips)]
        for j, chip in enumerate(chips):
            copy(1 + j, (*chip, C), me).wait_recv()
            passed[j].start()
        copy(0, sibling, me).wait_recv()
        for j, chip in enumerate(chips):
            copy(4 + j, (*chip, 1 - C), me).wait_recv()
        for cp in first + passed:
            cp.wait_send()
        mine.wait()

    any_spec = pl.BlockSpec(memory_space=pl.ANY)
    return _pc(
        body, name=name, out_shape=jax.ShapeDtypeStruct((N_DEV,) + x.shape, x.dtype),
        in_specs=[any_spec], out_specs=any_spec,
        scratch_shapes=[pltpu.SemaphoreType.DMA((7,)), pltpu.SemaphoreType.DMA((7,)), pltpu.SemaphoreType.DMA(())],
    )(x)


def _with_rider(body, n_in, n_out, n_scratch, rider, first, last):
    if rider is None:
        return body
    arrays, scatter = rider
    n = len(arrays)

    def wrapped(*refs):
        ins, r_ins = refs[:n_in], refs[n_in:n_in + n]
        outs = refs[n_in + n:n_in + n + n_out]
        r_outs = refs[n_in + n + n_out:n_in + 2 * n + n_out]
        scratch = refs[n_in + 2 * n + n_out:n_in + 2 * n + n_out + n_scratch]
        sems = refs[n_in + 2 * n + n_out + n_scratch:]

        @pl.when(first())
        def _():
            _exchange_start(r_ins, r_outs, sems, scatter)

        body(*ins, *outs, *scratch)

        @pl.when(last())
        def _():
            _exchange_wait(r_ins, r_outs, sems, scatter)

    return wrapped


def _rider_specs(rider):
    if rider is None:
        return [], [], [], [], []
    arrays, scatter = rider
    out_shape, sems = _exchange_shapes(arrays, scatter)
    any_spec = pl.BlockSpec(memory_space=pl.ANY)
    return list(arrays), [any_spec] * len(arrays), out_shape, [any_spec] * len(arrays), sems


def _mm_nn_res(a, b, res, tm, name):
    T, K = a.shape
    N = b.shape[1]

    def body(a_ref, b_ref, r_ref, o_ref):
        o_ref[...] = r_ref[...] + jnp.dot(a_ref[...], b_ref[...], preferred_element_type=F32)

    return _pc(
        body, name=name, grid=(T // tm,), out_shape=jax.ShapeDtypeStruct((T, N), F32),
        in_specs=[pl.BlockSpec((tm, K), lambda i: (i, 0)), pl.BlockSpec((K, N), lambda i: (0, 0)),
                  pl.BlockSpec((tm, N), lambda i: (i, 0))],
        out_specs=pl.BlockSpec((tm, N), lambda i: (i, 0)),
        compiler_params=_params("parallel"),
    )(a, b, res)


def _mm_tn(a, b, n_col_shards, tn, tk, name):
    T, M = a.shape
    N = b.shape[1]
    sw = N // n_col_shards
    per_step = tn // sw
    nk = T // tk

    def body(a_ref, b_ref, o_ref, acc_sc):
        k = pl.program_id(1)
        part = lax.dot_general(a_ref[...].astype(BF16), b_ref[...].astype(BF16), TN_DIMS,
                               preferred_element_type=F32)

        @pl.when(k == 0)
        def _():
            acc_sc[...] = part

        @pl.when(k > 0)
        def _():
            acc_sc[...] += part

        @pl.when(k == nk - 1)
        def _():
            for s in range(per_step):
                o_ref[s] = acc_sc[:, s * sw:(s + 1) * sw].astype(BF16)

    return _pc(
        body, name=name, grid=(N // tn, nk),
        out_shape=jax.ShapeDtypeStruct((n_col_shards, M, sw), BF16),
        in_specs=[pl.BlockSpec((tk, M), lambda j, k: (k, 0)), pl.BlockSpec((tk, tn), lambda j, k: (k, j))],
        out_specs=pl.BlockSpec((per_step, M, sw), lambda j, k: (j, 0, 0)),
        scratch_shapes=[pltpu.VMEM((M, tn), F32)],
        compiler_params=_params("parallel", "arbitrary"),
    )(a, b)


def _mm_tn_segs(a, segs, sw, tk, name, into=None, first_shard=0):
    T, M = a.shape
    N = sum(x.shape[1] for x in segs)
    assert N % sw == 0 and first_shard % (N // sw) == 0
    n_seg = len(segs)
    nk = T // tk
    ns = N // sw

    def body(a_ref, *rest):
        seg_refs, o_ref, acc_sc = rest[:n_seg], rest[-2], rest[-1]
        k = pl.program_id(0)
        slab = jnp.concatenate([r[...] for r in seg_refs], axis=1)
        part = lax.dot_general(a_ref[...], slab, TN_DIMS, preferred_element_type=F32)

        @pl.when(k == 0)
        def _():
            acc_sc[...] = part

        @pl.when(k > 0)
        def _():
            acc_sc[...] += part

        @pl.when(k == nk - 1)
        def _():
            for j in range(ns):
                o_ref[j] = acc_sc[:, j * sw:(j + 1) * sw].astype(BF16)

    extra, extra_specs, alias = [], [], {}
    out_shape = jax.ShapeDtypeStruct((ns, M, sw), BF16)
    if into is not None:
        extra, extra_specs, alias = [into], [pl.BlockSpec(memory_space=pl.ANY)], {1 + n_seg: 0}
        out_shape = jax.ShapeDtypeStruct(into.shape, BF16)
    blk = first_shard // ns
    return _pc(
        body, name=name, grid=(nk,), out_shape=out_shape,
        in_specs=[pl.BlockSpec((tk, M), lambda k: (k, 0))] + [pl.BlockSpec((tk, x.shape[1]), lambda k: (k, 0)) for x in segs]
        + extra_specs,
        out_specs=pl.BlockSpec((ns, M, sw), lambda k: (blk, 0, 0)),
        scratch_shapes=[pltpu.VMEM((M, N), F32)], input_output_aliases=alias,
        compiler_params=_params("arbitrary"),
    )(a, *segs, *extra)


def _proj_layout(D):
    return {"u": 0, "z_pool": POOL_W, "gates": 2 * POOL_W, "z_attn": 2 * POOL_W + 2 * D, "width": 2 * POOL_W + 2 * D + ATTN_W}


def _norm_inproj(x, g, w, tm):
    T, D = x.shape
    NW = w.shape[1]
    lay = _proj_layout(D)
    qkv0, za0, gl0 = 2 * POOL_W, 2 * POOL_W + 3 * ATTN_W, 2 * POOL_W + 4 * ATTN_W

    def body(x_ref, g_ref, w_ref, proj_ref, qkv_ref, h_ref):
        xv = x_ref[...]
        r = lax.rsqrt(jnp.mean(xv * xv, axis=-1, keepdims=True) + RMS_EPS)
        h = ((xv * r) * g_ref[...]).astype(BF16)
        h_ref[...] = h

        def cols(lo, hi):
            return jnp.dot(h, w_ref[:, lo:hi], preferred_element_type=F32)

        proj_ref[:, :lay["gates"]] = cols(0, qkv0)
        qkv_ref[...] = cols(qkv0, za0).astype(BF16)
        proj_ref[:, lay["gates"]:lay["z_attn"]] = cols(gl0, NW)
        proj_ref[:, lay["z_attn"]:] = cols(za0, gl0)

    return _pc(
        body, name="norm_inproj", grid=(T // tm,),
        out_shape=(jax.ShapeDtypeStruct((T, lay["width"]), F32), jax.ShapeDtypeStruct((T, 3 * ATTN_W), BF16),
                   jax.ShapeDtypeStruct((T, D), BF16)),
        in_specs=[pl.BlockSpec((tm, D), lambda i: (i, 0)), pl.BlockSpec((1, D), lambda i: (0, 0)),
                  pl.BlockSpec((D, NW), lambda i: (0, 0))],
        out_specs=(pl.BlockSpec((tm, lay["width"]), lambda i: (i, 0)), pl.BlockSpec((tm, 3 * ATTN_W), lambda i: (i, 0)),
                   pl.BlockSpec((tm, D), lambda i: (i, 0))),
        compiler_params=_params("parallel"),
    )(x, g, w)


def _window_sums(xh, forward):
    n = xh.shape[0]
    sums, s, step = [], xh, 1
    for _ in POOL_WINDOWS:
        s = s + pltpu.roll(s, step if forward else n - step, 0)
        sums.append(s)
        step *= 2
    return sums


def _pooled(u, halo, row):
    sums = _window_sums(jnp.concatenate([halo, u], axis=0), True)
    out = []
    for g, w in enumerate(POOL_WINDOWS):
        cols = slice(g * POOL_G, (g + 1) * POOL_G)
        cnt = jnp.minimum(row + 1, w).astype(F32)
        out.append(sums[g][HALO:, cols] / cnt - u[:, cols])
    return out


def _pool_fwd(proj, pool_w, scale, R):
    T = proj.shape[0]

    def body(u_ref, z_ref, pw_ref, sc_ref, y_ref, halo_sc):
        i = pl.program_id(0)

        @pl.when(i == 0)
        def _():
            halo_sc[...] = jnp.zeros_like(halo_sc)

        u = u_ref[...]
        row = i * R + lax.broadcasted_iota(jnp.int32, (R, 1), 0)
        pooled = _pooled(u, halo_sc[...], row)
        mixed = jnp.concatenate(
            [jnp.dot(pooled[g].astype(BF16), pw_ref[g].astype(BF16), preferred_element_type=F32)
             for g in range(len(POOL_WINDOWS))], axis=1)
        z = z_ref[...]
        y_ref[...] = ((mixed * sc_ref[...]) * (z * _sigmoid(z))).astype(BF16)
        halo_sc[...] = u[R - HALO:, :]

    return _pc(
        body, name="pool_fwd", grid=(T // R,), out_shape=jax.ShapeDtypeStruct((T, POOL_W), BF16),
        in_specs=[pl.BlockSpec((R, POOL_W), lambda i: (i, 0)), pl.BlockSpec((R, POOL_W), lambda i: (i, 1)),
                  pl.BlockSpec((4, POOL_G, POOL_G), lambda i: (0, 0, 0)), pl.BlockSpec((1, POOL_W), lambda i: (0, 0))],
        out_specs=pl.BlockSpec((R, POOL_W), lambda i: (i, 0)),
        scratch_shapes=[pltpu.VMEM((HALO, POOL_W), F32)],
        compiler_params=_params("arbitrary"),
    )(proj, proj, pool_w, scale)


def _softplus(l):
    return jnp.maximum(l, 0.0) + jnp.log(1.0 + jnp.exp(-jnp.abs(l)))


def _first_last_step(n0, n1):
    return (lambda: jnp.logical_and(pl.program_id(0) == 0, pl.program_id(1) == 0),
            lambda: jnp.logical_and(pl.program_id(0) == n0 - 1, pl.program_id(1) == n1 - 1))


def _head_lanes():
    lane = lax.broadcasted_iota(jnp.int32, (1, LANE), 1)
    return [lane < HEAD_DIM, lane >= HEAD_DIM]


def _head_masks(q, scale):
    qf = q.astype(F32) * scale
    return [jnp.where(m, qf, 0.0).astype(BF16) for m in _head_lanes()]


def _wide(c, width):
    return jnp.concatenate([c] * (width // LANE), axis=1)


def _max_row_norm2(x, heads):
    sq = x.astype(F32) * x.astype(F32)
    return [jnp.max(jnp.sum(jnp.where(m, sq, 0.0), axis=1, keepdims=True), axis=0, keepdims=True) for m in heads]


def _fill_blocks(src_ref, nk, tk, transposed_sc=None, masked_sc=None, norm_sc=None):
    heads = _head_lanes()
    if norm_sc is not None:
        for a in range(2):
            norm_sc[a][...] = jnp.zeros_like(norm_sc[a])

    def step(j, carry):
        rows = pl.ds(pl.multiple_of(j * tk, tk), tk)
        blk = src_ref[rows, :]
        if norm_sc is not None:
            for a, n2 in enumerate(_max_row_norm2(blk, heads)):
                norm_sc[a][...] = jnp.maximum(norm_sc[a][...], n2)
        if transposed_sc is not None:
            transposed_sc[j] = blk.astype(F32).T.astype(BF16)
        if masked_sc is not None:
            for a in range(2):
                masked_sc[a, rows, :] = jnp.where(heads[a], blk, jnp.zeros_like(blk))
        return carry

    lax.fori_loop(0, nk, step, 0)


def _attn_fwd(projb, proj, tq, rider=None):
    T = projb.shape[0]
    nq = T // tq
    tk, nk = tq, nq
    assert nk < LANE
    n_pairs = ATTN_W // LANE
    zb = (proj.shape[1] - ATTN_W) // LANE

    def body(q_ref, k_ref, v_ref, za_ref, o_ref, y_ref, c_ref, ws_ref, bs_ref, kT_sc, vm_sc, *per_head):
        c_sc, cm_sc, o_sc, kn_sc, l_sc, w_sc = (per_head[2 * n:2 * n + 2] for n in range(6))
        i = pl.program_id(1)

        @pl.when(i == 0)
        def _():
            _fill_blocks(k_ref, nk, tk, transposed_sc=kT_sc, norm_sc=kn_sc)
            _fill_blocks(v_ref, nk, tk, masked_sc=vm_sc)

        qs = _head_masks(q_ref[...], HEAD_DIM ** -0.5)
        lane = lax.broadcasted_iota(jnp.int32, (1, LANE), 1)
        valid = lax.broadcasted_iota(jnp.int32, (tq, tk), 1) < lax.broadcasted_iota(jnp.int32, (tq, tk), 0)
        suffix = (lax.broadcasted_iota(jnp.int32, (tk, tk), 0) >= lax.broadcasted_iota(jnp.int32, (tk, tk), 1)).astype(BF16)
        for a in range(2):
            c_sc[a][...] = jnp.zeros_like(c_sc[a])
            o_sc[a][...] = jnp.zeros_like(o_sc[a])
            cm_sc[a][...] = jnp.full(cm_sc[a].shape, NO_CARRY, F32)
        l_max = [jnp.sqrt(qn * kn_sc[a][...]) for a, qn in enumerate(_max_row_norm2(qs[0] + qs[1], _head_lanes()))]

        def logits(j, slot):
            kT = kT_sc[j]
            for a in range(2):
                l_sc[a][slot] = jnp.dot(qs[a], kT, preferred_element_type=F32)

        def values(j, slot):
            rows = pl.ds(pl.multiple_of(j * tk, tk), tk)
            for a in range(2):
                o_sc[a][...] += jnp.dot(w_sc[a][slot], vm_sc[a, rows, :], preferred_element_type=F32)

        def softplus(slot, masked):
            out = []
            for a in range(2):
                x = _softplus(l_sc[a][slot])
                out.append(jnp.where(valid, x, 0.0) if masked else x)
            return out

        def finish_weights(j, slot, sp, inc, mask, keep=None, cols=None):
            for a in range(2):
                c = c_sc[a][...]
                l = l_sc[a][slot]
                w = jnp.exp(l - inc[a] - _wide(c, tk))
                if mask is not None:
                    w = jnp.where(mask, w, 0.0)
                add = inc[a][:, 0:1]
                if keep is not None:
                    w = jnp.where(keep, w, 0.0)
                    add = jnp.where(keep, add, 0.0)
                w_sc[a][slot] = w.astype(BF16)
                if cols is not None:
                    beta = jnp.exp(l - sp[a])
                    ws_ref[a, 0, :, cols] = w
                    bs_ref[a, 0, :, cols] = beta if mask is None else jnp.where(mask, beta, 0.0)
                cm_sc[a][...] = jnp.where(lane == j, c, cm_sc[a][...])
                c_sc[a][...] = c + add

        def weights(j, slot):
            sp = softplus(slot, False)
            inc = [jnp.dot(sp[a].astype(BF16), suffix, preferred_element_type=F32) for a in range(2)]
            finish_weights(j, slot, sp, inc, None)

        def more():
            live = [jnp.min(c_sc[a][...], axis=0, keepdims=True) - l_max[a][0:1, :] <= ZERO_WEIGHT for a in range(2)]
            return jnp.max(jnp.where(jnp.logical_or(live[0], live[1]), 1, 0))

        logits(i, 0)
        logits(jnp.maximum(i - 1, 0), 1)
        sp = softplus(0, True) + softplus(1, False)
        inc = [jnp.dot(x.astype(BF16), suffix, preferred_element_type=F32) for x in sp]
        finish_weights(i, 0, sp[:2], inc[:2], valid, cols=slice(tk, 2 * tk))
        finish_weights(i - 1, 1, sp[2:], inc[2:], None, keep=i > 0, cols=slice(0, tk))
        values(i, 0)
        logits(jnp.maximum(i - 2, 0), 0)

        def step(t, slot):
            logits(jnp.maximum(i - t - 1, 0), 1 - slot)
            values(i - t + 1, 1 - slot)
            weights(i - t, slot)

        def two_steps(carry):
            tt, _ = carry
            step(2 * tt + 2, 0)
            step(2 * tt + 3, 1)
            return tt + 1, more()

        pairs, go = lax.while_loop(lambda c: jnp.logical_and(2 * c[0] + 3 <= i, c[1] > 0), two_steps, (0, more()))
        done = 1 + 2 * pairs
        one_more = jnp.logical_and(done + 1 == i, go > 0)

        @pl.when(one_more)
        def _():
            step(i, 0)
            values(0, 0)

        @pl.when(jnp.logical_not(one_more))
        def _():
            values(jnp.maximum(i - done, 0), 1)

        o = o_sc[0][...] + o_sc[1][...]
        o_ref[...] = o
        za = za_ref[...]
        y_ref[...] = (o * (za * _sigmoid(za))).astype(BF16)
        c_ref[0, 0] = jnp.where(lane == LANE - 1, l_max[0][0:1, :], cm_sc[0][...])
        c_ref[1, 0] = jnp.where(lane == LANE - 1, l_max[1][0:1, :], cm_sc[1][...])

    scratch = ([pltpu.VMEM((nk, LANE, tk), BF16), pltpu.VMEM((2, T, LANE), BF16)]
               + [pltpu.VMEM((tq, LANE), F32)] * 6 + [pltpu.VMEM((8, LANE), F32)] * 2
               + [pltpu.VMEM((2, tq, tk), F32)] * 2 + [pltpu.VMEM((2, tq, tk), BF16)] * 2)
    r_in, r_in_specs, r_out, r_out_specs, r_sems = _rider_specs(rider)
    body = _with_rider(body, 4, 5, len(scratch), rider, *_first_last_step(n_pairs, nq))
    kept = jax.ShapeDtypeStruct((2 * n_pairs, nq, tq, 2 * tk), F32)
    kept_spec = pl.BlockSpec((2, 1, tq, 2 * tk), lambda p, i: (p, i, 0, 0))
    return _pc(
        body, name="attn_fwd", grid=(n_pairs, nq),
        out_shape=tuple([jax.ShapeDtypeStruct((T, ATTN_W), F32), jax.ShapeDtypeStruct((T, ATTN_W), BF16),
                         jax.ShapeDtypeStruct((2 * n_pairs, nq, tq, LANE), F32), kept, kept] + r_out),
        in_specs=[pl.BlockSpec((tq, LANE), lambda p, i: (i, p)),
                  pl.BlockSpec((T, LANE), lambda p, i: (0, n_pairs + p)),
                  pl.BlockSpec((T, LANE), lambda p, i: (0, 2 * n_pairs + p)),
                  pl.BlockSpec((tq, LANE), lambda p, i: (i, zb + p))] + r_in_specs,
        out_specs=tuple([pl.BlockSpec((tq, LANE), lambda p, i: (i, p)), pl.BlockSpec((tq, LANE), lambda p, i: (i, p)),
                         pl.BlockSpec((2, 1, tq, LANE), lambda p, i: (p, i, 0, 0)), kept_spec, kept_spec] + r_out_specs),
        scratch_shapes=scratch + r_sems,
        compiler_params=_params("arbitrary", "arbitrary"),
    )(projb, projb, projb, proj, *r_in)


def _gates(gl0, gl1, bg, D):
    return _sigmoid(gl0 + bg[:, :D]), _sigmoid(gl1 + bg[:, D:])


def _merge_fwd(y_pool, y_attn, w_pu, w_au, proj, b_gate, tm):
    T = y_pool.shape[0]
    D = w_pu.shape[1]
    gb = _proj_layout(D)["gates"] // D

    def body(yp_ref, ya_ref, wpu_ref, wau_ref, gl0_ref, gl1_ref, bg_ref, m_ref):
        p = jnp.dot(yp_ref[...], wpu_ref[...], preferred_element_type=F32)
        a = jnp.dot(ya_ref[...], wau_ref[...], preferred_element_type=F32)
        g0, g1 = _gates(gl0_ref[...], gl1_ref[...], bg_ref[...], D)
        m_ref[...] = (g0 * p + g1 * a).astype(BF16)

    row = lambda i: (i, 0)
    fixed = lambda i: (0, 0)
    return _pc(
        body, name="merge_fwd", grid=(T // tm,), out_shape=jax.ShapeDtypeStruct((T, D), BF16),
        in_specs=[pl.BlockSpec((tm, POOL_W), row), pl.BlockSpec((tm, ATTN_W), row),
                  pl.BlockSpec((POOL_W, D), fixed), pl.BlockSpec((ATTN_W, D), fixed),
                  pl.BlockSpec((tm, D), lambda i: (i, gb)), pl.BlockSpec((tm, D), lambda i: (i, gb + 1)),
                  pl.BlockSpec((1, 2 * D), fixed)],
        out_specs=pl.BlockSpec((tm, D), row),
        compiler_params=_params("parallel"),
    )(y_pool, y_attn, w_pu, w_au, proj, proj, b_gate)


def _out_proj_final_loss(merged, w_out, x, g, target, tm):
    T, D = x.shape

    def body(m_ref, w_ref, x_ref, g_ref, t_ref, dx_ref, dg_ref, loss_ref):
        @pl.when(pl.program_id(0) == 0)
        def _():
            dg_ref[...] = jnp.zeros_like(dg_ref)
            loss_ref[...] = jnp.zeros_like(loss_ref)

        xv = x_ref[...] + jnp.dot(m_ref[...], w_ref[...], preferred_element_type=F32)
        gv = g_ref[...]
        r = lax.rsqrt(jnp.mean(xv * xv, axis=-1, keepdims=True) + RMS_EPS)
        xh = xv * r
        d = xh * gv - t_ref[...]
        loss_ref[...] += 0.5 * jnp.sum(jnp.mean(d * d, axis=-1, keepdims=True), axis=0, keepdims=True)
        dy = d * (1.0 / D)
        dg_ref[...] += jnp.sum(dy * xh, axis=0, keepdims=True)
        dh = dy * gv
        dx_ref[...] = r * (dh - xh * jnp.mean(dh * xh, axis=-1, keepdims=True))

    row = lambda i: (i, 0)
    fixed = lambda i: (0, 0)
    return _pc(
        body, name="out_proj_final_loss", grid=(T // tm,),
        out_shape=(jax.ShapeDtypeStruct((T, D), F32), jax.ShapeDtypeStruct((1, D), F32),
                   jax.ShapeDtypeStruct((8, LANE), F32)),
        in_specs=[pl.BlockSpec((tm, D), row), pl.BlockSpec((D, D), fixed), pl.BlockSpec((tm, D), row),
                  pl.BlockSpec((1, D), fixed), pl.BlockSpec((tm, D), row)],
        out_specs=(pl.BlockSpec((tm, D), row), pl.BlockSpec((1, D), fixed), pl.BlockSpec((8, LANE), fixed)),
        compiler_params=_params("arbitrary"),
    )(merged, w_out, x, g, target)


def _merge_bwd(dxo, w_out, y_pool, y_attn, w_pu, w_au, proj, b_gate, tm):
    T, D = dxo.shape
    gb = _proj_layout(D)["gates"] // D

    def body(dxo_ref, wout_ref, yp_ref, ya_ref, wpu_ref, wau_ref, gl0_ref, gl1_ref, bg_ref,
             dp_ref, da_ref, dgl_ref, dbg_ref):
        @pl.when(pl.program_id(0) == 0)
        def _():
            dbg_ref[...] = jnp.zeros_like(dbg_ref)

        dmv = lax.dot_general(dxo_ref[...].astype(BF16), wout_ref[...], NT_DIMS, preferred_element_type=F32)
        p = jnp.dot(yp_ref[...], wpu_ref[...], preferred_element_type=F32)
        a = jnp.dot(ya_ref[...], wau_ref[...], preferred_element_type=F32)
        g0, g1 = _gates(gl0_ref[...], gl1_ref[...], bg_ref[...], D)
        dp_ref[...] = (dmv * g0).astype(BF16)
        da_ref[...] = (dmv * g1).astype(BF16)
        dgl0 = dmv * p * (g0 * (1.0 - g0))
        dgl1 = dmv * a * (g1 * (1.0 - g1))
        dgl_ref[:, :D] = dgl0.astype(BF16)
        dgl_ref[:, D:] = dgl1.astype(BF16)
        dbg_ref[:, :D] += jnp.sum(dgl0, axis=0, keepdims=True)
        dbg_ref[:, D:] += jnp.sum(dgl1, axis=0, keepdims=True)

    row = lambda i: (i, 0)
    fixed = lambda i: (0, 0)
    return _pc(
        body, name="merge_bwd", grid=(T // tm,),
        out_shape=(jax.ShapeDtypeStruct((T, D), BF16), jax.ShapeDtypeStruct((T, D), BF16),
                   jax.ShapeDtypeStruct((T, 2 * D), BF16), jax.ShapeDtypeStruct((1, 2 * D), F32)),
        in_specs=[pl.BlockSpec((tm, D), row), pl.BlockSpec((D, D), fixed),
                  pl.BlockSpec((tm, POOL_W), row), pl.BlockSpec((tm, ATTN_W), row),
                  pl.BlockSpec((POOL_W, D), fixed), pl.BlockSpec((ATTN_W, D), fixed),
                  pl.BlockSpec((tm, D), lambda i: (i, gb)), pl.BlockSpec((tm, D), lambda i: (i, gb + 1)),
                  pl.BlockSpec((1, 2 * D), fixed)],
        out_specs=(pl.BlockSpec((tm, D), row), pl.BlockSpec((tm, D), row), pl.BlockSpec((tm, 2 * D), row),
                   pl.BlockSpec((1, 2 * D), fixed)),
        compiler_params=_params("arbitrary"),
    )(dxo, w_out, y_pool, y_attn, w_pu, w_au, proj, proj, b_gate)


def _pool_bwd(proj, dp, w_pu, pool_w, scale, R):
    T = proj.shape[0]
    D = w_pu.shape[1]
    nb = T // R
    hb = R // HALO

    def body(u_ref, up_ref, z_ref, dp_ref, wpu_ref, pw_ref, sc_ref, du_ref, dz_ref, dpw_ref, dsc_ref, halo_sc):
        i = pl.program_id(0)
        rb = nb - 1 - i

        @pl.when(i == 0)
        def _():
            halo_sc[...] = jnp.zeros_like(halo_sc)
            dpw_ref[...] = jnp.zeros_like(dpw_ref)
            dsc_ref[...] = jnp.zeros_like(dsc_ref)

        u = u_ref[...]
        row = rb * R + lax.broadcasted_iota(jnp.int32, (R, 1), 0)
        before = jnp.where(rb > 0, up_ref[...], 0.0)
        pooled = _pooled(u, before, row)
        pw = [pw_ref[g].astype(BF16) for g in range(len(POOL_WINDOWS))]
        mixed = jnp.concatenate(
            [jnp.dot(pooled[g].astype(BF16), pw[g], preferred_element_type=F32) for g in range(len(POOL_WINDOWS))],
            axis=1)
        sc = sc_ref[...]
        silu, dsilu = _silu_and_grad(z_ref[...])
        dyv = lax.dot_general(dp_ref[...], wpu_ref[...], NT_DIMS, preferred_element_type=F32)
        dmp = dyv * silu
        dz_ref[...] = (dyv * (mixed * sc) * dsilu).astype(BF16)
        dsc_ref[...] += jnp.sum(dmp * mixed, axis=0, keepdims=True)
        dmixed = (dmp * sc).astype(BF16)
        dpn = []
        dpooled = []
        for g, w in enumerate(POOL_WINDOWS):
            cols = slice(g * POOL_G, (g + 1) * POOL_G)
            dpw_ref[g] += lax.dot_general(pooled[g].astype(BF16), dmixed[:, cols], TN_DIMS,
                                          preferred_element_type=F32)
            dpg = lax.dot_general(dmixed[:, cols], pw[g], NT_DIMS, preferred_element_type=F32)
            dpooled.append(dpg)
            dpn.append(dpg / jnp.minimum(row + 1, w).astype(F32))
        dpn = jnp.concatenate(dpn, axis=1)
        sums = _window_sums(jnp.concatenate([dpn, halo_sc[...]], axis=0), False)
        du_ref[...] = jnp.concatenate(
            [sums[g][:R, g * POOL_G:(g + 1) * POOL_G] - dpooled[g] for g in range(len(POOL_WINDOWS))],
            axis=1).astype(BF16)
        halo_sc[...] = dpn[:HALO, :]

    rev = lambda i: (nb - 1 - i, 0)
    return _pc(
        body, name="pool_bwd", grid=(nb,),
        out_shape=(jax.ShapeDtypeStruct((T, POOL_W), BF16), jax.ShapeDtypeStruct((T, POOL_W), BF16),
                   jax.ShapeDtypeStruct((4, POOL_G, POOL_G), F32), jax.ShapeDtypeStruct((1, POOL_W), F32)),
        in_specs=[pl.BlockSpec((R, POOL_W), rev),
                  pl.BlockSpec((HALO, POOL_W), lambda i: (jnp.maximum((nb - 1 - i) * hb - 1, 0), 0)),
                  pl.BlockSpec((R, POOL_W), lambda i: (nb - 1 - i, 1)),
                  pl.BlockSpec((R, D), rev), pl.BlockSpec((POOL_W, D), lambda i: (0, 0)),
                  pl.BlockSpec((4, POOL_G, POOL_G), lambda i: (0, 0, 0)), pl.BlockSpec((1, POOL_W), lambda i: (0, 0))],
        out_specs=(pl.BlockSpec((R, POOL_W), rev), pl.BlockSpec((R, POOL_W), rev),
                   pl.BlockSpec((4, POOL_G, POOL_G), lambda i: (0, 0, 0)), pl.BlockSpec((1, POOL_W), lambda i: (0, 0))),
        scratch_shapes=[pltpu.VMEM((HALO, POOL_W), F32)],
        compiler_params=_params("arbitrary"),
    )(proj, proj, proj, dp, w_pu, pool_w, scale)


def _attn_gate_bwd(da, w_au, o, proj, tm):
    T, D = da.shape
    zb = (proj.shape[1] - ATTN_W) // ATTN_W

    def body(da_ref, wau_ref, o_ref, za_ref, do_ref, dza_ref):
        silu, dsilu = _silu_and_grad(za_ref[...])
        dyv = lax.dot_general(da_ref[...], wau_ref[...], NT_DIMS, preferred_element_type=F32)
        do_ref[...] = (dyv * silu).astype(BF16)
        dza_ref[...] = (dyv * o_ref[...] * dsilu).astype(BF16)

    row = lambda i: (i, 0)
    return _pc(
        body, name="attn_gate_bwd", grid=(T // tm,),
        out_shape=(jax.ShapeDtypeStruct((T, ATTN_W), BF16), jax.ShapeDtypeStruct((T, ATTN_W), BF16)),
        in_specs=[pl.BlockSpec((tm, D), row), pl.BlockSpec((ATTN_W, D), lambda i: (0, 0)),
                  pl.BlockSpec((tm, ATTN_W), row), pl.BlockSpec((tm, ATTN_W), lambda i: (i, zb))],
        out_specs=(pl.BlockSpec((tm, ATTN_W), row), pl.BlockSpec((tm, ATTN_W), row)),
        compiler_params=_params("parallel"),
    )(da, w_au, o, proj)


def _attn_bwd(projb, do, carries, kept_w, kept_beta, tq, rider=None):
    T = projb.shape[0]
    nq = T // tq
    tk, nk = tq, nq
    n_pairs = ATTN_W // LANE
    scale = HEAD_DIM ** -0.5

    def body(q_ref, k_ref, v_ref, do_ref, c_ref, ws_ref, bs_ref, dq_ref, dk_ref, dv_ref, kT_sc, vT_sc, km_sc, dkT_ref, dvT_ref,
             *per_head):
        f_sc, dq_sc, kn_sc, l_sc, dw_sc, dl_sc, w_sc = (per_head[2 * n:2 * n + 2] for n in range(7))
        i = pl.program_id(1)

        @pl.when(i == 0)
        def _():
            _fill_blocks(k_ref, nk, tk, transposed_sc=kT_sc, masked_sc=km_sc)
            _fill_blocks(v_ref, nk, tk, transposed_sc=vT_sc)
            dkT_ref[...] = jnp.zeros_like(dkT_ref)
            dvT_ref[...] = jnp.zeros_like(dvT_ref)

        qs = _head_masks(q_ref[...], scale)
        dos = _head_masks(do_ref[...], 1.0)
        qT = [x.astype(F32).T.astype(BF16) for x in qs]
        doT = [x.astype(F32).T.astype(BF16) for x in dos]
        lane = lax.broadcasted_iota(jnp.int32, (1, LANE), 1)
        valid = lax.broadcasted_iota(jnp.int32, (tq, tk), 1) < lax.broadcasted_iota(jnp.int32, (tq, tk), 0)
        kk0 = lax.broadcasted_iota(jnp.int32, (tk, tk), 0)
        kk1 = lax.broadcasted_iota(jnp.int32, (tk, tk), 1)
        suffix = (kk0 >= kk1).astype(BF16)
        prefix = (kk0 <= kk1).astype(BF16)
        for a in range(2):
            f_sc[a][...] = jnp.zeros_like(f_sc[a])
            dq_sc[a][...] = jnp.zeros_like(dq_sc[a])
        live = lane == i
        for a in range(2):
            cmin = jnp.min(c_ref[a, 0], axis=0, keepdims=True)
            l_max = jnp.max(jnp.where(lane == LANE - 1, cmin, -NO_CARRY), axis=1, keepdims=True)
            live = jnp.logical_or(live, cmin - l_max <= ZERO_WEIGHT)
        t0 = jnp.min(jnp.where(jnp.logical_and(live, lane <= i), lane, i))
        n = i - t0

        def products(j, slot, with_logits=True):
            kT = kT_sc[j]
            vT = vT_sc[j]
            for a in range(2):
                if with_logits:
                    l_sc[a][slot] = jnp.dot(qs[a], kT, preferred_element_type=F32)
                dw_sc[a][slot] = jnp.dot(dos[a], vT, preferred_element_type=F32)

        def gradients(j, slot):
            rows = pl.ds(pl.multiple_of(j * tk, tk), tk)
            dkT = []
            dvT = []
            for a in range(2):
                dlb = dl_sc[a][slot]
                dq_sc[a][...] += jnp.dot(dlb, km_sc[a, rows, :], preferred_element_type=F32)
                dkT.append(jnp.dot(qT[a], dlb, preferred_element_type=F32))
                dvT.append(jnp.dot(doT[a], w_sc[a][slot], preferred_element_type=F32))
            dkT_ref[j] += dkT[0] + dkT[1]
            dvT_ref[j] += dvT[0] + dvT[1]

        def elementwise(j, slot, masked):
            sp, inc, e, beta, p = [None] * 2, [None] * 2, [None] * 2, [None] * 2, [None] * 2
            for a in range(2):
                x = _softplus(l_sc[a][slot])
                sp[a] = jnp.where(valid, x, 0.0) if masked else x
            for a in range(2):
                inc[a] = jnp.dot(sp[a].astype(BF16), suffix, preferred_element_type=F32)
            for a in range(2):
                l = l_sc[a][slot]
                c = jnp.sum(jnp.where(lane == j, c_ref[a, 0], 0.0), axis=1, keepdims=True)
                w = jnp.exp(l - inc[a] - c)
                if masked:
                    w = jnp.where(valid, w, 0.0)
                w_sc[a][slot] = w.astype(BF16)
                beta[a] = jnp.exp(l - sp[a])
                e[a] = w * dw_sc[a][slot]
            for a in range(2):
                p[a] = jnp.dot(e[a].astype(BF16), prefix, preferred_element_type=F32)
            for a in range(2):
                f = f_sc[a][...]
                dl = e[a] - beta[a] * (p[a] + _wide(f, tk))
                if masked:
                    dl = jnp.where(valid, dl, 0.0)
                dl_sc[a][slot] = dl.astype(BF16)
                f_sc[a][...] = f + p[a][:, tk - 1:tk]

        def kept_blocks(tiles):
            keys = [(n_, a) for n_ in range(len(tiles)) for a in range(2)]
            for j, slot, cols in tiles:
                products(j, slot, with_logits=False)
            e, p = {}, {}
            for n_, a in keys:
                j, slot, cols = tiles[n_]
                w = ws_ref[a, 0, :, cols]
                w_sc[a][slot] = w.astype(BF16)
                e[n_, a] = w * dw_sc[a][slot]
            for k in keys:
                p[k] = jnp.dot(e[k].astype(BF16), prefix, preferred_element_type=F32)
            for a in range(2):
                f = f_sc[a][...]
                for n_, (j, slot, cols) in enumerate(tiles):
                    dl = e[n_, a] - bs_ref[a, 0, :, cols] * (p[n_, a] + _wide(f, tk))
                    dl_sc[a][slot] = dl.astype(BF16)
                    f = f + p[n_, a][:, tk - 1:tk]
                f_sc[a][...] = f

        def step(r, slot):
            products(t0 + r + 1, 1 - slot)
            gradients(t0 + jnp.maximum(r - 1, 0), 1 - slot)
            elementwise(t0 + r, slot, False)

        def last_two(slot, pending=True):
            if pending:
                gradients(i - 2, slot)
            kept_blocks([(i - 1, 1 - slot, slice(0, tk)), (i, slot, slice(tk, 2 * tk))])
            gradients(i - 1, 1 - slot)
            gradients(i, slot)

        @pl.when(n >= 2)
        def _():
            for a in range(2):
                dl_sc[a][1] = jnp.zeros((tq, tk), BF16)
                w_sc[a][1] = jnp.zeros((tq, tk), BF16)
            products(t0, 0)

        def two_steps(tt, carry):
            step(2 * tt, 0)
            step(2 * tt + 1, 1)
            return carry

        lax.fori_loop(0, jnp.maximum(n - 1, 0) // 2, two_steps, 0)

        @pl.when(n == 0)
        def _():
            kept_blocks([(i, 0, slice(tk, 2 * tk))])
            gradients(i, 0)

        @pl.when(n == 1)
        def _():
            last_two(1, pending=False)

        @pl.when(jnp.logical_and(n > 1, n % 2 == 1))
        def _():
            last_two(1)

        @pl.when(jnp.logical_and(n > 0, n % 2 == 0))
        def _():
            step(n - 2, 0)
            last_two(0)

        dq_ref[...] = ((dq_sc[0][...] + dq_sc[1][...]) * scale).astype(BF16)

        @pl.when(i == nq - 1)
        def _():
            def untranspose(j, carry):
                rows = pl.ds(pl.multiple_of(j * tk, tk), tk)
                dk_ref[rows, :] = dkT_ref[j].T.astype(BF16)
                dv_ref[rows, :] = dvT_ref[j].T.astype(BF16)
                return carry

            lax.fori_loop(0, nk, untranspose, 0)

    scratch = ([pltpu.VMEM((nk, LANE, tk), BF16), pltpu.VMEM((nk, LANE, tk), BF16), pltpu.VMEM((2, T, LANE), BF16),
                pltpu.VMEM((nk, LANE, tk), F32), pltpu.VMEM((nk, LANE, tk), F32)]
               + [pltpu.VMEM((tq, LANE), F32)] * 4 + [pltpu.VMEM((8, LANE), F32)] * 2
               + [pltpu.VMEM((2, tq, tk), F32)] * 4 + [pltpu.VMEM((2, tq, tk), BF16)] * 4)
    r_in, r_in_specs, r_out, r_out_specs, r_sems = _rider_specs(rider)
    body = _with_rider(body, 7, 3, len(scratch), rider, *_first_last_step(n_pairs, nq))
    kept_spec = pl.BlockSpec((2, 1, tq, 2 * tk), lambda p, i: (p, i, 0, 0))
    return _pc(
        body, name="attn_bwd", grid=(n_pairs, nq),
        out_shape=tuple([jax.ShapeDtypeStruct((T, ATTN_W), BF16)] * 3 + r_out),
        in_specs=[pl.BlockSpec((tq, LANE), lambda p, i: (i, p)),
                  pl.BlockSpec((T, LANE), lambda p, i: (0, n_pairs + p)),
                  pl.BlockSpec((T, LANE), lambda p, i: (0, 2 * n_pairs + p)),
                  pl.BlockSpec((tq, LANE), lambda p, i: (i, p)),
                  pl.BlockSpec((2, 1, tq, LANE), lambda p, i: (p, i, 0, 0)), kept_spec, kept_spec] + r_in_specs,
        out_specs=tuple([pl.BlockSpec((tq, LANE), lambda p, i: (i, p)), pl.BlockSpec((T, LANE), lambda p, i: (0, p)),
                         pl.BlockSpec((T, LANE), lambda p, i: (0, p))] + r_out_specs),
        scratch_shapes=scratch + r_sems,
        compiler_params=_params("arbitrary", "arbitrary"),
    )(projb, projb, projb, do, carries, kept_w, kept_beta, *r_in)


def _dh_norm_bwd(segs, w, x, g, dxo, tm, rider=None):
    T, D = x.shape
    nm = T // tm
    n_seg = len(segs)
    offs = [sum(y.shape[1] for y in segs[:n]) for n in range(n_seg + 1)]
    assert offs[-1] == w.shape[1]

    def body(*refs):
        seg_refs = refs[:n_seg]
        w_ref, x_ref, g_ref, dxo_ref, dx_ref, dg_ref = refs[n_seg:]

        @pl.when(pl.program_id(0) == 0)
        def _():
            dg_ref[...] = jnp.zeros_like(dg_ref)

        dhv = None
        for n in range(n_seg):
            part = lax.dot_general(seg_refs[n][...], w_ref[:, offs[n]:offs[n + 1]], NT_DIMS, preferred_element_type=F32)
            dhv = part if dhv is None else dhv + part
        xv = x_ref[...]
        r = lax.rsqrt(jnp.mean(xv * xv, axis=-1, keepdims=True) + RMS_EPS)
        xh = xv * r
        dg_ref[...] += jnp.sum(dhv * xh, axis=0, keepdims=True)
        dhg = dhv * g_ref[...]
        dx_ref[...] = dxo_ref[...] + r * (dhg - xh * jnp.mean(dhg * xh, axis=-1, keepdims=True))

    r_in, r_in_specs, r_out, r_out_specs, r_sems = _rider_specs(rider)
    body = _with_rider(body, n_seg + 4, 2, 0, rider, lambda: pl.program_id(0) == 0, lambda: pl.program_id(0) == nm - 1)
    row = lambda i: (i, 0)
    fixed = lambda i: (0, 0)
    return _pc(
        body, name="d_h_norm_bwd", grid=(nm,),
        out_shape=tuple([jax.ShapeDtypeStruct((T, D), F32), jax.ShapeDtypeStruct((1, D), F32)] + r_out),
        in_specs=[pl.BlockSpec((tm, y.shape[1]), row) for y in segs]
        + [pl.BlockSpec(w.shape, fixed), pl.BlockSpec((tm, D), row), pl.BlockSpec((1, D), fixed), pl.BlockSpec((tm, D), row)]
        + r_in_specs,
        out_specs=tuple([pl.BlockSpec((tm, D), row), pl.BlockSpec((1, D), fixed)] + r_out_specs),
        scratch_shapes=r_sems,
        compiler_params=_params("arbitrary"),
    )(*segs, w, x, g, dxo, *r_in)


def _adamw(pieces, w, m, v, name):
    G = len(pieces)
    rows, cols = pieces[0].shape[1:]
    assert w.shape == (G * rows, cols)
    br = rows
    while br * cols > 65536 and br % 16 == 0:
        br //= 2
    nb = rows // br
    c1 = 1.0 / (1.0 - ADAM_B1 ** ADAM_STEP)
    c2 = 1.0 / (1.0 - ADAM_B2 ** ADAM_STEP)

    def body(*refs):
        p_refs = refs[:G]
        w_ref, m_ref, v_ref, g_ref, d_ref, nm_ref, nv_ref = refs[G:]
        g = None
        for n, p_ref in enumerate(p_refs):
            gn = p_ref[0].astype(F32)
            for s in range(1, N_DEV):
                gn = gn + p_ref[s].astype(F32)
            g = gn if g is None else jnp.where(pl.program_id(0) == n, gn, g)
        nm = ADAM_B1 * m_ref[...] + (1.0 - ADAM_B1) * g
        nv = ADAM_B2 * v_ref[...] + (1.0 - ADAM_B2) * (g * g)
        g_ref[...] = g
        nm_ref[...] = nm
        nv_ref[...] = nv
        d_ref[...] = -ADAM_LR * ((nm * c1) / (jnp.sqrt(nv * c2) + ADAM_EPS) + ADAM_WD * w_ref[...])

    blk = pl.BlockSpec((br, cols), lambda n, i: (n * nb + i, 0))
    shape = jax.ShapeDtypeStruct((G * rows, cols), F32)
    return _pc(
        body, name=name, grid=(G, nb), out_shape=(shape, shape, shape, shape),
        in_specs=[pl.BlockSpec((N_DEV, br, cols), lambda n, i: (0, i, 0))] * G + [blk, blk, blk],
        out_specs=(blk, blk, blk, blk),
        compiler_params=_params("parallel", "parallel"),
    )(*pieces, w, m, v)


def _rows128(a):
    flat = a.reshape(-1)
    n = flat.shape[0]
    padded = -(-n // (8 * LANE)) * (8 * LANE)
    if padded != n:
        flat = jnp.concatenate([flat, jnp.zeros((padded - n,), flat.dtype)])
    return flat.reshape(-1, LANE)


def _pack(parts):
    return jnp.concatenate([_rows128(p) for p in parts], axis=0)


def _unpack(packed, like):
    out, r = [], 0
    for a in like:
        n = a.size
        nr = -(-n // (8 * LANE)) * 8
        out.append(packed[r:r + nr].reshape(-1)[:n].reshape(a.shape))
        r += nr
    return out


def kernel(x, norm_g, w_in, b_gate, pool_w, pool_scale, w_pool_up, w_attn_up, w_out, final_g, loss_target, m_norm_g, m_w_in, m_b_gate, m_pool_w, m_pool_scale, m_w_pool_up, m_w_attn_up, m_w_out, m_final_g, v_norm_g, v_w_in, v_b_gate, v_pool_w, v_pool_scale, v_w_pool_up, v_w_attn_up, v_w_out, v_final_g):
    L = norm_g.shape[0]
    T, D = x.shape[1], x.shape[2]
    NW = w_in.shape[2] * N_DEV
    assert NW == 2 * POOL_W + 4 * ATTN_W + 2 * D and x.shape[0] == 1
    tm = min(512, T)
    tq = min(256, T // 2)
    x0 = x.reshape(T, D)
    target = loss_target.reshape(T, D)

    assert L >= 2
    win_first = jnp.transpose(_gather_two_level(w_in[0].astype(BF16), "gather_w_in0"), (1, 0, 2)).reshape(D, NW)
    rest = [w_in[1:].astype(BF16), w_pool_up.astype(BF16), w_attn_up.astype(BF16), w_out.astype(BF16)]

    saved = []
    xl = x0
    for l in range(L):
        proj, projb, h = _norm_inproj(xl, norm_g[l:l + 1], win_first if l == 0 else win_rest[l - 1], min(256, T))
        y_pool = _pool_fwd(proj, pool_w[l], pool_scale[l:l + 1], tm)
        if l == 0:
            o, y_attn, carries, kept_w, kept_b, g_in, g_pu, g_au, g_out = _attn_fwd(projb, proj, tq, rider=(rest, [False] * 4))
            win_rest = jnp.transpose(g_in, (1, 2, 0, 3)).reshape(L - 1, D, NW)
            wpu_full = jnp.transpose(g_pu, (1, 2, 0, 3)).reshape(L, POOL_W, D)
            wau_full = jnp.transpose(g_au, (1, 2, 0, 3)).reshape(L, ATTN_W, D)
            wout_full = jnp.transpose(g_out, (1, 0, 2, 3)).reshape(L, D, D)
        else:
            o, y_attn, carries, kept_w, kept_b = _attn_fwd(projb, proj, tq)
        merged = _merge_fwd(y_pool, y_attn, wpu_full[l], wau_full[l], proj, b_gate[l:l + 1], min(256, T))
        saved.append((xl, proj, projb, h, y_pool, o, y_attn, (carries, kept_w, kept_b), merged))
        if l < L - 1:
            xl = _mm_nn_res(merged, wout_full[l], xl, tm, "out_proj")

    dx, d_final_g, loss_part = _out_proj_final_loss(merged, wout_full[L - 1], xl, final_g.reshape(1, D), target, tm)

    d_norm_g, d_b_gate, d_pool_w, d_pool_scale = [None] * L, [None] * L, [None] * L, [None] * L
    d_win, d_wpu, d_wau, d_wout = [None] * L, [None] * L, [None] * L, [None] * L
    small_like = [norm_g, b_gate, pool_w, pool_scale, final_g, jnp.zeros((8, LANE), F32)]
    for l in reversed(range(L)):
        xin, proj, projb, h, y_pool, o, y_attn, carries, merged = saved[l]
        win_l = win_first if l == 0 else win_rest[l - 1]
        d_wout[l] = _mm_tn(merged, dx, 1, D, min(1024, T), "d_w_out").reshape(N_DEV, D // N_DEV, D)
        dp, da, dgl, d_b_gate[l] = _merge_bwd(dx, wout_full[l], y_pool, y_attn, wpu_full[l], wau_full[l], proj,
                                              b_gate[l:l + 1], min(256, T))
        d_wpu[l] = _mm_tn(y_pool, dp, N_DEV, D, min(1024, T), "d_w_pool_up")
        d_wau[l] = _mm_tn(y_attn, da, N_DEV, D, min(1024, T), "d_w_attn_up")
        du, dzp, d_pool_w[l], d_pool_scale[l] = _pool_bwd(proj, dp, wpu_full[l], pool_w[l], pool_scale[l:l + 1], tm)
        do, dza = _attn_gate_bwd(da, wau_full[l], o, proj, tm)
        if l == 0:
            small = _pack([jnp.concatenate([jnp.zeros((1, D), F32)] + d_norm_g[1:], 0), jnp.concatenate(d_b_gate, 0),
                           jnp.stack(d_pool_w, 0), jnp.concatenate(d_pool_scale, 0), d_final_g, loss_part])
            early = d_win[1:] + d_wpu + d_wau + d_wout
            dq, dk, dv, got_small, *got_early = _attn_bwd(
                projb, do, *carries, tq, rider=([small] + early, [False] + [True] * len(early)))
        else:
            dq, dk, dv = _attn_bwd(projb, do, *carries, tq)
        segs = [du, dzp, dq, dk, dv, dza, dgl]
        sw = NW // N_DEV
        half = _mm_tn_segs(h, segs[5:], sw, min(1024, T), "d_w_in_b", into=jnp.zeros((N_DEV, D, sw), BF16), first_shard=N_DEV // 2)
        d_win[l] = _mm_tn_segs(h, segs[:5], sw, min(1024, T), "d_w_in_a", into=half)
        if l == 0:
            dx, d_norm_g[l], got_win0 = _dh_norm_bwd(segs, win_l, xin, norm_g[l:l + 1], dx, min(256, T),
                                                     rider=([d_win[0]], [True]))
        else:
            dx, d_norm_g[l] = _dh_norm_bwd(segs, win_l, xin, norm_g[l:l + 1], dx, min(256, T))

    got_norm0 = _exchange([d_norm_g[0].reshape(-1, LANE)], [False], "gather_norm_grad")[0]
    r_small = jnp.concatenate([got_norm0, got_small[:, D // LANE:]], axis=1)

    def update(pieces, w, m, v, name):
        cols = w.shape[-1]
        res = _adamw([p.reshape(N_DEV, -1, cols) for p in pieces], w.reshape(-1, cols), m.reshape(-1, cols),
                     v.reshape(-1, cols), name)
        return [r.reshape(w.shape) for r in res]

    u_in = update([got_win0] + got_early[:L - 1], w_in, m_w_in, v_w_in, "adamw_w_in")
    u_pu = update(got_early[L - 1:2 * L - 1], w_pool_up, m_w_pool_up, v_w_pool_up, "adamw_w_pool_up")
    u_au = update(got_early[2 * L - 1:3 * L - 1], w_attn_up, m_w_attn_up, v_w_attn_up, "adamw_w_attn_up")
    u_out = update(got_early[3 * L - 1:4 * L - 1], w_out, m_w_out, v_w_out, "adamw_w_out")
    zeros = small_like[-1]
    smalls = _adamw([r_small],
                    _pack([norm_g, b_gate, pool_w, pool_scale, final_g, zeros]),
                    _pack([m_norm_g, m_b_gate, m_pool_w, m_pool_scale, m_final_g, zeros]),
                    _pack([v_norm_g, v_b_gate, v_pool_w, v_pool_scale, v_final_g, zeros]), "adamw_small")
    s_g, s_d, s_m, s_v = [_unpack(s, small_like) for s in smalls]
    loss = s_g[5][0, 0]

    def ordered(k):
        s = (s_g, s_d, s_m, s_v)[k]
        return [s[0], u_in[k], s[1], s[2], s[3], u_pu[k], u_au[k], u_out[k], s[4]]

    return (loss, dx.reshape(x.shape), *ordered(0), *ordered(1), *ordered(2), *ordered(3))
```

```python
import jax
import jax.numpy as jnp
from jax import lax
from jax.experimental import pallas as pl
from jax.experimental.pallas import tpu as pltpu

F32 = jnp.float32
BF16 = jnp.bfloat16

N_DEV = 8
HEAD_DIM = 64
ATTN_W = 512
POOL_W = 512
POOL_G = 128
POOL_WINDOWS = (2, 4, 8, 16)
HALO = 16
LANE = 128
RMS_EPS = 1e-6
ZERO_WEIGHT = 110.0
NO_CARRY = 3.0e38
ADAM_LR, ADAM_B1, ADAM_B2, ADAM_EPS, ADAM_WD, ADAM_STEP = 0.001, 0.9, 0.999, 1e-08, 0.01, 10
VMEM_LIMIT = 56 * 1024 * 1024

NT_DIMS = (((1,), (1,)), ((), ()))
TN_DIMS = (((0,), (0,)), ((), ()))


def _pc(body, **kw):
    return pl.pallas_call(body, **kw)


def _params(*sem):
    return pltpu.CompilerParams(dimension_semantics=sem, vmem_limit_bytes=VMEM_LIMIT)


def _sigmoid(z):
    return 1.0 / (1.0 + jnp.exp(-z))


def _silu_and_grad(z):
    s = _sigmoid(z)
    return z * s, s * (1.0 + z * (1.0 - s))


def _my_index():
    return 4 * lax.axis_index("x") + 2 * lax.axis_index("y") + lax.axis_index("c")


def _peer(k):
    x, y, c = lax.axis_index("x"), lax.axis_index("y"), lax.axis_index("c")
    px = lax.rem(x + ((k >> 2) & 1), 2)
    py = lax.rem(y + ((k >> 1) & 1), 2)
    pc = lax.rem(c + (k & 1), 2)
    return (px, py, pc), 4 * px + 2 * py + pc


def _exchange_copies(ins, outs, sems, scatter):
    send_sems, recv_sems, local_sems = sems
    n = len(ins)
    me = _my_index()

    def src(a, idx):
        return ins[a].at[idx] if scatter[a] else ins[a]

    local = [pltpu.make_async_copy(src(a, me), outs[a].at[me], local_sems.at[a]) for a in range(n)]
    sends, arrivals = [], []
    for k in (1, 2, 4, 3, 5, 6, 7):
        dev, pidx = _peer(k)
        for a in range(n):
            sem = dict(send_sem=send_sems.at[a * N_DEV + k], recv_sem=recv_sems.at[a * N_DEV + k],
                       device_id=dev, device_id_type=pl.DeviceIdType.MESH)
            sends.append(pltpu.make_async_remote_copy(src_ref=src(a, pidx), dst_ref=outs[a].at[me], **sem))
            arrivals.append(pltpu.make_async_remote_copy(src_ref=src(a, pidx), dst_ref=outs[a].at[pidx], **sem))
    return local, sends, arrivals


def _exchange_start(ins, outs, sems, scatter):
    local, sends, _ = _exchange_copies(ins, outs, sems, scatter)
    for cp in local + sends:
        cp.start()


def _exchange_wait(ins, outs, sems, scatter):
    local, sends, arrivals = _exchange_copies(ins, outs, sems, scatter)
    for cp in arrivals:
        cp.wait_recv()
    for cp in sends:
        cp.wait_send()
    for cp in local:
        cp.wait()


def _exchange_shapes(arrays, scatter):
    n = len(arrays)
    out_shape = [jax.ShapeDtypeStruct((N_DEV,) + tuple(a.shape[1:] if s else a.shape), a.dtype)
                 for a, s in zip(arrays, scatter)]
    sems = [pltpu.SemaphoreType.DMA((n * N_DEV,)), pltpu.SemaphoreType.DMA((n * N_DEV,)),
            pltpu.SemaphoreType.DMA((n,))]
    return out_shape, sems


def _exchange(arrays, scatter, name):
    n = len(arrays)

    def body(*refs):
        ins, outs, sems = refs[:n], refs[n:2 * n], refs[2 * n:]
        _exchange_start(ins, outs, sems, scatter)
        _exchange_wait(ins, outs, sems, scatter)

    out_shape, sems = _exchange_shapes(arrays, scatter)
    any_spec = pl.BlockSpec(memory_space=pl.ANY)
    return _pc(
        body, name=name, out_shape=tuple(out_shape),
        in_specs=[any_spec] * n, out_specs=tuple([any_spec] * n), scratch_shapes=sems,
    )(*arrays)


def _gather_two_level(x, name):
    def body(x_ref, out_ref, send_sems, recv_sems, local_sem):
        X, Y, C = lax.axis_index("x"), lax.axis_index("y"), lax.axis_index("c")
        me, sibling = (X, Y, C), (X, Y, 1 - C)
        chips = [(1 - X, Y), (X, 1 - Y), (1 - X, 1 - Y)]

        def slot(px, py, pc):
            return out_ref.at[4 * px + 2 * py + pc]

        def copy(k, block, to, src=None):
            return pltpu.make_async_remote_copy(
                src_ref=slot(*block) if src is None else src, dst_ref=slot(*block),
                send_sem=send_sems.at[k], recv_sem=recv_sems.at[k], device_id=to, device_id_type=pl.DeviceIdType.MESH)

        mine = pltpu.make_async_copy(x_ref, slot(*me), local_sem)
        mine.start()
        first = [copy(0, me, sibling, src=x_ref)] + [copy(1 + j, me, (*chip, C), src=x_ref) for j, chip in enumerate(chips)]
        for cp in first:
            cp.start()
        passed = [copy(4 + j, (*chip, C), sibling) for j, chip in enumerate(chips)]
        for j, chip in enumerate(chips):
            copy(1 + j, (*chip, C), me).wait_recv()
            passed[j].start()
        copy(0, sibling, me).wait_recv()
        for j, chip in enumerate(chips):
            copy(4 + j, (*chip, 1 - C), me).wait_recv()
        for cp in first + passed:
            cp.wait_send()
        mine.wait()

    any_spec = pl.BlockSpec(memory_space=pl.ANY)
    return _pc(
        body, name=name, out_shape=jax.ShapeDtypeStruct((N_DEV,) + x.shape, x.dtype),
        in_specs=[any_spec], out_specs=any_spec,
        scratch_shapes=[pltpu.SemaphoreType.DMA((7,)), pltpu.SemaphoreType.DMA((7,)), pltpu.SemaphoreType.DMA(())],
    )(x)


def _with_rider(body, n_in, n_out, n_scratch, rider, first, last):
    if rider is None:
        return body
    arrays, scatter = rider
    n = len(arrays)

    def wrapped(*refs):
        ins, r_ins = refs[:n_in], refs[n_in:n_in + n]
        outs = refs[n_in + n:n_in + n + n_out]
        r_outs = refs[n_in + n + n_out:n_in + 2 * n + n_out]
        scratch = refs[n_in + 2 * n + n_out:n_in + 2 * n + n_out + n_scratch]
        sems = refs[n_in + 2 * n + n_out + n_scratch:]

        @pl.when(first())
        def _():
            _exchange_start(r_ins, r_outs, sems, scatter)

        body(*ins, *outs, *scratch)

        @pl.when(last())
        def _():
            _exchange_wait(r_ins, r_outs, sems, scatter)

    return wrapped


def _rider_specs(rider):
    if rider is None:
        return [], [], [], [], []
    arrays, scatter = rider
    out_shape, sems = _exchange_shapes(arrays, scatter)
    any_spec = pl.BlockSpec(memory_space=pl.ANY)
    return list(arrays), [any_spec] * len(arrays), out_shape, [any_spec] * len(arrays), sems


def _mm_nn_res(a, b, res, tm, name):
    T, K = a.shape
    N = b.shape[1]

    def body(a_ref, b_ref, r_ref, o_ref):
        o_ref[...] = r_ref[...] + jnp.dot(a_ref[...], b_ref[...], preferred_element_type=F32)

    return _pc(
        body, name=name, grid=(T // tm,), out_shape=jax.ShapeDtypeStruct((T, N), F32),
        in_specs=[pl.BlockSpec((tm, K), lambda i: (i, 0)), pl.BlockSpec((K, N), lambda i: (0, 0)),
                  pl.BlockSpec((tm, N), lambda i: (i, 0))],
        out_specs=pl.BlockSpec((tm, N), lambda i: (i, 0)),
        compiler_params=_params("parallel"),
    )(a, b, res)


def _mm_tn(a, b, n_col_shards, tn, tk, name):
    T, M = a.shape
    N = b.shape[1]
    sw = N // n_col_shards
    per_step = tn // sw
    nk = T // tk

    def body(a_ref, b_ref, o_ref, acc_sc):
        k = pl.program_id(1)
        part = lax.dot_general(a_ref[...].astype(BF16), b_ref[...].astype(BF16), TN_DIMS,
                               preferred_element_type=F32)

        @pl.when(k == 0)
        def _():
            acc_sc[...] = part

        @pl.when(k > 0)
        def _():
            acc_sc[...] += part

        @pl.when(k == nk - 1)
        def _():
            for s in range(per_step):
                o_ref[s] = acc_sc[:, s * sw:(s + 1) * sw].astype(BF16)

    return _pc(
        body, name=name, grid=(N // tn, nk),
        out_shape=jax.ShapeDtypeStruct((n_col_shards, M, sw), BF16),
        in_specs=[pl.BlockSpec((tk, M), lambda j, k: (k, 0)), pl.BlockSpec((tk, tn), lambda j, k: (k, j))],
        out_specs=pl.BlockSpec((per_step, M, sw), lambda j, k: (j, 0, 0)),
        scratch_shapes=[pltpu.VMEM((M, tn), F32)],
        compiler_params=_params("parallel", "arbitrary"),
    )(a, b)


def _mm_tn_segs(a, segs, sw, tk, name, into=None, first_shard=0):
    T, M = a.shape
    N = sum(x.shape[1] for x in segs)
    assert N % sw == 0 and first_shard % (N // sw) == 0
    n_seg = len(segs)
    nk = T // tk
    ns = N // sw

    def body(a_ref, *rest):
        seg_refs, o_ref, acc_sc = rest[:n_seg], rest[-2], rest[-1]
        k = pl.program_id(0)
        slab = jnp.concatenate([r[...] for r in seg_refs], axis=1)
        part = lax.dot_general(a_ref[...], slab, TN_DIMS, preferred_element_type=F32)

        @pl.when(k == 0)
        def _():
            acc_sc[...] = part

        @pl.when(k > 0)
        def _():
            acc_sc[...] += part

        @pl.when(k == nk - 1)
        def _():
            for j in range(ns):
                o_ref[j] = acc_sc[:, j * sw:(j + 1) * sw].astype(BF16)

    extra, extra_specs, alias = [], [], {}
    out_shape = jax.ShapeDtypeStruct((ns, M, sw), BF16)
    if into is not None:
        extra, extra_specs, alias = [into], [pl.BlockSpec(memory_space=pl.ANY)], {1 + n_seg: 0}
        out_shape = jax.ShapeDtypeStruct(into.shape, BF16)
    blk = first_shard // ns
    return _pc(
        body, name=name, grid=(nk,), out_shape=out_shape,
        in_specs=[pl.BlockSpec((tk, M), lambda k: (k, 0))] + [pl.BlockSpec((tk, x.shape[1]), lambda k: (k, 0)) for x in segs]
        + extra_specs,
        out_specs=pl.BlockSpec((ns, M, sw), lambda k: (blk, 0, 0)),
        scratch_shapes=[pltpu.VMEM((M, N), F32)], input_output_aliases=alias,
        compiler_params=_params("arbitrary"),
    )(a, *segs, *extra)


def _proj_layout(D):
    return {"u": 0, "z_pool": POOL_W, "gates": 2 * POOL_W, "z_attn": 2 * POOL_W + 2 * D, "width": 2 * POOL_W + 2 * D + ATTN_W}


def _norm_inproj(x, g, w, tm):
    T, D = x.shape
    NW = w.shape[1]
    lay = _proj_layout(D)
    qkv0, za0, gl0 = 2 * POOL_W, 2 * POOL_W + 3 * ATTN_W, 2 * POOL_W + 4 * ATTN_W

    def body(x_ref, g_ref, w_ref, proj_ref, qkv_ref, h_ref):
        xv = x_ref[...]
        r = lax.rsqrt(jnp.mean(xv * xv, axis=-1, keepdims=True) + RMS_EPS)
        h = ((xv * r) * g_ref[...]).astype(BF16)
        h_ref[...] = h

        def cols(lo, hi):
            return jnp.dot(h, w_ref[:, lo:hi], preferred_element_type=F32)

        proj_ref[:, :lay["gates"]] = cols(0, qkv0)
        qkv_ref[...] = cols(qkv0, za0).astype(BF16)
        proj_ref[:, lay["gates"]:lay["z_attn"]] = cols(gl0, NW)
        proj_ref[:, lay["z_attn"]:] = cols(za0, gl0)

    return _pc(
        body, name="norm_inproj", grid=(T // tm,),
        out_shape=(jax.ShapeDtypeStruct((T, lay["width"]), F32), jax.ShapeDtypeStruct((T, 3 * ATTN_W), BF16),
                   jax.ShapeDtypeStruct((T, D), BF16)),
        in_specs=[pl.BlockSpec((tm, D), lambda i: (i, 0)), pl.BlockSpec((1, D), lambda i: (0, 0)),
                  pl.BlockSpec((D, NW), lambda i: (0, 0))],
        out_specs=(pl.BlockSpec((tm, lay["width"]), lambda i: (i, 0)), pl.BlockSpec((tm, 3 * ATTN_W), lambda i: (i, 0)),
                   pl.BlockSpec((tm, D), lambda i: (i, 0))),
        compiler_params=_params("parallel"),
    )(x, g, w)


def _window_sums(xh, forward):
    n = xh.shape[0]
    sums, s, step = [], xh, 1
    for _ in POOL_WINDOWS:
        s = s + pltpu.roll(s, step if forward else n - step, 0)
        sums.append(s)
        step *= 2
    return sums


def _pooled(u, halo, row):
    sums = _window_sums(jnp.concatenate([halo, u], axis=0), True)
    out = []
    for g, w in enumerate(POOL_WINDOWS):
        cols = slice(g * POOL_G, (g + 1) * POOL_G)
        cnt = jnp.minimum(row + 1, w).astype(F32)
        out.append(sums[g][HALO:, cols] / cnt - u[:, cols])
    return out


def _pool_fwd(proj, pool_w, scale, R):
    T = proj.shape[0]

    def body(u_ref, z_ref, pw_ref, sc_ref, y_ref, halo_sc):
        i = pl.program_id(0)

        @pl.when(i == 0)
        def _():
            halo_sc[...] = jnp.zeros_like(halo_sc)

        u = u_ref[...]
        row = i * R + lax.broadcasted_iota(jnp.int32, (R, 1), 0)
        pooled = _pooled(u, halo_sc[...], row)
        mixed = jnp.concatenate(
            [jnp.dot(pooled[g].astype(BF16), pw_ref[g].astype(BF16), preferred_element_type=F32)
             for g in range(len(POOL_WINDOWS))], axis=1)
        z = z_ref[...]
        y_ref[...] = ((mixed * sc_ref[...]) * (z * _sigmoid(z))).astype(BF16)
        halo_sc[...] = u[R - HALO:, :]

    return _pc(
        body, name="pool_fwd", grid=(T // R,), out_shape=jax.ShapeDtypeStruct((T, POOL_W), BF16),
        in_specs=[pl.BlockSpec((R, POOL_W), lambda i: (i, 0)), pl.BlockSpec((R, POOL_W), lambda i: (i, 1)),
                  pl.BlockSpec((4, POOL_G, POOL_G), lambda i: (0, 0, 0)), pl.BlockSpec((1, POOL_W), lambda i: (0, 0))],
        out_specs=pl.BlockSpec((R, POOL_W), lambda i: (i, 0)),
        scratch_shapes=[pltpu.VMEM((HALO, POOL_W), F32)],
        compiler_params=_params("arbitrary"),
    )(proj, proj, pool_w, scale)


def _softplus(l):
    return jnp.maximum(l, 0.0) + jnp.log(1.0 + jnp.exp(-jnp.abs(l)))


def _first_last_step(n0, n1):
    return (lambda: jnp.logical_and(pl.program_id(0) == 0, pl.program_id(1) == 0),
            lambda: jnp.logical_and(pl.program_id(0) == n0 - 1, pl.program_id(1) == n1 - 1))


def _head_lanes():
    lane = lax.broadcasted_iota(jnp.int32, (1, LANE), 1)
    return [lane < HEAD_DIM, lane >= HEAD_DIM]


def _head_masks(q, scale):
    qf = q.astype(F32) * scale
    return [jnp.where(m, qf, 0.0).astype(BF16) for m in _head_lanes()]


def _wide(c, width):
    return jnp.concatenate([c] * (width // LANE), axis=1)


def _max_row_norm2(x, heads):
    sq = x.astype(F32) * x.astype(F32)
    return [jnp.max(jnp.sum(jnp.where(m, sq, 0.0), axis=1, keepdims=True), axis=0, keepdims=True) for m in heads]


def _fill_blocks(src_ref, nk, tk, transposed_sc=None, masked_sc=None, norm_sc=None):
    heads = _head_lanes()
    if norm_sc is not None:
        for a in range(2):
            norm_sc[a][...] = jnp.zeros_like(norm_sc[a])

    def step(j, carry):
        rows = pl.ds(pl.multiple_of(j * tk, tk), tk)
        blk = src_ref[rows, :]
        if norm_sc is not None:
            for a, n2 in enumerate(_max_row_norm2(blk, heads)):
                norm_sc[a][...] = jnp.maximum(norm_sc[a][...], n2)
        if transposed_sc is not None:
            transposed_sc[j] = blk.astype(F32).T.astype(BF16)
        if masked_sc is not None:
            for a in range(2):
                masked_sc[a, rows, :] = jnp.where(heads[a], blk, jnp.zeros_like(blk))
        return carry

    lax.fori_loop(0, nk, step, 0)


def _attn_fwd(projb, proj, tq, rider=None):
    T = projb.shape[0]
    nq = T // tq
    tk, nk = tq, nq
    assert nk < LANE
    n_pairs = ATTN_W // LANE
    zb = (proj.shape[1] - ATTN_W) // LANE

    def body(q_ref, k_ref, v_ref, za_ref, o_ref, y_ref, c_ref, ws_ref, bs_ref, kT_sc, vm_sc, *per_head):
        c_sc, cm_sc, o_sc, kn_sc, l_sc, w_sc = (per_head[2 * n:2 * n + 2] for n in range(6))
        i = pl.program_id(1)

        @pl.when(i == 0)
        def _():
            _fill_blocks(k_ref, nk, tk, transposed_sc=kT_sc, norm_sc=kn_sc)
            _fill_blocks(v_ref, nk, tk, masked_sc=vm_sc)

        qs = _head_masks(q_ref[...], HEAD_DIM ** -0.5)
        lane = lax.broadcasted_iota(jnp.int32, (1, LANE), 1)
        valid = lax.broadcasted_iota(jnp.int32, (tq, tk), 1) < lax.broadcasted_iota(jnp.int32, (tq, tk), 0)
        suffix = (lax.broadcasted_iota(jnp.int32, (tk, tk), 0) >= lax.broadcasted_iota(jnp.int32, (tk, tk), 1)).astype(BF16)
        for a in range(2):
            c_sc[a][...] = jnp.zeros_like(c_sc[a])
            o_sc[a][...] = jnp.zeros_like(o_sc[a])
            cm_sc[a][...] = jnp.full(cm_sc[a].shape, NO_CARRY, F32)
        l_max = [jnp.sqrt(qn * kn_sc[a][...]) for a, qn in enumerate(_max_row_norm2(qs[0] + qs[1], _head_lanes()))]

        def logits(j, slot):
            kT = kT_sc[j]
            for a in range(2):
                l_sc[a][slot] = jnp.dot(qs[a], kT, preferred_element_type=F32)

        def values(j, slot):
            rows = pl.ds(pl.multiple_of(j * tk, tk), tk)
            for a in range(2):
                o_sc[a][...] += jnp.dot(w_sc[a][slot], vm_sc[a, rows, :], preferred_element_type=F32)

        def softplus(slot, masked):
            out = []
            for a in range(2):
                x = _softplus(l_sc[a][slot])
                out.append(jnp.where(valid, x, 0.0) if masked else x)
            return out

        def finish_weights(j, slot, sp, inc, mask, keep=None, cols=None):
            for a in range(2):
                c = c_sc[a][...]
                l = l_sc[a][slot]
                w = jnp.exp(l - inc[a] - _wide(c, tk))
                if mask is not None:
                    w = jnp.where(mask, w, 0.0)
                add = inc[a][:, 0:1]
                if keep is not None:
                    w = jnp.where(keep, w, 0.0)
                    add = jnp.where(keep, add, 0.0)
                w_sc[a][slot] = w.astype(BF16)
                if cols is not None:
                    beta = jnp.exp(l - sp[a])
                    ws_ref[a, 0, :, cols] = w
                    bs_ref[a, 0, :, cols] = beta if mask is None else jnp.where(mask, beta, 0.0)
                cm_sc[a][...] = jnp.where(lane == j, c, cm_sc[a][...])
                c_sc[a][...] = c + add

        def weights(j, slot):
            sp = softplus(slot, False)
            inc = [jnp.dot(sp[a].astype(BF16), suffix, preferred_element_type=F32) for a in range(2)]
            finish_weights(j, slot, sp, inc, None)

        def more():
            live = [jnp.min(c_sc[a][...], axis=0, keepdims=True) - l_max[a][0:1, :] <= ZERO_WEIGHT for a in range(2)]
            return jnp.max(jnp.where(jnp.logical_or(live[0], live[1]), 1, 0))

        logits(i, 0)
        logits(jnp.maximum(i - 1, 0), 1)
        sp = softplus(0, True) + softplus(1, False)
        inc = [jnp.dot(x.astype(BF16), suffix, preferred_element_type=F32) for x in sp]
        finish_weights(i, 0, sp[:2], inc[:2], valid, cols=slice(tk, 2 * tk))
        finish_weights(i - 1, 1, sp[2:], inc[2:], None, keep=i > 0, cols=slice(0, tk))
        values(i, 0)
        logits(jnp.maximum(i - 2, 0), 0)

        def step(t, slot):
            logits(jnp.maximum(i - t - 1, 0), 1 - slot)
            values(i - t + 1, 1 - slot)
            weights(i - t, slot)

        def two_steps(carry):
            tt, _ = carry
            step(2 * tt + 2, 0)
            step(2 * tt + 3, 1)
            return tt + 1, more()

        pairs, go = lax.while_loop(lambda c: jnp.logical_and(2 * c[0] + 3 <= i, c[1] > 0), two_steps, (0, more()))
        done = 1 + 2 * pairs
        one_more = jnp.logical_and(done + 1 == i, go > 0)

        @pl.when(one_more)
        def _():
            step(i, 0)
            values(0, 0)

        @pl.when(jnp.logical_not(one_more))
        def _():
            values(jnp.maximum(i - done, 0), 1)

        o = o_sc[0][...] + o_sc[1][...]
        o_ref[...] = o
        za = za_ref[...]
        y_ref[...] = (o * (za * _sigmoid(za))).astype(BF16)
        c_ref[0, 0] = jnp.where(lane == LANE - 1, l_max[0][0:1, :], cm_sc[0][...])
        c_ref[1, 0] = jnp.where(lane == LANE - 1, l_max[1][0:1, :], cm_sc[1][...])

    scratch = ([pltpu.VMEM((nk, LANE, tk), BF16), pltpu.VMEM((2, T, LANE), BF16)]
               + [pltpu.VMEM((tq, LANE), F32)] * 6 + [pltpu.VMEM((8, LANE), F32)] * 2
               + [pltpu.VMEM((2, tq, tk), F32)] * 2 + [pltpu.VMEM((2, tq, tk), BF16)] * 2)
    r_in, r_in_specs, r_out, r_out_specs, r_sems = _rider_specs(rider)
    body = _with_rider(body, 4, 5, len(scratch), rider, *_first_last_step(n_pairs, nq))
    kept = jax.ShapeDtypeStruct((2 * n_pairs, nq, tq, 2 * tk), F32)
    kept_spec = pl.BlockSpec((2, 1, tq, 2 * tk), lambda p, i: (p, i, 0, 0))
    return _pc(
        body, name="attn_fwd", grid=(n_pairs, nq),
        out_shape=tuple([jax.ShapeDtypeStruct((T, ATTN_W), F32), jax.ShapeDtypeStruct((T, ATTN_W), BF16),
                         jax.ShapeDtypeStruct((2 * n_pairs, nq, tq, LANE), F32), kept, kept] + r_out),
        in_specs=[pl.BlockSpec((tq, LANE), lambda p, i: (i, p)),
                  pl.BlockSpec((T, LANE), lambda p, i: (0, n_pairs + p)),
                  pl.BlockSpec((T, LANE), lambda p, i: (0, 2 * n_pairs + p)),
                  pl.BlockSpec((tq, LANE), lambda p, i: (i, zb + p))] + r_in_specs,
        out_specs=tuple([pl.BlockSpec((tq, LANE), lambda p, i: (i, p)), pl.BlockSpec((tq, LANE), lambda p, i: (i, p)),
                         pl.BlockSpec((2, 1, tq, LANE), lambda p, i: (p, i, 0, 0)), kept_spec, kept_spec] + r_out_specs),
        scratch_shapes=scratch + r_sems,
        compiler_params=_params("arbitrary", "arbitrary"),
    )(projb, projb, projb, proj, *r_in)


def _gates(gl0, gl1, bg, D):
    return _sigmoid(gl0 + bg[:, :D]), _sigmoid(gl1 + bg[:, D:])


def _merge_fwd(y_pool, y_attn, w_pu, w_au, proj, b_gate, tm):
    T = y_pool.shape[0]
    D = w_pu.shape[1]
    gb = _proj_layout(D)["gates"] // D

    def body(yp_ref, ya_ref, wpu_ref, wau_ref, gl0_ref, gl1_ref, bg_ref, m_ref):
        p = jnp.dot(yp_ref[...], wpu_ref[...], preferred_element_type=F32)
        a = jnp.dot(ya_ref[...], wau_ref[...], preferred_element_type=F32)
        g0, g1 = _gates(gl0_ref[...], gl1_ref[...], bg_ref[...], D)
        m_ref[...] = (g0 * p + g1 * a).astype(BF16)

    row = lambda i: (i, 0)
    fixed = lambda i: (0, 0)
    return _pc(
        body, name="merge_fwd", grid=(T // tm,), out_shape=jax.ShapeDtypeStruct((T, D), BF16),
        in_specs=[pl.BlockSpec((tm, POOL_W), row), pl.BlockSpec((tm, ATTN_W), row),
                  pl.BlockSpec((POOL_W, D), fixed), pl.BlockSpec((ATTN_W, D), fixed),
                  pl.BlockSpec((tm, D), lambda i: (i, gb)), pl.BlockSpec((tm, D), lambda i: (i, gb + 1)),
                  pl.BlockSpec((1, 2 * D), fixed)],
        out_specs=pl.BlockSpec((tm, D), row),
        compiler_params=_params("parallel"),
    )(y_pool, y_attn, w_pu, w_au, proj, proj, b_gate)


def _out_proj_final_loss(merged, w_out, x, g, target, tm):
    T, D = x.shape

    def body(m_ref, w_ref, x_ref, g_ref, t_ref, dx_ref, dg_ref, loss_ref):
        @pl.when(pl.program_id(0) == 0)
        def _():
            dg_ref[...] = jnp.zeros_like(dg_ref)
            loss_ref[...] = jnp.zeros_like(loss_ref)

        xv = x_ref[...] + jnp.dot(m_ref[...], w_ref[...], preferred_element_type=F32)
        gv = g_ref[...]
        r = lax.rsqrt(jnp.mean(xv * xv, axis=-1, keepdims=True) + RMS_EPS)
        xh = xv * r
        d = xh * gv - t_ref[...]
        loss_ref[...] += 0.5 * jnp.sum(jnp.mean(d * d, axis=-1, keepdims=True), axis=0, keepdims=True)
        dy = d * (1.0 / D)
        dg_ref[...] += jnp.sum(dy * xh, axis=0, keepdims=True)
        dh = dy * gv
        dx_ref[...] = r * (dh - xh * jnp.mean(dh * xh, axis=-1, keepdims=True))

    row = lambda i: (i, 0)
    fixed = lambda i: (0, 0)
    return _pc(
        body, name="out_proj_final_loss", grid=(T // tm,),
        out_shape=(jax.ShapeDtypeStruct((T, D), F32), jax.ShapeDtypeStruct((1, D), F32),
                   jax.ShapeDtypeStruct((8, LANE), F32)),
        in_specs=[pl.BlockSpec((tm, D), row), pl.BlockSpec((D, D), fixed), pl.BlockSpec((tm, D), row),
                  pl.BlockSpec((1, D), fixed), pl.BlockSpec((tm, D), row)],
        out_specs=(pl.BlockSpec((tm, D), row), pl.BlockSpec((1, D), fixed), pl.BlockSpec((8, LANE), fixed)),
        compiler_params=_params("arbitrary"),
    )(merged, w_out, x, g, target)


def _merge_bwd(dxo, w_out, y_pool, y_attn, w_pu, w_au, proj, b_gate, tm):
    T, D = dxo.shape
    gb = _proj_layout(D)["gates"] // D

    def body(dxo_ref, wout_ref, yp_ref, ya_ref, wpu_ref, wau_ref, gl0_ref, gl1_ref, bg_ref,
             dp_ref, da_ref, dgl_ref, dbg_ref):
        @pl.when(pl.program_id(0) == 0)
        def _():
            dbg_ref[...] = jnp.zeros_like(dbg_ref)

        dmv = lax.dot_general(dxo_ref[...].astype(BF16), wout_ref[...], NT_DIMS, preferred_element_type=F32)
        p = jnp.dot(yp_ref[...], wpu_ref[...], preferred_element_type=F32)
        a = jnp.dot(ya_ref[...], wau_ref[...], preferred_element_type=F32)
        g0, g1 = _gates(gl0_ref[...], gl1_ref[...], bg_ref[...], D)
        dp_ref[...] = (dmv * g0).astype(BF16)
        da_ref[...] = (dmv * g1).astype(BF16)
        dgl0 = dmv * p * (g0 * (1.0 - g0))
        dgl1 = dmv * a * (g1 * (1.0 - g1))
        dgl_ref[:, :D] = dgl0.astype(BF16)
        dgl_ref[:, D:] = dgl1.astype(BF16)
        dbg_ref[:, :D] += jnp.sum(dgl0, axis=0, keepdims=True)
        dbg_ref[:, D:] += jnp.sum(dgl1, axis=0, keepdims=True)

    row = lambda i: (i, 0)
    fixed = lambda i: (0, 0)
    return _pc(
        body, name="merge_bwd", grid=(T // tm,),
        out_shape=(jax.ShapeDtypeStruct((T, D), BF16), jax.ShapeDtypeStruct((T, D), BF16),
                   jax.ShapeDtypeStruct((T, 2 * D), BF16), jax.ShapeDtypeStruct((1, 2 * D), F32)),
        in_specs=[pl.BlockSpec((tm, D), row), pl.BlockSpec((D, D), fixed),
                  pl.BlockSpec((tm, POOL_W), row), pl.BlockSpec((tm, ATTN_W), row),
                  pl.BlockSpec((POOL_W, D), fixed), pl.BlockSpec((ATTN_W, D), fixed),
                  pl.BlockSpec((tm, D), lambda i: (i, gb)), pl.BlockSpec((tm, D), lambda i: (i, gb + 1)),
                  pl.BlockSpec((1, 2 * D), fixed)],
        out_specs=(pl.BlockSpec((tm, D), row), pl.BlockSpec((tm, D), row), pl.BlockSpec((tm, 2 * D), row),
                   pl.BlockSpec((1, 2 * D), fixed)),
        compiler_params=_params("arbitrary"),
    )(dxo, w_out, y_pool, y_attn, w_pu, w_au, proj, proj, b_gate)


def _pool_bwd(proj, dp, w_pu, pool_w, scale, R):
    T = proj.shape[0]
    D = w_pu.shape[1]
    nb = T // R
    hb = R // HALO

    def body(u_ref, up_ref, z_ref, dp_ref, wpu_ref, pw_ref, sc_ref, du_ref, dz_ref, dpw_ref, dsc_ref, halo_sc):
        i = pl.program_id(0)
        rb = nb - 1 - i

        @pl.when(i == 0)
        def _():
            halo_sc[...] = jnp.zeros_like(halo_sc)
            dpw_ref[...] = jnp.zeros_like(dpw_ref)
            dsc_ref[...] = jnp.zeros_like(dsc_ref)

        u = u_ref[...]
        row = rb * R + lax.broadcasted_iota(jnp.int32, (R, 1), 0)
        before = jnp.where(rb > 0, up_ref[...], 0.0)
        pooled = _pooled(u, before, row)
        pw = [pw_ref[g].astype(BF16) for g in range(len(POOL_WINDOWS))]
        mixed = jnp.concatenate(
            [jnp.dot(pooled[g].astype(BF16), pw[g], preferred_element_type=F32) for g in range(len(POOL_WINDOWS))],
            axis=1)
        sc = sc_ref[...]
        silu, dsilu = _silu_and_grad(z_ref[...])
        dyv = lax.dot_general(dp_ref[...], wpu_ref[...], NT_DIMS, preferred_element_type=F32)
        dmp = dyv * silu
        dz_ref[...] = (dyv * (mixed * sc) * dsilu).astype(BF16)
        dsc_ref[...] += jnp.sum(dmp * mixed, axis=0, keepdims=True)
        dmixed = (dmp * sc).astype(BF16)
        dpn = []
        dpooled = []
        for g, w in enumerate(POOL_WINDOWS):
            cols = slice(g * POOL_G, (g + 1) * POOL_G)
            dpw_ref[g] += lax.dot_general(pooled[g].astype(BF16), dmixed[:, cols], TN_DIMS,
                                          preferred_element_type=F32)
            dpg = lax.dot_general(dmixed[:, cols], pw[g], NT_DIMS, preferred_element_type=F32)
            dpooled.append(dpg)
            dpn.append(dpg / jnp.minimum(row + 1, w).astype(F32))
        dpn = jnp.concatenate(dpn, axis=1)
        sums = _window_sums(jnp.concatenate([dpn, halo_sc[...]], axis=0), False)
        du_ref[...] = jnp.concatenate(
            [sums[g][:R, g * POOL_G:(g + 1) * POOL_G] - dpooled[g] for g in range(len(POOL_WINDOWS))],
            axis=1).astype(BF16)
        halo_sc[...] = dpn[:HALO, :]

    rev = lambda i: (nb - 1 - i, 0)
    return _pc(
        body, name="pool_bwd", grid=(nb,),
        out_shape=(jax.ShapeDtypeStruct((T, POOL_W), BF16), jax.ShapeDtypeStruct((T, POOL_W), BF16),
                   jax.ShapeDtypeStruct((4, POOL_G, POOL_G), F32), jax.ShapeDtypeStruct((1, POOL_W), F32)),
        in_specs=[pl.BlockSpec((R, POOL_W), rev),
                  pl.BlockSpec((HALO, POOL_W), lambda i: (jnp.maximum((nb - 1 - i) * hb - 1, 0), 0)),
                  pl.BlockSpec((R, POOL_W), lambda i: (nb - 1 - i, 1)),
                  pl.BlockSpec((R, D), rev), pl.BlockSpec((POOL_W, D), lambda i: (0, 0)),
                  pl.BlockSpec((4, POOL_G, POOL_G), lambda i: (0, 0, 0)), pl.BlockSpec((1, POOL_W), lambda i: (0, 0))],
        out_specs=(pl.BlockSpec((R, POOL_W), rev), pl.BlockSpec((R, POOL_W), rev),
                   pl.BlockSpec((4, POOL_G, POOL_G), lambda i: (0, 0, 0)), pl.BlockSpec((1, POOL_W), lambda i: (0, 0))),
        scratch_shapes=[pltpu.VMEM((HALO, POOL_W), F32)],
        compiler_params=_params("arbitrary"),
    )(proj, proj, proj, dp, w_pu, pool_w, scale)


def _attn_gate_bwd(da, w_au, o, proj, tm):
    T, D = da.shape
    zb = (proj.shape[1] - ATTN_W) // ATTN_W

    def body(da_ref, wau_ref, o_ref, za_ref, do_ref, dza_ref):
        silu, dsilu = _silu_and_grad(za_ref[...])
        dyv = lax.dot_general(da_ref[...], wau_ref[...], NT_DIMS, preferred_element_type=F32)
        do_ref[...] = (dyv * silu).astype(BF16)
        dza_ref[...] = (dyv * o_ref[...] * dsilu).astype(BF16)

    row = lambda i: (i, 0)
    return _pc(
        body, name="attn_gate_bwd", grid=(T // tm,),
        out_shape=(jax.ShapeDtypeStruct((T, ATTN_W), BF16), jax.ShapeDtypeStruct((T, ATTN_W), BF16)),
        in_specs=[pl.BlockSpec((tm, D), row), pl.BlockSpec((ATTN_W, D), lambda i: (0, 0)),
                  pl.BlockSpec((tm, ATTN_W), row), pl.BlockSpec((tm, ATTN_W), lambda i: (i, zb))],
        out_specs=(pl.BlockSpec((tm, ATTN_W), row), pl.BlockSpec((tm, ATTN_W), row)),
        compiler_params=_params("parallel"),
    )(da, w_au, o, proj)


def _attn_bwd(projb, do, carries, kept_w, kept_beta, tq, rider=None):
    T = projb.shape[0]
    nq = T // tq
    tk, nk = tq, nq
    n_pairs = ATTN_W // LANE
    scale = HEAD_DIM ** -0.5

    def body(q_ref, k_ref, v_ref, do_ref, c_ref, ws_ref, bs_ref, dq_ref, dk_ref, dv_ref, kT_sc, vT_sc, km_sc, dkT_ref, dvT_ref,
             *per_head):
        f_sc, dq_sc, l_sc, dw_sc, dl_sc, w_sc = (per_head[2 * n:2 * n + 2] for n in range(6))
        i = pl.program_id(1)

        @pl.when(i == 0)
        def _():
            _fill_blocks(k_ref, nk, tk, transposed_sc=kT_sc, masked_sc=km_sc)
            _fill_blocks(v_ref, nk, tk, transposed_sc=vT_sc)
            dkT_ref[...] = jnp.zeros_like(dkT_ref)
            dvT_ref[...] = jnp.zeros_like(dvT_ref)

        qs = _head_masks(q_ref[...], scale)
        dos = _head_masks(do_ref[...], 1.0)
        qT = [x.astype(F32).T.astype(BF16) for x in qs]
        doT = [x.astype(F32).T.astype(BF16) for x in dos]
        lane = lax.broadcasted_iota(jnp.int32, (1, LANE), 1)
        kk0 = lax.broadcasted_iota(jnp.int32, (tk, tk), 0)
        kk1 = lax.broadcasted_iota(jnp.int32, (tk, tk), 1)
        prefix = (kk0 <= kk1).astype(BF16)
        for a in range(2):
            f_sc[a][...] = jnp.zeros_like(f_sc[a])
            dq_sc[a][...] = jnp.zeros_like(dq_sc[a])
        live = lane == i
        for a in range(2):
            cmin = jnp.min(c_ref[a, 0], axis=0, keepdims=True)
            l_max = jnp.max(jnp.where(lane == LANE - 1, cmin, -NO_CARRY), axis=1, keepdims=True)
            live = jnp.logical_or(live, cmin - l_max <= ZERO_WEIGHT)
        t0 = jnp.min(jnp.where(jnp.logical_and(live, lane <= i), lane, i))
        n = i - t0

        def products(j, slot, with_logits=True):
            kT = kT_sc[j]
            vT = vT_sc[j]
            for a in range(2):
                if with_logits:
                    l_sc[a][slot] = jnp.dot(qs[a], kT, preferred_element_type=F32)
                dw_sc[a][slot] = jnp.dot(dos[a], vT, preferred_element_type=F32)

        def gradients(j, slot):
            rows = pl.ds(pl.multiple_of(j * tk, tk), tk)
            dkT = []
            dvT = []
            for a in range(2):
                dlb = dl_sc[a][slot]
                dq_sc[a][...] += jnp.dot(dlb, km_sc[a, rows, :], preferred_element_type=F32)
                dkT.append(jnp.dot(qT[a], dlb, preferred_element_type=F32))
                dvT.append(jnp.dot(doT[a], w_sc[a][slot], preferred_element_type=F32))
            dkT_ref[j] += dkT[0] + dkT[1]
            dvT_ref[j] += dvT[0] + dvT[1]

        def elementwise(j, slot):
            suffix = (kk0 >= kk1).astype(BF16)
            sp, inc, e, beta, p = [None] * 2, [None] * 2, [None] * 2, [None] * 2, [None] * 2
            for a in range(2):
                sp[a] = _softplus(l_sc[a][slot])
            for a in range(2):
                inc[a] = jnp.dot(sp[a].astype(BF16), suffix, preferred_element_type=F32)
            for a in range(2):
                l = l_sc[a][slot]
                c = jnp.sum(jnp.where(lane == j, c_ref[a, 0], 0.0), axis=1, keepdims=True)
                w = jnp.exp(l - inc[a] - c)
                w_sc[a][slot] = w.astype(BF16)
                beta[a] = jnp.exp(l - sp[a])
                e[a] = w * dw_sc[a][slot]
            for a in range(2):
                p[a] = jnp.dot(e[a].astype(BF16), prefix, preferred_element_type=F32)
            for a in range(2):
                f = f_sc[a][...]
                dl = e[a] - beta[a] * (p[a] + _wide(f, tk))
                dl_sc[a][slot] = dl.astype(BF16)
                f_sc[a][...] = f + p[a][:, tk - 1:tk]

        def kept_blocks(tiles):
            keys = [(n_, a) for n_ in range(len(tiles)) for a in range(2)]
            for j, slot, cols in tiles:
                products(j, slot, with_logits=False)
            e, p = {}, {}
            for n_, a in keys:
                j, slot, cols = tiles[n_]
                w = ws_ref[a, 0, :, cols]
                w_sc[a][slot] = w.astype(BF16)
                e[n_, a] = w * dw_sc[a][slot]
            for k in keys:
                p[k] = jnp.dot(e[k].astype(BF16), prefix, preferred_element_type=F32)
            for a in range(2):
                f = f_sc[a][...]
                for n_, (j, slot, cols) in enumerate(tiles):
                    dl = e[n_, a] - bs_ref[a, 0, :, cols] * (p[n_, a] + _wide(f, tk))
                    dl_sc[a][slot] = dl.astype(BF16)
                    f = f + p[n_, a][:, tk - 1:tk]
                f_sc[a][...] = f

        def step(r, slot):
            products(t0 + r + 1, 1 - slot)
            gradients(t0 + jnp.maximum(r - 1, 0), 1 - slot)
            elementwise(t0 + r, slot)

        def last_two(slot, pending=True):
            if pending:
                gradients(i - 2, slot)
            kept_blocks([(i - 1, 1 - slot, slice(0, tk)), (i, slot, slice(tk, 2 * tk))])
            gradients(i - 1, 1 - slot)
            gradients(i, slot)

        @pl.when(n >= 2)
        def _():
            for a in range(2):
                dl_sc[a][1] = jnp.zeros((tq, tk), BF16)
                w_sc[a][1] = jnp.zeros((tq, tk), BF16)
            products(t0, 0)

        def two_steps(tt, carry):
            step(2 * tt, 0)
            step(2 * tt + 1, 1)
            return carry

        lax.fori_loop(0, jnp.maximum(n - 1, 0) // 2, two_steps, 0)

        @pl.when(n == 0)
        def _():
            kept_blocks([(i, 0, slice(tk, 2 * tk))])
            gradients(i, 0)

        @pl.when(n == 1)
        def _():
            last_two(1, pending=False)

        @pl.when(jnp.logical_and(n > 1, n % 2 == 1))
        def _():
            last_two(1)

        @pl.when(jnp.logical_and(n > 0, n % 2 == 0))
        def _():
            step(n - 2, 0)
            last_two(0)

        dq_ref[...] = ((dq_sc[0][...] + dq_sc[1][...]) * scale).astype(BF16)

        @pl.when(i == nq - 1)
        def _():
            def untranspose(j, carry):
                rows = pl.ds(pl.multiple_of(j * tk, tk), tk)
                dk_ref[rows, :] = dkT_ref[j].T.astype(BF16)
                dv_ref[rows, :] = dvT_ref[j].T.astype(BF16)
                return carry

            lax.fori_loop(0, nk, untranspose, 0)

    scratch = ([pltpu.VMEM((nk, LANE, tk), BF16), pltpu.VMEM((nk, LANE, tk), BF16), pltpu.VMEM((2, T, LANE), BF16),
                pltpu.VMEM((nk, LANE, tk), F32), pltpu.VMEM((nk, LANE, tk), F32)]
               + [pltpu.VMEM((tq, LANE), F32)] * 4
               + [pltpu.VMEM((2, tq, tk), F32)] * 4 + [pltpu.VMEM((2, tq, tk), BF16)] * 4)
    r_in, r_in_specs, r_out, r_out_specs, r_sems = _rider_specs(rider)
    body = _with_rider(body, 7, 3, len(scratch), rider, *_first_last_step(n_pairs, nq))
    kept_spec = pl.BlockSpec((2, 1, tq, 2 * tk), lambda p, i: (p, i, 0, 0))
    return _pc(
        body, name="attn_bwd", grid=(n_pairs, nq),
        out_shape=tuple([jax.ShapeDtypeStruct((T, ATTN_W), BF16)] * 3 + r_out),
        in_specs=[pl.BlockSpec((tq, LANE), lambda p, i: (i, p)),
                  pl.BlockSpec((T, LANE), lambda p, i: (0, n_pairs + p)),
                  pl.BlockSpec((T, LANE), lambda p, i: (0, 2 * n_pairs + p)),
                  pl.BlockSpec((tq, LANE), lambda p, i: (i, p)),
                  pl.BlockSpec((2, 1, tq, LANE), lambda p, i: (p, i, 0, 0)), kept_spec, kept_spec] + r_in_specs,
        out_specs=tuple([pl.BlockSpec((tq, LANE), lambda p, i: (i, p)), pl.BlockSpec((T, LANE), lambda p, i: (0, p)),
                         pl.BlockSpec((T, LANE), lambda p, i: (0, p))] + r_out_specs),
        scratch_shapes=scratch + r_sems,
        compiler_params=_params("arbitrary", "arbitrary"),
    )(projb, projb, projb, do, carries, kept_w, kept_beta, *r_in)


def _dh_norm_bwd(segs, w, x, g, dxo, tm, rider=None):
    T, D = x.shape
    nm = T // tm
    n_seg = len(segs)
    offs = [sum(y.shape[1] for y in segs[:n]) for n in range(n_seg + 1)]
    assert offs[-1] == w.shape[1]

    def body(*refs):
        seg_refs = refs[:n_seg]
        w_ref, x_ref, g_ref, dxo_ref, dx_ref, dg_ref = refs[n_seg:]

        @pl.when(pl.program_id(0) == 0)
        def _():
            dg_ref[...] = jnp.zeros_like(dg_ref)

        dhv = None
        for n in range(n_seg):
            part = lax.dot_general(seg_refs[n][...], w_ref[:, offs[n]:offs[n + 1]], NT_DIMS, preferred_element_type=F32)
            dhv = part if dhv is None else dhv + part
        xv = x_ref[...]
        r = lax.rsqrt(jnp.mean(xv * xv, axis=-1, keepdims=True) + RMS_EPS)
        xh = xv * r
        dg_ref[...] += jnp.sum(dhv * xh, axis=0, keepdims=True)
        dhg = dhv * g_ref[...]
        dx_ref[...] = dxo_ref[...] + r * (dhg - xh * jnp.mean(dhg * xh, axis=-1, keepdims=True))

    r_in, r_in_specs, r_out, r_out_specs, r_sems = _rider_specs(rider)
    body = _with_rider(body, n_seg + 4, 2, 0, rider, lambda: pl.program_id(0) == 0, lambda: pl.program_id(0) == nm - 1)
    row = lambda i: (i, 0)
    fixed = lambda i: (0, 0)
    return _pc(
        body, name="d_h_norm_bwd", grid=(nm,),
        out_shape=tuple([jax.ShapeDtypeStruct((T, D), F32), jax.ShapeDtypeStruct((1, D), F32)] + r_out),
        in_specs=[pl.BlockSpec((tm, y.shape[1]), row) for y in segs]
        + [pl.BlockSpec(w.shape, fixed), pl.BlockSpec((tm, D), row), pl.BlockSpec((1, D), fixed), pl.BlockSpec((tm, D), row)]
        + r_in_specs,
        out_specs=tuple([pl.BlockSpec((tm, D), row), pl.BlockSpec((1, D), fixed)] + r_out_specs),
        scratch_shapes=r_sems,
        compiler_params=_params("arbitrary"),
    )(*segs, w, x, g, dxo, *r_in)


def _adamw(pieces, w, m, v, name):
    G = len(pieces)
    rows, cols = pieces[0].shape[1:]
    assert w.shape == (G * rows, cols)
    br = rows
    while br * cols > 65536 and br % 16 == 0:
        br //= 2
    nb = rows // br
    c1 = 1.0 / (1.0 - ADAM_B1 ** ADAM_STEP)
    c2 = 1.0 / (1.0 - ADAM_B2 ** ADAM_STEP)

    def body(*refs):
        p_refs = refs[:G]
        w_ref, m_ref, v_ref, g_ref, d_ref, nm_ref, nv_ref = refs[G:]
        g = None
        for n, p_ref in enumerate(p_refs):
            gn = p_ref[0].astype(F32)
            for s in range(1, N_DEV):
                gn = gn + p_ref[s].astype(F32)
            g = gn if g is None else jnp.where(pl.program_id(0) == n, gn, g)
        nm = ADAM_B1 * m_ref[...] + (1.0 - ADAM_B1) * g
        nv = ADAM_B2 * v_ref[...] + (1.0 - ADAM_B2) * (g * g)
        g_ref[...] = g
        nm_ref[...] = nm
        nv_ref[...] = nv
        d_ref[...] = -ADAM_LR * ((nm * c1) / (jnp.sqrt(nv * c2) + ADAM_EPS) + ADAM_WD * w_ref[...])

    blk = pl.BlockSpec((br, cols), lambda n, i: (n * nb + i, 0))
    shape = jax.ShapeDtypeStruct((G * rows, cols), F32)
    return _pc(
        body, name=name, grid=(G, nb), out_shape=(shape, shape, shape, shape),
        in_specs=[pl.BlockSpec((N_DEV, br, cols), lambda n, i: (0, i, 0))] * G + [blk, blk, blk],
        out_specs=(blk, blk, blk, blk),
        compiler_params=_params("parallel", "parallel"),
    )(*pieces, w, m, v)


def _rows128(a):
    flat = a.reshape(-1)
    n = flat.shape[0]
    padded = -(-n // (8 * LANE)) * (8 * LANE)
    if padded != n:
        flat = jnp.concatenate([flat, jnp.zeros((padded - n,), flat.dtype)])
    return flat.reshape(-1, LANE)


def _pack(parts):
    return jnp.concatenate([_rows128(p) for p in parts], axis=0)


def _unpack(packed, like):
    out, r = [], 0
    for a in like:
        n = a.size
        nr = -(-n // (8 * LANE)) * 8
        out.append(packed[r:r + nr].reshape(-1)[:n].reshape(a.shape))
        r += nr
    return out


def kernel(x, norm_g, w_in, b_gate, pool_w, pool_scale, w_pool_up, w_attn_up, w_out, final_g, loss_target, m_norm_g, m_w_in, m_b_gate, m_pool_w, m_pool_scale, m_w_pool_up, m_w_attn_up, m_w_out, m_final_g, v_norm_g, v_w_in, v_b_gate, v_pool_w, v_pool_scale, v_w_pool_up, v_w_attn_up, v_w_out, v_final_g):
    L = norm_g.shape[0]
    T, D = x.shape[1], x.shape[2]
    NW = w_in.shape[2] * N_DEV
    assert NW == 2 * POOL_W + 4 * ATTN_W + 2 * D and x.shape[0] == 1
    tm = min(512, T)
    tq = min(256, T // 2)
    x0 = x.reshape(T, D)
    target = loss_target.reshape(T, D)

    assert L >= 2
    win_first = jnp.transpose(_gather_two_level(w_in[0].astype(BF16), "gather_w_in0"), (1, 0, 2)).reshape(D, NW)
    rest = [w_in[1:].astype(BF16), w_pool_up.astype(BF16), w_attn_up.astype(BF16), w_out.astype(BF16)]

    saved = []
    xl = x0
    for l in range(L):
        proj, projb, h = _norm_inproj(xl, norm_g[l:l + 1], win_first if l == 0 else win_rest[l - 1], min(256, T))
        y_pool = _pool_fwd(proj, pool_w[l], pool_scale[l:l + 1], tm)
        if l == 0:
            o, y_attn, carries, kept_w, kept_b, g_in, g_pu, g_au, g_out = _attn_fwd(projb, proj, tq, rider=(rest, [False] * 4))
            win_rest = jnp.transpose(g_in, (1, 2, 0, 3)).reshape(L - 1, D, NW)
            wpu_full = jnp.transpose(g_pu, (1, 2, 0, 3)).reshape(L, POOL_W, D)
            wau_full = jnp.transpose(g_au, (1, 2, 0, 3)).reshape(L, ATTN_W, D)
            wout_full = jnp.transpose(g_out, (1, 0, 2, 3)).reshape(L, D, D)
        else:
            o, y_attn, carries, kept_w, kept_b = _attn_fwd(projb, proj, tq)
        merged = _merge_fwd(y_pool, y_attn, wpu_full[l], wau_full[l], proj, b_gate[l:l + 1], min(256, T))
        saved.append((xl, proj, projb, h, y_pool, o, y_attn, (carries, kept_w, kept_b), merged))
        if l < L - 1:
            xl = _mm_nn_res(merged, wout_full[l], xl, tm, "out_proj")

    dx, d_final_g, loss_part = _out_proj_final_loss(merged, wout_full[L - 1], xl, final_g.reshape(1, D), target, tm)

    d_norm_g, d_b_gate, d_pool_w, d_pool_scale = [None] * L, [None] * L, [None] * L, [None] * L
    d_win, d_wpu, d_wau, d_wout = [None] * L, [None] * L, [None] * L, [None] * L
    small_like = [norm_g, b_gate, pool_w, pool_scale, final_g, jnp.zeros((8, LANE), F32)]
    for l in reversed(range(L)):
        xin, proj, projb, h, y_pool, o, y_attn, carries, merged = saved[l]
        win_l = win_first if l == 0 else win_rest[l - 1]
        d_wout[l] = _mm_tn(merged, dx, 1, D, min(1024, T), "d_w_out").reshape(N_DEV, D // N_DEV, D)
        dp, da, dgl, d_b_gate[l] = _merge_bwd(dx, wout_full[l], y_pool, y_attn, wpu_full[l], wau_full[l], proj,
                                              b_gate[l:l + 1], min(256, T))
        d_wpu[l] = _mm_tn(y_pool, dp, N_DEV, D, min(1024, T), "d_w_pool_up")
        d_wau[l] = _mm_tn(y_attn, da, N_DEV, D, min(1024, T), "d_w_attn_up")
        du, dzp, d_pool_w[l], d_pool_scale[l] = _pool_bwd(proj, dp, wpu_full[l], pool_w[l], pool_scale[l:l + 1], tm)
        do, dza = _attn_gate_bwd(da, wau_full[l], o, proj, tm)
        if l == 0:
            small = _pack([jnp.concatenate([jnp.zeros((1, D), F32)] + d_norm_g[1:], 0), jnp.concatenate(d_b_gate, 0),
                           jnp.stack(d_pool_w, 0), jnp.concatenate(d_pool_scale, 0), d_final_g, loss_part])
            early = d_win[1:] + d_wpu + d_wau + d_wout
            dq, dk, dv, got_small, *got_early = _attn_bwd(
                projb, do, *carries, tq, rider=([small] + early, [False] + [True] * len(early)))
        else:
            dq, dk, dv = _attn_bwd(projb, do, *carries, tq)
        segs = [du, dzp, dq, dk, dv, dza, dgl]
        sw = NW // N_DEV
        half = _mm_tn_segs(h, segs[5:], sw, min(1024, T), "d_w_in_b", into=jnp.zeros((N_DEV, D, sw), BF16), first_shard=N_DEV // 2)
        d_win[l] = _mm_tn_segs(h, segs[:5], sw, min(1024, T), "d_w_in_a", into=half)
        if l == 0:
            dx, d_norm_g[l], got_win0 = _dh_norm_bwd(segs, win_l, xin, norm_g[l:l + 1], dx, min(256, T),
                                                     rider=([d_win[0]], [True]))
        else:
            dx, d_norm_g[l] = _dh_norm_bwd(segs, win_l, xin, norm_g[l:l + 1], dx, min(256, T))

    got_norm0 = _exchange([d_norm_g[0].reshape(-1, LANE)], [False], "gather_norm_grad")[0]
    r_small = jnp.concatenate([got_norm0, got_small[:, D // LANE:]], axis=1)

    def update(pieces, w, m, v, name):
        cols = w.shape[-1]
        res = _adamw([p.reshape(N_DEV, -1, cols) for p in pieces], w.reshape(-1, cols), m.reshape(-1, cols),
                     v.reshape(-1, cols), name)
        return [r.reshape(w.shape) for r in res]

    u_in = update([got_win0] + got_early[:L - 1], w_in, m_w_in, v_w_in, "adamw_w_in")
    u_pu = update(got_early[L - 1:2 * L - 1], w_pool_up, m_w_pool_up, v_w_pool_up, "adamw_w_pool_up")
    u_au = update(got_early[2 * L - 1:3 * L - 1], w_attn_up, m_w_attn_up, v_w_attn_up, "adamw_w_attn_up")
    u_out = update(got_early[3 * L - 1:4 * L - 1], w_out, m_w_out, v_w_out, "adamw_w_out")
    zeros = small_like[-1]
    smalls = _adamw([r_small],
                    _pack([norm_g, b_gate, pool_w, pool_scale, final_g, zeros]),
                    _pack([m_norm_g, m_b_gate, m_pool_w, m_pool_scale, m_final_g, zeros]),
                    _pack([v_norm_g, v_b_gate, v_pool_w, v_pool_scale, v_final_g, zeros]), "adamw_small")
    s_g, s_d, s_m, s_v = [_unpack(s, small_like) for s in smalls]
    loss = s_g[5][0, 0]

    def ordered(k):
        s = (s_g, s_d, s_m, s_v)[k]
        return [s[0], u_in[k], s[1], s[2], s[3], u_pu[k], u_au[k], u_out[k], s[4]]

    return (loss, dx.reshape(x.shape), *ordered(0), *ordered(1), *ordered(2), *ordered(3))
```

```python
import jax
import jax.numpy as jnp
from jax import lax
from jax.experimental import pallas as pl
from jax.experimental.pallas import tpu as pltpu

F32 = jnp.float32
BF16 = jnp.bfloat16

N_DEV = 8
HEAD_DIM = 64
ATTN_W = 512
POOL_W = 512
POOL_G = 128
POOL_WINDOWS = (2, 4, 8, 16)
HALO = 16
LANE = 128
RMS_EPS = 1e-6
ZERO_WEIGHT = 110.0
NO_CARRY = 3.0e38
ADAM_LR, ADAM_B1, ADAM_B2, ADAM_EPS, ADAM_WD, ADAM_STEP = 0.001, 0.9, 0.999, 1e-08, 0.01, 10
VMEM_LIMIT = 56 * 1024 * 1024

NT_DIMS = (((1,), (1,)), ((), ()))
TN_DIMS = (((0,), (0,)), ((), ()))


def _pc(body, **kw):
    return pl.pallas_call(body, **kw)


def _params(*sem):
    return pltpu.CompilerParams(dimension_semantics=sem, vmem_limit_bytes=VMEM_LIMIT)


def _sigmoid(z):
    return 1.0 / (1.0 + jnp.exp(-z))


def _silu_and_grad(z):
    s = _sigmoid(z)
    return z * s, s * (1.0 + z * (1.0 - s))


def _my_index():
    return 4 * lax.axis_index("x") + 2 * lax.axis_index("y") + lax.axis_index("c")


def _peer(k):
    x, y, c = lax.axis_index("x"), lax.axis_index("y"), lax.axis_index("c")
    px = lax.rem(x + ((k >> 2) & 1), 2)
    py = lax.rem(y + ((k >> 1) & 1), 2)
    pc = lax.rem(c + (k & 1), 2)
    return (px, py, pc), 4 * px + 2 * py + pc


def _exchange_copies(ins, outs, sems, scatter):
    send_sems, recv_sems, local_sems = sems
    n = len(ins)
    me = _my_index()

    def src(a, idx):
        return ins[a].at[idx] if scatter[a] else ins[a]

    local = [pltpu.make_async_copy(src(a, me), outs[a].at[me], local_sems.at[a]) for a in range(n)]
    sends, arrivals = [], []
    for k in (1, 2, 4, 3, 5, 6, 7):
        dev, pidx = _peer(k)
        for a in range(n):
            sem = dict(send_sem=send_sems.at[a * N_DEV + k], recv_sem=recv_sems.at[a * N_DEV + k],
                       device_id=dev, device_id_type=pl.DeviceIdType.MESH)
            sends.append(pltpu.make_async_remote_copy(src_ref=src(a, pidx), dst_ref=outs[a].at[me], **sem))
            arrivals.append(pltpu.make_async_remote_copy(src_ref=src(a, pidx), dst_ref=outs[a].at[pidx], **sem))
    return local, sends, arrivals


def _exchange_start(ins, outs, sems, scatter):
    local, sends, _ = _exchange_copies(ins, outs, sems, scatter)
    for cp in local + sends:
        cp.start()


def _exchange_wait(ins, outs, sems, scatter):
    local, sends, arrivals = _exchange_copies(ins, outs, sems, scatter)
    for cp in arrivals:
        cp.wait_recv()
    for cp in sends:
        cp.wait_send()
    for cp in local:
        cp.wait()


def _exchange_shapes(arrays, scatter):
    n = len(arrays)
    out_shape = [jax.ShapeDtypeStruct((N_DEV,) + tuple(a.shape[1:] if s else a.shape), a.dtype)
                 for a, s in zip(arrays, scatter)]
    sems = [pltpu.SemaphoreType.DMA((n * N_DEV,)), pltpu.SemaphoreType.DMA((n * N_DEV,)),
            pltpu.SemaphoreType.DMA((n,))]
    return out_shape, sems


def _exchange(arrays, scatter, name):
    n = len(arrays)

    def body(*refs):
        ins, outs, sems = refs[:n], refs[n:2 * n], refs[2 * n:]
        _exchange_start(ins, outs, sems, scatter)
        _exchange_wait(ins, outs, sems, scatter)

    out_shape, sems = _exchange_shapes(arrays, scatter)
    any_spec = pl.BlockSpec(memory_space=pl.ANY)
    return _pc(
        body, name=name, out_shape=tuple(out_shape),
        in_specs=[any_spec] * n, out_specs=tuple([any_spec] * n), scratch_shapes=sems,
    )(*arrays)


def _gather_two_level(x, name):
    def body(x_ref, out_ref, send_sems, recv_sems, local_sem):
        X, Y, C = lax.axis_index("x"), lax.axis_index("y"), lax.axis_index("c")
        me, sibling = (X, Y, C), (X, Y, 1 - C)
        chips = [(1 - X, Y), (X, 1 - Y), (1 - X, 1 - Y)]

        def slot(px, py, pc):
            return out_ref.at[4 * px + 2 * py + pc]

        def copy(k, block, to, src=None):
            return pltpu.make_async_remote_copy(
                src_ref=slot(*block) if src is None else src, dst_ref=slot(*block),
                send_sem=send_sems.at[k], recv_sem=recv_sems.at[k], device_id=to, device_id_type=pl.DeviceIdType.MESH)

        mine = pltpu.make_async_copy(x_ref, slot(*me), local_sem)
        mine.start()
        first = [copy(0, me, sibling, src=x_ref)] + [copy(1 + j, me, (*chip, C), src=x_ref) for j, chip in enumerate(chips)]
        for cp in first:
            cp.start()
        passed = [copy(4 + j, (*chip, C), sibling) for j, chip in enumerate(chips)]
        for j, chip in enumerate(chips):
            copy(1 + j, (*chip, C), me).wait_recv()
            passed[j].start()
        copy(0, sibling, me).wait_recv()
        for j, chip in enumerate(chips):
            copy(4 + j, (*chip, 1 - C), me).wait_recv()
        for cp in first + passed:
            cp.wait_send()
        mine.wait()

    any_spec = pl.BlockSpec(memory_space=pl.ANY)
    return _pc(
        body, name=name, out_shape=jax.ShapeDtypeStruct((N_DEV,) + x.shape, x.dtype),
        in_specs=[any_spec], out_specs=any_spec,
        scratch_shapes=[pltpu.SemaphoreType.DMA((7,)), pltpu.SemaphoreType.DMA((7,)), pltpu.SemaphoreType.DMA(())],
    )(x)


def _with_rider(body, n_in, n_out, n_scratch, rider, first, last):
    if rider is None:
        return body
    arrays, scatter = rider
    n = len(arrays)

    def wrapped(*refs):
        ins, r_ins = refs[:n_in], refs[n_in:n_in + n]
        outs = refs[n_in + n:n_in + n + n_out]
        r_outs = refs[n_in + n + n_out:n_in + 2 * n + n_out]
        scratch = refs[n_in + 2 * n + n_out:n_in + 2 * n + n_out + n_scratch]
        sems = refs[n_in + 2 * n + n_out + n_scratch:]

        @pl.when(first())
        def _():
            _exchange_start(r_ins, r_outs, sems, scatter)

        body(*ins, *outs, *scratch)

        @pl.when(last())
        def _():
            _exchange_wait(r_ins, r_outs, sems, scatter)

    return wrapped


def _rider_specs(rider):
    if rider is None:
        return [], [], [], [], []
    arrays, scatter = rider
    out_shape, sems = _exchange_shapes(arrays, scatter)
    any_spec = pl.BlockSpec(memory_space=pl.ANY)
    return list(arrays), [any_spec] * len(arrays), out_shape, [any_spec] * len(arrays), sems


def _mm_nn_res(a, b, res, tm, name):
    T, K = a.shape
    N = b.shape[1]

    def body(a_ref, b_ref, r_ref, o_ref):
        o_ref[...] = r_ref[...] + jnp.dot(a_ref[...], b_ref[...], preferred_element_type=F32)

    return _pc(
        body, name=name, grid=(T // tm,), out_shape=jax.ShapeDtypeStruct((T, N), F32),
        in_specs=[pl.BlockSpec((tm, K), lambda i: (i, 0)), pl.BlockSpec((K, N), lambda i: (0, 0)),
                  pl.BlockSpec((tm, N), lambda i: (i, 0))],
        out_specs=pl.BlockSpec((tm, N), lambda i: (i, 0)),
        compiler_params=_params("parallel"),
    )(a, b, res)


def _mm_tn(a, b, n_col_shards, tn, tk, name):
    T, M = a.shape
    N = b.shape[1]
    sw = N // n_col_shards
    per_step = tn // sw
    nk = T // tk

    def body(a_ref, b_ref, o_ref, acc_sc):
        k = pl.program_id(1)
        part = lax.dot_general(a_ref[...].astype(BF16), b_ref[...].astype(BF16), TN_DIMS,
                               preferred_element_type=F32)

        @pl.when(k == 0)
        def _():
            acc_sc[...] = part

        @pl.when(k > 0)
        def _():
            acc_sc[...] += part

        @pl.when(k == nk - 1)
        def _():
            for s in range(per_step):
                o_ref[s] = acc_sc[:, s * sw:(s + 1) * sw].astype(BF16)

    return _pc(
        body, name=name, grid=(N // tn, nk),
        out_shape=jax.ShapeDtypeStruct((n_col_shards, M, sw), BF16),
        in_specs=[pl.BlockSpec((tk, M), lambda j, k: (k, 0)), pl.BlockSpec((tk, tn), lambda j, k: (k, j))],
        out_specs=pl.BlockSpec((per_step, M, sw), lambda j, k: (j, 0, 0)),
        scratch_shapes=[pltpu.VMEM((M, tn), F32)],
        compiler_params=_params("parallel", "arbitrary"),
    )(a, b)


def _mm_tn_segs(a, segs, sw, tk, name, into=None, first_shard=0):
    T, M = a.shape
    N = sum(x.shape[1] for x in segs)
    assert N % sw == 0 and first_shard % (N // sw) == 0
    n_seg = len(segs)
    nk = T // tk
    ns = N // sw

    def body(a_ref, *rest):
        seg_refs, o_ref, acc_sc = rest[:n_seg], rest[-2], rest[-1]
        k = pl.program_id(0)
        slab = jnp.concatenate([r[...] for r in seg_refs], axis=1)
        part = lax.dot_general(a_ref[...], slab, TN_DIMS, preferred_element_type=F32)

        @pl.when(k == 0)
        def _():
            acc_sc[...] = part

        @pl.when(k > 0)
        def _():
            acc_sc[...] += part

        @pl.when(k == nk - 1)
        def _():
            for j in range(ns):
                o_ref[j] = acc_sc[:, j * sw:(j + 1) * sw].astype(BF16)

    extra, extra_specs, alias = [], [], {}
    out_shape = jax.ShapeDtypeStruct((ns, M, sw), BF16)
    if into is not None:
        extra, extra_specs, alias = [into], [pl.BlockSpec(memory_space=pl.ANY)], {1 + n_seg: 0}
        out_shape = jax.ShapeDtypeStruct(into.shape, BF16)
    blk = first_shard // ns
    return _pc(
        body, name=name, grid=(nk,), out_shape=out_shape,
        in_specs=[pl.BlockSpec((tk, M), lambda k: (k, 0))] + [pl.BlockSpec((tk, x.shape[1]), lambda k: (k, 0)) for x in segs]
        + extra_specs,
        out_specs=pl.BlockSpec((ns, M, sw), lambda k: (blk, 0, 0)),
        scratch_shapes=[pltpu.VMEM((M, N), F32)], input_output_aliases=alias,
        compiler_params=_params("arbitrary"),
    )(a, *segs, *extra)


def _proj_layout(D):
    return {"u": 0, "z_pool": POOL_W, "gates": 2 * POOL_W, "z_attn": 2 * POOL_W + 2 * D, "width": 2 * POOL_W + 2 * D + ATTN_W}


def _norm_inproj(x, g, w, tm):
    T, D = x.shape
    NW = w.shape[1]
    lay = _proj_layout(D)
    qkv0, za0, gl0 = 2 * POOL_W, 2 * POOL_W + 3 * ATTN_W, 2 * POOL_W + 4 * ATTN_W

    def body(x_ref, g_ref, w_ref, proj_ref, qkv_ref, h_ref):
        xv = x_ref[...]
        r = lax.rsqrt(jnp.mean(xv * xv, axis=-1, keepdims=True) + RMS_EPS)
        h = ((xv * r) * g_ref[...]).astype(BF16)
        h_ref[...] = h

        def cols(lo, hi):
            return jnp.dot(h, w_ref[:, lo:hi], preferred_element_type=F32)

        proj_ref[:, :lay["gates"]] = cols(0, qkv0)
        qkv_ref[...] = cols(qkv0, za0).astype(BF16)
        proj_ref[:, lay["gates"]:lay["z_attn"]] = cols(gl0, NW)
        proj_ref[:, lay["z_attn"]:] = cols(za0, gl0)

    return _pc(
        body, name="norm_inproj", grid=(T // tm,),
        out_shape=(jax.ShapeDtypeStruct((T, lay["width"]), F32), jax.ShapeDtypeStruct((T, 3 * ATTN_W), BF16),
                   jax.ShapeDtypeStruct((T, D), BF16)),
        in_specs=[pl.BlockSpec((tm, D), lambda i: (i, 0)), pl.BlockSpec((1, D), lambda i: (0, 0)),
                  pl.BlockSpec((D, NW), lambda i: (0, 0), pipeline_mode=pl.Buffered(1))],
        out_specs=(pl.BlockSpec((tm, lay["width"]), lambda i: (i, 0)), pl.BlockSpec((tm, 3 * ATTN_W), lambda i: (i, 0)),
                   pl.BlockSpec((tm, D), lambda i: (i, 0))),
        compiler_params=_params("parallel"),
    )(x, g, w)


def _window_sums(xh, forward):
    n = xh.shape[0]
    sums, s, step = [], xh, 1
    for _ in POOL_WINDOWS:
        s = s + pltpu.roll(s, step if forward else n - step, 0)
        sums.append(s)
        step *= 2
    return sums


def _pooled(u, halo, row):
    sums = _window_sums(jnp.concatenate([halo, u], axis=0), True)
    out = []
    for g, w in enumerate(POOL_WINDOWS):
        cols = slice(g * POOL_G, (g + 1) * POOL_G)
        cnt = jnp.minimum(row + 1, w).astype(F32)
        out.append(sums[g][HALO:, cols] / cnt - u[:, cols])
    return out


def _pool_fwd(proj, pool_w, scale, R):
    T = proj.shape[0]

    def body(u_ref, z_ref, pw_ref, sc_ref, y_ref, halo_sc):
        i = pl.program_id(0)

        @pl.when(i == 0)
        def _():
            halo_sc[...] = jnp.zeros_like(halo_sc)

        u = u_ref[...]
        row = i * R + lax.broadcasted_iota(jnp.int32, (R, 1), 0)
        pooled = _pooled(u, halo_sc[...], row)
        mixed = jnp.concatenate(
            [jnp.dot(pooled[g].astype(BF16), pw_ref[g].astype(BF16), preferred_element_type=F32)
             for g in range(len(POOL_WINDOWS))], axis=1)
        z = z_ref[...]
        y_ref[...] = ((mixed * sc_ref[...]) * (z * _sigmoid(z))).astype(BF16)
        halo_sc[...] = u[R - HALO:, :]

    return _pc(
        body, name="pool_fwd", grid=(T // R,), out_shape=jax.ShapeDtypeStruct((T, POOL_W), BF16),
        in_specs=[pl.BlockSpec((R, POOL_W), lambda i: (i, 0)), pl.BlockSpec((R, POOL_W), lambda i: (i, 1)),
                  pl.BlockSpec((4, POOL_G, POOL_G), lambda i: (0, 0, 0)), pl.BlockSpec((1, POOL_W), lambda i: (0, 0))],
        out_specs=pl.BlockSpec((R, POOL_W), lambda i: (i, 0)),
        scratch_shapes=[pltpu.VMEM((HALO, POOL_W), F32)],
        compiler_params=_params("arbitrary"),
    )(proj, proj, pool_w, scale)


def _softplus(l):
    return jnp.maximum(l, 0.0) + jnp.log(1.0 + jnp.exp(-jnp.abs(l)))


def _first_last_step(n0, n1):
    return (lambda: jnp.logical_and(pl.program_id(0) == 0, pl.program_id(1) == 0),
            lambda: jnp.logical_and(pl.program_id(0) == n0 - 1, pl.program_id(1) == n1 - 1))


def _head_lanes():
    lane = lax.broadcasted_iota(jnp.int32, (1, LANE), 1)
    return [lane < HEAD_DIM, lane >= HEAD_DIM]


def _head_masks(q, scale):
    qf = q.astype(F32) * scale
    return [jnp.where(m, qf, 0.0).astype(BF16) for m in _head_lanes()]


def _wide(c, width):
    return jnp.concatenate([c] * (width // LANE), axis=1)


def _max_row_norm2(x, heads):
    sq = x.astype(F32) * x.astype(F32)
    return [jnp.max(jnp.sum(jnp.where(m, sq, 0.0), axis=1, keepdims=True), axis=0, keepdims=True) for m in heads]


def _fill_blocks(src_ref, nk, tk, transposed_sc=None, masked_sc=None, norm_sc=None):
    heads = _head_lanes()
    if norm_sc is not None:
        for a in range(2):
            norm_sc[a][...] = jnp.zeros_like(norm_sc[a])

    def step(j, carry):
        rows = pl.ds(pl.multiple_of(j * tk, tk), tk)
        blk = src_ref[rows, :]
        if norm_sc is not None:
            for a, n2 in enumerate(_max_row_norm2(blk, heads)):
                norm_sc[a][...] = jnp.maximum(norm_sc[a][...], n2)
        if transposed_sc is not None:
            transposed_sc[j] = blk.astype(F32).T.astype(BF16)
        if masked_sc is not None:
            for a in range(2):
                masked_sc[a, rows, :] = jnp.where(heads[a], blk, jnp.zeros_like(blk))
        return carry

    lax.fori_loop(0, nk, step, 0)


def _attn_fwd(projb, proj, tq, rider=None):
    T = projb.shape[0]
    nq = T // tq
    tk, nk = tq, nq
    assert nk < LANE
    n_pairs = ATTN_W // LANE
    zb = (proj.shape[1] - ATTN_W) // LANE

    def body(q_ref, k_ref, v_ref, za_ref, o_ref, y_ref, c_ref, ws_ref, bs_ref, kT_sc, vm_sc, *per_head):
        c_sc, cm_sc, o_sc, kn_sc, l_sc, w_sc = (per_head[2 * n:2 * n + 2] for n in range(6))
        i = pl.program_id(1)

        @pl.when(i == 0)
        def _():
            _fill_blocks(k_ref, nk, tk, transposed_sc=kT_sc, norm_sc=kn_sc)
            _fill_blocks(v_ref, nk, tk, masked_sc=vm_sc)

        qs = _head_masks(q_ref[...], HEAD_DIM ** -0.5)
        lane = lax.broadcasted_iota(jnp.int32, (1, LANE), 1)
        valid = lax.broadcasted_iota(jnp.int32, (tq, tk), 1) < lax.broadcasted_iota(jnp.int32, (tq, tk), 0)
        suffix = (lax.broadcasted_iota(jnp.int32, (tk, tk), 0) >= lax.broadcasted_iota(jnp.int32, (tk, tk), 1)).astype(BF16)
        for a in range(2):
            c_sc[a][...] = jnp.zeros_like(c_sc[a])
            o_sc[a][...] = jnp.zeros_like(o_sc[a])
            cm_sc[a][...] = jnp.full(cm_sc[a].shape, NO_CARRY, F32)
        l_max = [jnp.sqrt(qn * kn_sc[a][...]) for a, qn in enumerate(_max_row_norm2(qs[0] + qs[1], _head_lanes()))]

        def logits(j, slot):
            kT = kT_sc[j]
            for a in range(2):
                l_sc[a][slot] = jnp.dot(qs[a], kT, preferred_element_type=F32)

        def values(j, slot):
            rows = pl.ds(pl.multiple_of(j * tk, tk), tk)
            for a in range(2):
                o_sc[a][...] += jnp.dot(w_sc[a][slot], vm_sc[a, rows, :], preferred_element_type=F32)

        def softplus(slot, masked):
            out = []
            for a in range(2):
                x = _softplus(l_sc[a][slot])
                out.append(jnp.where(valid, x, 0.0) if masked else x)
            return out

        def finish_weights(j, slot, sp, inc, mask, keep=None, cols=None):
            for a in range(2):
                c = c_sc[a][...]
                l = l_sc[a][slot]
                w = jnp.exp(l - inc[a] - _wide(c, tk))
                if mask is not None:
                    w = jnp.where(mask, w, 0.0)
                add = inc[a][:, 0:1]
                if keep is not None:
                    w = jnp.where(keep, w, 0.0)
                    add = jnp.where(keep, add, 0.0)
                w_sc[a][slot] = w.astype(BF16)
                if cols is not None:
                    beta = jnp.exp(l - sp[a])
                    ws_ref[a, 0, :, cols] = w
                    bs_ref[a, 0, :, cols] = beta if mask is None else jnp.where(mask, beta, 0.0)
                cm_sc[a][...] = jnp.where(lane == j, c, cm_sc[a][...])
                c_sc[a][...] = c + add

        def weights(j, slot):
            sp = softplus(slot, False)
            inc = [jnp.dot(sp[a].astype(BF16), suffix, preferred_element_type=F32) for a in range(2)]
            finish_weights(j, slot, sp, inc, None)

        def more():
            live = [jnp.min(c_sc[a][...], axis=0, keepdims=True) - l_max[a][0:1, :] <= ZERO_WEIGHT for a in range(2)]
            return jnp.max(jnp.where(jnp.logical_or(live[0], live[1]), 1, 0))

        logits(i, 0)
        logits(jnp.maximum(i - 1, 0), 1)
        sp = softplus(0, True) + softplus(1, False)
        inc = [jnp.dot(x.astype(BF16), suffix, preferred_element_type=F32) for x in sp]
        finish_weights(i, 0, sp[:2], inc[:2], valid, cols=slice(tk, 2 * tk))
        finish_weights(i - 1, 1, sp[2:], inc[2:], None, keep=i > 0, cols=slice(0, tk))
        values(i, 0)
        logits(jnp.maximum(i - 2, 0), 0)

        def step(t, slot):
            logits(jnp.maximum(i - t - 1, 0), 1 - slot)
            values(i - t + 1, 1 - slot)
            weights(i - t, slot)

        def two_steps(carry):
            tt, _ = carry
            step(2 * tt + 2, 0)
            step(2 * tt + 3, 1)
            return tt + 1, more()

        pairs, go = lax.while_loop(lambda c: jnp.logical_and(2 * c[0] + 3 <= i, c[1] > 0), two_steps, (0, more()))
        done = 1 + 2 * pairs
        one_more = jnp.logical_and(done + 1 == i, go > 0)

        @pl.when(one_more)
        def _():
            step(i, 0)
            values(0, 0)

        @pl.when(jnp.logical_not(one_more))
        def _():
            values(jnp.maximum(i - done, 0), 1)

        o = o_sc[0][...] + o_sc[1][...]
        o_ref[...] = o
        za = za_ref[...]
        y_ref[...] = (o * (za * _sigmoid(za))).astype(BF16)
        c_ref[0, 0] = jnp.where(lane == LANE - 1, l_max[0][0:1, :], cm_sc[0][...])
        c_ref[1, 0] = jnp.where(lane == LANE - 1, l_max[1][0:1, :], cm_sc[1][...])

    scratch = ([pltpu.VMEM((nk, LANE, tk), BF16), pltpu.VMEM((2, T, LANE), BF16)]
               + [pltpu.VMEM((tq, LANE), F32)] * 6 + [pltpu.VMEM((8, LANE), F32)] * 2
               + [pltpu.VMEM((2, tq, tk), F32)] * 2 + [pltpu.VMEM((2, tq, tk), BF16)] * 2)
    r_in, r_in_specs, r_out, r_out_specs, r_sems = _rider_specs(rider)
    body = _with_rider(body, 4, 5, len(scratch), rider, *_first_last_step(n_pairs, nq))
    kept = jax.ShapeDtypeStruct((2 * n_pairs, nq, tq, 2 * tk), F32)
    kept_spec = pl.BlockSpec((2, 1, tq, 2 * tk), lambda p, i: (p, i, 0, 0))
    return _pc(
        body, name="attn_fwd", grid=(n_pairs, nq),
        out_shape=tuple([jax.ShapeDtypeStruct((T, ATTN_W), F32), jax.ShapeDtypeStruct((T, ATTN_W), BF16),
                         jax.ShapeDtypeStruct((2 * n_pairs, nq, tq, LANE), F32), kept, kept] + r_out),
        in_specs=[pl.BlockSpec((tq, LANE), lambda p, i: (i, p)),
                  pl.BlockSpec((T, LANE), lambda p, i: (0, n_pairs + p)),
                  pl.BlockSpec((T, LANE), lambda p, i: (0, 2 * n_pairs + p)),
                  pl.BlockSpec((tq, LANE), lambda p, i: (i, zb + p))] + r_in_specs,
        out_specs=tuple([pl.BlockSpec((tq, LANE), lambda p, i: (i, p)), pl.BlockSpec((tq, LANE), lambda p, i: (i, p)),
                         pl.BlockSpec((2, 1, tq, LANE), lambda p, i: (p, i, 0, 0)), kept_spec, kept_spec] + r_out_specs),
        scratch_shapes=scratch + r_sems,
        compiler_params=_params("arbitrary", "arbitrary"),
    )(projb, projb, projb, proj, *r_in)


def _gates(gl0, gl1, bg, D):
    return _sigmoid(gl0 + bg[:, :D]), _sigmoid(gl1 + bg[:, D:])


def _merge_fwd(y_pool, y_attn, w_pu, w_au, proj, b_gate, tm):
    T = y_pool.shape[0]
    D = w_pu.shape[1]
    gb = _proj_layout(D)["gates"] // D

    def body(yp_ref, ya_ref, wpu_ref, wau_ref, gl0_ref, gl1_ref, bg_ref, m_ref):
        p = jnp.dot(yp_ref[...], wpu_ref[...], preferred_element_type=F32)
        a = jnp.dot(ya_ref[...], wau_ref[...], preferred_element_type=F32)
        g0, g1 = _gates(gl0_ref[...], gl1_ref[...], bg_ref[...], D)
        m_ref[...] = (g0 * p + g1 * a).astype(BF16)

    row = lambda i: (i, 0)
    fixed = lambda i: (0, 0)
    return _pc(
        body, name="merge_fwd", grid=(T // tm,), out_shape=jax.ShapeDtypeStruct((T, D), BF16),
        in_specs=[pl.BlockSpec((tm, POOL_W), row), pl.BlockSpec((tm, ATTN_W), row),
                  pl.BlockSpec((POOL_W, D), fixed, pipeline_mode=pl.Buffered(1)), pl.BlockSpec((ATTN_W, D), fixed, pipeline_mode=pl.Buffered(1)),
                  pl.BlockSpec((tm, D), lambda i: (i, gb)), pl.BlockSpec((tm, D), lambda i: (i, gb + 1)),
                  pl.BlockSpec((1, 2 * D), fixed)],
        out_specs=pl.BlockSpec((tm, D), row),
        compiler_params=_params("parallel"),
    )(y_pool, y_attn, w_pu, w_au, proj, proj, b_gate)


def _out_proj_final_loss(merged, w_out, x, g, target, tm):
    T, D = x.shape

    def body(m_ref, w_ref, x_ref, g_ref, t_ref, dx_ref, dg_ref, loss_ref):
        @pl.when(pl.program_id(0) == 0)
        def _():
            dg_ref[...] = jnp.zeros_like(dg_ref)
            loss_ref[...] = jnp.zeros_like(loss_ref)

        xv = x_ref[...] + jnp.dot(m_ref[...], w_ref[...], preferred_element_type=F32)
        gv = g_ref[...]
        r = lax.rsqrt(jnp.mean(xv * xv, axis=-1, keepdims=True) + RMS_EPS)
        xh = xv * r
        d = xh * gv - t_ref[...]
        loss_ref[...] += 0.5 * jnp.sum(jnp.mean(d * d, axis=-1, keepdims=True), axis=0, keepdims=True)
        dy = d * (1.0 / D)
        dg_ref[...] += jnp.sum(dy * xh, axis=0, keepdims=True)
        dh = dy * gv
        dx_ref[...] = r * (dh - xh * jnp.mean(dh * xh, axis=-1, keepdims=True))

    row = lambda i: (i, 0)
    fixed = lambda i: (0, 0)
    return _pc(
        body, name="out_proj_final_loss", grid=(T // tm,),
        out_shape=(jax.ShapeDtypeStruct((T, D), F32), jax.ShapeDtypeStruct((1, D), F32),
                   jax.ShapeDtypeStruct((8, LANE), F32)),
        in_specs=[pl.BlockSpec((tm, D), row), pl.BlockSpec((D, D), fixed, pipeline_mode=pl.Buffered(1)), pl.BlockSpec((tm, D), row),
                  pl.BlockSpec((1, D), fixed), pl.BlockSpec((tm, D), row)],
        out_specs=(pl.BlockSpec((tm, D), row), pl.BlockSpec((1, D), fixed), pl.BlockSpec((8, LANE), fixed)),
        compiler_params=_params("arbitrary"),
    )(merged, w_out, x, g, target)


def _merge_bwd(dxo, w_out, y_pool, y_attn, w_pu, w_au, proj, b_gate, tm):
    T, D = dxo.shape
    gb = _proj_layout(D)["gates"] // D

    def body(dxo_ref, wout_ref, yp_ref, ya_ref, wpu_ref, wau_ref, gl0_ref, gl1_ref, bg_ref,
             dp_ref, da_ref, dgl_ref, dbg_ref):
        @pl.when(pl.program_id(0) == 0)
        def _():
            dbg_ref[...] = jnp.zeros_like(dbg_ref)

        dmv = lax.dot_general(dxo_ref[...].astype(BF16), wout_ref[...], NT_DIMS, preferred_element_type=F32)
        p = jnp.dot(yp_ref[...], wpu_ref[...], preferred_element_type=F32)
        a = jnp.dot(ya_ref[...], wau_ref[...], preferred_element_type=F32)
        g0, g1 = _gates(gl0_ref[...], gl1_ref[...], bg_ref[...], D)
        dp_ref[...] = (dmv * g0).astype(BF16)
        da_ref[...] = (dmv * g1).astype(BF16)
        dgl0 = dmv * p * (g0 * (1.0 - g0))
        dgl1 = dmv * a * (g1 * (1.0 - g1))
        dgl_ref[:, :D] = dgl0.astype(BF16)
        dgl_ref[:, D:] = dgl1.astype(BF16)
        dbg_ref[:, :D] += jnp.sum(dgl0, axis=0, keepdims=True)
        dbg_ref[:, D:] += jnp.sum(dgl1, axis=0, keepdims=True)

    row = lambda i: (i, 0)
    fixed = lambda i: (0, 0)
    return _pc(
        body, name="merge_bwd", grid=(T // tm,),
        out_shape=(jax.ShapeDtypeStruct((T, D), BF16), jax.ShapeDtypeStruct((T, D), BF16),
                   jax.ShapeDtypeStruct((T, 2 * D), BF16), jax.ShapeDtypeStruct((1, 2 * D), F32)),
        in_specs=[pl.BlockSpec((tm, D), row), pl.BlockSpec((D, D), fixed, pipeline_mode=pl.Buffered(1)),
                  pl.BlockSpec((tm, POOL_W), row), pl.BlockSpec((tm, ATTN_W), row),
                  pl.BlockSpec((POOL_W, D), fixed, pipeline_mode=pl.Buffered(1)), pl.BlockSpec((ATTN_W, D), fixed, pipeline_mode=pl.Buffered(1)),
                  pl.BlockSpec((tm, D), lambda i: (i, gb)), pl.BlockSpec((tm, D), lambda i: (i, gb + 1)),
                  pl.BlockSpec((1, 2 * D), fixed)],
        out_specs=(pl.BlockSpec((tm, D), row), pl.BlockSpec((tm, D), row), pl.BlockSpec((tm, 2 * D), row),
                   pl.BlockSpec((1, 2 * D), fixed)),
        compiler_params=_params("arbitrary"),
    )(dxo, w_out, y_pool, y_attn, w_pu, w_au, proj, proj, b_gate)


def _pool_bwd(proj, dp, w_pu, pool_w, scale, R):
    T = proj.shape[0]
    D = w_pu.shape[1]
    nb = T // R
    hb = R // HALO

    def body(u_ref, up_ref, z_ref, dp_ref, wpu_ref, pw_ref, sc_ref, du_ref, dz_ref, dpw_ref, dsc_ref, halo_sc):
        i = pl.program_id(0)
        rb = nb - 1 - i

        @pl.when(i == 0)
        def _():
            halo_sc[...] = jnp.zeros_like(halo_sc)
            dpw_ref[...] = jnp.zeros_like(dpw_ref)
            dsc_ref[...] = jnp.zeros_like(dsc_ref)

        u = u_ref[...]
        row = rb * R + lax.broadcasted_iota(jnp.int32, (R, 1), 0)
        before = jnp.where(rb > 0, up_ref[...], 0.0)
        pooled = _pooled(u, before, row)
        pw = [pw_ref[g].astype(BF16) for g in range(len(POOL_WINDOWS))]
        mixed = jnp.concatenate(
            [jnp.dot(pooled[g].astype(BF16), pw[g], preferred_element_type=F32) for g in range(len(POOL_WINDOWS))],
            axis=1)
        sc = sc_ref[...]
        silu, dsilu = _silu_and_grad(z_ref[...])
        dyv = lax.dot_general(dp_ref[...], wpu_ref[...], NT_DIMS, preferred_element_type=F32)
        dmp = dyv * silu
        dz_ref[...] = (dyv * (mixed * sc) * dsilu).astype(BF16)
        dsc_ref[...] += jnp.sum(dmp * mixed, axis=0, keepdims=True)
        dmixed = (dmp * sc).astype(BF16)
        dpn = []
        dpooled = []
        for g, w in enumerate(POOL_WINDOWS):
            cols = slice(g * POOL_G, (g + 1) * POOL_G)
            dpw_ref[g] += lax.dot_general(pooled[g].astype(BF16), dmixed[:, cols], TN_DIMS,
                                          preferred_element_type=F32)
            dpg = lax.dot_general(dmixed[:, cols], pw[g], NT_DIMS, preferred_element_type=F32)
            dpooled.append(dpg)
            dpn.append(dpg / jnp.minimum(row + 1, w).astype(F32))
        dpn = jnp.concatenate(dpn, axis=1)
        sums = _window_sums(jnp.concatenate([dpn, halo_sc[...]], axis=0), False)
        du_ref[...] = jnp.concatenate(
            [sums[g][:R, g * POOL_G:(g + 1) * POOL_G] - dpooled[g] for g in range(len(POOL_WINDOWS))],
            axis=1).astype(BF16)
        halo_sc[...] = dpn[:HALO, :]

    rev = lambda i: (nb - 1 - i, 0)
    return _pc(
        body, name="pool_bwd", grid=(nb,),
        out_shape=(jax.ShapeDtypeStruct((T, POOL_W), BF16), jax.ShapeDtypeStruct((T, POOL_W), BF16),
                   jax.ShapeDtypeStruct((4, POOL_G, POOL_G), F32), jax.ShapeDtypeStruct((1, POOL_W), F32)),
        in_specs=[pl.BlockSpec((R, POOL_W), rev),
                  pl.BlockSpec((HALO, POOL_W), lambda i: (jnp.maximum((nb - 1 - i) * hb - 1, 0), 0)),
                  pl.BlockSpec((R, POOL_W), lambda i: (nb - 1 - i, 1)),
                  pl.BlockSpec((R, D), rev), pl.BlockSpec((POOL_W, D), lambda i: (0, 0)),
                  pl.BlockSpec((4, POOL_G, POOL_G), lambda i: (0, 0, 0)), pl.BlockSpec((1, POOL_W), lambda i: (0, 0))],
        out_specs=(pl.BlockSpec((R, POOL_W), rev), pl.BlockSpec((R, POOL_W), rev),
                   pl.BlockSpec((4, POOL_G, POOL_G), lambda i: (0, 0, 0)), pl.BlockSpec((1, POOL_W), lambda i: (0, 0))),
        scratch_shapes=[pltpu.VMEM((HALO, POOL_W), F32)],
        compiler_params=_params("arbitrary"),
    )(proj, proj, proj, dp, w_pu, pool_w, scale)


def _attn_gate_bwd(da, w_au, o, proj, tm):
    T, D = da.shape
    zb = (proj.shape[1] - ATTN_W) // ATTN_W

    def body(da_ref, wau_ref, o_ref, za_ref, do_ref, dza_ref):
        silu, dsilu = _silu_and_grad(za_ref[...])
        dyv = lax.dot_general(da_ref[...], wau_ref[...], NT_DIMS, preferred_element_type=F32)
        do_ref[...] = (dyv * silu).astype(BF16)
        dza_ref[...] = (dyv * o_ref[...] * dsilu).astype(BF16)

    row = lambda i: (i, 0)
    return _pc(
        body, name="attn_gate_bwd", grid=(T // tm,),
        out_shape=(jax.ShapeDtypeStruct((T, ATTN_W), BF16), jax.ShapeDtypeStruct((T, ATTN_W), BF16)),
        in_specs=[pl.BlockSpec((tm, D), row), pl.BlockSpec((ATTN_W, D), lambda i: (0, 0)),
                  pl.BlockSpec((tm, ATTN_W), row), pl.BlockSpec((tm, ATTN_W), lambda i: (i, zb))],
        out_specs=(pl.BlockSpec((tm, ATTN_W), row), pl.BlockSpec((tm, ATTN_W), row)),
        compiler_params=_params("parallel"),
    )(da, w_au, o, proj)


def _attn_bwd(projb, do, carries, kept_w, kept_beta, tq, rider=None):
    T = projb.shape[0]
    nq = T // tq
    tk, nk = tq, nq
    n_pairs = ATTN_W // LANE
    scale = HEAD_DIM ** -0.5

    def body(q_ref, k_ref, v_ref, do_ref, c_ref, ws_ref, bs_ref, dq_ref, dk_ref, dv_ref, kT_sc, vT_sc, km_sc, dkT_ref, dvT_ref,
             *per_head):
        f_sc, dq_sc, l_sc, dw_sc, dl_sc, w_sc = (per_head[2 * n:2 * n + 2] for n in range(6))
        i = pl.program_id(1)

        @pl.when(i == 0)
        def _():
            _fill_blocks(k_ref, nk, tk, transposed_sc=kT_sc, masked_sc=km_sc)
            _fill_blocks(v_ref, nk, tk, transposed_sc=vT_sc)
            dkT_ref[...] = jnp.zeros_like(dkT_ref)
            dvT_ref[...] = jnp.zeros_like(dvT_ref)

        qs = _head_masks(q_ref[...], scale)
        dos = _head_masks(do_ref[...], 1.0)
        qT = [x.astype(F32).T.astype(BF16) for x in qs]
        doT = [x.astype(F32).T.astype(BF16) for x in dos]
        lane = lax.broadcasted_iota(jnp.int32, (1, LANE), 1)
        kk0 = lax.broadcasted_iota(jnp.int32, (tk, tk), 0)
        kk1 = lax.broadcasted_iota(jnp.int32, (tk, tk), 1)
        prefix = (kk0 <= kk1).astype(BF16)
        for a in range(2):
            f_sc[a][...] = jnp.zeros_like(f_sc[a])
            dq_sc[a][...] = jnp.zeros_like(dq_sc[a])
        live = lane == i
        for a in range(2):
            cmin = jnp.min(c_ref[a, 0], axis=0, keepdims=True)
            l_max = jnp.max(jnp.where(lane == LANE - 1, cmin, -NO_CARRY), axis=1, keepdims=True)
            live = jnp.logical_or(live, cmin - l_max <= ZERO_WEIGHT)
        t0 = jnp.min(jnp.where(jnp.logical_and(live, lane <= i), lane, i))
        n = i - t0

        def products(j, slot, with_logits=True):
            kT = kT_sc[j]
            vT = vT_sc[j]
            for a in range(2):
                if with_logits:
                    l_sc[a][slot] = jnp.dot(qs[a], kT, preferred_element_type=F32)
                dw_sc[a][slot] = jnp.dot(dos[a], vT, preferred_element_type=F32)

        def gradients(j, slot):
            rows = pl.ds(pl.multiple_of(j * tk, tk), tk)
            dkT = []
            dvT = []
            for a in range(2):
                dlb = dl_sc[a][slot]
                dq_sc[a][...] += jnp.dot(dlb, km_sc[a, rows, :], preferred_element_type=F32)
                dkT.append(jnp.dot(qT[a], dlb, preferred_element_type=F32))
                dvT.append(jnp.dot(doT[a], w_sc[a][slot], preferred_element_type=F32))
            dkT_ref[j] += dkT[0] + dkT[1]
            dvT_ref[j] += dvT[0] + dvT[1]

        def elementwise(j, slot):
            suffix = (kk0 >= kk1).astype(BF16)
            sp, inc, e, beta, p = [None] * 2, [None] * 2, [None] * 2, [None] * 2, [None] * 2
            for a in range(2):
                sp[a] = _softplus(l_sc[a][slot])
            for a in range(2):
                inc[a] = jnp.dot(sp[a].astype(BF16), suffix, preferred_element_type=F32)
            for a in range(2):
                l = l_sc[a][slot]
                c = jnp.sum(jnp.where(lane == j, c_ref[a, 0], 0.0), axis=1, keepdims=True)
                w = jnp.exp(l - inc[a] - c)
                w_sc[a][slot] = w.astype(BF16)
                beta[a] = jnp.exp(l - sp[a])
                e[a] = w * dw_sc[a][slot]
            for a in range(2):
                p[a] = jnp.dot(e[a].astype(BF16), prefix, preferred_element_type=F32)
            for a in range(2):
                f = f_sc[a][...]
                dl = e[a] - beta[a] * (p[a] + _wide(f, tk))
                dl_sc[a][slot] = dl.astype(BF16)
                f_sc[a][...] = f + p[a][:, tk - 1:tk]

        def kept_blocks(tiles):
            keys = [(n_, a) for n_ in range(len(tiles)) for a in range(2)]
            for j, slot, cols in tiles:
                products(j, slot, with_logits=False)
            e, p = {}, {}
            for n_, a in keys:
                j, slot, cols = tiles[n_]
                w = ws_ref[a, 0, :, cols]
                w_sc[a][slot] = w.astype(BF16)
                e[n_, a] = w * dw_sc[a][slot]
            for k in keys:
                p[k] = jnp.dot(e[k].astype(BF16), prefix, preferred_element_type=F32)
            for a in range(2):
                f = f_sc[a][...]
                for n_, (j, slot, cols) in enumerate(tiles):
                    dl = e[n_, a] - bs_ref[a, 0, :, cols] * (p[n_, a] + _wide(f, tk))
                    dl_sc[a][slot] = dl.astype(BF16)
                    f = f + p[n_, a][:, tk - 1:tk]
                f_sc[a][...] = f

        def step(r, slot):
            products(t0 + r + 1, 1 - slot)
            gradients(t0 + jnp.maximum(r - 1, 0), 1 - slot)
            elementwise(t0 + r, slot)

        def last_two(slot, pending=True):
            if pending:
                gradients(i - 2, slot)
            kept_blocks([(i - 1, 1 - slot, slice(0, tk)), (i, slot, slice(tk, 2 * tk))])
            gradients(i - 1, 1 - slot)
            gradients(i, slot)

        @pl.when(n >= 2)
        def _():
            for a in range(2):
                dl_sc[a][1] = jnp.zeros((tq, tk), BF16)
                w_sc[a][1] = jnp.zeros((tq, tk), BF16)
            products(t0, 0)

        def two_steps(tt, carry):
            step(2 * tt, 0)
            step(2 * tt + 1, 1)
            return carry

        lax.fori_loop(0, jnp.maximum(n - 1, 0) // 2, two_steps, 0)

        @pl.when(n == 0)
        def _():
            kept_blocks([(i, 0, slice(tk, 2 * tk))])
            gradients(i, 0)

        @pl.when(n == 1)
        def _():
            last_two(1, pending=False)

        @pl.when(jnp.logical_and(n > 1, n % 2 == 1))
        def _():
            last_two(1)

        @pl.when(jnp.logical_and(n > 0, n % 2 == 0))
        def _():
            step(n - 2, 0)
            last_two(0)

        dq_ref[...] = ((dq_sc[0][...] + dq_sc[1][...]) * scale).astype(BF16)

        @pl.when(i == nq - 1)
        def _():
            def untranspose(j, carry):
                rows = pl.ds(pl.multiple_of(j * tk, tk), tk)
                dk_ref[rows, :] = dkT_ref[j].T.astype(BF16)
                dv_ref[rows, :] = dvT_ref[j].T.astype(BF16)
                return carry

            lax.fori_loop(0, nk, untranspose, 0)

    scratch = ([pltpu.VMEM((nk, LANE, tk), BF16), pltpu.VMEM((nk, LANE, tk), BF16), pltpu.VMEM((2, T, LANE), BF16),
                pltpu.VMEM((nk, LANE, tk), F32), pltpu.VMEM((nk, LANE, tk), F32)]
               + [pltpu.VMEM((tq, LANE), F32)] * 4
               + [pltpu.VMEM((2, tq, tk), F32)] * 4 + [pltpu.VMEM((2, tq, tk), BF16)] * 4)
    r_in, r_in_specs, r_out, r_out_specs, r_sems = _rider_specs(rider)
    body = _with_rider(body, 7, 3, len(scratch), rider, *_first_last_step(n_pairs, nq))
    kept_spec = pl.BlockSpec((2, 1, tq, 2 * tk), lambda p, i: (p, i, 0, 0))
    return _pc(
        body, name="attn_bwd", grid=(n_pairs, nq),
        out_shape=tuple([jax.ShapeDtypeStruct((T, ATTN_W), BF16)] * 3 + r_out),
        in_specs=[pl.BlockSpec((tq, LANE), lambda p, i: (i, p)),
                  pl.BlockSpec((T, LANE), lambda p, i: (0, n_pairs + p)),
                  pl.BlockSpec((T, LANE), lambda p, i: (0, 2 * n_pairs + p)),
                  pl.BlockSpec((tq, LANE), lambda p, i: (i, p)),
                  pl.BlockSpec((2, 1, tq, LANE), lambda p, i: (p, i, 0, 0)), kept_spec, kept_spec] + r_in_specs,
        out_specs=tuple([pl.BlockSpec((tq, LANE), lambda p, i: (i, p)), pl.BlockSpec((T, LANE), lambda p, i: (0, p)),
                         pl.BlockSpec((T, LANE), lambda p, i: (0, p))] + r_out_specs),
        scratch_shapes=scratch + r_sems,
        compiler_params=_params("arbitrary", "arbitrary"),
    )(projb, projb, projb, do, carries, kept_w, kept_beta, *r_in)


def _dh_norm_bwd(segs, w, x, g, dxo, tm, rider=None):
    T, D = x.shape
    nm = T // tm
    n_seg = len(segs)
    offs = [sum(y.shape[1] for y in segs[:n]) for n in range(n_seg + 1)]
    assert offs[-1] == w.shape[1]

    def body(*refs):
        seg_refs = refs[:n_seg]
        w_ref, x_ref, g_ref, dxo_ref, dx_ref, dg_ref = refs[n_seg:]

        @pl.when(pl.program_id(0) == 0)
        def _():
            dg_ref[...] = jnp.zeros_like(dg_ref)

        dhv = None
        for n in range(n_seg):
            part = lax.dot_general(seg_refs[n][...], w_ref[:, offs[n]:offs[n + 1]], NT_DIMS, preferred_element_type=F32)
            dhv = part if dhv is None else dhv + part
        xv = x_ref[...]
        r = lax.rsqrt(jnp.mean(xv * xv, axis=-1, keepdims=True) + RMS_EPS)
        xh = xv * r
        dg_ref[...] += jnp.sum(dhv * xh, axis=0, keepdims=True)
        dhg = dhv * g_ref[...]
        dx_ref[...] = dxo_ref[...] + r * (dhg - xh * jnp.mean(dhg * xh, axis=-1, keepdims=True))

    r_in, r_in_specs, r_out, r_out_specs, r_sems = _rider_specs(rider)
    body = _with_rider(body, n_seg + 4, 2, 0, rider, lambda: pl.program_id(0) == 0, lambda: pl.program_id(0) == nm - 1)
    row = lambda i: (i, 0)
    fixed = lambda i: (0, 0)
    return _pc(
        body, name="d_h_norm_bwd", grid=(nm,),
        out_shape=tuple([jax.ShapeDtypeStruct((T, D), F32), jax.ShapeDtypeStruct((1, D), F32)] + r_out),
        in_specs=[pl.BlockSpec((tm, y.shape[1]), row) for y in segs]
        + [pl.BlockSpec(w.shape, fixed, pipeline_mode=pl.Buffered(1)), pl.BlockSpec((tm, D), row), pl.BlockSpec((1, D), fixed), pl.BlockSpec((tm, D), row)]
        + r_in_specs,
        out_specs=tuple([pl.BlockSpec((tm, D), row), pl.BlockSpec((1, D), fixed)] + r_out_specs),
        scratch_shapes=r_sems,
        compiler_params=_params("arbitrary"),
    )(*segs, w, x, g, dxo, *r_in)


def _adamw(pieces, w, m, v, name):
    G = len(pieces)
    rows, cols = pieces[0].shape[1:]
    assert w.shape == (G * rows, cols)
    br = rows
    while br * cols > 65536 and br % 16 == 0:
        br //= 2
    nb = rows // br
    c1 = 1.0 / (1.0 - ADAM_B1 ** ADAM_STEP)
    c2 = 1.0 / (1.0 - ADAM_B2 ** ADAM_STEP)

    def body(*refs):
        p_refs = refs[:G]
        w_ref, m_ref, v_ref, g_ref, d_ref, nm_ref, nv_ref = refs[G:]
        g = None
        for n, p_ref in enumerate(p_refs):
            gn = p_ref[0].astype(F32)
            for s in range(1, N_DEV):
                gn = gn + p_ref[s].astype(F32)
            g = gn if g is None else jnp.where(pl.program_id(0) == n, gn, g)
        nm = ADAM_B1 * m_ref[...] + (1.0 - ADAM_B1) * g
        nv = ADAM_B2 * v_ref[...] + (1.0 - ADAM_B2) * (g * g)
        g_ref[...] = g
        nm_ref[...] = nm
        nv_ref[...] = nv
        d_ref[...] = -ADAM_LR * ((nm * c1) / (jnp.sqrt(nv * c2) + ADAM_EPS) + ADAM_WD * w_ref[...])

    blk = pl.BlockSpec((br, cols), lambda n, i: (n * nb + i, 0))
    shape = jax.ShapeDtypeStruct((G * rows, cols), F32)
    return _pc(
        body, name=name, grid=(G, nb), out_shape=(shape, shape, shape, shape),
        in_specs=[pl.BlockSpec((N_DEV, br, cols), lambda n, i: (0, i, 0))] * G + [blk, blk, blk],
        out_specs=(blk, blk, blk, blk),
        compiler_params=_params("parallel", "parallel"),
    )(*pieces, w, m, v)


def _rows128(a):
    flat = a.reshape(-1)
    n = flat.shape[0]
    padded = -(-n // (8 * LANE)) * (8 * LANE)
    if padded != n:
        flat = jnp.concatenate([flat, jnp.zeros((padded - n,), flat.dtype)])
    return flat.reshape(-1, LANE)


def _pack(parts):
    return jnp.concatenate([_rows128(p) for p in parts], axis=0)


def _unpack(packed, like):
    out, r = [], 0
    for a in like:
        n = a.size
        nr = -(-n // (8 * LANE)) * 8
        out.append(packed[r:r + nr].reshape(-1)[:n].reshape(a.shape))
        r += nr
    return out


def kernel(x, norm_g, w_in, b_gate, pool_w, pool_scale, w_pool_up, w_attn_up, w_out, final_g, loss_target, m_norm_g, m_w_in, m_b_gate, m_pool_w, m_pool_scale, m_w_pool_up, m_w_attn_up, m_w_out, m_final_g, v_norm_g, v_w_in, v_b_gate, v_pool_w, v_pool_scale, v_w_pool_up, v_w_attn_up, v_w_out, v_final_g):
    L = norm_g.shape[0]
    T, D = x.shape[1], x.shape[2]
    NW = w_in.shape[2] * N_DEV
    assert NW == 2 * POOL_W + 4 * ATTN_W + 2 * D and x.shape[0] == 1
    tm = min(512, T)
    tq = min(256, T // 2)
    x0 = x.reshape(T, D)
    target = loss_target.reshape(T, D)

    assert L >= 2
    win_first = jnp.transpose(_gather_two_level(w_in[0].astype(BF16), "gather_w_in0"), (1, 0, 2)).reshape(D, NW)
    rest = [w_in[1:].astype(BF16), w_pool_up.astype(BF16), w_attn_up.astype(BF16), w_out.astype(BF16)]

    saved = []
    xl = x0
    for l in range(L):
        proj, projb, h = _norm_inproj(xl, norm_g[l:l + 1], win_first if l == 0 else win_rest[l - 1], min(512, T))
        y_pool = _pool_fwd(proj, pool_w[l], pool_scale[l:l + 1], tm)
        if l == 0:
            o, y_attn, carries, kept_w, kept_b, g_in, g_pu, g_au, g_out = _attn_fwd(projb, proj, tq, rider=(rest, [False] * 4))
            win_rest = jnp.transpose(g_in, (1, 2, 0, 3)).reshape(L - 1, D, NW)
            wpu_full = jnp.transpose(g_pu, (1, 2, 0, 3)).reshape(L, POOL_W, D)
            wau_full = jnp.transpose(g_au, (1, 2, 0, 3)).reshape(L, ATTN_W, D)
            wout_full = jnp.transpose(g_out, (1, 0, 2, 3)).reshape(L, D, D)
        else:
            o, y_attn, carries, kept_w, kept_b = _attn_fwd(projb, proj, tq)
        merged = _merge_fwd(y_pool, y_attn, wpu_full[l], wau_full[l], proj, b_gate[l:l + 1], min(512, T))
        saved.append((xl, proj, projb, h, y_pool, o, y_attn, (carries, kept_w, kept_b), merged))
        if l < L - 1:
            xl = _mm_nn_res(merged, wout_full[l], xl, tm, "out_proj")

    dx, d_final_g, loss_part = _out_proj_final_loss(merged, wout_full[L - 1], xl, final_g.reshape(1, D), target, tm)

    d_norm_g, d_b_gate, d_pool_w, d_pool_scale = [None] * L, [None] * L, [None] * L, [None] * L
    d_win, d_wpu, d_wau, d_wout = [None] * L, [None] * L, [None] * L, [None] * L
    small_like = [norm_g, b_gate, pool_w, pool_scale, final_g, jnp.zeros((8, LANE), F32)]
    for l in reversed(range(L)):
        xin, proj, projb, h, y_pool, o, y_attn, carries, merged = saved[l]
        win_l = win_first if l == 0 else win_rest[l - 1]
        d_wout[l] = _mm_tn(merged, dx, 1, D, min(1024, T), "d_w_out").reshape(N_DEV, D // N_DEV, D)
        dp, da, dgl, d_b_gate[l] = _merge_bwd(dx, wout_full[l], y_pool, y_attn, wpu_full[l], wau_full[l], proj,
                                              b_gate[l:l + 1], min(512, T))
        d_wpu[l] = _mm_tn(y_pool, dp, N_DEV, D, min(1024, T), "d_w_pool_up")
        d_wau[l] = _mm_tn(y_attn, da, N_DEV, D, min(1024, T), "d_w_attn_up")
        du, dzp, d_pool_w[l], d_pool_scale[l] = _pool_bwd(proj, dp, wpu_full[l], pool_w[l], pool_scale[l:l + 1], tm)
        do, dza = _attn_gate_bwd(da, wau_full[l], o, proj, tm)
        if l == 0:
            small = _pack([jnp.concatenate([jnp.zeros((1, D), F32)] + d_norm_g[1:], 0), jnp.concatenate(d_b_gate, 0),
                           jnp.stack(d_pool_w, 0), jnp.concatenate(d_pool_scale, 0), d_final_g, loss_part])
            early = d_win[1:] + d_wpu + d_wau + d_wout
            dq, dk, dv, got_small, *got_early = _attn_bwd(
                projb, do, *carries, tq, rider=([small] + early, [False] + [True] * len(early)))
        else:
            dq, dk, dv = _attn_bwd(projb, do, *carries, tq)
        segs = [du, dzp, dq, dk, dv, dza, dgl]
        sw = NW // N_DEV
        half = _mm_tn_segs(h, segs[5:], sw, min(1024, T), "d_w_in_b", into=jnp.zeros((N_DEV, D, sw), BF16), first_shard=N_DEV // 2)
        d_win[l] = _mm_tn_segs(h, segs[:5], sw, min(1024, T), "d_w_in_a", into=half)
        if l == 0:
            dx, d_norm_g[l], got_win0 = _dh_norm_bwd(segs, win_l, xin, norm_g[l:l + 1], dx, min(512, T),
                                                     rider=([d_win[0]], [True]))
        else:
            dx, d_norm_g[l] = _dh_norm_bwd(segs, win_l, xin, norm_g[l:l + 1], dx, min(512, T))

    got_norm0 = _exchange([d_norm_g[0].reshape(-1, LANE)], [False], "gather_norm_grad")[0]
    r_small = jnp.concatenate([got_norm0, got_small[:, D // LANE:]], axis=1)

    def update(pieces, w, m, v, name):
        cols = w.shape[-1]
        res = _adamw([p.reshape(N_DEV, -1, cols) for p in pieces], w.reshape(-1, cols), m.reshape(-1, cols),
                     v.reshape(-1, cols), name)
        return [r.reshape(w.shape) for r in res]

    u_in = update([got_win0] + got_early[:L - 1], w_in, m_w_in, v_w_in, "adamw_w_in")
    u_pu = update(got_early[L - 1:2 * L - 1], w_pool_up, m_w_pool_up, v_w_pool_up, "adamw_w_pool_up")
    u_au = update(got_early[2 * L - 1:3 * L - 1], w_attn_up, m_w_attn_up, v_w_attn_up, "adamw_w_attn_up")
    u_out = update(got_early[3 * L - 1:4 * L - 1], w_out, m_w_out, v_w_out, "adamw_w_out")
    zeros = small_like[-1]
    smalls = _adamw([r_small],
                    _pack([norm_g, b_gate, pool_w, pool_scale, final_g, zeros]),
                    _pack([m_norm_g, m_b_gate, m_pool_w, m_pool_scale, m_final_g, zeros]),
                    _pack([v_norm_g, v_b_gate, v_pool_w, v_pool_scale, v_final_g, zeros]), "adamw_small")
    s_g, s_d, s_m, s_v = [_unpack(s, small_like) for s in smalls]
    loss = s_g[5][0, 0]

    def ordered(k):
        s = (s_g, s_d, s_m, s_v)[k]
        return [s[0], u_in[k], s[1], s[2], s[3], u_pu[k], u_au[k], u_out[k], s[4]]

    return (loss, dx.reshape(x.shape), *ordered(0), *ordered(1), *ordered(2), *ordered(3))
```
